```python
import math
import jax
import jax.numpy as jnp
from jax import lax
import numpy as np

D_MODEL = 1024
BATCH = 2
SEQ = 8192
DEPTH = 2

GRID_W = 64
CTX_LEN = 256
N_BRANCH = 4
BRANCH_W = D_MODEL // 4
N_HEADS = 4
HEAD_DIM = BRANCH_W // N_HEADS
GLA_RANK = 16
GLA_NORMALIZER = 16.0
CHUNK = 64
CONV_W = 5
DIFF_QK = HEAD_DIM // 2
ROPE_AXIS = DIFF_QK // 2
ROPE_BASE = 10000.0
ATTN_BLOCK = 128
N_EXPERTS = 16
N_GROUPS = 4
EXPERTS_PER_GROUP = N_EXPERTS // N_GROUPS
TOP_K = 2
D_EXPERT = 512
EPS = 1e-6

IN_SIZES = (
    BRANCH_W, BRANCH_W, BRANCH_W, BRANCH_W, GLA_RANK, GLA_RANK,
    BRANCH_W, BRANCH_W, BRANCH_W, BRANCH_W, 2 * N_HEADS, 2 * N_HEADS,
    BRANCH_W, BRANCH_W, BRANCH_W, BRANCH_W, 2 * N_HEADS, 2 * N_HEADS,
    BRANCH_W, BRANCH_W, BRANCH_W,
    N_BRANCH * D_MODEL,
)
IN_COLS = sum(IN_SIZES)

kernel_name = 'hybrid_gla_mlstm_gdn_diffattn_grouped_moe'


def _split_cols(p):
    return jnp.split(p, [int(o) for o in np.cumsum(IN_SIZES)[:-1]], axis=-1)


def _rms(x, g):
    xf = x.astype(jnp.float32)
    y = xf * lax.rsqrt(jnp.mean(xf * xf, axis=-1, keepdims=True) + EPS)
    return (y * g.astype(jnp.float32)).astype(x.dtype)


def _heads(t):
    b, n, w = t.shape
    return t.reshape(b, n, N_HEADS, w // N_HEADS).transpose(0, 2, 1, 3)


def _gate_heads(t):
    return jnp.swapaxes(t, 1, 2)


def _merge_heads(o):
    b, h, n, d = o.shape
    return o.transpose(0, 2, 1, 3).reshape(b, n, h * d)


def _head_norm(y, g):
    b, n, w = y.shape
    yh = y.reshape(b, n, N_HEADS, w // N_HEADS)
    yh = yh * lax.rsqrt(jnp.mean(yh * yh, axis=-1, keepdims=True) + EPS)
    return yh.reshape(b, n, w) * g.astype(jnp.float32)


def _l2(t):
    return t * lax.rsqrt(jnp.sum(t * t, axis=-1, keepdims=True) + EPS)


def _conv_centred(x, w):
    ch = x.shape[-1]
    return lax.conv_general_dilated(
        x, w[:, None, :].astype(x.dtype), window_strides=(1,),
        padding=((CONV_W // 2, CONV_W // 2),),
        dimension_numbers=('NWC', 'WIO', 'NWC'), feature_group_count=ch)


def _axial_rope_tables(row, col):
    inv = ROPE_BASE ** (-jnp.arange(0, ROPE_AXIS, 2, dtype=jnp.float32) / ROPE_AXIS)
    ang = jnp.concatenate([row.astype(jnp.float32)[:, None] * inv,
                           col.astype(jnp.float32)[:, None] * inv], axis=-1)
    return jnp.cos(ang), jnp.sin(ang)


def _rope(x, cos, sin):
    x1, x2 = jnp.split(x, 2, axis=-1)
    return jnp.concatenate([x1 * cos - x2 * sin, x1 * sin + x2 * cos], axis=-1)


def _bidirectional(scan_fn, s0, lat_f, ctx_f, lat_b, ctx_b):
    rev = lambda ts: tuple(jnp.flip(t, axis=2) for t in ts)
    oc_f, sc_f = scan_fn(*ctx_f, s0)
    o_f, _ = scan_fn(*lat_f, sc_f)
    oc_b, sc_b = scan_fn(*rev(ctx_b), s0)
    o_b, _ = scan_fn(*rev(lat_b), sc_b)
    return o_f + jnp.flip(o_b, axis=2), oc_f + jnp.flip(oc_b, axis=2)


def _gla_chunked(q, k, v, log_a, s0):
    b, h, n, dk = q.shape
    dv = v.shape[-1]
    nc = n // CHUNK
    q, k, log_a = (t.reshape(b, h, nc, CHUNK, dk) for t in (q, k, log_a))
    v = v.reshape(b, h, nc, CHUNK, dv)
    tril = jnp.tril(jnp.ones((CHUNK, CHUNK), dtype=bool))
    cum = jnp.cumsum(log_a, axis=3)
    cum_end = cum[:, :, :, -1:, :]
    q_in = q * jnp.exp(cum)
    att = jnp.einsum('bhncd,bhnsd->bhncs', q_in, k * jnp.exp(-cum))
    att = jnp.where(tril, att, 0.0)
    kv = jnp.einsum('bhnsd,bhnse->bhnde', k * jnp.exp(cum_end - cum), v)
    decay = jnp.exp(cum_end[:, :, :, 0, :])

    def step(s, xs):
        dec, kv_n = xs
        return dec[..., None] * s + kv_n, s

    s_fin, s_in = lax.scan(step, s0, (jnp.moveaxis(decay, 2, 0), jnp.moveaxis(kv, 2, 0)))
    s_in = jnp.moveaxis(s_in, 0, 2)
    o = jnp.einsum('bhncs,bhnse->bhnce', att, v) + jnp.einsum('bhncd,bhnde->bhnce', q_in, s_in)
    return o.reshape(b, h, n, dv), s_fin


def _gla_branch(cols, cols_c, a2, a_bias, norm_g, with_ctx):
    def prep(cs):
        q, k, v, g, r_f, r_b = (t.astype(jnp.float32) for t in cs)
        la_f = jax.nn.log_sigmoid(r_f @ a2[0] + a_bias[0]) / GLA_NORMALIZER
        la_b = jax.nn.log_sigmoid(r_b @ a2[1] + a_bias[1]) / GLA_NORMALIZER
        qkv = (_heads(q) * HEAD_DIM ** -0.5, _heads(k), _heads(v))
        return qkv + (_heads(la_f),), qkv + (_heads(la_b),), g

    lat_f, lat_b, g = prep(cols)
    ctx_f, ctx_b, g_c = prep(cols_c)
    s0 = jnp.zeros((g.shape[0], N_HEADS, HEAD_DIM, HEAD_DIM), jnp.float32)
    o, o_c = _bidirectional(_gla_chunked, s0, lat_f, ctx_f, lat_b, ctx_b)
    out = lambda oo, gg: _head_norm(_merge_heads(oo), norm_g) * jax.nn.silu(gg)
    return out(o, g), (out(o_c, g_c) if with_ctx else None)


def _mlstm_chunked(q, k, v, i_pre, log_f, state):
    b, h, n, dk = q.shape
    dv = v.shape[-1]
    nc = n // CHUNK
    q, k = (t.reshape(b, h, nc, CHUNK, dk) for t in (q, k))
    v = v.reshape(b, h, nc, CHUNK, dv)
    i_pre, log_f = (t.reshape(b, h, nc, CHUNK) for t in (i_pre, log_f))
    tril = jnp.tril(jnp.ones((CHUNK, CHUNK), dtype=bool))
    f_cum = jnp.cumsum(log_f, axis=-1)
    f_end = f_cum[..., -1]
    log_w_end = f_end[..., None] - f_cum + i_pre
    a_end = jnp.max(log_w_end, axis=-1)
    w_end = jnp.exp(log_w_end - a_end[..., None])
    kv = jnp.einsum('bhns,bhnsd,bhnse->bhnde', w_end, k, v)
    k_sum = jnp.einsum('bhns,bhnsd->bhnd', w_end, k)

    def step(carry, xs):
        c_mat, n_vec, m = carry
        fe, ae, kv_n, ks_n = xs
        m_new = jnp.maximum(fe + m, ae)
        old = jnp.exp(fe + m - m_new)
        new = jnp.exp(ae - m_new)
        c_new = old[..., None, None] * c_mat + new[..., None, None] * kv_n
        n_new = old[..., None] * n_vec + new[..., None] * ks_n
        return (c_new, n_new, m_new), carry

    xs = tuple(jnp.moveaxis(t, 2, 0) for t in (f_end, a_end, kv, k_sum))
    final, (c_in, n_in, m_in) = lax.scan(step, state, xs)
    c_in, n_in, m_in = (jnp.moveaxis(t, 0, 2) for t in (c_in, n_in, m_in))
    log_d = jnp.where(tril, f_cum[..., :, None] - f_cum[..., None, :] + i_pre[..., None, :], -jnp.inf)
    g = f_cum + m_in[..., None]
    m = jnp.maximum(g, jnp.max(log_d, axis=-1))
    s = jnp.einsum('bhncd,bhnsd->bhncs', q, k) * jnp.exp(log_d - m[..., None])
    e = jnp.exp(g - m)
    num = jnp.einsum('bhncs,bhnse->bhnce', s, v) + e[..., None] * jnp.einsum('bhncd,bhnde->bhnce', q, c_in)
    den = jnp.sum(s, axis=-1) + e * jnp.einsum('bhncd,bhnd->bhnc', q, n_in)
    hid = num / jnp.maximum(jnp.abs(den), jnp.exp(-m))[..., None]
    return hid.reshape(b, h, n, dv), final


def _mlstm_branch(cols, cols_c, gate_b, norm_g, with_ctx):
    def prep(cs):
        q, k, v, o, g_f, g_b = (t.astype(jnp.float32) for t in cs)
        qkv = (_heads(q), _heads(k) * HEAD_DIM ** -0.5, _heads(v))

        def gates(gp, bias):
            i_p, f_p = jnp.split(gp, 2, axis=-1)
            return (_gate_heads(i_p + bias[0]), _gate_heads(jax.nn.log_sigmoid(f_p + bias[1])))

        return qkv + gates(g_f, gate_b[0]), qkv + gates(g_b, gate_b[1]), o

    lat_f, lat_b, o_g = prep(cols)
    ctx_f, ctx_b, o_gc = prep(cols_c)
    bsz = o_g.shape[0]
    s0 = (jnp.zeros((bsz, N_HEADS, HEAD_DIM, HEAD_DIM), jnp.float32),
          jnp.zeros((bsz, N_HEADS, HEAD_DIM), jnp.float32),
          jnp.zeros((bsz, N_HEADS), jnp.float32))
    o, o_c = _bidirectional(_mlstm_chunked, s0, lat_f, ctx_f, lat_b, ctx_b)
    out = lambda oo, gg: _head_norm(_merge_heads(oo), norm_g) * jax.nn.sigmoid(gg)
    return out(o, o_g), (out(o_c, o_gc) if with_ctx else None)


def _gdn_chunked(q, k, v, beta, g, s0):
    b, h, n, dk = q.shape
    dv = v.shape[-1]
    nc = n // CHUNK
    q, k = (t.reshape(b, h, nc, CHUNK, dk) for t in (q, k))
    v = v.reshape(b, h, nc, CHUNK, dv)
    beta, g = (t.reshape(b, h, nc, CHUNK) for t in (beta, g))
    tril = jnp.tril(jnp.ones((CHUNK, CHUNK), dtype=bool))
    strict = jnp.tril(jnp.ones((CHUNK, CHUNK), dtype=bool), -1)
    cum = jnp.cumsum(g, axis=-1)
    cum_end = cum[..., -1]
    gam = jnp.exp(jnp.where(tril, cum[..., :, None] - cum[..., None, :], -jnp.inf))
    k_beta = k * beta[..., None]
    v_beta = v * beta[..., None]
    a_mat = jnp.where(strict, jnp.einsum('bhncd,bhnsd->bhncs', k_beta, k) * gam, 0.0)
    a_mat = a_mat + jnp.eye(CHUNK, dtype=a_mat.dtype)
    rhs = jnp.concatenate([k_beta * jnp.exp(cum)[..., None], v_beta], axis=-1)
    sol = lax.linalg.triangular_solve(a_mat, rhs, left_side=True, lower=True, unit_diagonal=True)
    w, u = sol[..., :dk], sol[..., dk:]
    attn = jnp.einsum('bhncd,bhnsd->bhncs', q, k) * gam
    q_dec = q * jnp.exp(cum)[..., None]
    k_end = k * jnp.exp(cum_end[..., None] - cum)[..., None]
    dec_end = jnp.exp(cum_end)

    def step(s, xs):
        at_n, w_n, u_n, qd_n, ke_n, de_n = xs
        v_new = u_n - jnp.einsum('bhcd,bhde->bhce', w_n, s)
        o_n = jnp.einsum('bhcd,bhde->bhce', qd_n, s) + jnp.einsum('bhcs,bhse->bhce', at_n, v_new)
        s = de_n[..., None, None] * s + jnp.einsum('bhsd,bhse->bhde', ke_n, v_new)
        return s, o_n

    xs = tuple(jnp.moveaxis(t, 2, 0) for t in (attn, w, u, q_dec, k_end, dec_end))
    s_fin, o = lax.scan(step, s0, xs)
    return jnp.moveaxis(o, 0, 2).reshape(b, h, n, dv), s_fin


def _gdn_branch(cols, cols_c, conv_w, a_log, dt_bias, norm_g, with_ctx):
    def prep(cs):
        q, k, v, og, g_f, g_b = (t.astype(jnp.float32) for t in cs)
        qkv = jax.nn.silu(_conv_centred(jnp.concatenate([q, k, v], axis=-1), conv_w))
        q, k, v = jnp.split(qkv, 3, axis=-1)
        qkv = (_l2(_heads(q)) * HEAD_DIM ** -0.5, _l2(_heads(k)), _heads(v))

        def gates(gp, d):
            b_p, a_p = jnp.split(gp, 2, axis=-1)
            decay = -jnp.exp(a_log[d]) * jax.nn.softplus(a_p + dt_bias[d])
            return (_gate_heads(jax.nn.sigmoid(b_p)), _gate_heads(decay))

        return qkv + gates(g_f, 0), qkv + gates(g_b, 1), og

    lat_f, lat_b, og = prep(cols)
    ctx_f, ctx_b, og_c = prep(cols_c)
    s0 = jnp.zeros((og.shape[0], N_HEADS, HEAD_DIM, HEAD_DIM), jnp.float32)
    o, o_c = _bidirectional(_gdn_chunked, s0, lat_f, ctx_f, lat_b, ctx_b)
    out = lambda oo, gg: _head_norm(_merge_heads(oo), norm_g) * jax.nn.silu(gg)
    return out(o, og), (out(o_c, og_c) if with_ctx else None)


def _diff_branch(cols, cols_c, cos, sin, lam_p, norm_g, lam_init, with_ctx):
    halves = lambda t: t.reshape(*t.shape[:-1], 2, DIFF_QK)
    q, k, v = (_heads(t.astype(jnp.float32)) for t in cols)
    qc, kc, vc = (_heads(t.astype(jnp.float32)) for t in cols_c)
    q = _rope(halves(q), cos[:, None], sin[:, None])
    k = _rope(halves(k), cos[:, None], sin[:, None])
    qc, kc = halves(qc), halves(kc)
    lp = lam_p.astype(jnp.float32)
    lam = jnp.exp(jnp.sum(lp[0] * lp[1], -1)) - jnp.exp(jnp.sum(lp[2] * lp[3], -1)) + lam_init
    lam = lam[None, :, None, None]
    scale = DIFF_QK ** -0.5

    def attend(qb, keys, vals):
        s = jnp.einsum('bhqid,bhkid->ibhqk', qb, keys) * scale
        p = jax.nn.softmax(s, axis=-1)
        return jnp.einsum('bhqk,bhkd->bhqd', p[0] - lam * p[1], vals)

    k_all = jnp.concatenate([kc, k], axis=2)
    v_all = jnp.concatenate([vc, v], axis=2)
    b, h, n = q.shape[:3]
    qb = jnp.moveaxis(q.reshape(b, h, n // ATTN_BLOCK, ATTN_BLOCK, 2, DIFF_QK), 2, 0)
    o = lax.map(lambda blk: attend(blk, k_all, v_all), qb)
    o = jnp.moveaxis(o, 0, 2).reshape(b, h, n, HEAD_DIM)
    out = lambda oo: _head_norm(_merge_heads(oo), norm_g) * (1.0 - lam_init)
    return out(o), (out(attend(qc, kc, vc)) if with_ctx else None)


def _merge(branches, gate_cols, w_branch, w_out, dtype):
    gates = jnp.split(jax.nn.sigmoid(gate_cols.astype(jnp.float32)), N_BRANCH, axis=-1)
    terms = [gt * (yb.astype(dtype) @ w_branch[i]) for i, (gt, yb) in enumerate(zip(gates, branches))]
    acc = terms[0]
    for t in terms[1:]:
        acc = acc + t
    return acc.astype(dtype) @ w_out


def _moe(h, router_w, router_b, w_gate, w_up, w_down):
    lead = h.shape[:-1]
    scores = jax.nn.sigmoid((h @ router_w).astype(jnp.float32))
    sel = scores + router_b.astype(jnp.float32)
    grp = sel.reshape(*lead, N_GROUPS, EXPERTS_PER_GROUP)
    grp_score = jnp.sum(lax.top_k(grp, 2)[0], axis=-1)
    _, g_idx = lax.top_k(grp_score, 1)
    e_mask = jnp.repeat(jax.nn.one_hot(g_idx[..., 0], N_GROUPS, dtype=jnp.bool_), EXPERTS_PER_GROUP, axis=-1)
    _, e_idx = lax.top_k(jnp.where(e_mask, sel, -jnp.inf), TOP_K)
    w_sel = jnp.take_along_axis(scores, e_idx, axis=-1)
    w_sel = w_sel / jnp.sum(w_sel, axis=-1, keepdims=True)
    gates = jnp.sum(jax.nn.one_hot(e_idx, N_EXPERTS, dtype=jnp.float32) * w_sel[..., None], axis=-2)
    gates = gates.astype(h.dtype)
    out = gates[..., 0:1] * ((jax.nn.silu(h @ w_gate[0]) * (h @ w_up[0])) @ w_down[0])
    for e in range(1, N_EXPERTS):
        out = out + gates[..., e:e + 1] * ((jax.nn.silu(h @ w_gate[e]) * (h @ w_up[e])) @ w_down[e])
    return out


def setup_inputs(seed: int = 0) -> dict:
    key = jax.random.key(seed)
    ks = jax.random.split(key, 26)
    f32 = jnp.float32
    nrm = lambda kk, shape, s: jax.random.normal(kk, shape, f32) * s
    d = D_MODEL
    x = nrm(ks[0], (BATCH, SEQ, d), 1.0)
    c = nrm(ks[1], (BATCH, d), 1.0)
    ctx = nrm(ks[2], (BATCH, CTX_LEN, d), 1.0)
    c_ctx = nrm(ks[3], (d,), 1.0)
    w_mod = nrm(ks[4], (DEPTH, d, 6 * d), 0.5 * d ** -0.5)
    b_mod = nrm(ks[5], (DEPTH, 6 * d), 0.02)
    norm_mix = 1.0 + nrm(ks[6], (DEPTH, d), 0.02)
    norm_ffn = 1.0 + nrm(ks[7], (DEPTH, d), 0.02)
    w_in = nrm(ks[8], (DEPTH, d, IN_COLS), d ** -0.5)
    gla_a2 = nrm(ks[9], (DEPTH, 2, GLA_RANK, BRANCH_W), GLA_RANK ** -0.5)
    gla_ab = nrm(ks[10], (DEPTH, 2, BRANCH_W), 0.1)
    i_bias = nrm(ks[11], (DEPTH, 2, 1, N_HEADS), 0.1)
    f_bias = jnp.linspace(3.0, 6.0, N_HEADS, dtype=f32) + nrm(ks[12], (DEPTH, 2, 1, N_HEADS), 0.1)
    mlstm_gate_b = jnp.concatenate([i_bias, f_bias], axis=2)
    gdn_conv = nrm(ks[13], (DEPTH, CONV_W, 3 * BRANCH_W), CONV_W ** -0.5)
    gdn_a_log = jnp.log(jax.random.uniform(ks[14], (DEPTH, 2, N_HEADS), f32, 1.0, 16.0))
    dt = jnp.exp(jax.random.uniform(ks[15], (DEPTH, 2, N_HEADS), f32, math.log(1e-3), math.log(1e-1)))
    gdn_dt_bias = dt + jnp.log(-jnp.expm1(-dt))
    diff_lambda = nrm(ks[16], (DEPTH, 4, N_HEADS, DIFF_QK), 0.1)
    head_norm = 1.0 + nrm(ks[17], (DEPTH, N_BRANCH, BRANCH_W), 0.02)
    w_branch = nrm(ks[18], (DEPTH, N_BRANCH, BRANCH_W, d), BRANCH_W ** -0.5)
    w_out = nrm(ks[19], (DEPTH, d, d), d ** -0.5)
    router_w = nrm(ks[20], (d, N_EXPERTS), d ** -0.5)
    router_b = nrm(ks[21], (N_EXPERTS,), 0.01)
    w_gate = nrm(ks[22], (DEPTH, N_EXPERTS, d, D_EXPERT), d ** -0.5)
    w_up = nrm(ks[23], (DEPTH, N_EXPERTS, d, D_EXPERT), d ** -0.5)
    w_down = nrm(ks[24], (DEPTH, N_EXPERTS, D_EXPERT, d), D_EXPERT ** -0.5)
    norm_final = 1.0 + nrm(ks[25], (d,), 0.02)
    return {'x': x, 'c': c, 'ctx': ctx, 'c_ctx': c_ctx, 'w_mod': w_mod, 'b_mod': b_mod,
            'norm_mix': norm_mix, 'norm_ffn': norm_ffn, 'w_in': w_in, 'gla_a2': gla_a2, 'gla_ab': gla_ab,
            'mlstm_gate_b': mlstm_gate_b, 'gdn_conv': gdn_conv, 'gdn_a_log': gdn_a_log,
            'gdn_dt_bias': gdn_dt_bias, 'diff_lambda': diff_lambda, 'head_norm': head_norm,
            'w_branch': w_branch, 'w_out': w_out, 'router_w': router_w, 'router_b': router_b,
            'w_gate': w_gate, 'w_up': w_up, 'w_down': w_down, 'norm_final': norm_final}


def reference(x, c, ctx, c_ctx, w_mod, b_mod, norm_mix, norm_ffn, w_in, gla_a2, gla_ab, mlstm_gate_b,
              gdn_conv, gdn_a_log, gdn_dt_bias, diff_lambda, head_norm, w_branch, w_out, router_w,
              router_b, w_gate, w_up, w_down, norm_final):
    n_tok = x.shape[1]
    n_rows = n_tok // GRID_W
    row = jnp.repeat(jnp.arange(n_rows), GRID_W)
    col = jnp.tile(jnp.arange(GRID_W), n_rows)
    cos, sin = _axial_rope_tables(row, col)
    xc = ctx
    for l in range(DEPTH):
        with_ctx = l < DEPTH - 1
        lam_init = 0.8 - 0.6 * math.exp(-0.3 * l)
        mod = jax.nn.silu(c) @ w_mod[l] + b_mod[l]
        mod_c = jax.nn.silu(c_ctx) @ w_mod[l] + b_mod[l]
        sh1, sc1, g1, sh2, sc2, g2 = jnp.split(mod[:, None, :], 6, axis=-1)
        shc1, scc1, gc1, shc2, scc2, gc2 = jnp.split(mod_c[None, None, :], 6, axis=-1)
        h = _rms(x, norm_mix[l]) * (1 + sc1) + sh1
        hc = _rms(xc, norm_mix[l]) * (1 + scc1) + shc1
        cl = _split_cols(h @ w_in[l])
        cc = _split_cols(hc @ w_in[l])
        ya, ya_c = _gla_branch(cl[0:6], cc[0:6], gla_a2[l], gla_ab[l], head_norm[l, 0], with_ctx)
        yb, yb_c = _mlstm_branch(cl[6:12], cc[6:12], mlstm_gate_b[l], head_norm[l, 1], with_ctx)
        yg, yg_c = _gdn_branch(cl[12:18], cc[12:18], gdn_conv[l], gdn_a_log[l], gdn_dt_bias[l],
                               head_norm[l, 2], with_ctx)
        yd, yd_c = _diff_branch(cl[18:21], cc[18:21], cos, sin, diff_lambda[l], head_norm[l, 3],
                                lam_init, with_ctx)
        x = x + g1 * _merge([ya, yb, yg, yd], cl[21], w_branch[l], w_out[l], x.dtype)
        if with_ctx:
            xc = xc + gc1 * _merge([ya_c, yb_c, yg_c, yd_c], cc[21], w_branch[l], w_out[l], xc.dtype)
        h = _rms(x, norm_ffn[l]) * (1 + sc2) + sh2
        x = x + g2 * _moe(h, router_w, router_b, w_gate[l], w_up[l], w_down[l])
        if with_ctx:
            hc = _rms(xc, norm_ffn[l]) * (1 + scc2) + shc2
            xc = xc + gc2 * _moe(hc, router_w, router_b, w_gate[l], w_up[l], w_down[l])
    return _rms(x, norm_final)
```

```python
import functools
import math

import numpy as np
import jax
import jax.numpy as jnp
from jax import lax
from jax.experimental import pallas as pl
from jax.experimental.pallas import tpu as pltpu

N_HEADS = 4
HEAD_DIM = 64
BRANCH_W = N_HEADS * HEAD_DIM
CHUNK = 64
GLA_RANK = 16
GLA_NORMALIZER = 16.0
CONV_W = 5
DIFF_QK = HEAD_DIM // 2
ROPE_AXIS = DIFF_QK // 2
ROPE_BASE = 10000.0
GRID_W = 64
N_EXPERTS = 16
N_GROUPS = 4
EXPERTS_PER_GROUP = 4
EPS = 1e-6
N_BRANCH = 4

ROW_TILE = 256
PROJ_COLS = 8192
BLK_GDN_QKV, BLK_DIFF_QKV, BLK_GDN_G = 0, 3, 6
BLK_GLA_Q, BLK_GLA_K, BLK_GLA_V, BLK_GLA_G = 7, 8, 9, 10
BLK_ML_Q, BLK_ML_K, BLK_ML_V, BLK_ML_G = 11, 28, 29, 30
BLK_GATES = 12
SMALL_BLOCK = 31
SM_GLA_RF, SM_GLA_RB, SM_ML_F, SM_ML_B, SM_GD_F, SM_GD_B = 0, 16, 32, 40, 48, 56
NEG_BIG = -1e30
VMEM_LIMIT = 56 * 1024 * 1024

F32 = jnp.float32
BF16 = jnp.bfloat16


def _cparams(*sem):
    return pltpu.CompilerParams(dimension_semantics=sem, vmem_limit_bytes=VMEM_LIMIT)


def _dot(a, b):
    return jnp.dot(a, b, preferred_element_type=F32)


def _dot_nt(a, b):
    return lax.dot_general(a, b, (((1,), (1,)), ((), ())), preferred_element_type=F32)


def _dot_tn(a, b):
    return lax.dot_general(a, b, (((0,), (0,)), ((), ())), preferred_element_type=F32)


def _split_hl(x):
    hi = x.astype(BF16)
    lo = (x - hi.astype(F32)).astype(BF16)
    return hi, lo


def _dot_hl(a, x):
    hi, lo = _split_hl(x)
    return _dot(a, hi) + _dot(a, lo)


def _dot_hl_r(x, a):
    hi, lo = _split_hl(x)
    return _dot(hi, a) + _dot(lo, a)


def _iota(shape, dim):
    return lax.broadcasted_iota(jnp.int32, shape, dim)


def _softplus(x):
    return jnp.maximum(x, 0.0) + jnp.log(1.0 + jnp.exp(-jnp.abs(x)))


def _log_sigmoid(x):
    return -_softplus(-x)


def _silu(x):
    return x * jax.nn.sigmoid(x)


def _lane_head(shape):
    return _iota(shape, len(shape) - 1) // HEAD_DIM


def _stack4(x):
    lh = _lane_head(x.shape)
    zero = jnp.zeros_like(x)
    return jnp.concatenate([jnp.where(lh == h, x, zero) for h in range(N_HEADS)], axis=0)


def _blk_mask():
    r = _iota((BRANCH_W, BRANCH_W), 0) // HEAD_DIM
    c = _iota((BRANCH_W, BRANCH_W), 1) // HEAD_DIM
    return r == c


def _ones_blk():
    return jnp.where(_blk_mask(), 1.0, 0.0).astype(BF16)


def _tri_hs(reverse, strict):
    t = _iota((CHUNK, BRANCH_W), 0)
    s = _iota((CHUNK, BRANCH_W), 1) % CHUNK
    if reverse:
        return (s > t) if strict else (s >= t)
    return (s < t) if strict else (s <= t)


def _eye_hs():
    t = _iota((CHUNK, BRANCH_W), 0)
    s = _iota((CHUNK, BRANCH_W), 1) % CHUNK
    return t == s


def _cum_mat(reverse):
    t = _iota((CHUNK, CHUNK), 0)
    s = _iota((CHUNK, CHUNK), 1)
    m = (s >= t) if reverse else (s <= t)
    return jnp.where(m, 1.0, 0.0).astype(BF16)


def _row_of(col_rep):
    return jnp.sum(jnp.where(_eye_hs(), col_rep, 0.0), axis=0, keepdims=True)


def _seg_max(x):
    lane = _iota(x.shape, 1)
    n = x.shape[1]
    for sh in (1, 2, 4, 8, 16, 32):
        up = pltpu.roll(x, n - sh, 1)
        dn = pltpu.roll(x, sh, 1)
        x = jnp.maximum(x, jnp.where((lane & sh) == 0, up, dn))
    return x


def _seg_sum(x, ones_blk):
    return _dot_hl_r(x, ones_blk)


def _mod_kernel(c_ref, w_ref, b_ref, o_ref):
    a = _silu(c_ref[...]).astype(BF16)
    o_ref[...] = _dot(a, w_ref[...].astype(BF16)) + b_ref[...]


def _modulation(cond, w_mod, b_mod):
    depth, d, n = w_mod.shape
    r = cond.shape[0]
    tn = 1536
    return pl.pallas_call(
        _mod_kernel,
        grid=(depth, n // tn),
        in_specs=[pl.BlockSpec((r, d), lambda l, j: (0, 0)),
                  pl.BlockSpec((None, d, tn), lambda l, j: (l, 0, j)),
                  pl.BlockSpec((None, 1, tn), lambda l, j: (l, 0, j))],
        out_specs=pl.BlockSpec((None, r, tn), lambda l, j: (l, 0, j)),
        out_shape=jax.ShapeDtypeStruct((depth, r, n), F32),
        compiler_params=_cparams("parallel", "parallel"),
        name="modulation",
    )(cond, w_mod, b_mod.reshape(depth, 1, n))


def _rms_mod(x, g, shift, scale):
    y = x * lax.rsqrt(jnp.mean(x * x, axis=-1, keepdims=True) + EPS)
    return (y * g) * (1.0 + scale) + shift


def _inproj_kernel(x_ref, mod_ref, g_ref, w_ref, o_ref, *, d, n_chunk):
    mod = mod_ref[...]
    h = _rms_mod(x_ref[...], g_ref[...], mod[:, 0:d], mod[:, d:2 * d]).astype(BF16)
    for n0 in range(0, PROJ_COLS, n_chunk):
        o_ref[:, n0:n0 + n_chunk] = _dot(h, w_ref[:, n0:n0 + n_chunk]).astype(BF16)


def _mod_row_map(n_ctx_tiles, n_batch):
    return lambda b, i: (jnp.where(i < n_ctx_tiles, n_batch, b), 0, 0)


def _inproj(xa, mods3, g, w_perm, n_ctx_tiles):
    bsz, t, d = xa.shape
    return pl.pallas_call(
        functools.partial(_inproj_kernel, d=d, n_chunk=1024),
        grid=(bsz, t // ROW_TILE),
        in_specs=[pl.BlockSpec((None, ROW_TILE, d), lambda b, i: (b, i, 0)),
                  pl.BlockSpec((None, 1, 6 * d), _mod_row_map(n_ctx_tiles, bsz)),
                  pl.BlockSpec((1, d), lambda b, i: (0, 0)),
                  pl.BlockSpec((d, PROJ_COLS), lambda b, i: (0, 0), pipeline_mode=pl.Buffered(1))],
        out_specs=pl.BlockSpec((None, ROW_TILE, PROJ_COLS), lambda b, i: (b, i, 0)),
        out_shape=jax.ShapeDtypeStruct((bsz, t, PROJ_COLS), BF16),
        compiler_params=_cparams("parallel", "parallel"),
        name="inproj",
    )(xa, mods3, g.reshape(1, d), w_perm)


def _proj_perm():
    sizes = (256, 256, 256, 256, 16, 16, 256, 256, 256, 256, 8, 8, 256, 256, 256, 256, 8, 8, 256, 256, 256, 4096)
    off = np.concatenate([[0], np.cumsum(sizes)])
    seg = lambda i: np.arange(off[i], off[i + 1])
    wide = {BLK_GDN_QKV: 12, BLK_GDN_QKV + 1: 13, BLK_GDN_QKV + 2: 14, BLK_GDN_G: 15,
            BLK_DIFF_QKV: 18, BLK_DIFF_QKV + 1: 19, BLK_DIFF_QKV + 2: 20,
            BLK_GLA_Q: 0, BLK_GLA_K: 1, BLK_GLA_V: 2, BLK_GLA_G: 3,
            BLK_ML_Q: 6, BLK_ML_K: 7, BLK_ML_V: 8, BLK_ML_G: 9}
    perm = np.full((PROJ_COLS,), -1, np.int64)
    for j, i in wide.items():
        perm[j * BRANCH_W:(j + 1) * BRANCH_W] = seg(i)
    perm[BLK_GATES * BRANCH_W:BLK_GATES * BRANCH_W + 4096] = seg(21)
    base = SMALL_BLOCK * BRANCH_W
    for lane0, i in ((SM_GLA_RF, 4), (SM_GLA_RB, 5), (SM_ML_F, 10), (SM_ML_B, 11), (SM_GD_F, 16), (SM_GD_B, 17)):
        s = seg(i)
        perm[base + lane0:base + lane0 + len(s)] = s
    return perm


def _permute_w_in(w_in):
    perm = _proj_perm()
    w = jnp.take(w_in, jnp.asarray(np.maximum(perm, 0)), axis=-1)
    return jnp.where(jnp.asarray(perm >= 0), w, 0.0).astype(BF16)


SCAN_CHUNKS = ROW_TILE // CHUNK


def _scan_block_map(col, nb, reverse):
    if reverse:
        return lambda b, j: (b, jnp.where(j == 0, 0, nb - j), col)
    return lambda b, j: (b, j, col)


def _chunk_order(reverse):
    return range(SCAN_CHUNKS - 1, -1, -1) if reverse else range(SCAN_CHUNKS)


def _expand_small(small, lane0, count):
    src = _iota((BRANCH_W, BRANCH_W), 0)
    dst_head = _iota((BRANCH_W, BRANCH_W), 1) // HEAD_DIM
    e = jnp.where(src == lane0 + dst_head, 1.0, 0.0).astype(BF16)
    return _dot(small, e)


def _gla_kernel(q_ref, k_ref, v_ref, sm_ref, a2_ref, ab_ref, o_ref, st_ref, *, reverse):
    @pl.when(pl.program_id(1) == 0)
    def _():
        st_ref[...] = jnp.zeros_like(st_ref)

    cum_mat = _cum_mat(reverse)
    tri = _tri_hs(reverse, strict=False)
    blk = _blk_mask()
    a2 = a2_ref[...]
    ab = ab_ref[...]
    for c in _chunk_order(reverse):
        rows = slice(c * CHUNK, (c + 1) * CHUNK)
        q = q_ref[rows, :].astype(F32) * (HEAD_DIM ** -0.5)
        k = k_ref[rows, :].astype(F32)
        v = v_ref[rows, :]
        la = _log_sigmoid(_dot(sm_ref[rows, :], a2) + ab) / GLA_NORMALIZER
        cum = _dot_hl(cum_mat, la)
        tot = jnp.sum(la, axis=0, keepdims=True)
        q_in = (q * jnp.exp(cum)).astype(BF16)
        k_out = (k * jnp.exp(-cum)).astype(BF16)
        k_end = (k * jnp.exp(tot - cum)).astype(BF16)
        att = jnp.where(tri, _dot_nt(q_in, _stack4(k_out)), 0.0).astype(BF16)
        st = st_ref[...]
        o = _dot(att, _stack4(v)) + _dot_nt(q_in, st.astype(BF16))
        o_ref[rows, :] = o.astype(o_ref.dtype)
        kv_t = _dot_tn(v, k_end)
        st_ref[...] = st * jnp.exp(tot) + jnp.where(blk, kv_t, 0.0)


def _gla_scan(proj, a2pad, ab, reverse):
    bsz, t, _ = proj.shape
    nb = t // ROW_TILE
    blk = lambda col: pl.BlockSpec((None, ROW_TILE, BRANCH_W), _scan_block_map(col, nb, reverse))
    return pl.pallas_call(
        functools.partial(_gla_kernel, reverse=reverse),
        grid=(bsz, nb),
        in_specs=[blk(BLK_GLA_Q), blk(BLK_GLA_K), blk(BLK_GLA_V), blk(SMALL_BLOCK),
                  pl.BlockSpec((BRANCH_W, BRANCH_W), lambda b, j: (0, 0)),
                  pl.BlockSpec((1, BRANCH_W), lambda b, j: (0, 0))],
        out_specs=blk(0),
        out_shape=jax.ShapeDtypeStruct((bsz, t, BRANCH_W), BF16),
        scratch_shapes=[pltpu.VMEM((BRANCH_W, BRANCH_W), F32)],
        compiler_params=_cparams("parallel", "arbitrary"),
        name="gla_bwd" if reverse else "gla_fwd",
    )(proj, proj, proj, proj, a2pad, ab)


def _gla_params(a2, ab, d):
    lane0 = SM_GLA_RB if d else SM_GLA_RF
    pad = jnp.zeros((BRANCH_W, BRANCH_W), F32).at[lane0:lane0 + GLA_RANK].set(a2[d])
    return pad.astype(BF16), ab[d].reshape(1, BRANCH_W)


def _gla_branch(proj, a2, ab):
    return tuple(_gla_scan(proj, *_gla_params(a2, ab, d), reverse=bool(d)) for d in (0, 1))


def _mlstm_kernel(q_ref, k_ref, v_ref, sm_ref, bias_ref, o_ref, ct_ref, nm_ref, *, reverse, lane0):
    @pl.when(pl.program_id(1) == 0)
    def _():
        ct_ref[...] = jnp.zeros_like(ct_ref)
        nm_ref[...] = jnp.zeros_like(nm_ref)

    cum_mat = _cum_mat(reverse)
    tri = _tri_hs(reverse, strict=False)
    blk = _blk_mask()
    ones_blk = _ones_blk()
    b_i = bias_ref[0:1, :]
    b_f = bias_ref[1:2, :]
    for c in _chunk_order(reverse):
        rows = slice(c * CHUNK, (c + 1) * CHUNK)
        q = q_ref[rows, :]
        qf = q.astype(F32)
        k = k_ref[rows, :].astype(F32) * (HEAD_DIM ** -0.5)
        v = v_ref[rows, :]
        sm = sm_ref[rows, :]
        i_pre = _expand_small(sm, lane0, N_HEADS) + b_i
        log_f = _log_sigmoid(_expand_small(sm, lane0 + N_HEADS, N_HEADS) + b_f)
        f_cum = _dot_hl(cum_mat, log_f)
        f_tot = jnp.sum(log_f, axis=0, keepdims=True)
        u = i_pre - f_cum
        lw = f_tot + u
        a_end = jnp.max(lw, axis=0, keepdims=True)
        kw = k * jnp.exp(lw - a_end)
        k_sum = jnp.sum(kw, axis=0, keepdims=True)
        kv_t = _dot_tn(v, kw.astype(BF16))

        n_in = nm_ref[0:1, :]
        m_in = nm_ref[1:2, :]
        ct = ct_ref[...]
        log_d = jnp.where(tri, f_cum + _row_of(u), NEG_BIG)
        g = f_cum + m_in
        m_t = jnp.maximum(g, _seg_max(log_d))
        s = (_dot_nt(q, _stack4(k.astype(BF16))) * jnp.exp(log_d - m_t)).astype(BF16)
        e = jnp.exp(g - m_t)
        num = _dot(s, _stack4(v)) + e * _dot_nt(q, ct.astype(BF16))
        den = _dot(s, ones_blk) + e * _seg_sum(qf * n_in, ones_blk)
        o_ref[rows, :] = (num / jnp.maximum(jnp.abs(den), jnp.exp(-m_t))).astype(o_ref.dtype)

        m_new = jnp.maximum(f_tot + m_in, a_end)
        old = jnp.exp(f_tot + m_in - m_new)
        new = jnp.exp(a_end - m_new)
        ct_ref[...] = ct * old + jnp.where(blk, kv_t, 0.0) * new
        nm_ref[0:1, :] = n_in * old + k_sum * new
        nm_ref[1:2, :] = m_new


def _mlstm_scan(proj, bias, reverse):
    bsz, t, _ = proj.shape
    nb = t // ROW_TILE
    blk = lambda col: pl.BlockSpec((None, ROW_TILE, BRANCH_W), _scan_block_map(col, nb, reverse))
    return pl.pallas_call(
        functools.partial(_mlstm_kernel, reverse=reverse, lane0=SM_ML_B if reverse else SM_ML_F),
        grid=(bsz, nb),
        in_specs=[blk(BLK_ML_Q), blk(BLK_ML_K), blk(BLK_ML_V), blk(SMALL_BLOCK),
                  pl.BlockSpec((8, BRANCH_W), lambda b, j: (0, 0))],
        out_specs=blk(0),
        out_shape=jax.ShapeDtypeStruct((bsz, t, BRANCH_W), BF16),
        scratch_shapes=[pltpu.VMEM((BRANCH_W, BRANCH_W), F32), pltpu.VMEM((8, BRANCH_W), F32)],
        compiler_params=_cparams("parallel", "arbitrary"),
        name="mlstm_bwd" if reverse else "mlstm_fwd",
    )(proj, proj, proj, proj, bias)


def _head_rows(vals):
    rows = [jnp.repeat(v.astype(F32), HEAD_DIM) for v in vals]
    rows += [jnp.zeros((BRANCH_W,), F32)] * (8 - len(rows))
    return jnp.stack(rows)


def _mlstm_branch(proj, gate_b):
    return tuple(_mlstm_scan(proj, _head_rows([gate_b[d, 0], gate_b[d, 1]]), reverse=bool(d)) for d in (0, 1))


HALO = 8
QKV_W = 3 * BRANCH_W


def _gdn_prep_kernel(prev_ref, cur_ref, next_ref, w_ref, o_ref, *, n_ctx_tiles, n_tiles):
    i = pl.program_id(1)
    has_prev = jnp.logical_and(i != 0, i != n_ctx_tiles).astype(F32)
    has_next = jnp.logical_and(i != n_ctx_tiles - 1, i != n_tiles - 1).astype(F32)
    padded = jnp.concatenate([prev_ref[...].astype(F32) * has_prev, cur_ref[...].astype(F32),
                              next_ref[...].astype(F32) * has_next], axis=0)
    w = w_ref[...]
    acc = jnp.zeros((ROW_TILE, QKV_W), F32)
    for j in range(CONV_W):
        off = HALO + j - CONV_W // 2
        acc = acc + padded[off:off + ROW_TILE, :] * w[j:j + 1, :]
    y = _silu(acc)
    ones_blk = _ones_blk()
    q = y[:, 0:BRANCH_W]
    k = y[:, BRANCH_W:2 * BRANCH_W]
    q = q * lax.rsqrt(_seg_sum(q * q, ones_blk) + EPS) * (HEAD_DIM ** -0.5)
    k = k * lax.rsqrt(_seg_sum(k * k, ones_blk) + EPS)
    o_ref[:, 0:BRANCH_W] = q.astype(o_ref.dtype)
    o_ref[:, BRANCH_W:2 * BRANCH_W] = k.astype(o_ref.dtype)
    o_ref[:, 2 * BRANCH_W:] = y[:, 2 * BRANCH_W:].astype(o_ref.dtype)


def _gdn_prep(proj, conv_w, n_ctx_tiles):
    bsz, t, _ = proj.shape
    nt = t // ROW_TILE
    per = ROW_TILE // HALO
    last = t // HALO - 1
    qkv_blk = BLK_GDN_QKV * BRANCH_W // QKV_W
    w8 = jnp.zeros((8, QKV_W), F32).at[:CONV_W].set(conv_w)
    return pl.pallas_call(
        functools.partial(_gdn_prep_kernel, n_ctx_tiles=n_ctx_tiles, n_tiles=nt),
        grid=(bsz, nt),
        in_specs=[pl.BlockSpec((None, HALO, QKV_W), lambda b, i: (b, jnp.maximum(i * per - 1, 0), qkv_blk)),
                  pl.BlockSpec((None, ROW_TILE, QKV_W), lambda b, i: (b, i, qkv_blk)),
                  pl.BlockSpec((None, HALO, QKV_W), lambda b, i: (b, jnp.minimum((i + 1) * per, last), qkv_blk)),
                  pl.BlockSpec((8, QKV_W), lambda b, i: (0, 0))],
        out_specs=pl.BlockSpec((None, ROW_TILE, QKV_W), lambda b, i: (b, i, 0)),
        out_shape=jax.ShapeDtypeStruct((bsz, t, QKV_W), BF16),
        compiler_params=_cparams("parallel", "parallel"),
        name="gdn_prep",
    )(proj, proj, proj, w8)


def _gdn_kernel(q_ref, k_ref, v_ref, sm_ref, par_ref, o_ref, s_ref, *, reverse, lane0):
    @pl.when(pl.program_id(1) == 0)
    def _():
        s_ref[...] = jnp.zeros_like(s_ref)

    cum_mat = _cum_mat(reverse)
    tri = _tri_hs(reverse, strict=False)
    tri_strict = _tri_hs(reverse, strict=True)
    blk = _blk_mask()
    eye = jnp.where(_iota((BRANCH_W, BRANCH_W), 0) == _iota((BRANCH_W, BRANCH_W), 1), 1.0, 0.0)
    lh = _lane_head((CHUNK, BRANCH_W))
    a_scale = jnp.exp(par_ref[0:1, :])
    dt_bias = par_ref[1:2, :]

    def unstack(m):
        out = jnp.zeros((CHUNK, BRANCH_W), F32)
        for h in range(N_HEADS):
            out = out + jnp.where(lh == h, m[h * CHUNK:(h + 1) * CHUNK, :], 0.0)
        return out

    for c in _chunk_order(reverse):
        rows = slice(c * CHUNK, (c + 1) * CHUNK)
        q = q_ref[rows, :]
        kb16 = k_ref[rows, :]
        k = kb16.astype(F32)
        v = v_ref[rows, :].astype(F32)
        sm = sm_ref[rows, :]
        beta = jax.nn.sigmoid(_expand_small(sm, lane0, N_HEADS))
        g = -a_scale * _softplus(_expand_small(sm, lane0 + N_HEADS, N_HEADS) + dt_bias)
        cum = _dot_hl(cum_mat, g)
        tot = jnp.sum(g, axis=0, keepdims=True)
        diff = cum - _row_of(cum)
        gam = jnp.where(tri, jnp.exp(jnp.where(tri, diff, 0.0)), 0.0)
        k_beta = k * beta
        k4 = _stack4(kb16)
        a_hs = jnp.where(tri_strict, _dot_nt(k_beta.astype(BF16), k4) * gam, 0.0)
        attn = (_dot_nt(q, k4) * gam).astype(BF16)

        p = -_stack4(a_hs)
        t_inv = eye + p
        for _ in range(5):
            pb = p.astype(BF16)
            p = _dot(pb, pb)
            t_inv = t_inv + _dot(t_inv.astype(BF16), p.astype(BF16))
        rhs = jnp.concatenate([k_beta * jnp.exp(cum), v * beta], axis=1).astype(BF16)
        wu = _dot(t_inv.astype(BF16), jnp.concatenate([rhs] * N_HEADS, axis=0))
        w = unstack(wu[:, :BRANCH_W])
        u = unstack(wu[:, BRANCH_W:])

        s = s_ref[...]
        sb = s.astype(BF16)
        v_new = (u - _dot(w.astype(BF16), sb)).astype(BF16)
        q_dec = (q.astype(F32) * jnp.exp(cum)).astype(BF16)
        k_end = (k * jnp.exp(tot - cum)).astype(BF16)
        o_ref[rows, :] = (_dot(q_dec, sb) + _dot(attn, _stack4(v_new))).astype(o_ref.dtype)
        s_ref[...] = s * jnp.exp(tot) + jnp.where(blk, _dot_tn(k_end, v_new), 0.0)


def _gdn_scan(gqkv, proj, par, reverse):
    bsz, t, _ = proj.shape
    nb = t // ROW_TILE
    blk = lambda col: pl.BlockSpec((None, ROW_TILE, BRANCH_W), _scan_block_map(col, nb, reverse))
    return pl.pallas_call(
        functools.partial(_gdn_kernel, reverse=reverse, lane0=SM_GD_B if reverse else SM_GD_F),
        grid=(bsz, nb),
        in_specs=[blk(0), blk(1), blk(2), blk(SMALL_BLOCK),
                  pl.BlockSpec((8, BRANCH_W), lambda b, j: (0, 0))],
        out_specs=blk(0),
        out_shape=jax.ShapeDtypeStruct((bsz, t, BRANCH_W), BF16),
        scratch_shapes=[pltpu.VMEM((BRANCH_W, BRANCH_W), F32)],
        compiler_params=_cparams("parallel", "arbitrary"),
        name="gdn_bwd" if reverse else "gdn_fwd",
    )(gqkv, gqkv, gqkv, proj, par)


def _gdn_branch(proj, conv_w, a_log, dt_bias, n_ctx_tiles):
    gqkv = _gdn_prep(proj, conv_w, n_ctx_tiles)
    return tuple(_gdn_scan(gqkv, proj, _head_rows([a_log[d], dt_bias[d]]), reverse=bool(d))
                 for d in (0, 1))


KV_TILE = ROW_TILE
VX_W = 128


def _rope_tables(n_lat, n_ctx):
    pos = jnp.arange(n_lat)
    row, col = pos // GRID_W, pos % GRID_W
    inv = ROPE_BASE ** (-jnp.arange(0, ROPE_AXIS, 2, dtype=F32) / ROPE_AXIS)
    ang = jnp.concatenate([row.astype(F32)[:, None] * inv, col.astype(F32)[:, None] * inv], axis=-1)
    cos = jnp.concatenate([jnp.ones((n_ctx, ROPE_AXIS), F32), jnp.cos(ang)], axis=0)
    sin = jnp.concatenate([jnp.zeros((n_ctx, ROPE_AXIS), F32), jnp.sin(ang)], axis=0)
    reps = BRANCH_W // DIFF_QK
    cos_l = jnp.tile(jnp.concatenate([cos, cos], axis=-1), (1, reps))
    sin_l = jnp.tile(jnp.concatenate([-sin, sin], axis=-1), (1, reps))
    return cos_l, sin_l


def _rope_prep_kernel(x_ref, cos_ref, sin_ref, q_ref, kt_ref, vx_ref):
    lane = _iota((ROW_TILE, BRANCH_W), 1)
    first_half = (lane % DIFF_QK) < ROPE_AXIS
    cos = cos_ref[...]
    sin = sin_ref[...]

    def rope(x):
        partner = jnp.where(first_half, pltpu.roll(x, BRANCH_W - ROPE_AXIS, 1), pltpu.roll(x, ROPE_AXIS, 1))
        return x * cos + partner * sin

    q = rope(x_ref[:, 0:BRANCH_W].astype(F32)) * (DIFF_QK ** -0.5)
    k = rope(x_ref[:, BRANCH_W:2 * BRANCH_W].astype(F32))
    q_ref[...] = q.astype(q_ref.dtype)
    kt_ref[...] = jnp.transpose(k).astype(kt_ref.dtype)
    v = x_ref[:, 2 * BRANCH_W:]
    src = _iota((BRANCH_W, VX_W), 0)
    dst = _iota((BRANCH_W, VX_W), 1)
    ones_col = jnp.where(_iota((ROW_TILE, VX_W), 1) == HEAD_DIM, 1.0, 0.0)
    for h in range(N_HEADS):
        sel = jnp.where(jnp.logical_and(src == h * HEAD_DIM + dst, dst < HEAD_DIM), 1.0, 0.0).astype(BF16)
        vx_ref[h] = (_dot(v, sel) + ones_col).astype(vx_ref.dtype)


def _rope_prep(proj, cos_l, sin_l):
    bsz, t, _ = proj.shape
    nt = t // ROW_TILE
    qkv_blk = BLK_DIFF_QKV * BRANCH_W // QKV_W
    tab = pl.BlockSpec((ROW_TILE, BRANCH_W), lambda b, i: (i, 0))
    return pl.pallas_call(
        _rope_prep_kernel,
        grid=(bsz, nt),
        in_specs=[pl.BlockSpec((None, ROW_TILE, QKV_W), lambda b, i: (b, i, qkv_blk)), tab, tab],
        out_specs=[pl.BlockSpec((None, ROW_TILE, BRANCH_W), lambda b, i: (b, i, 0)),
                   pl.BlockSpec((None, None, BRANCH_W, KV_TILE), lambda b, i: (b, i, 0, 0)),
                   pl.BlockSpec((None, None, N_HEADS, KV_TILE, VX_W), lambda b, i: (b, i, 0, 0, 0))],
        out_shape=[jax.ShapeDtypeStruct((bsz, t, BRANCH_W), BF16),
                   jax.ShapeDtypeStruct((bsz, nt, BRANCH_W, KV_TILE), BF16),
                   jax.ShapeDtypeStruct((bsz, nt, N_HEADS, KV_TILE, VX_W), BF16)],
        compiler_params=_cparams("parallel", "parallel"),
        name="rope_prep",
    )(proj, cos_l, sin_l)


def _attn_kernel(q_ref, kt_ref, vx_ref, lp_ref, o_ref, *, q_tile0, n_ctx_tiles, n_kv_tiles, lam_init):
    qi = pl.program_id(1) + q_tile0
    nk = jnp.where(qi < n_ctx_tiles, n_ctx_tiles, n_kv_tiles)
    q = q_ref[...]
    lane = _iota(q.shape, 1)
    zero = jnp.zeros_like(q)

    lp = lp_ref[...]
    grp_src = _iota((VX_W, VX_W), 0) // DIFF_QK
    prod1 = lp[0:1, :] * lp[1:2, :]
    prod2 = lp[2:3, :] * lp[3:4, :]

    out = jnp.zeros((ROW_TILE, BRANCH_W), F32)
    for h in range(N_HEADS):
        pick = jnp.where(grp_src == h, 1.0, 0.0).astype(BF16)
        lam = jnp.exp(_dot_hl_r(prod1, pick)) - jnp.exp(_dot_hl_r(prod2, pick)) + lam_init
        lo = h * HEAD_DIM
        qm = [jnp.where(jnp.logical_and(lane >= lo + i * DIFF_QK, lane < lo + (i + 1) * DIFF_QK), q, zero)
              for i in range(2)]

        def body(j, carry, qm=qm, h=h):
            kt = kt_ref[j]
            vx = vx_ref[j, h]
            new = []
            for i in range(2):
                m, acc = carry[i]
                s = _dot(qm[i], kt)
                m_new = jnp.maximum(m, jnp.max(s, axis=1, keepdims=True))
                p = jnp.exp(s - m_new)
                acc = acc * jnp.exp(m - m_new) + _dot(p.astype(BF16), vx)
                new.append((m_new, acc))
            return tuple(new)

        init = tuple((jnp.full((ROW_TILE, 1), NEG_BIG, F32), jnp.zeros((ROW_TILE, VX_W), F32)) for _ in range(2))
        (_, a0), (_, a1) = lax.fori_loop(0, nk, body, init)
        o_h = a0 / a0[:, HEAD_DIM:HEAD_DIM + 1] - lam * (a1 / a1[:, HEAD_DIM:HEAD_DIM + 1])
        src = _iota((VX_W, BRANCH_W), 0)
        dst = _iota((VX_W, BRANCH_W), 1)
        place = jnp.where(jnp.logical_and(dst == src + lo, src < HEAD_DIM), 1.0, 0.0).astype(BF16)
        out = out + _dot(o_h.astype(BF16), place)
    o_ref[...] = out.astype(o_ref.dtype)


def _diff_attention(q_rot, kt, vx, lam_p, lam_init, n_ctx_tiles, with_ctx):
    bsz, t, _ = q_rot.shape
    nt = t // ROW_TILE
    q_tile0 = 0 if with_ctx else n_ctx_tiles
    lp = jnp.zeros((8, VX_W), F32).at[:4].set(lam_p.reshape(4, N_HEADS * DIFF_QK))
    return pl.pallas_call(
        functools.partial(_attn_kernel, q_tile0=q_tile0, n_ctx_tiles=n_ctx_tiles, n_kv_tiles=nt,
                          lam_init=lam_init),
        grid=(bsz, nt - q_tile0),
        in_specs=[pl.BlockSpec((None, ROW_TILE, BRANCH_W), lambda b, i: (b, i + q_tile0, 0)),
                  pl.BlockSpec((None, nt, BRANCH_W, KV_TILE), lambda b, i: (b, 0, 0, 0)),
                  pl.BlockSpec((None, nt, N_HEADS, KV_TILE, VX_W), lambda b, i: (b, 0, 0, 0, 0)),
                  pl.BlockSpec((8, VX_W), lambda b, i: (0, 0))],
        out_specs=pl.BlockSpec((None, ROW_TILE, BRANCH_W), lambda b, i: (b, i, 0)),
        out_shape=jax.ShapeDtypeStruct((bsz, t - q_tile0 * ROW_TILE, BRANCH_W), BF16),
        compiler_params=_cparams("parallel", "arbitrary"),
        name="diff_attention",
    )(q_rot, kt, vx, lp)


def _diff_branch(proj, tables, lam_p, lam_init, n_ctx_tiles, with_ctx):
    q_rot, kt, vx = _rope_prep(proj, *tables)
    return _diff_attention(q_rot, kt, vx, lam_p, lam_init, n_ctx_tiles, with_ctx)


def _prepare(x, c, ctx, c_ctx, w_mod, b_mod):
    bsz, _, d = x.shape
    ct = ctx.shape[1]
    assert ct % ROW_TILE == 0 and x.shape[1] % ROW_TILE == 0
    xa = jnp.concatenate([ctx, x], axis=1)
    rows = 8 * ((bsz + 1 + 7) // 8)
    cond = jnp.zeros((rows, d), F32).at[:bsz].set(c).at[bsz].set(c_ctx)
    mods = _modulation(cond, w_mod, b_mod)
    return dict(xa=xa, mods=mods, n_ctx_tiles=ct // ROW_TILE, ct=ct)


def _layer_inproj(st, l, norm_mix, w_in):
    mods3 = st["mods"][l][:, None, :]
    return _inproj(st["xa"], mods3, norm_mix[l], _permute_w_in(w_in[l]), st["n_ctx_tiles"])


def _merge_kernel(x_ref, mod_ref, glf_ref, glb_ref, mlf_ref, mlb_ref, gdf_ref, gdb_ref, at_ref,
                  g_gla_ref, g_ml_ref, g_gd_ref, mg0_ref, mg1_ref, mg2_ref, mg3_ref,
                  hn_ref, wb_ref, wo_ref, o_ref, *, d, lam_init):
    ones_blk = _ones_blk()

    def head_norm(o, i):
        ms = _seg_sum(o * o, ones_blk) * (1.0 / HEAD_DIM)
        return o * lax.rsqrt(ms + EPS) * hn_ref[i:i + 1, :]

    f32 = lambda r: r[...].astype(F32)
    ys = [head_norm(f32(glf_ref) + f32(glb_ref), 0) * _silu(f32(g_gla_ref)),
          head_norm(f32(mlf_ref) + f32(mlb_ref), 1) * jax.nn.sigmoid(f32(g_ml_ref)),
          head_norm(f32(gdf_ref) + f32(gdb_ref), 2) * _silu(f32(g_gd_ref)),
          head_norm(f32(at_ref), 3) * (1.0 - lam_init)]
    acc = None
    for i, (y, mg_ref) in enumerate(zip(ys, (mg0_ref, mg1_ref, mg2_ref, mg3_ref))):
        term = jax.nn.sigmoid(f32(mg_ref)) * _dot(y.astype(BF16), wb_ref[i])
        acc = term if acc is None else acc + term
    out = _dot(acc.astype(BF16), wo_ref[...])
    g1 = mod_ref[...][:, 2 * d:3 * d]
    o_ref[...] = x_ref[...] + g1 * out


def _merge(xa, mods3, proj, scans, attn, hn, wb, wo, n_ctx_tiles, with_ctx, lam_init):
    bsz, t, d = xa.shape
    off = 0 if with_ctx else n_ctx_tiles
    nt = t // ROW_TILE - off
    rows = lambda col: pl.BlockSpec((None, ROW_TILE, BRANCH_W), lambda b, i: (b, i + off, col))
    gate = lambda j: pl.BlockSpec((None, ROW_TILE, d), lambda b, i: (b, i + off, BLK_GATES * BRANCH_W // d + j))
    const = lambda shape: pl.BlockSpec(shape, lambda b, i: (0,) * len(shape))
    return pl.pallas_call(
        functools.partial(_merge_kernel, d=d, lam_init=lam_init),
        grid=(bsz, nt),
        in_specs=[pl.BlockSpec((None, ROW_TILE, d), lambda b, i: (b, i + off, 0)),
                  pl.BlockSpec((None, 1, 6 * d), lambda b, i: (jnp.where(i + off < n_ctx_tiles, bsz, b), 0, 0))]
                 + [rows(0)] * 6
                 + [pl.BlockSpec((None, ROW_TILE, BRANCH_W), lambda b, i: (b, i, 0))]
                 + [rows(BLK_GLA_G), rows(BLK_ML_G), rows(BLK_GDN_G)]
                 + [gate(j) for j in range(N_BRANCH)]
                 + [const((8, BRANCH_W)), const((N_BRANCH, BRANCH_W, d)), const((d, d))],
        out_specs=pl.BlockSpec((None, ROW_TILE, d), lambda b, i: (b, i, 0)),
        out_shape=jax.ShapeDtypeStruct((bsz, nt * ROW_TILE, d), F32),
        compiler_params=_cparams("parallel", "parallel"),
        name="merge",
    )(xa, mods3, *scans, attn, proj, proj, proj, proj, proj, proj, proj, hn, wb, wo)


GATE_LANES = 128


def _router_gates_t(hf, rw_ref, rb_ref):
    h_hi, h_lo = _split_hl(hf)
    w_hi, w_lo = _split_hl(rw_ref[...])
    logits = _dot_nt(w_hi, h_hi) + _dot_nt(w_hi, h_lo) + _dot_nt(w_lo, h_hi)
    scores = jax.nn.sigmoid(logits)
    sel = scores + rb_ref[...]
    s = [sel[e:e + 1, :] for e in range(N_EXPERTS)]
    sc = [scores[e:e + 1, :] for e in range(N_EXPERTS)]
    grp = []
    for g in range(N_GROUPS):
        a, b, c, dd = s[4 * g:4 * g + 4]
        grp.append(functools.reduce(jnp.maximum, [a + b, a + c, a + dd, b + c, b + dd, c + dd]))
    gmax = functools.reduce(jnp.maximum, grp)
    chosen, taken = [], None
    for g in range(N_GROUPS):
        hit = grp[g] == gmax
        if taken is not None:
            hit = jnp.logical_and(hit, jnp.logical_not(taken))
        taken = hit if taken is None else jnp.logical_or(taken, hit)
        chosen.append(hit)
    ms = [jnp.where(chosen[e // EXPERTS_PER_GROUP], s[e], NEG_BIG) for e in range(N_EXPERTS)]

    def first_argmax(vals):
        top = functools.reduce(jnp.maximum, vals)
        hits, seen = [], None
        for v in vals:
            hit = v == top
            if seen is not None:
                hit = jnp.logical_and(hit, jnp.logical_not(seen))
            seen = hit if seen is None else jnp.logical_or(seen, hit)
            hits.append(hit)
        return hits

    oh1 = first_argmax(ms)
    oh2 = first_argmax([jnp.where(o, NEG_BIG, v) for o, v in zip(oh1, ms)])
    zero = jnp.zeros_like(sc[0])
    w1 = functools.reduce(jnp.add, [jnp.where(o, v, zero) for o, v in zip(oh1, sc)])
    w2 = functools.reduce(jnp.add, [jnp.where(o, v, zero) for o, v in zip(oh2, sc)])
    tot = w1 + w2
    rows = [jnp.where(o1, w1 / tot, zero) + jnp.where(o2, w2 / tot, zero) for o1, o2 in zip(oh1, oh2)]
    return jnp.concatenate(rows, axis=0)


def _moe_dense_kernel(x_ref, mod_ref, g_ref, rw_ref, rb_ref, wg_ref, wu_ref, wd_ref, o_ref,
                      h_scr, gate_scr, acc_scr, *, d):
    e = pl.program_id(2)

    @pl.when(e == 0)
    def _():
        mod = mod_ref[...]
        hf = _rms_mod(x_ref[...], g_ref[...], mod[:, 3 * d:4 * d], mod[:, 4 * d:5 * d])
        h_scr[...] = hf.astype(BF16)
        gates_t = _router_gates_t(hf, rw_ref, rb_ref)
        eye = jnp.where(_iota((N_EXPERTS, GATE_LANES), 0) == _iota((N_EXPERTS, GATE_LANES), 1), 1.0, 0.0).astype(BF16)
        g_hi, g_lo = _split_hl(gates_t)
        gate_scr[...] = _dot_tn(g_hi, eye) + _dot_tn(g_lo, eye)
        acc_scr[...] = jnp.zeros_like(acc_scr)

    h = h_scr[...]
    pick = jnp.where(_iota((GATE_LANES, GATE_LANES), 0) == e, 1.0, 0.0).astype(BF16)
    ge = _dot_hl_r(gate_scr[...], pick)[:, 0:1]
    act = (_silu(_dot(h, wg_ref[...])) * _dot(h, wu_ref[...])).astype(BF16)
    acc_scr[...] += ge * _dot(act, wd_ref[...])

    @pl.when(e == pl.num_programs(2) - 1)
    def _():
        g2 = mod_ref[...][:, 5 * d:6 * d]
        o_ref[...] = x_ref[...] + g2 * acc_scr[...]


def _moe_dense(xa, mods3, g, rw_t, rb, wg, wu, wd, n_ctx_tiles):
    bsz, t, d = xa.shape
    ne, _, de = wg.shape
    return pl.pallas_call(
        functools.partial(_moe_dense_kernel, d=d),
        grid=(bsz, t // ROW_TILE, ne),
        in_specs=[pl.BlockSpec((None, ROW_TILE, d), lambda b, i, e: (b, i, 0)),
                  pl.BlockSpec((None, 1, 6 * d), lambda b, i, e: (jnp.where(i < n_ctx_tiles, bsz, b), 0, 0)),
                  pl.BlockSpec((1, d), lambda b, i, e: (0, 0)),
                  pl.BlockSpec((ne, d), lambda b, i, e: (0, 0)),
                  pl.BlockSpec((ne, ROW_TILE), lambda b, i, e: (0, 0)),
                  pl.BlockSpec((None, d, de), lambda b, i, e: (e, 0, 0)),
                  pl.BlockSpec((None, d, de), lambda b, i, e: (e, 0, 0)),
                  pl.BlockSpec((None, de, d), lambda b, i, e: (e, 0, 0))],
        out_specs=pl.BlockSpec((None, ROW_TILE, d), lambda b, i, e: (b, i, 0)),
        out_shape=jax.ShapeDtypeStruct((bsz, t, d), F32),
        scratch_shapes=[pltpu.VMEM((ROW_TILE, d), BF16), pltpu.VMEM((ROW_TILE, GATE_LANES), F32),
                        pltpu.VMEM((ROW_TILE, d), F32)],
        compiler_params=_cparams("parallel", "parallel", "arbitrary"),
        name="moe_dense",
    )(xa, mods3, g.reshape(1, d), rw_t, rb, wg, wu, wd)


def _final_norm_kernel(x_ref, g_ref, o_ref):
    x = x_ref[...]
    o_ref[...] = x * lax.rsqrt(jnp.mean(x * x, axis=-1, keepdims=True) + EPS) * g_ref[...]


def _final_norm(x, g):
    bsz, t, d = x.shape
    return pl.pallas_call(
        _final_norm_kernel,
        grid=(bsz, t // ROW_TILE),
        in_specs=[pl.BlockSpec((None, ROW_TILE, d), lambda b, i: (b, i, 0)), pl.BlockSpec((1, d), lambda b, i: (0, 0))],
        out_specs=pl.BlockSpec((None, ROW_TILE, d), lambda b, i: (b, i, 0)),
        out_shape=jax.ShapeDtypeStruct((bsz, t, d), F32),
        compiler_params=_cparams("parallel", "parallel"),
        name="final_norm",
    )(x, g.reshape(1, d))


def kernel(x, c, ctx, c_ctx, w_mod, b_mod, norm_mix, norm_ffn, w_in, gla_a2, gla_ab, mlstm_gate_b, gdn_conv, gdn_a_log, gdn_dt_bias, diff_lambda, head_norm, w_branch, w_out, router_w, router_b, w_gate, w_up, w_down, norm_final):
    depth = w_in.shape[0]
    st = _prepare(x, c, ctx, c_ctx, w_mod, b_mod)
    n_ctx_tiles = st["n_ctx_tiles"]
    tables = _rope_tables(x.shape[1], st["ct"])
    rw_t = router_w.T
    rb = jnp.broadcast_to(router_b.astype(F32)[:, None], (N_EXPERTS, ROW_TILE))
    for l in range(depth):
        with_ctx = l < depth - 1
        lam_init = 0.8 - 0.6 * math.exp(-0.3 * l)
        mods3 = st["mods"][l][:, None, :]
        proj = _layer_inproj(st, l, norm_mix, w_in)
        scans = (*_gla_branch(proj, gla_a2[l], gla_ab[l]),
                 *_mlstm_branch(proj, mlstm_gate_b[l]),
                 *_gdn_branch(proj, gdn_conv[l], gdn_a_log[l], gdn_dt_bias[l], n_ctx_tiles))
        attn = _diff_branch(proj, tables, diff_lambda[l], lam_init, n_ctx_tiles, with_ctx)
        hn = jnp.zeros((8, BRANCH_W), F32).at[:N_BRANCH].set(head_norm[l])
        xa = _merge(st["xa"], mods3, proj, scans, attn, hn, w_branch[l].astype(BF16), w_out[l].astype(BF16),
                    n_ctx_tiles, with_ctx, lam_init)
        if not with_ctx:
            n_ctx_tiles = 0
        xa = _moe_dense(xa, mods3, norm_ffn[l], rw_t, rb, w_gate[l].astype(BF16), w_up[l].astype(BF16),
                        w_down[l].astype(BF16), n_ctx_tiles)
        st = dict(st, xa=xa, n_ctx_tiles=n_ctx_tiles)
    return _final_norm(st["xa"], norm_final)
```

```python
import functools
import math

import numpy as np
import jax
import jax.numpy as jnp
from jax import lax
from jax.experimental import pallas as pl
from jax.experimental.pallas import tpu as pltpu

N_HEADS = 4
HEAD_DIM = 64
BRANCH_W = N_HEADS * HEAD_DIM
CHUNK = 64
GLA_RANK = 16
GLA_NORMALIZER = 16.0
CONV_W = 5
DIFF_QK = HEAD_DIM // 2
ROPE_AXIS = DIFF_QK // 2
ROPE_BASE = 10000.0
GRID_W = 64
N_EXPERTS = 16
N_GROUPS = 4
EXPERTS_PER_GROUP = 4
EPS = 1e-6
N_BRANCH = 4

ROW_TILE = 256
PROJ_COLS = 8192
BLK_GDN_QKV, BLK_DIFF_QKV, BLK_GDN_G = 0, 3, 6
BLK_GLA_Q, BLK_GLA_K, BLK_GLA_V, BLK_GLA_G = 7, 8, 9, 10
BLK_ML_Q, BLK_ML_K, BLK_ML_V, BLK_ML_G = 11, 28, 29, 30
BLK_GATES = 12
SMALL_BLOCK = 31
SM_GLA_RF, SM_GLA_RB, SM_ML_F, SM_ML_B, SM_GD_F, SM_GD_B = 0, 16, 32, 40, 48, 56
NEG_BIG = -1e30
VMEM_LIMIT = 56 * 1024 * 1024

F32 = jnp.float32
BF16 = jnp.bfloat16


def _cparams(*sem):
    return pltpu.CompilerParams(dimension_semantics=sem, vmem_limit_bytes=VMEM_LIMIT)


def _dot(a, b):
    return jnp.dot(a, b, preferred_element_type=F32)


def _dot_nt(a, b):
    return lax.dot_general(a, b, (((1,), (1,)), ((), ())), preferred_element_type=F32)


def _dot_tn(a, b):
    return lax.dot_general(a, b, (((0,), (0,)), ((), ())), preferred_element_type=F32)


def _split_hl(x):
    hi = x.astype(BF16)
    lo = (x - hi.astype(F32)).astype(BF16)
    return hi, lo


def _dot_hl(a, x):
    hi, lo = _split_hl(x)
    return _dot(a, hi) + _dot(a, lo)


def _dot_hl_r(x, a):
    hi, lo = _split_hl(x)
    return _dot(hi, a) + _dot(lo, a)


def _iota(shape, dim):
    return lax.broadcasted_iota(jnp.int32, shape, dim)


def _softplus(x):
    return jnp.maximum(x, 0.0) + jnp.log(1.0 + jnp.exp(-jnp.abs(x)))


def _log_sigmoid(x):
    return -_softplus(-x)


def _silu(x):
    return x * jax.nn.sigmoid(x)


def _lane_head(shape):
    return _iota(shape, len(shape) - 1) // HEAD_DIM


def _stack4(x):
    lh = _lane_head(x.shape)
    zero = jnp.zeros_like(x)
    return jnp.concatenate([jnp.where(lh == h, x, zero) for h in range(N_HEADS)], axis=0)


def _blk_mask():
    r = _iota((BRANCH_W, BRANCH_W), 0) // HEAD_DIM
    c = _iota((BRANCH_W, BRANCH_W), 1) // HEAD_DIM
    return r == c


def _ones_blk():
    return jnp.where(_blk_mask(), 1.0, 0.0).astype(BF16)


def _tri_hs(reverse, strict):
    t = _iota((CHUNK, BRANCH_W), 0)
    s = _iota((CHUNK, BRANCH_W), 1) % CHUNK
    if reverse:
        return (s > t) if strict else (s >= t)
    return (s < t) if strict else (s <= t)


def _eye_hs():
    t = _iota((CHUNK, BRANCH_W), 0)
    s = _iota((CHUNK, BRANCH_W), 1) % CHUNK
    return t == s


def _cum_mat(reverse):
    t = _iota((CHUNK, CHUNK), 0)
    s = _iota((CHUNK, CHUNK), 1)
    m = (s >= t) if reverse else (s <= t)
    return jnp.where(m, 1.0, 0.0).astype(BF16)


def _row_of(col_rep):
    return jnp.sum(jnp.where(_eye_hs(), col_rep, 0.0), axis=0, keepdims=True)


def _seg_max(x):
    lane = _iota(x.shape, 1)
    n = x.shape[1]
    for sh in (1, 2, 4, 8, 16, 32):
        up = pltpu.roll(x, n - sh, 1)
        dn = pltpu.roll(x, sh, 1)
        x = jnp.maximum(x, jnp.where((lane & sh) == 0, up, dn))
    return x


def _seg_sum(x, ones_blk):
    return _dot_hl_r(x, ones_blk)


def _mod_kernel(c_ref, w_ref, b_ref, o_ref):
    a = _silu(c_ref[...]).astype(BF16)
    o_ref[...] = _dot(a, w_ref[...].astype(BF16)) + b_ref[...]


def _modulation(cond, w_mod, b_mod):
    depth, d, n = w_mod.shape
    r = cond.shape[0]
    tn = 1536
    return pl.pallas_call(
        _mod_kernel,
        grid=(depth, n // tn),
        in_specs=[pl.BlockSpec((r, d), lambda l, j: (0, 0)),
                  pl.BlockSpec((None, d, tn), lambda l, j: (l, 0, j)),
                  pl.BlockSpec((None, 1, tn), lambda l, j: (l, 0, j))],
        out_specs=pl.BlockSpec((None, r, tn), lambda l, j: (l, 0, j)),
        out_shape=jax.ShapeDtypeStruct((depth, r, n), F32),
        compiler_params=_cparams("parallel", "parallel"),
        name="modulation",
    )(cond, w_mod, b_mod.reshape(depth, 1, n))


def _rms_mod(x, g, shift, scale):
    y = x * lax.rsqrt(jnp.mean(x * x, axis=-1, keepdims=True) + EPS)
    return (y * g) * (1.0 + scale) + shift


def _inproj_kernel(x_ref, mod_ref, g_ref, w_ref, o_ref, *, d, n_chunk):
    mod = mod_ref[...]
    h = _rms_mod(x_ref[...], g_ref[...], mod[:, 0:d], mod[:, d:2 * d]).astype(BF16)
    for n0 in range(0, PROJ_COLS, n_chunk):
        o_ref[:, n0:n0 + n_chunk] = _dot(h, w_ref[:, n0:n0 + n_chunk]).astype(BF16)


def _mod_row_map(n_ctx_tiles, n_batch):
    return lambda b, i: (jnp.where(i < n_ctx_tiles, n_batch, b), 0, 0)


def _inproj(xa, mods3, g, w_perm, n_ctx_tiles):
    bsz, t, d = xa.shape
    return pl.pallas_call(
        functools.partial(_inproj_kernel, d=d, n_chunk=1024),
        grid=(bsz, t // ROW_TILE),
        in_specs=[pl.BlockSpec((None, ROW_TILE, d), lambda b, i: (b, i, 0)),
                  pl.BlockSpec((None, 1, 6 * d), _mod_row_map(n_ctx_tiles, bsz)),
                  pl.BlockSpec((1, d), lambda b, i: (0, 0)),
                  pl.BlockSpec((d, PROJ_COLS), lambda b, i: (0, 0), pipeline_mode=pl.Buffered(1))],
        out_specs=pl.BlockSpec((None, ROW_TILE, PROJ_COLS), lambda b, i: (b, i, 0)),
        out_shape=jax.ShapeDtypeStruct((bsz, t, PROJ_COLS), BF16),
        compiler_params=_cparams("parallel", "parallel"),
        name="inproj",
    )(xa, mods3, g.reshape(1, d), w_perm)


def _proj_perm():
    sizes = (256, 256, 256, 256, 16, 16, 256, 256, 256, 256, 8, 8, 256, 256, 256, 256, 8, 8, 256, 256, 256, 4096)
    off = np.concatenate([[0], np.cumsum(sizes)])
    seg = lambda i: np.arange(off[i], off[i + 1])
    wide = {BLK_GDN_QKV: 12, BLK_GDN_QKV + 1: 13, BLK_GDN_QKV + 2: 14, BLK_GDN_G: 15,
            BLK_DIFF_QKV: 18, BLK_DIFF_QKV + 1: 19, BLK_DIFF_QKV + 2: 20,
            BLK_GLA_Q: 0, BLK_GLA_K: 1, BLK_GLA_V: 2, BLK_GLA_G: 3,
            BLK_ML_Q: 6, BLK_ML_K: 7, BLK_ML_V: 8, BLK_ML_G: 9}
    perm = np.full((PROJ_COLS,), -1, np.int64)
    for j, i in wide.items():
        perm[j * BRANCH_W:(j + 1) * BRANCH_W] = seg(i)
    perm[BLK_GATES * BRANCH_W:BLK_GATES * BRANCH_W + 4096] = seg(21)
    base = SMALL_BLOCK * BRANCH_W
    for lane0, i in ((SM_GLA_RF, 4), (SM_GLA_RB, 5), (SM_ML_F, 10), (SM_ML_B, 11), (SM_GD_F, 16), (SM_GD_B, 17)):
        s = seg(i)
        perm[base + lane0:base + lane0 + len(s)] = s
    return perm


def _permute_w_in(w_in):
    perm = _proj_perm()
    w = jnp.take(w_in, jnp.asarray(np.maximum(perm, 0)), axis=-1)
    return jnp.where(jnp.asarray(perm >= 0), w, 0.0).astype(BF16)


SCAN_CHUNKS = ROW_TILE // CHUNK


def _scan_block_map(col, nb, reverse):
    if reverse:
        return lambda b, j: (b, jnp.where(j == 0, 0, nb - j), col)
    return lambda b, j: (b, j, col)


def _chunk_order(reverse):
    return range(SCAN_CHUNKS - 1, -1, -1) if reverse else range(SCAN_CHUNKS)


def _expand_small(small, lane0, count):
    src = _iota((BRANCH_W, BRANCH_W), 0)
    dst_head = _iota((BRANCH_W, BRANCH_W), 1) // HEAD_DIM
    e = jnp.where(src == lane0 + dst_head, 1.0, 0.0).astype(BF16)
    return _dot(small, e)


def _gla_kernel(q_ref, k_ref, v_ref, sm_ref, a2_ref, ab_ref, o_ref, st_ref, *, reverse):
    @pl.when(pl.program_id(1) == 0)
    def _():
        st_ref[...] = jnp.zeros_like(st_ref)

    cum_mat = _cum_mat(reverse)
    tri = _tri_hs(reverse, strict=False)
    blk = _blk_mask()
    a2 = a2_ref[...]
    ab = ab_ref[...]
    for c in _chunk_order(reverse):
        rows = slice(c * CHUNK, (c + 1) * CHUNK)
        q = q_ref[rows, :].astype(F32) * (HEAD_DIM ** -0.5)
        k = k_ref[rows, :].astype(F32)
        v = v_ref[rows, :]
        la = _log_sigmoid(_dot(sm_ref[rows, :], a2) + ab) / GLA_NORMALIZER
        cum = _dot_hl(cum_mat, la)
        tot = jnp.sum(la, axis=0, keepdims=True)
        q_in = (q * jnp.exp(cum)).astype(BF16)
        k_out = (k * jnp.exp(-cum)).astype(BF16)
        k_end = (k * jnp.exp(tot - cum)).astype(BF16)
        att = jnp.where(tri, _dot_nt(q_in, _stack4(k_out)), 0.0).astype(BF16)
        st = st_ref[...]
        o = _dot(att, _stack4(v)) + _dot_nt(q_in, st.astype(BF16))
        o_ref[rows, :] = o.astype(o_ref.dtype)
        kv_t = _dot_tn(v, k_end)
        st_ref[...] = st * jnp.exp(tot) + jnp.where(blk, kv_t, 0.0)


def _gla_scan(proj, a2pad, ab, reverse):
    bsz, t, _ = proj.shape
    nb = t // ROW_TILE
    blk = lambda col: pl.BlockSpec((None, ROW_TILE, BRANCH_W), _scan_block_map(col, nb, reverse))
    return pl.pallas_call(
        functools.partial(_gla_kernel, reverse=reverse),
        grid=(bsz, nb),
        in_specs=[blk(BLK_GLA_Q), blk(BLK_GLA_K), blk(BLK_GLA_V), blk(SMALL_BLOCK),
                  pl.BlockSpec((BRANCH_W, BRANCH_W), lambda b, j: (0, 0)),
                  pl.BlockSpec((1, BRANCH_W), lambda b, j: (0, 0))],
        out_specs=blk(0),
        out_shape=jax.ShapeDtypeStruct((bsz, t, BRANCH_W), BF16),
        scratch_shapes=[pltpu.VMEM((BRANCH_W, BRANCH_W), F32)],
        compiler_params=_cparams("parallel", "arbitrary"),
        name="gla_bwd" if reverse else "gla_fwd",
    )(proj, proj, proj, proj, a2pad, ab)


def _gla_params(a2, ab, d):
    lane0 = SM_GLA_RB if d else SM_GLA_RF
    pad = jnp.zeros((BRANCH_W, BRANCH_W), F32).at[lane0:lane0 + GLA_RANK].set(a2[d])
    return pad.astype(BF16), ab[d].reshape(1, BRANCH_W)


def _gla_branch(proj, a2, ab):
    return tuple(_gla_scan(proj, *_gla_params(a2, ab, d), reverse=bool(d)) for d in (0, 1))


def _mlstm_kernel(q_ref, k_ref, v_ref, sm_ref, bias_ref, o_ref, ct_ref, nm_ref, *, reverse, lane0):
    @pl.when(pl.program_id(1) == 0)
    def _():
        ct_ref[...] = jnp.zeros_like(ct_ref)
        nm_ref[...] = jnp.zeros_like(nm_ref)

    cum_mat = _cum_mat(reverse)
    tri = _tri_hs(reverse, strict=False)
    blk = _blk_mask()
    ones_blk = _ones_blk()
    b_i = bias_ref[0:1, :]
    b_f = bias_ref[1:2, :]
    for c in _chunk_order(reverse):
        rows = slice(c * CHUNK, (c + 1) * CHUNK)
        q = q_ref[rows, :]
        qf = q.astype(F32)
        k = k_ref[rows, :].astype(F32) * (HEAD_DIM ** -0.5)
        v = v_ref[rows, :]
        sm = sm_ref[rows, :]
        i_pre = _expand_small(sm, lane0, N_HEADS) + b_i
        log_f = _log_sigmoid(_expand_small(sm, lane0 + N_HEADS, N_HEADS) + b_f)
        f_cum = _dot_hl(cum_mat, log_f)
        f_tot = jnp.sum(log_f, axis=0, keepdims=True)
        u = i_pre - f_cum
        lw = f_tot + u
        a_end = jnp.max(lw, axis=0, keepdims=True)
        kw = k * jnp.exp(lw - a_end)
        k_sum = jnp.sum(kw, axis=0, keepdims=True)
        kv_t = _dot_tn(v, kw.astype(BF16))

        n_in = nm_ref[0:1, :]
        m_in = nm_ref[1:2, :]
        ct = ct_ref[...]
        log_d = jnp.where(tri, f_cum + _row_of(u), NEG_BIG)
        g = f_cum + m_in
        m_t = jnp.maximum(g, _seg_max(log_d))
        s = (_dot_nt(q, _stack4(k.astype(BF16))) * jnp.exp(log_d - m_t)).astype(BF16)
        e = jnp.exp(g - m_t)
        num = _dot(s, _stack4(v)) + e * _dot_nt(q, ct.astype(BF16))
        den = _dot(s, ones_blk) + e * _seg_sum(qf * n_in, ones_blk)
        o_ref[rows, :] = (num / jnp.maximum(jnp.abs(den), jnp.exp(-m_t))).astype(o_ref.dtype)

        m_new = jnp.maximum(f_tot + m_in, a_end)
        old = jnp.exp(f_tot + m_in - m_new)
        new = jnp.exp(a_end - m_new)
        ct_ref[...] = ct * old + jnp.where(blk, kv_t, 0.0) * new
        nm_ref[0:1, :] = n_in * old + k_sum * new
        nm_ref[1:2, :] = m_new


def _mlstm_scan(proj, bias, reverse):
    bsz, t, _ = proj.shape
    nb = t // ROW_TILE
    blk = lambda col: pl.BlockSpec((None, ROW_TILE, BRANCH_W), _scan_block_map(col, nb, reverse))
    return pl.pallas_call(
        functools.partial(_mlstm_kernel, reverse=reverse, lane0=SM_ML_B if reverse else SM_ML_F),
        grid=(bsz, nb),
        in_specs=[blk(BLK_ML_Q), blk(BLK_ML_K), blk(BLK_ML_V), blk(SMALL_BLOCK),
                  pl.BlockSpec((8, BRANCH_W), lambda b, j: (0, 0))],
        out_specs=blk(0),
        out_shape=jax.ShapeDtypeStruct((bsz, t, BRANCH_W), BF16),
        scratch_shapes=[pltpu.VMEM((BRANCH_W, BRANCH_W), F32), pltpu.VMEM((8, BRANCH_W), F32)],
        compiler_params=_cparams("parallel", "arbitrary"),
        name="mlstm_bwd" if reverse else "mlstm_fwd",
    )(proj, proj, proj, proj, bias)


def _head_rows(vals):
    rows = [jnp.repeat(v.astype(F32), HEAD_DIM) for v in vals]
    rows += [jnp.zeros((BRANCH_W,), F32)] * (8 - len(rows))
    return jnp.stack(rows)


def _mlstm_branch(proj, gate_b):
    return tuple(_mlstm_scan(proj, _head_rows([gate_b[d, 0], gate_b[d, 1]]), reverse=bool(d)) for d in (0, 1))


HALO = 8
QKV_W = 3 * BRANCH_W


def _gdn_prep_kernel(prev_ref, cur_ref, next_ref, w_ref, o_ref, *, n_ctx_tiles, n_tiles):
    i = pl.program_id(1)
    has_prev = jnp.logical_and(i != 0, i != n_ctx_tiles).astype(F32)
    has_next = jnp.logical_and(i != n_ctx_tiles - 1, i != n_tiles - 1).astype(F32)
    padded = jnp.concatenate([prev_ref[...].astype(F32) * has_prev, cur_ref[...].astype(F32),
                              next_ref[...].astype(F32) * has_next], axis=0)
    w = w_ref[...]
    acc = jnp.zeros((ROW_TILE, QKV_W), F32)
    for j in range(CONV_W):
        off = HALO + j - CONV_W // 2
        acc = acc + padded[off:off + ROW_TILE, :] * w[j:j + 1, :]
    y = _silu(acc)
    ones_blk = _ones_blk()
    q = y[:, 0:BRANCH_W]
    k = y[:, BRANCH_W:2 * BRANCH_W]
    q = q * lax.rsqrt(_seg_sum(q * q, ones_blk) + EPS) * (HEAD_DIM ** -0.5)
    k = k * lax.rsqrt(_seg_sum(k * k, ones_blk) + EPS)
    o_ref[:, 0:BRANCH_W] = q.astype(o_ref.dtype)
    o_ref[:, BRANCH_W:2 * BRANCH_W] = k.astype(o_ref.dtype)
    o_ref[:, 2 * BRANCH_W:] = y[:, 2 * BRANCH_W:].astype(o_ref.dtype)


def _gdn_prep(proj, conv_w, n_ctx_tiles):
    bsz, t, _ = proj.shape
    nt = t // ROW_TILE
    per = ROW_TILE // HALO
    last = t // HALO - 1
    qkv_blk = BLK_GDN_QKV * BRANCH_W // QKV_W
    w8 = jnp.zeros((8, QKV_W), F32).at[:CONV_W].set(conv_w)
    return pl.pallas_call(
        functools.partial(_gdn_prep_kernel, n_ctx_tiles=n_ctx_tiles, n_tiles=nt),
        grid=(bsz, nt),
        in_specs=[pl.BlockSpec((None, HALO, QKV_W), lambda b, i: (b, jnp.maximum(i * per - 1, 0), qkv_blk)),
                  pl.BlockSpec((None, ROW_TILE, QKV_W), lambda b, i: (b, i, qkv_blk)),
                  pl.BlockSpec((None, HALO, QKV_W), lambda b, i: (b, jnp.minimum((i + 1) * per, last), qkv_blk)),
                  pl.BlockSpec((8, QKV_W), lambda b, i: (0, 0))],
        out_specs=pl.BlockSpec((None, ROW_TILE, QKV_W), lambda b, i: (b, i, 0)),
        out_shape=jax.ShapeDtypeStruct((bsz, t, QKV_W), BF16),
        compiler_params=_cparams("parallel", "parallel"),
        name="gdn_prep",
    )(proj, proj, proj, w8)


def _gdn_kernel(q_ref, k_ref, v_ref, sm_ref, par_ref, o_ref, s_ref, *, reverse, lane0):
    @pl.when(pl.program_id(1) == 0)
    def _():
        s_ref[...] = jnp.zeros_like(s_ref)

    cum_mat = _cum_mat(reverse)
    tri = _tri_hs(reverse, strict=False)
    tri_strict = _tri_hs(reverse, strict=True)
    blk = _blk_mask()
    eye = jnp.where(_iota((BRANCH_W, BRANCH_W), 0) == _iota((BRANCH_W, BRANCH_W), 1), 1.0, 0.0)
    lh = _lane_head((CHUNK, BRANCH_W))
    a_scale = jnp.exp(par_ref[0:1, :])
    dt_bias = par_ref[1:2, :]

    def unstack(m):
        out = jnp.zeros((CHUNK, BRANCH_W), F32)
        for h in range(N_HEADS):
            out = out + jnp.where(lh == h, m[h * CHUNK:(h + 1) * CHUNK, :], 0.0)
        return out

    for c in _chunk_order(reverse):
        rows = slice(c * CHUNK, (c + 1) * CHUNK)
        q = q_ref[rows, :]
        kb16 = k_ref[rows, :]
        k = kb16.astype(F32)
        v = v_ref[rows, :].astype(F32)
        sm = sm_ref[rows, :]
        beta = jax.nn.sigmoid(_expand_small(sm, lane0, N_HEADS))
        g = -a_scale * _softplus(_expand_small(sm, lane0 + N_HEADS, N_HEADS) + dt_bias)
        cum = _dot_hl(cum_mat, g)
        tot = jnp.sum(g, axis=0, keepdims=True)
        diff = cum - _row_of(cum)
        gam = jnp.where(tri, jnp.exp(jnp.where(tri, diff, 0.0)), 0.0)
        k_beta = k * beta
        k4 = _stack4(kb16)
        a_hs = jnp.where(tri_strict, _dot_nt(k_beta.astype(BF16), k4) * gam, 0.0)
        attn = (_dot_nt(q, k4) * gam).astype(BF16)

        p = -_stack4(a_hs)
        t_inv = eye + p
        for _ in range(5):
            pb = p.astype(BF16)
            p = _dot(pb, pb)
            t_inv = t_inv + _dot(t_inv.astype(BF16), p.astype(BF16))
        rhs = jnp.concatenate([k_beta * jnp.exp(cum), v * beta], axis=1).astype(BF16)
        wu = _dot(t_inv.astype(BF16), jnp.concatenate([rhs] * N_HEADS, axis=0))
        w = unstack(wu[:, :BRANCH_W])
        u = unstack(wu[:, BRANCH_W:])

        s = s_ref[...]
        sb = s.astype(BF16)
        v_new = (u - _dot(w.astype(BF16), sb)).astype(BF16)
        q_dec = (q.astype(F32) * jnp.exp(cum)).astype(BF16)
        k_end = (k * jnp.exp(tot - cum)).astype(BF16)
        o_ref[rows, :] = (_dot(q_dec, sb) + _dot(attn, _stack4(v_new))).astype(o_ref.dtype)
        s_ref[...] = s * jnp.exp(tot) + jnp.where(blk, _dot_tn(k_end, v_new), 0.0)


def _gdn_scan(gqkv, proj, par, reverse):
    bsz, t, _ = proj.shape
    nb = t // ROW_TILE
    blk = lambda col: pl.BlockSpec((None, ROW_TILE, BRANCH_W), _scan_block_map(col, nb, reverse))
    return pl.pallas_call(
        functools.partial(_gdn_kernel, reverse=reverse, lane0=SM_GD_B if reverse else SM_GD_F),
        grid=(bsz, nb),
        in_specs=[blk(0), blk(1), blk(2), blk(SMALL_BLOCK),
                  pl.BlockSpec((8, BRANCH_W), lambda b, j: (0, 0))],
        out_specs=blk(0),
        out_shape=jax.ShapeDtypeStruct((bsz, t, BRANCH_W), BF16),
        scratch_shapes=[pltpu.VMEM((BRANCH_W, BRANCH_W), F32)],
        compiler_params=_cparams("parallel", "arbitrary"),
        name="gdn_bwd" if reverse else "gdn_fwd",
    )(gqkv, gqkv, gqkv, proj, par)


def _gdn_branch(proj, conv_w, a_log, dt_bias, n_ctx_tiles):
    gqkv = _gdn_prep(proj, conv_w, n_ctx_tiles)
    return tuple(_gdn_scan(gqkv, proj, _head_rows([a_log[d], dt_bias[d]]), reverse=bool(d))
                 for d in (0, 1))


VX_W = 128
LOG2E = 1.4426950408889634


def _kv_tile(t):
    return next(k for k in (3 * ROW_TILE, 2 * ROW_TILE, ROW_TILE) if t % k == 0)


def _rope_tables(n_lat, n_ctx):
    pos = jnp.arange(n_lat)
    row, col = pos // GRID_W, pos % GRID_W
    inv = ROPE_BASE ** (-jnp.arange(0, ROPE_AXIS, 2, dtype=F32) / ROPE_AXIS)
    ang = jnp.concatenate([row.astype(F32)[:, None] * inv, col.astype(F32)[:, None] * inv], axis=-1)
    cos = jnp.concatenate([jnp.ones((n_ctx, ROPE_AXIS), F32), jnp.cos(ang)], axis=0)
    sin = jnp.concatenate([jnp.zeros((n_ctx, ROPE_AXIS), F32), jnp.sin(ang)], axis=0)
    reps = BRANCH_W // DIFF_QK
    cos_l = jnp.tile(jnp.concatenate([cos, cos], axis=-1), (1, reps))
    sin_l = jnp.tile(jnp.concatenate([-sin, sin], axis=-1), (1, reps))
    return cos_l, sin_l


def _rope_prep_kernel(x_ref, cos_ref, sin_ref, q_ref, kt_ref, vx_ref):
    lane = _iota((ROW_TILE, BRANCH_W), 1)
    first_half = (lane % DIFF_QK) < ROPE_AXIS
    cos = cos_ref[...]
    sin = sin_ref[...]

    def rope(x):
        partner = jnp.where(first_half, pltpu.roll(x, BRANCH_W - ROPE_AXIS, 1), pltpu.roll(x, ROPE_AXIS, 1))
        return x * cos + partner * sin

    q = rope(x_ref[:, 0:BRANCH_W].astype(F32)) * (DIFF_QK ** -0.5 * LOG2E)
    k = rope(x_ref[:, BRANCH_W:2 * BRANCH_W].astype(F32))
    q_ref[...] = q.astype(q_ref.dtype)
    kt_ref[...] = jnp.transpose(k).astype(kt_ref.dtype)
    v = x_ref[:, 2 * BRANCH_W:]
    src = _iota((BRANCH_W, VX_W), 0)
    dst = _iota((BRANCH_W, VX_W), 1)
    ones_col = jnp.where(_iota((ROW_TILE, VX_W), 1) == HEAD_DIM, 1.0, 0.0)
    for h in range(N_HEADS):
        sel = jnp.where(jnp.logical_and(src == h * HEAD_DIM + dst, dst < HEAD_DIM), 1.0, 0.0).astype(BF16)
        vx_ref[h] = (_dot(v, sel) + ones_col).astype(vx_ref.dtype)


def _rope_prep(proj, cos_l, sin_l):
    bsz, t, _ = proj.shape
    nt = t // ROW_TILE
    kvt = _kv_tile(t)
    r = kvt // ROW_TILE
    qkv_blk = BLK_DIFF_QKV * BRANCH_W // QKV_W
    tab = pl.BlockSpec((ROW_TILE, BRANCH_W), lambda b, i: (i, 0))
    return pl.pallas_call(
        _rope_prep_kernel,
        grid=(bsz, nt),
        in_specs=[pl.BlockSpec((None, ROW_TILE, QKV_W), lambda b, i: (b, i, qkv_blk)), tab, tab],
        out_specs=[pl.BlockSpec((None, ROW_TILE, BRANCH_W), lambda b, i: (b, i, 0)),
                   pl.BlockSpec((None, None, BRANCH_W, ROW_TILE), lambda b, i: (b, i // r, 0, i % r)),
                   pl.BlockSpec((None, None, N_HEADS, ROW_TILE, VX_W), lambda b, i: (b, i // r, 0, i % r, 0))],
        out_shape=[jax.ShapeDtypeStruct((bsz, t, BRANCH_W), BF16),
                   jax.ShapeDtypeStruct((bsz, t // kvt, BRANCH_W, kvt), BF16),
                   jax.ShapeDtypeStruct((bsz, t // kvt, N_HEADS, kvt, VX_W), BF16)],
        compiler_params=_cparams("parallel", "parallel"),
        name="rope_prep",
    )(proj, cos_l, sin_l)


N_CHAINS = 2 * N_HEADS


def _attn_kernel(q_ref, kt_ref, vx_ref, lp_ref, o_ref, qm_scr, s0_scr, s1_scr, m0_scr, m1_scr, al0_scr, al1_scr,
                 acc_scr, *, nk, lam_init):
    s_bufs, m_bufs, al_bufs = (s0_scr, s1_scr), (m0_scr, m1_scr), (al0_scr, al1_scr)
    q = q_ref[...]
    lane = _iota(q.shape, 1)
    zero = jnp.zeros_like(q)
    for c in range(N_CHAINS):
        lo = c * DIFF_QK
        qm_scr[c] = jnp.where(jnp.logical_and(lane >= lo, lane < lo + DIFF_QK), q, zero)
    acc_scr[...] = jnp.zeros_like(acc_scr)

    def stage_a(j, slot, first=False):
        kt = kt_ref[j]
        for c in range(N_CHAINS):
            s = _dot(qm_scr[c], kt)
            s_bufs[slot][c] = s
            row_max = jnp.max(s, axis=1, keepdims=True)
            if first:
                m_bufs[slot][c] = row_max
                al_bufs[slot][c] = jnp.zeros_like(row_max)
            else:
                m_old = m_bufs[1 - slot][c]
                m_new = jnp.maximum(m_old, row_max)
                m_bufs[slot][c] = m_new
                al_bufs[slot][c] = jnp.exp2(m_old - m_new)

    def stage_b(j, slot):
        for c in range(N_CHAINS):
            p = jnp.exp2(s_bufs[slot][c] - m_bufs[slot][c]).astype(BF16)
            acc_scr[c] = acc_scr[c] * al_bufs[slot][c] + _dot(p, vx_ref[j, c // 2])

    stage_a(0, 0, first=True)

    def body(i, carry):
        j = 2 * i + 1
        stage_a(j, 1)
        stage_b(j - 1, 0)
        stage_a(j + 1, 0)
        stage_b(j, 1)
        return carry

    n_pairs = (nk - 1) // 2
    lax.fori_loop(0, n_pairs, body, 0)
    if nk % 2 == 0:
        stage_a(nk - 1, 1)
        stage_b(nk - 2, 0)
    stage_b(nk - 1, (nk - 1) % 2)

    lp = lp_ref[...]
    grp_src = _iota((VX_W, VX_W), 0) // DIFF_QK
    prod1 = lp[0:1, :] * lp[1:2, :]
    prod2 = lp[2:3, :] * lp[3:4, :]
    src = _iota((VX_W, BRANCH_W), 0)
    dst = _iota((VX_W, BRANCH_W), 1)
    out = jnp.zeros((ROW_TILE, BRANCH_W), F32)
    for h in range(N_HEADS):
        pick = jnp.where(grp_src == h, 1.0, 0.0).astype(BF16)
        lam = jnp.exp(_dot_hl_r(prod1, pick)) - jnp.exp(_dot_hl_r(prod2, pick)) + lam_init
        a0 = acc_scr[2 * h]
        a1 = acc_scr[2 * h + 1]
        o_h = a0 / a0[:, HEAD_DIM:HEAD_DIM + 1] - lam * (a1 / a1[:, HEAD_DIM:HEAD_DIM + 1])
        place = jnp.where(jnp.logical_and(dst == src + h * HEAD_DIM, src < HEAD_DIM), 1.0, 0.0).astype(BF16)
        out = out + _dot(o_h.astype(BF16), place)
    o_ref[...] = out.astype(o_ref.dtype)


def _diff_attention(q_rot, kt, vx, lam_p, lam_init, q_tile0, n_q_tiles, kv_len):
    bsz = q_rot.shape[0]
    n_kv_arr, _, kvt = kt.shape[1:]
    tk = min(kv_len, kvt)
    assert kv_len % tk == 0 and kvt % tk == 0
    nk = kv_len // tk
    lp = jnp.zeros((8, VX_W), F32).at[:4].set(lam_p.reshape(4, N_HEADS * DIFF_QK))
    return pl.pallas_call(
        functools.partial(_attn_kernel, nk=nk, lam_init=lam_init),
        grid=(bsz, n_q_tiles),
        in_specs=[pl.BlockSpec((None, ROW_TILE, BRANCH_W), lambda b, i: (b, i + q_tile0, 0)),
                  pl.BlockSpec((None, nk, BRANCH_W, tk), lambda b, i: (b, 0, 0, 0), pipeline_mode=pl.Buffered(1)),
                  pl.BlockSpec((None, nk, N_HEADS, tk, VX_W), lambda b, i: (b, 0, 0, 0, 0),
                               pipeline_mode=pl.Buffered(1)),
                  pl.BlockSpec((8, VX_W), lambda b, i: (0, 0))],
        out_specs=pl.BlockSpec((None, ROW_TILE, BRANCH_W), lambda b, i: (b, i, 0)),
        out_shape=jax.ShapeDtypeStruct((bsz, n_q_tiles * ROW_TILE, BRANCH_W), BF16),
        scratch_shapes=[pltpu.VMEM((N_CHAINS, ROW_TILE, BRANCH_W), BF16)]
                       + [pltpu.VMEM((N_CHAINS, ROW_TILE, tk), F32)] * 2
                       + [pltpu.VMEM((N_CHAINS, ROW_TILE, 1), F32)] * 4
                       + [pltpu.VMEM((N_CHAINS, ROW_TILE, VX_W), F32)],
        compiler_params=_cparams("parallel", "arbitrary"),
        name="diff_attention",
    )(q_rot, kt, vx, lp)


def _diff_branch(proj, tables, lam_p, lam_init, n_ctx_tiles, with_ctx):
    t = proj.shape[1]
    q_rot, kt, vx = _rope_prep(proj, *tables)
    lat = _diff_attention(q_rot, kt, vx, lam_p, lam_init, n_ctx_tiles, t // ROW_TILE - n_ctx_tiles, t)
    if not with_ctx:
        return lat
    ctx = _diff_attention(q_rot, kt, vx, lam_p, lam_init, 0, n_ctx_tiles, n_ctx_tiles * ROW_TILE)
    return jnp.concatenate([ctx, lat], axis=1)


def _prepare(x, c, ctx, c_ctx, w_mod, b_mod):
    bsz, _, d = x.shape
    ct = ctx.shape[1]
    assert ct % ROW_TILE == 0 and x.shape[1] % ROW_TILE == 0
    xa = jnp.concatenate([ctx, x], axis=1)
    rows = 8 * ((bsz + 1 + 7) // 8)
    cond = jnp.zeros((rows, d), F32).at[:bsz].set(c).at[bsz].set(c_ctx)
    mods = _modulation(cond, w_mod, b_mod)
    return dict(xa=xa, mods=mods, n_ctx_tiles=ct // ROW_TILE, ct=ct)


def _layer_inproj(st, l, norm_mix, w_in):
    mods3 = st["mods"][l][:, None, :]
    return _inproj(st["xa"], mods3, norm_mix[l], _permute_w_in(w_in[l]), st["n_ctx_tiles"])


def _merge_kernel(x_ref, mod_ref, glf_ref, glb_ref, mlf_ref, mlb_ref, gdf_ref, gdb_ref, at_ref,
                  g_gla_ref, g_ml_ref, g_gd_ref, mg0_ref, mg1_ref, mg2_ref, mg3_ref,
                  hn_ref, wb_ref, wo_ref, o_ref, *, d, lam_init):
    ones_blk = _ones_blk()

    def head_norm(o, i):
        ms = _seg_sum(o * o, ones_blk) * (1.0 / HEAD_DIM)
        return o * lax.rsqrt(ms + EPS) * hn_ref[i:i + 1, :]

    f32 = lambda r: r[...].astype(F32)
    ys = [head_norm(f32(glf_ref) + f32(glb_ref), 0) * _silu(f32(g_gla_ref)),
          head_norm(f32(mlf_ref) + f32(mlb_ref), 1) * jax.nn.sigmoid(f32(g_ml_ref)),
          head_norm(f32(gdf_ref) + f32(gdb_ref), 2) * _silu(f32(g_gd_ref)),
          head_norm(f32(at_ref), 3) * (1.0 - lam_init)]
    acc = None
    for i, (y, mg_ref) in enumerate(zip(ys, (mg0_ref, mg1_ref, mg2_ref, mg3_ref))):
        term = jax.nn.sigmoid(f32(mg_ref)) * _dot(y.astype(BF16), wb_ref[i])
        acc = term if acc is None else acc + term
    out = _dot(acc.astype(BF16), wo_ref[...])
    g1 = mod_ref[...][:, 2 * d:3 * d]
    o_ref[...] = x_ref[...] + g1 * out


def _merge(xa, mods3, proj, scans, attn, hn, wb, wo, n_ctx_tiles, with_ctx, lam_init):
    bsz, t, d = xa.shape
    off = 0 if with_ctx else n_ctx_tiles
    nt = t // ROW_TILE - off
    rows = lambda col: pl.BlockSpec((None, ROW_TILE, BRANCH_W), lambda b, i: (b, i + off, col))
    gate = lambda j: pl.BlockSpec((None, ROW_TILE, d), lambda b, i: (b, i + off, BLK_GATES * BRANCH_W // d + j))
    const = lambda shape: pl.BlockSpec(shape, lambda b, i: (0,) * len(shape))
    return pl.pallas_call(
        functools.partial(_merge_kernel, d=d, lam_init=lam_init),
        grid=(bsz, nt),
        in_specs=[pl.BlockSpec((None, ROW_TILE, d), lambda b, i: (b, i + off, 0)),
                  pl.BlockSpec((None, 1, 6 * d), lambda b, i: (jnp.where(i + off < n_ctx_tiles, bsz, b), 0, 0))]
                 + [rows(0)] * 6
                 + [pl.BlockSpec((None, ROW_TILE, BRANCH_W), lambda b, i: (b, i, 0))]
                 + [rows(BLK_GLA_G), rows(BLK_ML_G), rows(BLK_GDN_G)]
                 + [gate(j) for j in range(N_BRANCH)]
                 + [const((8, BRANCH_W)), const((N_BRANCH, BRANCH_W, d)), const((d, d))],
        out_specs=pl.BlockSpec((None, ROW_TILE, d), lambda b, i: (b, i, 0)),
        out_shape=jax.ShapeDtypeStruct((bsz, nt * ROW_TILE, d), F32),
        compiler_params=_cparams("parallel", "parallel"),
        name="merge",
    )(xa, mods3, *scans, attn, proj, proj, proj, proj, proj, proj, proj, hn, wb, wo)


GATE_LANES = 128


def _router_gates_t(hf, rw_ref, rb_ref):
    h_hi, h_lo = _split_hl(hf)
    w_hi, w_lo = _split_hl(rw_ref[...])
    logits = _dot_nt(w_hi, h_hi) + _dot_nt(w_hi, h_lo) + _dot_nt(w_lo, h_hi)
    scores = jax.nn.sigmoid(logits)
    sel = scores + rb_ref[...]
    s = [sel[e:e + 1, :] for e in range(N_EXPERTS)]
    sc = [scores[e:e + 1, :] for e in range(N_EXPERTS)]
    grp = []
    for g in range(N_GROUPS):
        a, b, c, dd = s[4 * g:4 * g + 4]
        grp.append(functools.reduce(jnp.maximum, [a + b, a + c, a + dd, b + c, b + dd, c + dd]))
    gmax = functools.reduce(jnp.maximum, grp)
    chosen, taken = [], None
    for g in range(N_GROUPS):
        hit = grp[g] == gmax
        if taken is not None:
            hit = jnp.logical_and(hit, jnp.logical_not(taken))
        taken = hit if taken is None else jnp.logical_or(taken, hit)
        chosen.append(hit)
    ms = [jnp.where(chosen[e // EXPERTS_PER_GROUP], s[e], NEG_BIG) for e in range(N_EXPERTS)]

    def first_argmax(vals):
        top = functools.reduce(jnp.maximum, vals)
        hits, seen = [], None
        for v in vals:
            hit = v == top
            if seen is not None:
                hit = jnp.logical_and(hit, jnp.logical_not(seen))
            seen = hit if seen is None else jnp.logical_or(seen, hit)
            hits.append(hit)
        return hits

    oh1 = first_argmax(ms)
    oh2 = first_argmax([jnp.where(o, NEG_BIG, v) for o, v in zip(oh1, ms)])
    zero = jnp.zeros_like(sc[0])
    w1 = functools.reduce(jnp.add, [jnp.where(o, v, zero) for o, v in zip(oh1, sc)])
    w2 = functools.reduce(jnp.add, [jnp.where(o, v, zero) for o, v in zip(oh2, sc)])
    tot = w1 + w2
    rows = [jnp.where(o1, w1 / tot, zero) + jnp.where(o2, w2 / tot, zero) for o1, o2 in zip(oh1, oh2)]
    return jnp.concatenate(rows, axis=0)


def _moe_dense_kernel(x_ref, mod_ref, g_ref, rw_ref, rb_ref, wg_ref, wu_ref, wd_ref, o_ref,
                      h_scr, gate_scr, acc_scr, *, d):
    e = pl.program_id(2)

    @pl.when(e == 0)
    def _():
        mod = mod_ref[...]
        hf = _rms_mod(x_ref[...], g_ref[...], mod[:, 3 * d:4 * d], mod[:, 4 * d:5 * d])
        h_scr[...] = hf.astype(BF16)
        gates_t = _router_gates_t(hf, rw_ref, rb_ref)
        eye = jnp.where(_iota((N_EXPERTS, GATE_LANES), 0) == _iota((N_EXPERTS, GATE_LANES), 1), 1.0, 0.0).astype(BF16)
        g_hi, g_lo = _split_hl(gates_t)
        gate_scr[...] = _dot_tn(g_hi, eye) + _dot_tn(g_lo, eye)
        acc_scr[...] = jnp.zeros_like(acc_scr)

    h = h_scr[...]
    pick = jnp.where(_iota((GATE_LANES, GATE_LANES), 0) == e, 1.0, 0.0).astype(BF16)
    ge = _dot_hl_r(gate_scr[...], pick)[:, 0:1]
    act = (_silu(_dot(h, wg_ref[...])) * _dot(h, wu_ref[...])).astype(BF16)
    acc_scr[...] += ge * _dot(act, wd_ref[...])

    @pl.when(e == pl.num_programs(2) - 1)
    def _():
        g2 = mod_ref[...][:, 5 * d:6 * d]
        o_ref[...] = x_ref[...] + g2 * acc_scr[...]


def _moe_dense(xa, mods3, g, rw_t, rb, wg, wu, wd, n_ctx_tiles):
    bsz, t, d = xa.shape
    ne, _, de = wg.shape
    return pl.pallas_call(
        functools.partial(_moe_dense_kernel, d=d),
        grid=(bsz, t // ROW_TILE, ne),
        in_specs=[pl.BlockSpec((None, ROW_TILE, d), lambda b, i, e: (b, i, 0)),
                  pl.BlockSpec((None, 1, 6 * d), lambda b, i, e: (jnp.where(i < n_ctx_tiles, bsz, b), 0, 0)),
                  pl.BlockSpec((1, d), lambda b, i, e: (0, 0)),
                  pl.BlockSpec((ne, d), lambda b, i, e: (0, 0)),
                  pl.BlockSpec((ne, ROW_TILE), lambda b, i, e: (0, 0)),
                  pl.BlockSpec((None, d, de), lambda b, i, e: (e, 0, 0)),
                  pl.BlockSpec((None, d, de), lambda b, i, e: (e, 0, 0)),
                  pl.BlockSpec((None, de, d), lambda b, i, e: (e, 0, 0))],
        out_specs=pl.BlockSpec((None, ROW_TILE, d), lambda b, i, e: (b, i, 0)),
        out_shape=jax.ShapeDtypeStruct((bsz, t, d), F32),
        scratch_shapes=[pltpu.VMEM((ROW_TILE, d), BF16), pltpu.VMEM((ROW_TILE, GATE_LANES), F32),
                        pltpu.VMEM((ROW_TILE, d), F32)],
        compiler_params=_cparams("parallel", "parallel", "arbitrary"),
        name="moe_dense",
    )(xa, mods3, g.reshape(1, d), rw_t, rb, wg, wu, wd)


def _final_norm_kernel(x_ref, g_ref, o_ref):
    x = x_ref[...]
    o_ref[...] = x * lax.rsqrt(jnp.mean(x * x, axis=-1, keepdims=True) + EPS) * g_ref[...]


def _final_norm(x, g):
    bsz, t, d = x.shape
    return pl.pallas_call(
        _final_norm_kernel,
        grid=(bsz, t // ROW_TILE),
        in_specs=[pl.BlockSpec((None, ROW_TILE, d), lambda b, i: (b, i, 0)), pl.BlockSpec((1, d), lambda b, i: (0, 0))],
        out_specs=pl.BlockSpec((None, ROW_TILE, d), lambda b, i: (b, i, 0)),
        out_shape=jax.ShapeDtypeStruct((bsz, t, d), F32),
        compiler_params=_cparams("parallel", "parallel"),
        name="final_norm",
    )(x, g.reshape(1, d))


def kernel(x, c, ctx, c_ctx, w_mod, b_mod, norm_mix, norm_ffn, w_in, gla_a2, gla_ab, mlstm_gate_b, gdn_conv, gdn_a_log, gdn_dt_bias, diff_lambda, head_norm, w_branch, w_out, router_w, router_b, w_gate, w_up, w_down, norm_final):
    depth = w_in.shape[0]
    st = _prepare(x, c, ctx, c_ctx, w_mod, b_mod)
    n_ctx_tiles = st["n_ctx_tiles"]
    tables = _rope_tables(x.shape[1], st["ct"])
    rw_t = router_w.T
    rb = jnp.broadcast_to(router_b.astype(F32)[:, None], (N_EXPERTS, ROW_TILE))
    for l in range(depth):
        with_ctx = l < depth - 1
        lam_init = 0.8 - 0.6 * math.exp(-0.3 * l)
        mods3 = st["mods"][l][:, None, :]
        proj = _layer_inproj(st, l, norm_mix, w_in)
        scans = (*_gla_branch(proj, gla_a2[l], gla_ab[l]),
                 *_mlstm_branch(proj, mlstm_gate_b[l]),
                 *_gdn_branch(proj, gdn_conv[l], gdn_a_log[l], gdn_dt_bias[l], n_ctx_tiles))
        attn = _diff_branch(proj, tables, diff_lambda[l], lam_init, n_ctx_tiles, with_ctx)
        hn = jnp.zeros((8, BRANCH_W), F32).at[:N_BRANCH].set(head_norm[l])
        xa = _merge(st["xa"], mods3, proj, scans, attn, hn, w_branch[l].astype(BF16), w_out[l].astype(BF16),
                    n_ctx_tiles, with_ctx, lam_init)
        if not with_ctx:
            n_ctx_tiles = 0
        xa = _moe_dense(xa, mods3, norm_ffn[l], rw_t, rb, w_gate[l].astype(BF16), w_up[l].astype(BF16),
                        w_down[l].astype(BF16), n_ctx_tiles)
        st = dict(st, xa=xa, n_ctx_tiles=n_ctx_tiles)
    return _final_norm(st["xa"], norm_final)
```

```python
import functools
import math

import numpy as np
import jax
import jax.numpy as jnp
from jax import lax
from jax.experimental import pallas as pl
from jax.experimental.pallas import tpu as pltpu

N_HEADS = 4
HEAD_DIM = 64
BRANCH_W = N_HEADS * HEAD_DIM
CHUNK = 64
GLA_RANK = 16
GLA_NORMALIZER = 16.0
CONV_W = 5
DIFF_QK = HEAD_DIM // 2
ROPE_AXIS = DIFF_QK // 2
ROPE_BASE = 10000.0
GRID_W = 64
N_EXPERTS = 16
N_GROUPS = 4
EXPERTS_PER_GROUP = 4
EPS = 1e-6
N_BRANCH = 4

ROW_TILE = 256
PROJ_COLS = 8192
BLK_GDN_QKV, BLK_DIFF_QKV, BLK_GDN_G = 0, 3, 6
BLK_GLA_Q, BLK_GLA_K, BLK_GLA_V, BLK_GLA_G = 7, 8, 9, 10
BLK_ML_Q, BLK_ML_K, BLK_ML_V, BLK_ML_G = 11, 28, 29, 30
BLK_GATES = 12
SMALL_BLOCK = 31
SM_GLA_RF, SM_GLA_RB, SM_ML_F, SM_ML_B, SM_GD_F, SM_GD_B = 0, 16, 32, 40, 48, 56
NEG_BIG = -1e30
VMEM_LIMIT = 56 * 1024 * 1024

F32 = jnp.float32
BF16 = jnp.bfloat16


def _cparams(*sem):
    return pltpu.CompilerParams(dimension_semantics=sem, vmem_limit_bytes=VMEM_LIMIT)


def _dot(a, b):
    return jnp.dot(a, b, preferred_element_type=F32)


def _dot_nt(a, b):
    return lax.dot_general(a, b, (((1,), (1,)), ((), ())), preferred_element_type=F32)


def _dot_tn(a, b):
    return lax.dot_general(a, b, (((0,), (0,)), ((), ())), preferred_element_type=F32)


def _split_hl(x):
    hi = x.astype(BF16)
    lo = (x - hi.astype(F32)).astype(BF16)
    return hi, lo


def _dot_hl(a, x):
    hi, lo = _split_hl(x)
    return _dot(a, hi) + _dot(a, lo)


def _dot_hl_r(x, a):
    hi, lo = _split_hl(x)
    return _dot(hi, a) + _dot(lo, a)


def _iota(shape, dim):
    return lax.broadcasted_iota(jnp.int32, shape, dim)


def _softplus(x):
    return jnp.maximum(x, 0.0) + jnp.log(1.0 + jnp.exp(-jnp.abs(x)))


def _log_sigmoid(x):
    return -_softplus(-x)


def _silu(x):
    return x * jax.nn.sigmoid(x)


def _lane_head(shape):
    return _iota(shape, len(shape) - 1) // HEAD_DIM


def _stack4(x):
    lh = _lane_head(x.shape)
    zero = jnp.zeros_like(x)
    return jnp.concatenate([jnp.where(lh == h, x, zero) for h in range(N_HEADS)], axis=0)


def _blk_mask():
    r = _iota((BRANCH_W, BRANCH_W), 0) // HEAD_DIM
    c = _iota((BRANCH_W, BRANCH_W), 1) // HEAD_DIM
    return r == c


def _ones_blk():
    return jnp.where(_blk_mask(), 1.0, 0.0).astype(BF16)


def _tri_hs(reverse, strict):
    t = _iota((CHUNK, BRANCH_W), 0)
    s = _iota((CHUNK, BRANCH_W), 1) % CHUNK
    if reverse:
        return (s > t) if strict else (s >= t)
    return (s < t) if strict else (s <= t)


def _eye_hs():
    t = _iota((CHUNK, BRANCH_W), 0)
    s = _iota((CHUNK, BRANCH_W), 1) % CHUNK
    return t == s


def _cum_mat(reverse):
    t = _iota((CHUNK, CHUNK), 0)
    s = _iota((CHUNK, CHUNK), 1)
    m = (s >= t) if reverse else (s <= t)
    return jnp.where(m, 1.0, 0.0).astype(BF16)


def _row_of(col_rep):
    return jnp.sum(jnp.where(_eye_hs(), col_rep, 0.0), axis=0, keepdims=True)


def _seg_max(x):
    lane = _iota(x.shape, 1)
    n = x.shape[1]
    for sh in (1, 2, 4, 8, 16, 32):
        up = pltpu.roll(x, n - sh, 1)
        dn = pltpu.roll(x, sh, 1)
        x = jnp.maximum(x, jnp.where((lane & sh) == 0, up, dn))
    return x


def _seg_sum(x, ones_blk):
    return _dot_hl_r(x, ones_blk)


def _mod_kernel(c_ref, w_ref, b_ref, o_ref):
    a = _silu(c_ref[...]).astype(BF16)
    o_ref[...] = _dot(a, w_ref[...].astype(BF16)) + b_ref[...]


def _modulation(cond, w_mod, b_mod):
    depth, d, n = w_mod.shape
    r = cond.shape[0]
    tn = 1536
    return pl.pallas_call(
        _mod_kernel,
        grid=(depth, n // tn),
        in_specs=[pl.BlockSpec((r, d), lambda l, j: (0, 0)),
                  pl.BlockSpec((None, d, tn), lambda l, j: (l, 0, j)),
                  pl.BlockSpec((None, 1, tn), lambda l, j: (l, 0, j))],
        out_specs=pl.BlockSpec((None, r, tn), lambda l, j: (l, 0, j)),
        out_shape=jax.ShapeDtypeStruct((depth, r, n), F32),
        compiler_params=_cparams("parallel", "parallel"),
        name="modulation",
    )(cond, w_mod, b_mod.reshape(depth, 1, n))


def _rms_mod(x, g, shift, scale):
    y = x * lax.rsqrt(jnp.mean(x * x, axis=-1, keepdims=True) + EPS)
    return (y * g) * (1.0 + scale) + shift


def _inproj_kernel(x_ref, mod_ref, g_ref, w_ref, o_ref, *, d, n_chunk):
    mod = mod_ref[...]
    h = _rms_mod(x_ref[...], g_ref[...], mod[:, 0:d], mod[:, d:2 * d]).astype(BF16)
    for n0 in range(0, PROJ_COLS, n_chunk):
        o_ref[:, n0:n0 + n_chunk] = _dot(h, w_ref[:, n0:n0 + n_chunk]).astype(BF16)


def _mod_row_map(n_ctx_tiles, n_batch):
    return lambda b, i: (jnp.where(i < n_ctx_tiles, n_batch, b), 0, 0)


def _inproj(xa, mods3, g, w_perm, n_ctx_tiles):
    bsz, t, d = xa.shape
    return pl.pallas_call(
        functools.partial(_inproj_kernel, d=d, n_chunk=1024),
        grid=(bsz, t // ROW_TILE),
        in_specs=[pl.BlockSpec((None, ROW_TILE, d), lambda b, i: (b, i, 0)),
                  pl.BlockSpec((None, 1, 6 * d), _mod_row_map(n_ctx_tiles, bsz)),
                  pl.BlockSpec((1, d), lambda b, i: (0, 0)),
                  pl.BlockSpec((d, PROJ_COLS), lambda b, i: (0, 0), pipeline_mode=pl.Buffered(1))],
        out_specs=pl.BlockSpec((None, ROW_TILE, PROJ_COLS), lambda b, i: (b, i, 0)),
        out_shape=jax.ShapeDtypeStruct((bsz, t, PROJ_COLS), BF16),
        compiler_params=_cparams("parallel", "parallel"),
        name="inproj",
    )(xa, mods3, g.reshape(1, d), w_perm)


def _proj_perm():
    sizes = (256, 256, 256, 256, 16, 16, 256, 256, 256, 256, 8, 8, 256, 256, 256, 256, 8, 8, 256, 256, 256, 4096)
    off = np.concatenate([[0], np.cumsum(sizes)])
    seg = lambda i: np.arange(off[i], off[i + 1])
    wide = {BLK_GDN_QKV: 12, BLK_GDN_QKV + 1: 13, BLK_GDN_QKV + 2: 14, BLK_GDN_G: 15,
            BLK_DIFF_QKV: 18, BLK_DIFF_QKV + 1: 19, BLK_DIFF_QKV + 2: 20,
            BLK_GLA_Q: 0, BLK_GLA_K: 1, BLK_GLA_V: 2, BLK_GLA_G: 3,
            BLK_ML_Q: 6, BLK_ML_K: 7, BLK_ML_V: 8, BLK_ML_G: 9}
    perm = np.full((PROJ_COLS,), -1, np.int64)
    for j, i in wide.items():
        perm[j * BRANCH_W:(j + 1) * BRANCH_W] = seg(i)
    perm[BLK_GATES * BRANCH_W:BLK_GATES * BRANCH_W + 4096] = seg(21)
    base = SMALL_BLOCK * BRANCH_W
    for lane0, i in ((SM_GLA_RF, 4), (SM_GLA_RB, 5), (SM_ML_F, 10), (SM_ML_B, 11), (SM_GD_F, 16), (SM_GD_B, 17)):
        s = seg(i)
        perm[base + lane0:base + lane0 + len(s)] = s
    return perm


def _permute_w_in(w_in):
    perm = _proj_perm()
    w = jnp.take(w_in, jnp.asarray(np.maximum(perm, 0)), axis=-1)
    return jnp.where(jnp.asarray(perm >= 0), w, 0.0).astype(BF16)


SCAN_CHUNKS = ROW_TILE // CHUNK


def _scan_block_map(col, nb, reverse):
    if reverse:
        return lambda j: (0, jnp.where(j == 0, 0, nb - j), col)
    return lambda j: (0, j, col)


def _scan_steps(reverse, bsz):
    order = range(SCAN_CHUNKS - 1, -1, -1) if reverse else range(SCAN_CHUNKS)
    return [(b, slice(c * CHUNK, (c + 1) * CHUNK)) for c in order for b in range(bsz)]


def _scan_call(kernel_fn, name, arrays, cols, consts, scratch, reverse):
    bsz, t, _ = arrays[0].shape
    nb = t // ROW_TILE
    blk = lambda col: pl.BlockSpec((bsz, ROW_TILE, BRANCH_W), _scan_block_map(col, nb, reverse))
    const = lambda a: pl.BlockSpec(a.shape, lambda j: (0,) * a.ndim)
    return pl.pallas_call(
        functools.partial(kernel_fn, reverse=reverse, bsz=bsz),
        grid=(nb,),
        in_specs=[blk(c) for c in cols] + [const(a) for a in consts],
        out_specs=blk(0),
        out_shape=jax.ShapeDtypeStruct((bsz, t, BRANCH_W), BF16),
        scratch_shapes=[pltpu.VMEM((bsz,) + s, F32) for s in scratch],
        compiler_params=_cparams("arbitrary"),
        name=name + ("_bwd" if reverse else "_fwd"),
    )(*arrays, *consts)


def _expand_small(small, lane0, count):
    src = _iota((BRANCH_W, BRANCH_W), 0)
    dst_head = _iota((BRANCH_W, BRANCH_W), 1) // HEAD_DIM
    e = jnp.where(src == lane0 + dst_head, 1.0, 0.0).astype(BF16)
    return _dot(small, e)


def _gla_kernel(q_ref, k_ref, v_ref, sm_ref, a2_ref, ab_ref, o_ref, st_ref, *, reverse, bsz):
    @pl.when(pl.program_id(0) == 0)
    def _():
        st_ref[...] = jnp.zeros_like(st_ref)

    cum_mat = _cum_mat(reverse)
    tri = _tri_hs(reverse, strict=False)
    blk = _blk_mask()
    a2 = a2_ref[...]
    ab = ab_ref[...]
    steps = _scan_steps(reverse, bsz)
    la = [_log_sigmoid(_dot(sm_ref[b, rows, :], a2) + ab) / GLA_NORMALIZER for b, rows in steps]
    cum = [_dot_hl(cum_mat, x) for x in la]
    tot = [jnp.sum(x, axis=0, keepdims=True) for x in la]
    q_in = [(q_ref[b, rows, :].astype(F32) * (HEAD_DIM ** -0.5) * jnp.exp(c)).astype(BF16)
            for (b, rows), c in zip(steps, cum)]
    k_out = [(k_ref[b, rows, :].astype(F32) * jnp.exp(-c)).astype(BF16) for (b, rows), c in zip(steps, cum)]
    k_end = [(k_ref[b, rows, :].astype(F32) * jnp.exp(t - c)).astype(BF16)
             for (b, rows), c, t in zip(steps, cum, tot)]
    att = [jnp.where(tri, _dot_nt(qi, _stack4(ko)), 0.0).astype(BF16) for qi, ko in zip(q_in, k_out)]
    o_intra = [_dot(a, _stack4(v_ref[b, rows, :])) for (b, rows), a in zip(steps, att)]
    kv_t = [jnp.where(blk, _dot_tn(v_ref[b, rows, :], ke), 0.0) for (b, rows), ke in zip(steps, k_end)]
    for i, (b, rows) in enumerate(steps):
        st = st_ref[b]
        o_ref[b, rows, :] = (o_intra[i] + _dot_nt(q_in[i], st.astype(BF16))).astype(o_ref.dtype)
        st_ref[b] = st * jnp.exp(tot[i]) + kv_t[i]


def _gla_scan(proj, a2pad, ab, reverse):
    return _scan_call(_gla_kernel, "gla", [proj] * 4, [BLK_GLA_Q, BLK_GLA_K, BLK_GLA_V, SMALL_BLOCK],
                      [a2pad, ab], [(BRANCH_W, BRANCH_W)], reverse)


def _gla_params(a2, ab, d):
    lane0 = SM_GLA_RB if d else SM_GLA_RF
    pad = jnp.zeros((BRANCH_W, BRANCH_W), F32).at[lane0:lane0 + GLA_RANK].set(a2[d])
    return pad.astype(BF16), ab[d].reshape(1, BRANCH_W)


def _gla_branch(proj, a2, ab):
    return tuple(_gla_scan(proj, *_gla_params(a2, ab, d), reverse=bool(d)) for d in (0, 1))


def _mlstm_kernel(q_ref, k_ref, v_ref, sm_ref, bias_ref, o_ref, ct_ref, nm_ref, *, reverse, bsz):
    @pl.when(pl.program_id(0) == 0)
    def _():
        ct_ref[...] = jnp.zeros_like(ct_ref)
        nm_ref[...] = jnp.zeros_like(nm_ref)

    lane0 = SM_ML_B if reverse else SM_ML_F
    cum_mat = _cum_mat(reverse)
    tri = _tri_hs(reverse, strict=False)
    blk = _blk_mask()
    ones_blk = _ones_blk()
    b_i = bias_ref[0:1, :]
    b_f = bias_ref[1:2, :]
    steps = _scan_steps(reverse, bsz)
    each = lambda f, *ls: [f(*a) for a in zip(*ls)]
    sm = [sm_ref[b, rows, :] for b, rows in steps]
    q = [q_ref[b, rows, :] for b, rows in steps]
    i_pre = [_expand_small(x, lane0, N_HEADS) + b_i for x in sm]
    log_f = [_log_sigmoid(_expand_small(x, lane0 + N_HEADS, N_HEADS) + b_f) for x in sm]
    f_cum = [_dot_hl(cum_mat, x) for x in log_f]
    f_tot = [jnp.sum(x, axis=0, keepdims=True) for x in log_f]
    u = each(lambda i, f: i - f, i_pre, f_cum)
    lw = each(lambda t, x: t + x, f_tot, u)
    a_end = [jnp.max(x, axis=0, keepdims=True) for x in lw]
    k = [k_ref[b, rows, :].astype(F32) * (HEAD_DIM ** -0.5) for b, rows in steps]
    kw = each(lambda kk, x, a: kk * jnp.exp(x - a), k, lw, a_end)
    k_sum = [jnp.sum(x, axis=0, keepdims=True) for x in kw]
    kv_t = [jnp.where(blk, _dot_tn(v_ref[b, rows, :], x.astype(BF16)), 0.0) for (b, rows), x in zip(steps, kw)]
    log_d = each(lambda f, x: jnp.where(tri, f + _row_of(x), NEG_BIG), f_cum, u)
    mx = [_seg_max(x) for x in log_d]
    s = each(lambda qq, kk, ld, m: (_dot_nt(qq, _stack4(kk.astype(BF16))) * jnp.exp(ld - m)).astype(BF16),
             q, k, log_d, mx)
    num1 = [_dot(x, _stack4(v_ref[b, rows, :])) for (b, rows), x in zip(steps, s)]
    den1 = [_dot(x, ones_blk) for x in s]
    for i, (b, rows) in enumerate(steps):
        n_in = nm_ref[b, 0:1, :]
        m_in = nm_ref[b, 1:2, :]
        ct = ct_ref[b]
        g = f_cum[i] + m_in
        m_t = jnp.maximum(g, mx[i])
        e = jnp.exp(g - m_t)
        r = jnp.exp(mx[i] - m_t)
        num = r * num1[i] + e * _dot_nt(q[i], ct.astype(BF16))
        den = r * den1[i] + e * _seg_sum(q[i].astype(F32) * n_in, ones_blk)
        o_ref[b, rows, :] = (num / jnp.maximum(jnp.abs(den), jnp.exp(-m_t))).astype(o_ref.dtype)

        m_new = jnp.maximum(f_tot[i] + m_in, a_end[i])
        old = jnp.exp(f_tot[i] + m_in - m_new)
        new = jnp.exp(a_end[i] - m_new)
        ct_ref[b] = ct * old + kv_t[i] * new
        nm_ref[b, 0:1, :] = n_in * old + k_sum[i] * new
        nm_ref[b, 1:2, :] = m_new


def _mlstm_scan(proj, bias, reverse):
    return _scan_call(_mlstm_kernel, "mlstm", [proj] * 4, [BLK_ML_Q, BLK_ML_K, BLK_ML_V, SMALL_BLOCK],
                      [bias], [(BRANCH_W, BRANCH_W), (8, BRANCH_W)], reverse)


def _head_rows(vals):
    rows = [jnp.repeat(v.astype(F32), HEAD_DIM) for v in vals]
    rows += [jnp.zeros((BRANCH_W,), F32)] * (8 - len(rows))
    return jnp.stack(rows)


def _mlstm_branch(proj, gate_b):
    return tuple(_mlstm_scan(proj, _head_rows([gate_b[d, 0], gate_b[d, 1]]), reverse=bool(d)) for d in (0, 1))


HALO = 8
QKV_W = 3 * BRANCH_W


def _gdn_prep_kernel(prev_ref, cur_ref, next_ref, w_ref, o_ref, *, n_ctx_tiles, n_tiles):
    i = pl.program_id(1)
    has_prev = jnp.logical_and(i != 0, i != n_ctx_tiles).astype(F32)
    has_next = jnp.logical_and(i != n_ctx_tiles - 1, i != n_tiles - 1).astype(F32)
    padded = jnp.concatenate([prev_ref[...].astype(F32) * has_prev, cur_ref[...].astype(F32),
                              next_ref[...].astype(F32) * has_next], axis=0)
    w = w_ref[...]
    acc = jnp.zeros((ROW_TILE, QKV_W), F32)
    for j in range(CONV_W):
        off = HALO + j - CONV_W // 2
        acc = acc + padded[off:off + ROW_TILE, :] * w[j:j + 1, :]
    y = _silu(acc)
    ones_blk = _ones_blk()
    q = y[:, 0:BRANCH_W]
    k = y[:, BRANCH_W:2 * BRANCH_W]
    q = q * lax.rsqrt(_seg_sum(q * q, ones_blk) + EPS) * (HEAD_DIM ** -0.5)
    k = k * lax.rsqrt(_seg_sum(k * k, ones_blk) + EPS)
    o_ref[:, 0:BRANCH_W] = q.astype(o_ref.dtype)
    o_ref[:, BRANCH_W:2 * BRANCH_W] = k.astype(o_ref.dtype)
    o_ref[:, 2 * BRANCH_W:] = y[:, 2 * BRANCH_W:].astype(o_ref.dtype)


def _gdn_prep(proj, conv_w, n_ctx_tiles):
    bsz, t, _ = proj.shape
    nt = t // ROW_TILE
    per = ROW_TILE // HALO
    last = t // HALO - 1
    qkv_blk = BLK_GDN_QKV * BRANCH_W // QKV_W
    w8 = jnp.zeros((8, QKV_W), F32).at[:CONV_W].set(conv_w)
    return pl.pallas_call(
        functools.partial(_gdn_prep_kernel, n_ctx_tiles=n_ctx_tiles, n_tiles=nt),
        grid=(bsz, nt),
        in_specs=[pl.BlockSpec((None, HALO, QKV_W), lambda b, i: (b, jnp.maximum(i * per - 1, 0), qkv_blk)),
                  pl.BlockSpec((None, ROW_TILE, QKV_W), lambda b, i: (b, i, qkv_blk)),
                  pl.BlockSpec((None, HALO, QKV_W), lambda b, i: (b, jnp.minimum((i + 1) * per, last), qkv_blk)),
                  pl.BlockSpec((8, QKV_W), lambda b, i: (0, 0))],
        out_specs=pl.BlockSpec((None, ROW_TILE, QKV_W), lambda b, i: (b, i, 0)),
        out_shape=jax.ShapeDtypeStruct((bsz, t, QKV_W), BF16),
        compiler_params=_cparams("parallel", "parallel"),
        name="gdn_prep",
    )(proj, proj, proj, w8)


def _gdn_kernel(q_ref, k_ref, v_ref, sm_ref, par_ref, o_ref, s_ref, *, reverse, bsz):
    @pl.when(pl.program_id(0) == 0)
    def _():
        s_ref[...] = jnp.zeros_like(s_ref)

    lane0 = SM_GD_B if reverse else SM_GD_F
    cum_mat = _cum_mat(reverse)
    tri = _tri_hs(reverse, strict=False)
    tri_strict = _tri_hs(reverse, strict=True)
    blk = _blk_mask()
    eye = jnp.where(_eye_hs(), 1.0, 0.0)
    a_scale = jnp.exp(par_ref[0:1, :])
    dt_bias = par_ref[1:2, :]
    steps = _scan_steps(reverse, bsz)
    each = lambda f, *ls: [f(*a) for a in zip(*ls)]
    sm = [sm_ref[b, rows, :] for b, rows in steps]
    q = [q_ref[b, rows, :] for b, rows in steps]
    kb16 = [k_ref[b, rows, :] for b, rows in steps]
    beta = [jax.nn.sigmoid(_expand_small(x, lane0, N_HEADS)) for x in sm]
    g = [-a_scale * _softplus(_expand_small(x, lane0 + N_HEADS, N_HEADS) + dt_bias) for x in sm]
    cum = [_dot_hl(cum_mat, x) for x in g]
    tot = [jnp.sum(x, axis=0, keepdims=True) for x in g]
    gam = [jnp.where(tri, jnp.exp(jnp.where(tri, c - _row_of(c), 0.0)), 0.0) for c in cum]
    k_beta = each(lambda kk, bb: kk.astype(F32) * bb, kb16, beta)
    k4 = [_stack4(x) for x in kb16]
    a_hs = each(lambda kb, kk, gm: jnp.where(tri_strict, _dot_nt(kb.astype(BF16), kk) * gm, 0.0), k_beta, k4, gam)
    attn = each(lambda qq, kk, gm: (_dot_nt(qq, kk) * gm).astype(BF16), q, k4, gam)

    p = [-x for x in a_hs]
    t_inv = [eye + x for x in p]
    for _ in range(5):
        p = [_dot(x.astype(BF16), _stack4(x.astype(BF16))) for x in p]
        t_inv = each(lambda t, x: t + _dot(t.astype(BF16), _stack4(x.astype(BF16))), t_inv, p)
    w = each(lambda t, kb, c: _dot(t.astype(BF16), _stack4((kb * jnp.exp(c)).astype(BF16))).astype(BF16),
             t_inv, k_beta, cum)
    u = [_dot(t.astype(BF16), _stack4((v_ref[b, rows, :].astype(F32) * bb).astype(BF16)))
         for (b, rows), t, bb in zip(steps, t_inv, beta)]
    q_dec = each(lambda qq, c: (qq.astype(F32) * jnp.exp(c)).astype(BF16), q, cum)
    k_end = each(lambda kk, t, c: (kk.astype(F32) * jnp.exp(t - c)).astype(BF16), kb16, tot, cum)
    for i, (b, rows) in enumerate(steps):
        s = s_ref[b]
        sb = s.astype(BF16)
        v_new = (u[i] - _dot(w[i], sb)).astype(BF16)
        o_ref[b, rows, :] = (_dot(q_dec[i], sb) + _dot(attn[i], _stack4(v_new))).astype(o_ref.dtype)
        s_ref[b] = s * jnp.exp(tot[i]) + jnp.where(blk, _dot_tn(k_end[i], v_new), 0.0)


def _gdn_scan(gqkv, proj, par, reverse):
    return _scan_call(_gdn_kernel, "gdn", [gqkv, gqkv, gqkv, proj], [0, 1, 2, SMALL_BLOCK],
                      [par], [(BRANCH_W, BRANCH_W)], reverse)


def _gdn_branch(proj, conv_w, a_log, dt_bias, n_ctx_tiles):
    gqkv = _gdn_prep(proj, conv_w, n_ctx_tiles)
    return tuple(_gdn_scan(gqkv, proj, _head_rows([a_log[d], dt_bias[d]]), reverse=bool(d))
                 for d in (0, 1))


VX_W = 128
LOG2E = 1.4426950408889634


def _kv_tile(t):
    return next(k for k in (3 * ROW_TILE, 2 * ROW_TILE, ROW_TILE) if t % k == 0)


def _rope_tables(n_lat, n_ctx):
    pos = jnp.arange(n_lat)
    row, col = pos // GRID_W, pos % GRID_W
    inv = ROPE_BASE ** (-jnp.arange(0, ROPE_AXIS, 2, dtype=F32) / ROPE_AXIS)
    ang = jnp.concatenate([row.astype(F32)[:, None] * inv, col.astype(F32)[:, None] * inv], axis=-1)
    cos = jnp.concatenate([jnp.ones((n_ctx, ROPE_AXIS), F32), jnp.cos(ang)], axis=0)
    sin = jnp.concatenate([jnp.zeros((n_ctx, ROPE_AXIS), F32), jnp.sin(ang)], axis=0)
    reps = BRANCH_W // DIFF_QK
    cos_l = jnp.tile(jnp.concatenate([cos, cos], axis=-1), (1, reps))
    sin_l = jnp.tile(jnp.concatenate([-sin, sin], axis=-1), (1, reps))
    return cos_l, sin_l


def _rope_prep_kernel(x_ref, cos_ref, sin_ref, q_ref, kt_ref, vx_ref):
    lane = _iota((ROW_TILE, BRANCH_W), 1)
    first_half = (lane % DIFF_QK) < ROPE_AXIS
    cos = cos_ref[...]
    sin = sin_ref[...]

    def rope(x):
        partner = jnp.where(first_half, pltpu.roll(x, BRANCH_W - ROPE_AXIS, 1), pltpu.roll(x, ROPE_AXIS, 1))
        return x * cos + partner * sin

    q = rope(x_ref[:, 0:BRANCH_W].astype(F32)) * (DIFF_QK ** -0.5 * LOG2E)
    k = rope(x_ref[:, BRANCH_W:2 * BRANCH_W].astype(F32))
    q_ref[...] = q.astype(q_ref.dtype)
    kt_ref[...] = jnp.transpose(k).astype(kt_ref.dtype)
    v = x_ref[:, 2 * BRANCH_W:]
    src = _iota((BRANCH_W, VX_W), 0)
    dst = _iota((BRANCH_W, VX_W), 1)
    ones_col = jnp.where(_iota((ROW_TILE, VX_W), 1) == HEAD_DIM, 1.0, 0.0)
    for h in range(N_HEADS):
        sel = jnp.where(jnp.logical_and(src == h * HEAD_DIM + dst, dst < HEAD_DIM), 1.0, 0.0).astype(BF16)
        vx_ref[h] = (_dot(v, sel) + ones_col).astype(vx_ref.dtype)


def _rope_prep(proj, cos_l, sin_l):
    bsz, t, _ = proj.shape
    nt = t // ROW_TILE
    kvt = _kv_tile(t)
    r = kvt // ROW_TILE
    qkv_blk = BLK_DIFF_QKV * BRANCH_W // QKV_W
    tab = pl.BlockSpec((ROW_TILE, BRANCH_W), lambda b, i: (i, 0))
    return pl.pallas_call(
        _rope_prep_kernel,
        grid=(bsz, nt),
        in_specs=[pl.BlockSpec((None, ROW_TILE, QKV_W), lambda b, i: (b, i, qkv_blk)), tab, tab],
        out_specs=[pl.BlockSpec((None, ROW_TILE, BRANCH_W), lambda b, i: (b, i, 0)),
                   pl.BlockSpec((None, None, BRANCH_W, ROW_TILE), lambda b, i: (b, i // r, 0, i % r)),
                   pl.BlockSpec((None, None, N_HEADS, ROW_TILE, VX_W), lambda b, i: (b, i // r, 0, i % r, 0))],
        out_shape=[jax.ShapeDtypeStruct((bsz, t, BRANCH_W), BF16),
                   jax.ShapeDtypeStruct((bsz, t // kvt, BRANCH_W, kvt), BF16),
                   jax.ShapeDtypeStruct((bsz, t // kvt, N_HEADS, kvt, VX_W), BF16)],
        compiler_params=_cparams("parallel", "parallel"),
        name="rope_prep",
    )(proj, cos_l, sin_l)


N_CHAINS = 2 * N_HEADS


def _attn_kernel(q_ref, kt_ref, vx_ref, lp_ref, o_ref, qm_scr, s0_scr, s1_scr, m0_scr, m1_scr, al0_scr, al1_scr,
                 acc_scr, *, nk, lam_init):
    s_bufs, m_bufs, al_bufs = (s0_scr, s1_scr), (m0_scr, m1_scr), (al0_scr, al1_scr)
    q = q_ref[...]
    lane = _iota(q.shape, 1)
    zero = jnp.zeros_like(q)
    for c in range(N_CHAINS):
        lo = c * DIFF_QK
        qm_scr[c] = jnp.where(jnp.logical_and(lane >= lo, lane < lo + DIFF_QK), q, zero)
    acc_scr[...] = jnp.zeros_like(acc_scr)

    def stage_a(j, slot, first=False):
        kt = kt_ref[j]
        for c in range(N_CHAINS):
            s = _dot(qm_scr[c], kt)
            s_bufs[slot][c] = s
            row_max = jnp.max(s, axis=1, keepdims=True)
            if first:
                m_bufs[slot][c] = row_max
                al_bufs[slot][c] = jnp.zeros_like(row_max)
            else:
                m_old = m_bufs[1 - slot][c]
                m_new = jnp.maximum(m_old, row_max)
                m_bufs[slot][c] = m_new
                al_bufs[slot][c] = jnp.exp2(m_old - m_new)

    def stage_b(j, slot):
        for c in range(N_CHAINS):
            p = jnp.exp2(s_bufs[slot][c] - m_bufs[slot][c]).astype(BF16)
            acc_scr[c] = acc_scr[c] * al_bufs[slot][c] + _dot(p, vx_ref[j, c // 2])

    stage_a(0, 0, first=True)

    def body(i, carry):
        j = 2 * i + 1
        stage_a(j, 1)
        stage_b(j - 1, 0)
        stage_a(j + 1, 0)
        stage_b(j, 1)
        return carry

    n_pairs = (nk - 1) // 2
    lax.fori_loop(0, n_pairs, body, 0)
    if nk % 2 == 0:
        stage_a(nk - 1, 1)
        stage_b(nk - 2, 0)
    stage_b(nk - 1, (nk - 1) % 2)

    lp = lp_ref[...]
    grp_src = _iota((VX_W, VX_W), 0) // DIFF_QK
    prod1 = lp[0:1, :] * lp[1:2, :]
    prod2 = lp[2:3, :] * lp[3:4, :]
    src = _iota((VX_W, BRANCH_W), 0)
    dst = _iota((VX_W, BRANCH_W), 1)
    out = jnp.zeros((ROW_TILE, BRANCH_W), F32)
    for h in range(N_HEADS):
        pick = jnp.where(grp_src == h, 1.0, 0.0).astype(BF16)
        lam = jnp.exp(_dot_hl_r(prod1, pick)) - jnp.exp(_dot_hl_r(prod2, pick)) + lam_init
        a0 = acc_scr[2 * h]
        a1 = acc_scr[2 * h + 1]
        o_h = a0 / a0[:, HEAD_DIM:HEAD_DIM + 1] - lam * (a1 / a1[:, HEAD_DIM:HEAD_DIM + 1])
        place = jnp.where(jnp.logical_and(dst == src + h * HEAD_DIM, src < HEAD_DIM), 1.0, 0.0).astype(BF16)
        out = out + _dot(o_h.astype(BF16), place)
    o_ref[...] = out.astype(o_ref.dtype)


def _diff_attention(q_rot, kt, vx, lam_p, lam_init, q_tile0, n_q_tiles, kv_len):
    bsz = q_rot.shape[0]
    n_kv_arr, _, kvt = kt.shape[1:]
    tk = min(kv_len, kvt)
    assert kv_len % tk == 0 and kvt % tk == 0
    nk = kv_len // tk
    lp = jnp.zeros((8, VX_W), F32).at[:4].set(lam_p.reshape(4, N_HEADS * DIFF_QK))
    return pl.pallas_call(
        functools.partial(_attn_kernel, nk=nk, lam_init=lam_init),
        grid=(bsz, n_q_tiles),
        in_specs=[pl.BlockSpec((None, ROW_TILE, BRANCH_W), lambda b, i: (b, i + q_tile0, 0)),
                  pl.BlockSpec((None, nk, BRANCH_W, tk), lambda b, i: (b, 0, 0, 0), pipeline_mode=pl.Buffered(1)),
                  pl.BlockSpec((None, nk, N_HEADS, tk, VX_W), lambda b, i: (b, 0, 0, 0, 0),
                               pipeline_mode=pl.Buffered(1)),
                  pl.BlockSpec((8, VX_W), lambda b, i: (0, 0))],
        out_specs=pl.BlockSpec((None, ROW_TILE, BRANCH_W), lambda b, i: (b, i, 0)),
        out_shape=jax.ShapeDtypeStruct((bsz, n_q_tiles * ROW_TILE, BRANCH_W), BF16),
        scratch_shapes=[pltpu.VMEM((N_CHAINS, ROW_TILE, BRANCH_W), BF16)]
                       + [pltpu.VMEM((N_CHAINS, ROW_TILE, tk), F32)] * 2
                       + [pltpu.VMEM((N_CHAINS, ROW_TILE, 1), F32)] * 4
                       + [pltpu.VMEM((N_CHAINS, ROW_TILE, VX_W), F32)],
        compiler_params=_cparams("parallel", "arbitrary"),
        name="diff_attention",
    )(q_rot, kt, vx, lp)


def _diff_branch(proj, tables, lam_p, lam_init, n_ctx_tiles, with_ctx):
    t = proj.shape[1]
    q_rot, kt, vx = _rope_prep(proj, *tables)
    lat = _diff_attention(q_rot, kt, vx, lam_p, lam_init, n_ctx_tiles, t // ROW_TILE - n_ctx_tiles, t)
    if not with_ctx:
        return lat
    ctx = _diff_attention(q_rot, kt, vx, lam_p, lam_init, 0, n_ctx_tiles, n_ctx_tiles * ROW_TILE)
    return jnp.concatenate([ctx, lat], axis=1)


def _prepare(x, c, ctx, c_ctx, w_mod, b_mod):
    bsz, _, d = x.shape
    ct = ctx.shape[1]
    assert ct % ROW_TILE == 0 and x.shape[1] % ROW_TILE == 0
    xa = jnp.concatenate([ctx, x], axis=1)
    rows = 8 * ((bsz + 1 + 7) // 8)
    cond = jnp.zeros((rows, d), F32).at[:bsz].set(c).at[bsz].set(c_ctx)
    mods = _modulation(cond, w_mod, b_mod)
    return dict(xa=xa, mods=mods, n_ctx_tiles=ct // ROW_TILE, ct=ct)


def _layer_inproj(st, l, norm_mix, w_in):
    mods3 = st["mods"][l][:, None, :]
    return _inproj(st["xa"], mods3, norm_mix[l], _permute_w_in(w_in[l]), st["n_ctx_tiles"])


def _merge_kernel(x_ref, mod_ref, glf_ref, glb_ref, mlf_ref, mlb_ref, gdf_ref, gdb_ref, at_ref,
                  g_gla_ref, g_ml_ref, g_gd_ref, mg0_ref, mg1_ref, mg2_ref, mg3_ref,
                  hn_ref, wb_ref, wo_ref, o_ref, *, d, lam_init):
    ones_blk = _ones_blk()

    def head_norm(o, i):
        ms = _seg_sum(o * o, ones_blk) * (1.0 / HEAD_DIM)
        return o * lax.rsqrt(ms + EPS) * hn_ref[i:i + 1, :]

    f32 = lambda r: r[...].astype(F32)
    ys = [head_norm(f32(glf_ref) + f32(glb_ref), 0) * _silu(f32(g_gla_ref)),
          head_norm(f32(mlf_ref) + f32(mlb_ref), 1) * jax.nn.sigmoid(f32(g_ml_ref)),
          head_norm(f32(gdf_ref) + f32(gdb_ref), 2) * _silu(f32(g_gd_ref)),
          head_norm(f32(at_ref), 3) * (1.0 - lam_init)]
    acc = None
    for i, (y, mg_ref) in enumerate(zip(ys, (mg0_ref, mg1_ref, mg2_ref, mg3_ref))):
        term = jax.nn.sigmoid(f32(mg_ref)) * _dot(y.astype(BF16), wb_ref[i])
        acc = term if acc is None else acc + term
    out = _dot(acc.astype(BF16), wo_ref[...])
    g1 = mod_ref[...][:, 2 * d:3 * d]
    o_ref[...] = x_ref[...] + g1 * out


def _merge(xa, mods3, proj, scans, attn, hn, wb, wo, n_ctx_tiles, with_ctx, lam_init):
    bsz, t, d = xa.shape
    off = 0 if with_ctx else n_ctx_tiles
    nt = t // ROW_TILE - off
    rows = lambda col: pl.BlockSpec((None, ROW_TILE, BRANCH_W), lambda b, i: (b, i + off, col))
    gate = lambda j: pl.BlockSpec((None, ROW_TILE, d), lambda b, i: (b, i + off, BLK_GATES * BRANCH_W // d + j))
    const = lambda shape: pl.BlockSpec(shape, lambda b, i: (0,) * len(shape))
    return pl.pallas_call(
        functools.partial(_merge_kernel, d=d, lam_init=lam_init),
        grid=(bsz, nt),
        in_specs=[pl.BlockSpec((None, ROW_TILE, d), lambda b, i: (b, i + off, 0)),
                  pl.BlockSpec((None, 1, 6 * d), lambda b, i: (jnp.where(i + off < n_ctx_tiles, bsz, b), 0, 0))]
                 + [rows(0)] * 6
                 + [pl.BlockSpec((None, ROW_TILE, BRANCH_W), lambda b, i: (b, i, 0))]
                 + [rows(BLK_GLA_G), rows(BLK_ML_G), rows(BLK_GDN_G)]
                 + [gate(j) for j in range(N_BRANCH)]
                 + [const((8, BRANCH_W)), const((N_BRANCH, BRANCH_W, d)), const((d, d))],
        out_specs=pl.BlockSpec((None, ROW_TILE, d), lambda b, i: (b, i, 0)),
        out_shape=jax.ShapeDtypeStruct((bsz, nt * ROW_TILE, d), F32),
        compiler_params=_cparams("parallel", "parallel"),
        name="merge",
    )(xa, mods3, *scans, attn, proj, proj, proj, proj, proj, proj, proj, hn, wb, wo)


GATE_LANES = 128


def _router_gates_t(hf, rw_ref, rb_ref):
    h_hi, h_lo = _split_hl(hf)
    w_hi, w_lo = _split_hl(rw_ref[...])
    logits = _dot_nt(w_hi, h_hi) + _dot_nt(w_hi, h_lo) + _dot_nt(w_lo, h_hi)
    scores = jax.nn.sigmoid(logits)
    sel = scores + rb_ref[...]
    s = [sel[e:e + 1, :] for e in range(N_EXPERTS)]
    sc = [scores[e:e + 1, :] for e in range(N_EXPERTS)]
    grp = []
    for g in range(N_GROUPS):
        a, b, c, dd = s[4 * g:4 * g + 4]
        grp.append(functools.reduce(jnp.maximum, [a + b, a + c, a + dd, b + c, b + dd, c + dd]))
    gmax = functools.reduce(jnp.maximum, grp)
    chosen, taken = [], None
    for g in range(N_GROUPS):
        hit = grp[g] == gmax
        if taken is not None:
            hit = jnp.logical_and(hit, jnp.logical_not(taken))
        taken = hit if taken is None else jnp.logical_or(taken, hit)
        chosen.append(hit)
    ms = [jnp.where(chosen[e // EXPERTS_PER_GROUP], s[e], NEG_BIG) for e in range(N_EXPERTS)]

    def first_argmax(vals):
        top = functools.reduce(jnp.maximum, vals)
        hits, seen = [], None
        for v in vals:
            hit = v == top
            if seen is not None:
                hit = jnp.logical_and(hit, jnp.logical_not(seen))
            seen = hit if seen is None else jnp.logical_or(seen, hit)
            hits.append(hit)
        return hits

    oh1 = first_argmax(ms)
    oh2 = first_argmax([jnp.where(o, NEG_BIG, v) for o, v in zip(oh1, ms)])
    zero = jnp.zeros_like(sc[0])
    w1 = functools.reduce(jnp.add, [jnp.where(o, v, zero) for o, v in zip(oh1, sc)])
    w2 = functools.reduce(jnp.add, [jnp.where(o, v, zero) for o, v in zip(oh2, sc)])
    tot = w1 + w2
    rows = [jnp.where(o1, w1 / tot, zero) + jnp.where(o2, w2 / tot, zero) for o1, o2 in zip(oh1, oh2)]
    return jnp.concatenate(rows, axis=0)


def _moe_dense_kernel(x_ref, *rest, d, bsz):
    mod_refs = rest[:bsz]
    g_ref, rw_ref, rb_ref, wg_ref, wu_ref, wd_ref, o_ref, h_scr, gate_scr, acc_scr = rest[bsz:]
    e = pl.program_id(1)

    @pl.when(e == 0)
    def _():
        eye = jnp.where(_iota((N_EXPERTS, GATE_LANES), 0) == _iota((N_EXPERTS, GATE_LANES), 1), 1.0, 0.0).astype(BF16)
        for b in range(bsz):
            rows = slice(b * ROW_TILE, (b + 1) * ROW_TILE)
            mod = mod_refs[b][...]
            hf = _rms_mod(x_ref[b], g_ref[...], mod[:, 3 * d:4 * d], mod[:, 4 * d:5 * d])
            h_scr[rows, :] = hf.astype(BF16)
            g_hi, g_lo = _split_hl(_router_gates_t(hf, rw_ref, rb_ref))
            gate_scr[rows, :] = _dot_tn(g_hi, eye) + _dot_tn(g_lo, eye)
        acc_scr[...] = jnp.zeros_like(acc_scr)

    h = h_scr[...]
    pick = jnp.where(_iota((GATE_LANES, GATE_LANES), 0) == e, 1.0, 0.0).astype(BF16)
    ge = _dot_hl_r(gate_scr[...], pick)[:, 0:1]
    act = (_silu(_dot(h, wg_ref[...])) * _dot(h, wu_ref[...])).astype(BF16)
    acc_scr[...] += ge * _dot(act, wd_ref[...])

    @pl.when(e == pl.num_programs(1) - 1)
    def _():
        for b in range(bsz):
            g2 = mod_refs[b][...][:, 5 * d:6 * d]
            o_ref[b] = x_ref[b] + g2 * acc_scr[b * ROW_TILE:(b + 1) * ROW_TILE, :]


def _moe_dense(xa, mods3, g, rw_t, rb, wg, wu, wd, n_ctx_tiles):
    bsz, t, d = xa.shape
    ne, _, de = wg.shape
    mod_spec = lambda b: pl.BlockSpec((None, 1, 6 * d), lambda i, e: (jnp.where(i < n_ctx_tiles, bsz, b), 0, 0))
    return pl.pallas_call(
        functools.partial(_moe_dense_kernel, d=d, bsz=bsz),
        grid=(t // ROW_TILE, ne),
        in_specs=[pl.BlockSpec((bsz, ROW_TILE, d), lambda i, e: (0, i, 0))]
                 + [mod_spec(b) for b in range(bsz)]
                 + [pl.BlockSpec((1, d), lambda i, e: (0, 0)),
                    pl.BlockSpec((ne, d), lambda i, e: (0, 0)),
                    pl.BlockSpec((ne, ROW_TILE), lambda i, e: (0, 0)),
                    pl.BlockSpec((None, d, de), lambda i, e: (e, 0, 0)),
                    pl.BlockSpec((None, d, de), lambda i, e: (e, 0, 0)),
                    pl.BlockSpec((None, de, d), lambda i, e: (e, 0, 0))],
        out_specs=pl.BlockSpec((bsz, ROW_TILE, d), lambda i, e: (0, i, 0)),
        out_shape=jax.ShapeDtypeStruct((bsz, t, d), F32),
        scratch_shapes=[pltpu.VMEM((bsz * ROW_TILE, d), BF16), pltpu.VMEM((bsz * ROW_TILE, GATE_LANES), F32),
                        pltpu.VMEM((bsz * ROW_TILE, d), F32)],
        compiler_params=_cparams("parallel", "arbitrary"),
        name="moe_dense",
    )(xa, *([mods3] * bsz), g.reshape(1, d), rw_t, rb, wg, wu, wd)


def _final_norm_kernel(x_ref, g_ref, o_ref):
    x = x_ref[...]
    o_ref[...] = x * lax.rsqrt(jnp.mean(x * x, axis=-1, keepdims=True) + EPS) * g_ref[...]


def _final_norm(x, g):
    bsz, t, d = x.shape
    return pl.pallas_call(
        _final_norm_kernel,
        grid=(bsz, t // ROW_TILE),
        in_specs=[pl.BlockSpec((None, ROW_TILE, d), lambda b, i: (b, i, 0)), pl.BlockSpec((1, d), lambda b, i: (0, 0))],
        out_specs=pl.BlockSpec((None, ROW_TILE, d), lambda b, i: (b, i, 0)),
        out_shape=jax.ShapeDtypeStruct((bsz, t, d), F32),
        compiler_params=_cparams("parallel", "parallel"),
        name="final_norm",
    )(x, g.reshape(1, d))


def kernel(x, c, ctx, c_ctx, w_mod, b_mod, norm_mix, norm_ffn, w_in, gla_a2, gla_ab, mlstm_gate_b, gdn_conv, gdn_a_log, gdn_dt_bias, diff_lambda, head_norm, w_branch, w_out, router_w, router_b, w_gate, w_up, w_down, norm_final):
    depth = w_in.shape[0]
    st = _prepare(x, c, ctx, c_ctx, w_mod, b_mod)
    n_ctx_tiles = st["n_ctx_tiles"]
    tables = _rope_tables(x.shape[1], st["ct"])
    rw_t = router_w.T
    rb = jnp.broadcast_to(router_b.astype(F32)[:, None], (N_EXPERTS, ROW_TILE))
    for l in range(depth):
        with_ctx = l < depth - 1
        lam_init = 0.8 - 0.6 * math.exp(-0.3 * l)
        mods3 = st["mods"][l][:, None, :]
        proj = _layer_inproj(st, l, norm_mix, w_in)
        scans = (*_gla_branch(proj, gla_a2[l], gla_ab[l]),
                 *_mlstm_branch(proj, mlstm_gate_b[l]),
                 *_gdn_branch(proj, gdn_conv[l], gdn_a_log[l], gdn_dt_bias[l], n_ctx_tiles))
        attn = _diff_branch(proj, tables, diff_lambda[l], lam_init, n_ctx_tiles, with_ctx)
        hn = jnp.zeros((8, BRANCH_W), F32).at[:N_BRANCH].set(head_norm[l])
        xa = _merge(st["xa"], mods3, proj, scans, attn, hn, w_branch[l].astype(BF16), w_out[l].astype(BF16),
                    n_ctx_tiles, with_ctx, lam_init)
        if not with_ctx:
            n_ctx_tiles = 0
        xa = _moe_dense(xa, mods3, norm_ffn[l], rw_t, rb, w_gate[l].astype(BF16), w_up[l].astype(BF16),
                        w_down[l].astype(BF16), n_ctx_tiles)
        st = dict(st, xa=xa, n_ctx_tiles=n_ctx_tiles)
    return _final_norm(st["xa"], norm_final)
```

```python
import functools
import math

import numpy as np
import jax
import jax.numpy as jnp
from jax import lax
from jax.experimental import pallas as pl
from jax.experimental.pallas import tpu as pltpu

N_HEADS = 4
HEAD_DIM = 64
BRANCH_W = N_HEADS * HEAD_DIM
CHUNK = 64
GLA_RANK = 16
GLA_NORMALIZER = 16.0
CONV_W = 5
DIFF_QK = HEAD_DIM // 2
ROPE_AXIS = DIFF_QK // 2
ROPE_BASE = 10000.0
GRID_W = 64
N_EXPERTS = 16
N_GROUPS = 4
EXPERTS_PER_GROUP = 4
EPS = 1e-6
N_BRANCH = 4

ROW_TILE = 256
PROJ_COLS = 8192
BLK_GDN_QKV, BLK_DIFF_QKV, BLK_GDN_G = 0, 3, 6
BLK_GLA_Q, BLK_GLA_K, BLK_GLA_V, BLK_GLA_G = 7, 8, 9, 10
BLK_ML_Q, BLK_ML_K, BLK_ML_V, BLK_ML_G = 11, 28, 29, 30
BLK_GATES = 12
SMALL_BLOCK = 31
SM_GLA_RF, SM_GLA_RB, SM_ML_F, SM_ML_B, SM_GD_F, SM_GD_B = 0, 16, 32, 40, 48, 56
NEG_BIG = -1e30
VMEM_LIMIT = 56 * 1024 * 1024

F32 = jnp.float32
BF16 = jnp.bfloat16


def _cparams(*sem):
    return pltpu.CompilerParams(dimension_semantics=sem, vmem_limit_bytes=VMEM_LIMIT)


def _dot(a, b):
    return jnp.dot(a, b, preferred_element_type=F32)


def _dot_nt(a, b):
    return lax.dot_general(a, b, (((1,), (1,)), ((), ())), preferred_element_type=F32)


def _dot_tn(a, b):
    return lax.dot_general(a, b, (((0,), (0,)), ((), ())), preferred_element_type=F32)


def _split_hl(x):
    hi = x.astype(BF16)
    lo = (x - hi.astype(F32)).astype(BF16)
    return hi, lo


def _dot_hl(a, x):
    hi, lo = _split_hl(x)
    return _dot(a, hi) + _dot(a, lo)


def _dot_hl_r(x, a):
    hi, lo = _split_hl(x)
    return _dot(hi, a) + _dot(lo, a)


def _iota(shape, dim):
    return lax.broadcasted_iota(jnp.int32, shape, dim)


def _softplus(x):
    return jnp.maximum(x, 0.0) + jnp.log(1.0 + jnp.exp(-jnp.abs(x)))


def _log_sigmoid(x):
    return -_softplus(-x)


def _silu(x):
    return x * jax.nn.sigmoid(x)


def _lane_head(shape):
    return _iota(shape, len(shape) - 1) // HEAD_DIM


def _stack4(x):
    lh = _lane_head(x.shape)
    zero = jnp.zeros_like(x)
    return jnp.concatenate([jnp.where(lh == h, x, zero) for h in range(N_HEADS)], axis=0)


def _blk_mask():
    r = _iota((BRANCH_W, BRANCH_W), 0) // HEAD_DIM
    c = _iota((BRANCH_W, BRANCH_W), 1) // HEAD_DIM
    return r == c


def _ones_blk():
    return jnp.where(_blk_mask(), 1.0, 0.0).astype(BF16)


def _tri_hs(reverse, strict):
    t = _iota((CHUNK, BRANCH_W), 0)
    s = _iota((CHUNK, BRANCH_W), 1) % CHUNK
    if reverse:
        return (s > t) if strict else (s >= t)
    return (s < t) if strict else (s <= t)


def _eye_hs():
    t = _iota((CHUNK, BRANCH_W), 0)
    s = _iota((CHUNK, BRANCH_W), 1) % CHUNK
    return t == s


def _cum_mat(reverse):
    t = _iota((CHUNK, CHUNK), 0)
    s = _iota((CHUNK, CHUNK), 1)
    m = (s >= t) if reverse else (s <= t)
    return jnp.where(m, 1.0, 0.0).astype(BF16)


def _row_of(col_rep):
    return jnp.sum(jnp.where(_eye_hs(), col_rep, 0.0), axis=0, keepdims=True)


def _seg_max(x):
    lane = _iota(x.shape, 1)
    n = x.shape[1]
    for sh in (1, 2, 4, 8, 16, 32):
        up = pltpu.roll(x, n - sh, 1)
        dn = pltpu.roll(x, sh, 1)
        x = jnp.maximum(x, jnp.where((lane & sh) == 0, up, dn))
    return x


def _seg_sum(x, ones_blk):
    return _dot_hl_r(x, ones_blk)


def _mod_kernel(c_ref, w_ref, b_ref, o_ref):
    a = _silu(c_ref[...]).astype(BF16)
    o_ref[...] = _dot(a, w_ref[...].astype(BF16)) + b_ref[...]


def _modulation(cond, w_mod, b_mod):
    depth, d, n = w_mod.shape
    r = cond.shape[0]
    tn = 1536
    return pl.pallas_call(
        _mod_kernel,
        grid=(depth, n // tn),
        in_specs=[pl.BlockSpec((r, d), lambda l, j: (0, 0)),
                  pl.BlockSpec((None, d, tn), lambda l, j: (l, 0, j)),
                  pl.BlockSpec((None, 1, tn), lambda l, j: (l, 0, j))],
        out_specs=pl.BlockSpec((None, r, tn), lambda l, j: (l, 0, j)),
        out_shape=jax.ShapeDtypeStruct((depth, r, n), F32),
        compiler_params=_cparams("parallel", "parallel"),
        name="modulation",
    )(cond, w_mod, b_mod.reshape(depth, 1, n))


def _rms_mod(x, g, shift, scale):
    y = x * lax.rsqrt(jnp.mean(x * x, axis=-1, keepdims=True) + EPS)
    return (y * g) * (1.0 + scale) + shift


def _inproj_kernel(x_ref, mod_ref, g_ref, w_ref, o_ref, *, d, n_chunk):
    mod = mod_ref[...]
    h = _rms_mod(x_ref[...], g_ref[...], mod[:, 0:d], mod[:, d:2 * d]).astype(BF16)
    for n0 in range(0, PROJ_COLS, n_chunk):
        o_ref[:, n0:n0 + n_chunk] = _dot(h, w_ref[:, n0:n0 + n_chunk]).astype(BF16)


def _mod_row_map(n_ctx_tiles, n_batch):
    return lambda b, i: (jnp.where(i < n_ctx_tiles, n_batch, b), 0, 0)


def _inproj(xa, mods3, g, w_perm, n_ctx_tiles):
    bsz, t, d = xa.shape
    return pl.pallas_call(
        functools.partial(_inproj_kernel, d=d, n_chunk=1024),
        grid=(bsz, t // ROW_TILE),
        in_specs=[pl.BlockSpec((None, ROW_TILE, d), lambda b, i: (b, i, 0)),
                  pl.BlockSpec((None, 1, 6 * d), _mod_row_map(n_ctx_tiles, bsz)),
                  pl.BlockSpec((1, d), lambda b, i: (0, 0)),
                  pl.BlockSpec((d, PROJ_COLS), lambda b, i: (0, 0), pipeline_mode=pl.Buffered(1))],
        out_specs=pl.BlockSpec((None, ROW_TILE, PROJ_COLS), lambda b, i: (b, i, 0)),
        out_shape=jax.ShapeDtypeStruct((bsz, t, PROJ_COLS), BF16),
        compiler_params=_cparams("parallel", "parallel"),
        name="inproj",
    )(xa, mods3, g.reshape(1, d), w_perm)


def _proj_perm():
    sizes = (256, 256, 256, 256, 16, 16, 256, 256, 256, 256, 8, 8, 256, 256, 256, 256, 8, 8, 256, 256, 256, 4096)
    off = np.concatenate([[0], np.cumsum(sizes)])
    seg = lambda i: np.arange(off[i], off[i + 1])
    wide = {BLK_GDN_QKV: 12, BLK_GDN_QKV + 1: 13, BLK_GDN_QKV + 2: 14, BLK_GDN_G: 15,
            BLK_DIFF_QKV: 18, BLK_DIFF_QKV + 1: 19, BLK_DIFF_QKV + 2: 20,
            BLK_GLA_Q: 0, BLK_GLA_K: 1, BLK_GLA_V: 2, BLK_GLA_G: 3,
            BLK_ML_Q: 6, BLK_ML_K: 7, BLK_ML_V: 8, BLK_ML_G: 9}
    perm = np.full((PROJ_COLS,), -1, np.int64)
    for j, i in wide.items():
        perm[j * BRANCH_W:(j + 1) * BRANCH_W] = seg(i)
    perm[BLK_GATES * BRANCH_W:BLK_GATES * BRANCH_W + 4096] = seg(21)
    base = SMALL_BLOCK * BRANCH_W
    for lane0, i in ((SM_GLA_RF, 4), (SM_GLA_RB, 5), (SM_ML_F, 10), (SM_ML_B, 11), (SM_GD_F, 16), (SM_GD_B, 17)):
        s = seg(i)
        perm[base + lane0:base + lane0 + len(s)] = s
    return perm


def _permute_w_in(w_in):
    perm = _proj_perm()
    cuts = [0] + [i for i in range(1, PROJ_COLS)
                  if (perm[i] < 0) != (perm[i - 1] < 0) or (perm[i] >= 0 and perm[i] != perm[i - 1] + 1)] + [PROJ_COLS]
    w = w_in.astype(BF16)
    runs = [jnp.zeros((w.shape[0], b - a), BF16) if perm[a] < 0 else w[:, perm[a]:perm[a] + b - a]
            for a, b in zip(cuts[:-1], cuts[1:])]
    return jnp.concatenate(runs, axis=-1)


SCAN_CHUNKS = ROW_TILE // CHUNK


def _scan_block_map(col, nb, reverse):
    if reverse:
        return lambda j: (0, jnp.where(j == 0, 0, nb - j), col)
    return lambda j: (0, j, col)


def _scan_steps(reverse, bsz):
    order = range(SCAN_CHUNKS - 1, -1, -1) if reverse else range(SCAN_CHUNKS)
    return [(b, slice(c * CHUNK, (c + 1) * CHUNK)) for c in order for b in range(bsz)]


def _scan_call(kernel_fn, name, arrays, cols, consts, scratch, reverse):
    bsz, t, _ = arrays[0].shape
    nb = t // ROW_TILE
    blk = lambda col: pl.BlockSpec((bsz, ROW_TILE, BRANCH_W), _scan_block_map(col, nb, reverse))
    const = lambda a: pl.BlockSpec(a.shape, lambda j: (0,) * a.ndim)
    return pl.pallas_call(
        functools.partial(kernel_fn, reverse=reverse, bsz=bsz),
        grid=(nb,),
        in_specs=[blk(c) for c in cols] + [const(a) for a in consts],
        out_specs=blk(0),
        out_shape=jax.ShapeDtypeStruct((bsz, t, BRANCH_W), BF16),
        scratch_shapes=[pltpu.VMEM((bsz,) + s, F32) for s in scratch],
        compiler_params=_cparams("arbitrary"),
        name=name + ("_bwd" if reverse else "_fwd"),
    )(*arrays, *consts)


def _expand_small(small, lane0, count):
    src = _iota((BRANCH_W, BRANCH_W), 0)
    dst_head = _iota((BRANCH_W, BRANCH_W), 1) // HEAD_DIM
    e = jnp.where(src == lane0 + dst_head, 1.0, 0.0).astype(BF16)
    return _dot(small, e)


def _gla_kernel(q_ref, k_ref, v_ref, sm_ref, a2_ref, ab_ref, o_ref, st_ref, *, reverse, bsz):
    @pl.when(pl.program_id(0) == 0)
    def _():
        st_ref[...] = jnp.zeros_like(st_ref)

    cum_mat = _cum_mat(reverse)
    tri = _tri_hs(reverse, strict=False)
    blk = _blk_mask()
    a2 = a2_ref[...]
    ab = ab_ref[...]
    steps = _scan_steps(reverse, bsz)
    la = [_log_sigmoid(_dot(sm_ref[b, rows, :], a2) + ab) / GLA_NORMALIZER for b, rows in steps]
    cum = [_dot_hl(cum_mat, x) for x in la]
    tot = [jnp.sum(x, axis=0, keepdims=True) for x in la]
    q_in = [(q_ref[b, rows, :].astype(F32) * (HEAD_DIM ** -0.5) * jnp.exp(c)).astype(BF16)
            for (b, rows), c in zip(steps, cum)]
    k_out = [(k_ref[b, rows, :].astype(F32) * jnp.exp(-c)).astype(BF16) for (b, rows), c in zip(steps, cum)]
    k_end = [(k_ref[b, rows, :].astype(F32) * jnp.exp(t - c)).astype(BF16)
             for (b, rows), c, t in zip(steps, cum, tot)]
    att = [jnp.where(tri, _dot_nt(qi, _stack4(ko)), 0.0).astype(BF16) for qi, ko in zip(q_in, k_out)]
    o_intra = [_dot(a, _stack4(v_ref[b, rows, :])) for (b, rows), a in zip(steps, att)]
    kv_t = [jnp.where(blk, _dot_tn(v_ref[b, rows, :], ke), 0.0) for (b, rows), ke in zip(steps, k_end)]
    for i, (b, rows) in enumerate(steps):
        st = st_ref[b]
        o_ref[b, rows, :] = (o_intra[i] + _dot_nt(q_in[i], st.astype(BF16))).astype(o_ref.dtype)
        st_ref[b] = st * jnp.exp(tot[i]) + kv_t[i]


def _gla_scan(proj, a2pad, ab, reverse):
    return _scan_call(_gla_kernel, "gla", [proj] * 4, [BLK_GLA_Q, BLK_GLA_K, BLK_GLA_V, SMALL_BLOCK],
                      [a2pad, ab], [(BRANCH_W, BRANCH_W)], reverse)


def _gla_params(a2, ab, d):
    lane0 = SM_GLA_RB if d else SM_GLA_RF
    pad = jnp.zeros((BRANCH_W, BRANCH_W), F32).at[lane0:lane0 + GLA_RANK].set(a2[d])
    return pad.astype(BF16), ab[d].reshape(1, BRANCH_W)


def _gla_branch(proj, a2, ab):
    return tuple(_gla_scan(proj, *_gla_params(a2, ab, d), reverse=bool(d)) for d in (0, 1))


def _mlstm_kernel(q_ref, k_ref, v_ref, sm_ref, bias_ref, o_ref, ct_ref, nm_ref, *, reverse, bsz):
    @pl.when(pl.program_id(0) == 0)
    def _():
        ct_ref[...] = jnp.zeros_like(ct_ref)
        nm_ref[...] = jnp.zeros_like(nm_ref)

    lane0 = SM_ML_B if reverse else SM_ML_F
    cum_mat = _cum_mat(reverse)
    tri = _tri_hs(reverse, strict=False)
    blk = _blk_mask()
    ones_blk = _ones_blk()
    b_i = bias_ref[0:1, :]
    b_f = bias_ref[1:2, :]
    steps = _scan_steps(reverse, bsz)
    each = lambda f, *ls: [f(*a) for a in zip(*ls)]
    sm = [sm_ref[b, rows, :] for b, rows in steps]
    q = [q_ref[b, rows, :] for b, rows in steps]
    i_pre = [_expand_small(x, lane0, N_HEADS) + b_i for x in sm]
    log_f = [_log_sigmoid(_expand_small(x, lane0 + N_HEADS, N_HEADS) + b_f) for x in sm]
    f_cum = [_dot_hl(cum_mat, x) for x in log_f]
    f_tot = [jnp.sum(x, axis=0, keepdims=True) for x in log_f]
    u = each(lambda i, f: i - f, i_pre, f_cum)
    lw = each(lambda t, x: t + x, f_tot, u)
    a_end = [jnp.max(x, axis=0, keepdims=True) for x in lw]
    k = [k_ref[b, rows, :].astype(F32) * (HEAD_DIM ** -0.5) for b, rows in steps]
    kw = each(lambda kk, x, a: kk * jnp.exp(x - a), k, lw, a_end)
    k_sum = [jnp.sum(x, axis=0, keepdims=True) for x in kw]
    kv_t = [jnp.where(blk, _dot_tn(v_ref[b, rows, :], x.astype(BF16)), 0.0) for (b, rows), x in zip(steps, kw)]
    log_d = each(lambda f, x: jnp.where(tri, f + _row_of(x), NEG_BIG), f_cum, u)
    mx = [_seg_max(x) for x in log_d]
    s = each(lambda qq, kk, ld, m: (_dot_nt(qq, _stack4(kk.astype(BF16))) * jnp.exp(ld - m)).astype(BF16),
             q, k, log_d, mx)
    num1 = [_dot(x, _stack4(v_ref[b, rows, :])) for (b, rows), x in zip(steps, s)]
    den1 = [_dot(x, ones_blk) for x in s]
    for i, (b, rows) in enumerate(steps):
        n_in = nm_ref[b, 0:1, :]
        m_in = nm_ref[b, 1:2, :]
        ct = ct_ref[b]
        g = f_cum[i] + m_in
        m_t = jnp.maximum(g, mx[i])
        e = jnp.exp(g - m_t)
        r = jnp.exp(mx[i] - m_t)
        num = r * num1[i] + e * _dot_nt(q[i], ct.astype(BF16))
        den = r * den1[i] + e * _seg_sum(q[i].astype(F32) * n_in, ones_blk)
        o_ref[b, rows, :] = (num / jnp.maximum(jnp.abs(den), jnp.exp(-m_t))).astype(o_ref.dtype)

        m_new = jnp.maximum(f_tot[i] + m_in, a_end[i])
        old = jnp.exp(f_tot[i] + m_in - m_new)
        new = jnp.exp(a_end[i] - m_new)
        ct_ref[b] = ct * old + kv_t[i] * new
        nm_ref[b, 0:1, :] = n_in * old + k_sum[i] * new
        nm_ref[b, 1:2, :] = m_new


def _mlstm_scan(proj, bias, reverse):
    return _scan_call(_mlstm_kernel, "mlstm", [proj] * 4, [BLK_ML_Q, BLK_ML_K, BLK_ML_V, SMALL_BLOCK],
                      [bias], [(BRANCH_W, BRANCH_W), (8, BRANCH_W)], reverse)


def _head_rows(vals):
    rows = [jnp.repeat(v.astype(F32), HEAD_DIM) for v in vals]
    rows += [jnp.zeros((BRANCH_W,), F32)] * (8 - len(rows))
    return jnp.stack(rows)


def _mlstm_branch(proj, gate_b):
    return tuple(_mlstm_scan(proj, _head_rows([gate_b[d, 0], gate_b[d, 1]]), reverse=bool(d)) for d in (0, 1))


HALO = 8
QKV_W = 3 * BRANCH_W


def _gdn_prep_kernel(prev_ref, cur_ref, next_ref, w_ref, o_ref, *, n_ctx_tiles, n_tiles):
    i = pl.program_id(1)
    has_prev = jnp.logical_and(i != 0, i != n_ctx_tiles).astype(F32)
    has_next = jnp.logical_and(i != n_ctx_tiles - 1, i != n_tiles - 1).astype(F32)
    padded = jnp.concatenate([prev_ref[...].astype(F32) * has_prev, cur_ref[...].astype(F32),
                              next_ref[...].astype(F32) * has_next], axis=0)
    w = w_ref[...]
    acc = jnp.zeros((ROW_TILE, QKV_W), F32)
    for j in range(CONV_W):
        off = HALO + j - CONV_W // 2
        acc = acc + padded[off:off + ROW_TILE, :] * w[j:j + 1, :]
    y = _silu(acc)
    ones_blk = _ones_blk()
    q = y[:, 0:BRANCH_W]
    k = y[:, BRANCH_W:2 * BRANCH_W]
    q = q * lax.rsqrt(_seg_sum(q * q, ones_blk) + EPS) * (HEAD_DIM ** -0.5)
    k = k * lax.rsqrt(_seg_sum(k * k, ones_blk) + EPS)
    o_ref[:, 0:BRANCH_W] = q.astype(o_ref.dtype)
    o_ref[:, BRANCH_W:2 * BRANCH_W] = k.astype(o_ref.dtype)
    o_ref[:, 2 * BRANCH_W:] = y[:, 2 * BRANCH_W:].astype(o_ref.dtype)


def _gdn_prep(proj, conv_w, n_ctx_tiles):
    bsz, t, _ = proj.shape
    nt = t // ROW_TILE
    per = ROW_TILE // HALO
    last = t // HALO - 1
    qkv_blk = BLK_GDN_QKV * BRANCH_W // QKV_W
    w8 = jnp.zeros((8, QKV_W), F32).at[:CONV_W].set(conv_w)
    return pl.pallas_call(
        functools.partial(_gdn_prep_kernel, n_ctx_tiles=n_ctx_tiles, n_tiles=nt),
        grid=(bsz, nt),
        in_specs=[pl.BlockSpec((None, HALO, QKV_W), lambda b, i: (b, jnp.maximum(i * per - 1, 0), qkv_blk)),
                  pl.BlockSpec((None, ROW_TILE, QKV_W), lambda b, i: (b, i, qkv_blk)),
                  pl.BlockSpec((None, HALO, QKV_W), lambda b, i: (b, jnp.minimum((i + 1) * per, last), qkv_blk)),
                  pl.BlockSpec((8, QKV_W), lambda b, i: (0, 0))],
        out_specs=pl.BlockSpec((None, ROW_TILE, QKV_W), lambda b, i: (b, i, 0)),
        out_shape=jax.ShapeDtypeStruct((bsz, t, QKV_W), BF16),
        compiler_params=_cparams("parallel", "parallel"),
        name="gdn_prep",
    )(proj, proj, proj, w8)


def _gdn_kernel(q_ref, k_ref, v_ref, sm_ref, par_ref, o_ref, s_ref, *, reverse, bsz):
    @pl.when(pl.program_id(0) == 0)
    def _():
        s_ref[...] = jnp.zeros_like(s_ref)

    lane0 = SM_GD_B if reverse else SM_GD_F
    cum_mat = _cum_mat(reverse)
    tri = _tri_hs(reverse, strict=False)
    tri_strict = _tri_hs(reverse, strict=True)
    blk = _blk_mask()
    eye = jnp.where(_eye_hs(), 1.0, 0.0)
    a_scale = jnp.exp(par_ref[0:1, :])
    dt_bias = par_ref[1:2, :]
    steps = _scan_steps(reverse, bsz)
    each = lambda f, *ls: [f(*a) for a in zip(*ls)]
    sm = [sm_ref[b, rows, :] for b, rows in steps]
    q = [q_ref[b, rows, :] for b, rows in steps]
    kb16 = [k_ref[b, rows, :] for b, rows in steps]
    beta = [jax.nn.sigmoid(_expand_small(x, lane0, N_HEADS)) for x in sm]
    g = [-a_scale * _softplus(_expand_small(x, lane0 + N_HEADS, N_HEADS) + dt_bias) for x in sm]
    cum = [_dot_hl(cum_mat, x) for x in g]
    tot = [jnp.sum(x, axis=0, keepdims=True) for x in g]
    gam = [jnp.where(tri, jnp.exp(jnp.where(tri, c - _row_of(c), 0.0)), 0.0) for c in cum]
    k_beta = each(lambda kk, bb: kk.astype(F32) * bb, kb16, beta)
    k4 = [_stack4(x) for x in kb16]
    a_hs = each(lambda kb, kk, gm: jnp.where(tri_strict, _dot_nt(kb.astype(BF16), kk) * gm, 0.0), k_beta, k4, gam)
    attn = each(lambda qq, kk, gm: (_dot_nt(qq, kk) * gm).astype(BF16), q, k4, gam)

    p = [-x for x in a_hs]
    t_inv = [eye + x for x in p]
    for _ in range(5):
        p = [_dot(x.astype(BF16), _stack4(x.astype(BF16))) for x in p]
        t_inv = each(lambda t, x: t + _dot(t.astype(BF16), _stack4(x.astype(BF16))), t_inv, p)
    w = each(lambda t, kb, c: _dot(t.astype(BF16), _stack4((kb * jnp.exp(c)).astype(BF16))).astype(BF16),
             t_inv, k_beta, cum)
    u = [_dot(t.astype(BF16), _stack4((v_ref[b, rows, :].astype(F32) * bb).astype(BF16)))
         for (b, rows), t, bb in zip(steps, t_inv, beta)]
    q_dec = each(lambda qq, c: (qq.astype(F32) * jnp.exp(c)).astype(BF16), q, cum)
    k_end = each(lambda kk, t, c: (kk.astype(F32) * jnp.exp(t - c)).astype(BF16), kb16, tot, cum)
    for i, (b, rows) in enumerate(steps):
        s = s_ref[b]
        sb = s.astype(BF16)
        v_new = (u[i] - _dot(w[i], sb)).astype(BF16)
        o_ref[b, rows, :] = (_dot(q_dec[i], sb) + _dot(attn[i], _stack4(v_new))).astype(o_ref.dtype)
        s_ref[b] = s * jnp.exp(tot[i]) + jnp.where(blk, _dot_tn(k_end[i], v_new), 0.0)


def _gdn_scan(gqkv, proj, par, reverse):
    return _scan_call(_gdn_kernel, "gdn", [gqkv, gqkv, gqkv, proj], [0, 1, 2, SMALL_BLOCK],
                      [par], [(BRANCH_W, BRANCH_W)], reverse)


def _gdn_branch(proj, conv_w, a_log, dt_bias, n_ctx_tiles):
    gqkv = _gdn_prep(proj, conv_w, n_ctx_tiles)
    return tuple(_gdn_scan(gqkv, proj, _head_rows([a_log[d], dt_bias[d]]), reverse=bool(d))
                 for d in (0, 1))


VX_W = 128
LOG2E = 1.4426950408889634


def _kv_tile(t):
    return next(k for k in (3 * ROW_TILE, 2 * ROW_TILE, ROW_TILE) if t % k == 0)


def _rope_tables(n_lat, n_ctx):
    pos = jnp.arange(n_lat)
    row, col = pos // GRID_W, pos % GRID_W
    inv = ROPE_BASE ** (-jnp.arange(0, ROPE_AXIS, 2, dtype=F32) / ROPE_AXIS)
    ang = jnp.concatenate([row.astype(F32)[:, None] * inv, col.astype(F32)[:, None] * inv], axis=-1)
    cos = jnp.concatenate([jnp.ones((n_ctx, ROPE_AXIS), F32), jnp.cos(ang)], axis=0)
    sin = jnp.concatenate([jnp.zeros((n_ctx, ROPE_AXIS), F32), jnp.sin(ang)], axis=0)
    reps = BRANCH_W // DIFF_QK
    cos_l = jnp.tile(jnp.concatenate([cos, cos], axis=-1), (1, reps))
    sin_l = jnp.tile(jnp.concatenate([-sin, sin], axis=-1), (1, reps))
    return cos_l, sin_l


def _rope_prep_kernel(x_ref, cos_ref, sin_ref, q_ref, kt_ref, vx_ref):
    lane = _iota((ROW_TILE, BRANCH_W), 1)
    first_half = (lane % DIFF_QK) < ROPE_AXIS
    cos = cos_ref[...]
    sin = sin_ref[...]

    def rope(x):
        partner = jnp.where(first_half, pltpu.roll(x, BRANCH_W - ROPE_AXIS, 1), pltpu.roll(x, ROPE_AXIS, 1))
        return x * cos + partner * sin

    q = rope(x_ref[:, 0:BRANCH_W].astype(F32)) * (DIFF_QK ** -0.5 * LOG2E)
    k = rope(x_ref[:, BRANCH_W:2 * BRANCH_W].astype(F32))
    q_ref[...] = q.astype(q_ref.dtype)
    kt_ref[...] = jnp.transpose(k).astype(kt_ref.dtype)
    v = x_ref[:, 2 * BRANCH_W:]
    src = _iota((BRANCH_W, VX_W), 0)
    dst = _iota((BRANCH_W, VX_W), 1)
    ones_col = jnp.where(_iota((ROW_TILE, VX_W), 1) == HEAD_DIM, 1.0, 0.0)
    for h in range(N_HEADS):
        sel = jnp.where(jnp.logical_and(src == h * HEAD_DIM + dst, dst < HEAD_DIM), 1.0, 0.0).astype(BF16)
        vx_ref[h] = (_dot(v, sel) + ones_col).astype(vx_ref.dtype)


def _rope_prep(proj, cos_l, sin_l):
    bsz, t, _ = proj.shape
    nt = t // ROW_TILE
    kvt = _kv_tile(t)
    r = kvt // ROW_TILE
    qkv_blk = BLK_DIFF_QKV * BRANCH_W // QKV_W
    tab = pl.BlockSpec((ROW_TILE, BRANCH_W), lambda b, i: (i, 0))
    return pl.pallas_call(
        _rope_prep_kernel,
        grid=(bsz, nt),
        in_specs=[pl.BlockSpec((None, ROW_TILE, QKV_W), lambda b, i: (b, i, qkv_blk)), tab, tab],
        out_specs=[pl.BlockSpec((None, ROW_TILE, BRANCH_W), lambda b, i: (b, i, 0)),
                   pl.BlockSpec((None, None, BRANCH_W, ROW_TILE), lambda b, i: (b, i // r, 0, i % r)),
                   pl.BlockSpec((None, None, N_HEADS, ROW_TILE, VX_W), lambda b, i: (b, i // r, 0, i % r, 0))],
        out_shape=[jax.ShapeDtypeStruct((bsz, t, BRANCH_W), BF16),
                   jax.ShapeDtypeStruct((bsz, t // kvt, BRANCH_W, kvt), BF16),
                   jax.ShapeDtypeStruct((bsz, t // kvt, N_HEADS, kvt, VX_W), BF16)],
        compiler_params=_cparams("parallel", "parallel"),
        name="rope_prep",
    )(proj, cos_l, sin_l)


N_CHAINS = 2 * N_HEADS


def _attn_kernel(q_ref, kt_ref, vx_ref, lp_ref, o_ref, qm_scr, s0_scr, s1_scr, m0_scr, m1_scr, al0_scr, al1_scr,
                 acc_scr, *, nk, lam_init):
    s_bufs, m_bufs, al_bufs = (s0_scr, s1_scr), (m0_scr, m1_scr), (al0_scr, al1_scr)
    q = q_ref[...]
    lane = _iota(q.shape, 1)
    zero = jnp.zeros_like(q)
    for c in range(N_CHAINS):
        lo = c * DIFF_QK
        qm_scr[c] = jnp.where(jnp.logical_and(lane >= lo, lane < lo + DIFF_QK), q, zero)
    acc_scr[...] = jnp.zeros_like(acc_scr)

    def stage_a(j, slot, first=False):
        kt = kt_ref[j]
        for c in range(N_CHAINS):
            s = _dot(qm_scr[c], kt)
            s_bufs[slot][c] = s
            row_max = jnp.max(s, axis=1, keepdims=True)
            if first:
                m_bufs[slot][c] = row_max
                al_bufs[slot][c] = jnp.zeros_like(row_max)
            else:
                m_old = m_bufs[1 - slot][c]
                m_new = jnp.maximum(m_old, row_max)
                m_bufs[slot][c] = m_new
                al_bufs[slot][c] = jnp.exp2(m_old - m_new)

    def stage_b(j, slot):
        for c in range(N_CHAINS):
            p = jnp.exp2(s_bufs[slot][c] - m_bufs[slot][c]).astype(BF16)
            acc_scr[c] = acc_scr[c] * al_bufs[slot][c] + _dot(p, vx_ref[j, c // 2])

    stage_a(0, 0, first=True)

    def body(i, carry):
        j = 2 * i + 1
        stage_a(j, 1)
        stage_b(j - 1, 0)
        stage_a(j + 1, 0)
        stage_b(j, 1)
        return carry

    n_pairs = (nk - 1) // 2
    lax.fori_loop(0, n_pairs, body, 0)
    if nk % 2 == 0:
        stage_a(nk - 1, 1)
        stage_b(nk - 2, 0)
    stage_b(nk - 1, (nk - 1) % 2)

    lp = lp_ref[...]
    grp_src = _iota((VX_W, VX_W), 0) // DIFF_QK
    prod1 = lp[0:1, :] * lp[1:2, :]
    prod2 = lp[2:3, :] * lp[3:4, :]
    src = _iota((VX_W, BRANCH_W), 0)
    dst = _iota((VX_W, BRANCH_W), 1)
    out = jnp.zeros((ROW_TILE, BRANCH_W), F32)
    for h in range(N_HEADS):
        pick = jnp.where(grp_src == h, 1.0, 0.0).astype(BF16)
        lam = jnp.exp(_dot_hl_r(prod1, pick)) - jnp.exp(_dot_hl_r(prod2, pick)) + lam_init
        a0 = acc_scr[2 * h]
        a1 = acc_scr[2 * h + 1]
        o_h = a0 / a0[:, HEAD_DIM:HEAD_DIM + 1] - lam * (a1 / a1[:, HEAD_DIM:HEAD_DIM + 1])
        place = jnp.where(jnp.logical_and(dst == src + h * HEAD_DIM, src < HEAD_DIM), 1.0, 0.0).astype(BF16)
        out = out + _dot(o_h.astype(BF16), place)
    o_ref[...] = out.astype(o_ref.dtype)


def _diff_attention(q_rot, kt, vx, lam_p, lam_init, q_tile0, n_q_tiles, kv_len):
    bsz = q_rot.shape[0]
    n_kv_arr, _, kvt = kt.shape[1:]
    tk = min(kv_len, kvt)
    assert kv_len % tk == 0 and kvt % tk == 0
    nk = kv_len // tk
    lp = jnp.zeros((8, VX_W), F32).at[:4].set(lam_p.reshape(4, N_HEADS * DIFF_QK))
    return pl.pallas_call(
        functools.partial(_attn_kernel, nk=nk, lam_init=lam_init),
        grid=(bsz, n_q_tiles),
        in_specs=[pl.BlockSpec((None, ROW_TILE, BRANCH_W), lambda b, i: (b, i + q_tile0, 0)),
                  pl.BlockSpec((None, nk, BRANCH_W, tk), lambda b, i: (b, 0, 0, 0), pipeline_mode=pl.Buffered(1)),
                  pl.BlockSpec((None, nk, N_HEADS, tk, VX_W), lambda b, i: (b, 0, 0, 0, 0),
                               pipeline_mode=pl.Buffered(1)),
                  pl.BlockSpec((8, VX_W), lambda b, i: (0, 0))],
        out_specs=pl.BlockSpec((None, ROW_TILE, BRANCH_W), lambda b, i: (b, i, 0)),
        out_shape=jax.ShapeDtypeStruct((bsz, n_q_tiles * ROW_TILE, BRANCH_W), BF16),
        scratch_shapes=[pltpu.VMEM((N_CHAINS, ROW_TILE, BRANCH_W), BF16)]
                       + [pltpu.VMEM((N_CHAINS, ROW_TILE, tk), F32)] * 2
                       + [pltpu.VMEM((N_CHAINS, ROW_TILE, 1), F32)] * 4
                       + [pltpu.VMEM((N_CHAINS, ROW_TILE, VX_W), F32)],
        compiler_params=_cparams("parallel", "arbitrary"),
        name="diff_attention",
    )(q_rot, kt, vx, lp)


def _diff_branch(proj, tables, lam_p, lam_init, n_ctx_tiles, with_ctx):
    t = proj.shape[1]
    q_rot, kt, vx = _rope_prep(proj, *tables)
    lat = _diff_attention(q_rot, kt, vx, lam_p, lam_init, n_ctx_tiles, t // ROW_TILE - n_ctx_tiles, t)
    if not with_ctx:
        return lat
    ctx = _diff_attention(q_rot, kt, vx, lam_p, lam_init, 0, n_ctx_tiles, n_ctx_tiles * ROW_TILE)
    return jnp.concatenate([ctx, lat], axis=1)


def _prepare(x, c, ctx, c_ctx, w_mod, b_mod):
    bsz, _, d = x.shape
    ct = ctx.shape[1]
    assert ct % ROW_TILE == 0 and x.shape[1] % ROW_TILE == 0
    xa = jnp.concatenate([ctx, x], axis=1)
    rows = 8 * ((bsz + 1 + 7) // 8)
    cond = jnp.zeros((rows, d), F32).at[:bsz].set(c).at[bsz].set(c_ctx)
    mods = _modulation(cond, w_mod, b_mod)
    return dict(xa=xa, mods=mods, n_ctx_tiles=ct // ROW_TILE, ct=ct)


def _layer_inproj(st, l, norm_mix, w_in):
    mods3 = st["mods"][l][:, None, :]
    return _inproj(st["xa"], mods3, norm_mix[l], _permute_w_in(w_in[l]), st["n_ctx_tiles"])


def _merge_kernel(x_ref, mod_ref, glf_ref, glb_ref, mlf_ref, mlb_ref, gdf_ref, gdb_ref, at_ref,
                  g_gla_ref, g_ml_ref, g_gd_ref, mg0_ref, mg1_ref, mg2_ref, mg3_ref,
                  hn_ref, wb_ref, wo_ref, o_ref, *, d, lam_init):
    ones_blk = _ones_blk()

    def head_norm(o, i):
        ms = _seg_sum(o * o, ones_blk) * (1.0 / HEAD_DIM)
        return o * lax.rsqrt(ms + EPS) * hn_ref[i:i + 1, :]

    f32 = lambda r: r[...].astype(F32)
    ys = [head_norm(f32(glf_ref) + f32(glb_ref), 0) * _silu(f32(g_gla_ref)),
          head_norm(f32(mlf_ref) + f32(mlb_ref), 1) * jax.nn.sigmoid(f32(g_ml_ref)),
          head_norm(f32(gdf_ref) + f32(gdb_ref), 2) * _silu(f32(g_gd_ref)),
          head_norm(f32(at_ref), 3) * (1.0 - lam_init)]
    acc = None
    for i, (y, mg_ref) in enumerate(zip(ys, (mg0_ref, mg1_ref, mg2_ref, mg3_ref))):
        term = jax.nn.sigmoid(f32(mg_ref)) * _dot(y.astype(BF16), wb_ref[i])
        acc = term if acc is None else acc + term
    out = _dot(acc.astype(BF16), wo_ref[...])
    g1 = mod_ref[...][:, 2 * d:3 * d]
    o_ref[...] = x_ref[...] + g1 * out


def _merge(xa, mods3, proj, scans, attn, hn, wb, wo, n_ctx_tiles, with_ctx, lam_init):
    bsz, t, d = xa.shape
    off = 0 if with_ctx else n_ctx_tiles
    nt = t // ROW_TILE - off
    rows = lambda col: pl.BlockSpec((None, ROW_TILE, BRANCH_W), lambda b, i: (b, i + off, col))
    gate = lambda j: pl.BlockSpec((None, ROW_TILE, d), lambda b, i: (b, i + off, BLK_GATES * BRANCH_W // d + j))
    const = lambda shape: pl.BlockSpec(shape, lambda b, i: (0,) * len(shape))
    return pl.pallas_call(
        functools.partial(_merge_kernel, d=d, lam_init=lam_init),
        grid=(bsz, nt),
        in_specs=[pl.BlockSpec((None, ROW_TILE, d), lambda b, i: (b, i + off, 0)),
                  pl.BlockSpec((None, 1, 6 * d), lambda b, i: (jnp.where(i + off < n_ctx_tiles, bsz, b), 0, 0))]
                 + [rows(0)] * 6
                 + [pl.BlockSpec((None, ROW_TILE, BRANCH_W), lambda b, i: (b, i, 0))]
                 + [rows(BLK_GLA_G), rows(BLK_ML_G), rows(BLK_GDN_G)]
                 + [gate(j) for j in range(N_BRANCH)]
                 + [const((8, BRANCH_W)), const((N_BRANCH, BRANCH_W, d)), const((d, d))],
        out_specs=pl.BlockSpec((None, ROW_TILE, d), lambda b, i: (b, i, 0)),
        out_shape=jax.ShapeDtypeStruct((bsz, nt * ROW_TILE, d), F32),
        compiler_params=_cparams("parallel", "parallel"),
        name="merge",
    )(xa, mods3, *scans, attn, proj, proj, proj, proj, proj, proj, proj, hn, wb, wo)


GATE_LANES = 128


def _router_gates_t(hf, rw_ref, rb_ref, as_gates=True):
    h_hi, h_lo = _split_hl(hf)
    w_hi, w_lo = _split_hl(rw_ref[...])
    logits = _dot_nt(w_hi, h_hi) + _dot_nt(w_hi, h_lo) + _dot_nt(w_lo, h_hi)
    scores = jax.nn.sigmoid(logits)
    sel = scores + rb_ref[...]
    s = [sel[e:e + 1, :] for e in range(N_EXPERTS)]
    sc = [scores[e:e + 1, :] for e in range(N_EXPERTS)]
    grp = []
    for g in range(N_GROUPS):
        a, b, c, dd = s[4 * g:4 * g + 4]
        grp.append(functools.reduce(jnp.maximum, [a + b, a + c, a + dd, b + c, b + dd, c + dd]))
    gmax = functools.reduce(jnp.maximum, grp)
    chosen, taken = [], None
    for g in range(N_GROUPS):
        hit = grp[g] == gmax
        if taken is not None:
            hit = jnp.logical_and(hit, jnp.logical_not(taken))
        taken = hit if taken is None else jnp.logical_or(taken, hit)
        chosen.append(hit)
    ms = [jnp.where(chosen[e // EXPERTS_PER_GROUP], s[e], NEG_BIG) for e in range(N_EXPERTS)]

    def first_argmax(vals):
        top = functools.reduce(jnp.maximum, vals)
        hits, seen = [], None
        for v in vals:
            hit = v == top
            if seen is not None:
                hit = jnp.logical_and(hit, jnp.logical_not(seen))
            seen = hit if seen is None else jnp.logical_or(seen, hit)
            hits.append(hit)
        return hits

    oh1 = first_argmax(ms)
    oh2 = first_argmax([jnp.where(o, NEG_BIG, v) for o, v in zip(oh1, ms)])
    zero = jnp.zeros_like(sc[0])
    w1 = functools.reduce(jnp.add, [jnp.where(o, v, zero) for o, v in zip(oh1, sc)])
    w2 = functools.reduce(jnp.add, [jnp.where(o, v, zero) for o, v in zip(oh2, sc)])
    tot = w1 + w2
    if not as_gates:
        ids = [functools.reduce(jnp.add, [jnp.where(o, float(e), 0.0) for e, o in enumerate(oh)]) for oh in (oh1, oh2)]
        return ids[0], ids[1], w1 / tot, w2 / tot
    rows = [jnp.where(o1, w1 / tot, zero) + jnp.where(o2, w2 / tot, zero) for o1, o2 in zip(oh1, oh2)]
    return jnp.concatenate(rows, axis=0)


def _moe_dense_kernel(x_ref, *rest, d, bsz):
    mod_refs = rest[:bsz]
    g_ref, rw_ref, rb_ref, wg_ref, wu_ref, wd_ref, o_ref, h_scr, gate_scr, acc_scr = rest[bsz:]
    e = pl.program_id(1)

    @pl.when(e == 0)
    def _():
        eye = jnp.where(_iota((N_EXPERTS, GATE_LANES), 0) == _iota((N_EXPERTS, GATE_LANES), 1), 1.0, 0.0).astype(BF16)
        for b in range(bsz):
            rows = slice(b * ROW_TILE, (b + 1) * ROW_TILE)
            mod = mod_refs[b][...]
            hf = _rms_mod(x_ref[b], g_ref[...], mod[:, 3 * d:4 * d], mod[:, 4 * d:5 * d])
            h_scr[rows, :] = hf.astype(BF16)
            g_hi, g_lo = _split_hl(_router_gates_t(hf, rw_ref, rb_ref))
            gate_scr[rows, :] = _dot_tn(g_hi, eye) + _dot_tn(g_lo, eye)
        acc_scr[...] = jnp.zeros_like(acc_scr)

    h = h_scr[...]
    pick = jnp.where(_iota((GATE_LANES, GATE_LANES), 0) == e, 1.0, 0.0).astype(BF16)
    ge = _dot_hl_r(gate_scr[...], pick)[:, 0:1]
    act = (_silu(_dot(h, wg_ref[...])) * _dot(h, wu_ref[...])).astype(BF16)
    acc_scr[...] += ge * _dot(act, wd_ref[...])

    @pl.when(e == pl.num_programs(1) - 1)
    def _():
        for b in range(bsz):
            g2 = mod_refs[b][...][:, 5 * d:6 * d]
            o_ref[b] = x_ref[b] + g2 * acc_scr[b * ROW_TILE:(b + 1) * ROW_TILE, :]


def _moe_dense(xa, mods3, g, rw_t, rb, wg, wu, wd, n_ctx_tiles):
    bsz, t, d = xa.shape
    ne, _, de = wg.shape
    mod_spec = lambda b: pl.BlockSpec((None, 1, 6 * d), lambda i, e: (jnp.where(i < n_ctx_tiles, bsz, b), 0, 0))
    return pl.pallas_call(
        functools.partial(_moe_dense_kernel, d=d, bsz=bsz),
        grid=(t // ROW_TILE, ne),
        in_specs=[pl.BlockSpec((bsz, ROW_TILE, d), lambda i, e: (0, i, 0))]
                 + [mod_spec(b) for b in range(bsz)]
                 + [pl.BlockSpec((1, d), lambda i, e: (0, 0)),
                    pl.BlockSpec((ne, d), lambda i, e: (0, 0)),
                    pl.BlockSpec((ne, ROW_TILE), lambda i, e: (0, 0)),
                    pl.BlockSpec((None, d, de), lambda i, e: (e, 0, 0)),
                    pl.BlockSpec((None, d, de), lambda i, e: (e, 0, 0)),
                    pl.BlockSpec((None, de, d), lambda i, e: (e, 0, 0))],
        out_specs=pl.BlockSpec((bsz, ROW_TILE, d), lambda i, e: (0, i, 0)),
        out_shape=jax.ShapeDtypeStruct((bsz, t, d), F32),
        scratch_shapes=[pltpu.VMEM((bsz * ROW_TILE, d), BF16), pltpu.VMEM((bsz * ROW_TILE, GATE_LANES), F32),
                        pltpu.VMEM((bsz * ROW_TILE, d), F32)],
        compiler_params=_cparams("parallel", "arbitrary"),
        name="moe_dense",
    )(xa, *([mods3] * bsz), g.reshape(1, d), rw_t, rb, wg, wu, wd)


TOK_ROWS = 8
FFN_TILE = ROW_TILE


def _store_token_tiles(ref, val):
    n = val.shape[0]
    for s in range(TOK_ROWS):
        ref[pl.ds(s, n, stride=TOK_ROWS), :] = val[:, s * 128:(s + 1) * 128]


def _load_token_tiles(ref, n):
    return jnp.concatenate([ref[pl.ds(s, n, stride=TOK_ROWS), :] for s in range(TOK_ROWS)], axis=1)


def _moe_router_kernel(x_ref, mod_ref, g_ref, rw_ref, rb_ref, h_ref, ids_ref, wcol_ref, *, d):
    mod = mod_ref[...]
    hf = _rms_mod(x_ref[...], g_ref[...], mod[:, 3 * d:4 * d], mod[:, 4 * d:5 * d])
    _store_token_tiles(h_ref, hf)
    e1, e2, w1, w2 = _router_gates_t(hf, rw_ref, rb_ref, as_gates=False)
    pad = jnp.zeros((6, ROW_TILE), F32)
    ids_ref[...] = jnp.concatenate([e1, e2, pad], axis=0).astype(jnp.int32)
    eye = jnp.where(_iota((8, GATE_LANES), 0) == _iota((8, GATE_LANES), 1), 1.0, 0.0).astype(BF16)
    w_hi, w_lo = _split_hl(jnp.concatenate([w1, w2, pad], axis=0))
    wcol_ref[...] = _dot_tn(w_hi, eye) + _dot_tn(w_lo, eye)


def _moe_router(xa, mods3, g, rw_t, rb, n_ctx_tiles):
    bsz, t, d = xa.shape
    nt = t // ROW_TILE
    n = bsz * t
    flat = lambda b, i: b * nt + i
    return pl.pallas_call(
        functools.partial(_moe_router_kernel, d=d),
        grid=(bsz, nt),
        in_specs=[pl.BlockSpec((None, ROW_TILE, d), lambda b, i: (b, i, 0)),
                  pl.BlockSpec((None, 1, 6 * d), _mod_row_map(n_ctx_tiles, bsz)),
                  pl.BlockSpec((1, d), lambda b, i: (0, 0)),
                  pl.BlockSpec((N_EXPERTS, d), lambda b, i: (0, 0)),
                  pl.BlockSpec((N_EXPERTS, ROW_TILE), lambda b, i: (0, 0))],
        out_specs=[pl.BlockSpec((ROW_TILE * TOK_ROWS, 128), lambda b, i: (flat(b, i), 0)),
                   pl.BlockSpec((8, ROW_TILE), lambda b, i: (0, flat(b, i))),
                   pl.BlockSpec((ROW_TILE, GATE_LANES), lambda b, i: (flat(b, i), 0))],
        out_shape=[jax.ShapeDtypeStruct((n * TOK_ROWS, 128), F32),
                   jax.ShapeDtypeStruct((8, n), jnp.int32),
                   jax.ShapeDtypeStruct((n, GATE_LANES), F32)],
        compiler_params=_cparams("parallel", "parallel"),
        name="moe_router",
    )(xa, mods3, g.reshape(1, d), rw_t, rb)


def _moe_plan(ids, n):
    n_tiles = -(-2 * n // FFN_TILE) + N_EXPERTS
    eid = ids[:2].reshape(2 * n)
    onehot = (eid[:, None] == jnp.arange(N_EXPERTS, dtype=jnp.int32)[None, :]).astype(jnp.int32)
    csum = jnp.cumsum(onehot, axis=0)
    rank = jnp.sum(csum * onehot, axis=1) - 1
    counts = csum[-1]
    padded = (counts + FFN_TILE - 1) // FFN_TILE * FFN_TILE
    ends = jnp.cumsum(padded)
    dest = (ends - padded)[eid] + rank
    tok = jnp.tile(jnp.arange(n, dtype=jnp.int32), 2)
    src = jnp.zeros((n_tiles * FFN_TILE,), jnp.int32).at[dest].set(tok)
    tile_start = jnp.arange(n_tiles, dtype=jnp.int32) * FFN_TILE
    tile_expert = jnp.minimum(jnp.searchsorted(ends, tile_start, side="right"), N_EXPERTS - 1).astype(jnp.int32)
    n_used = (ends[-1] // FFN_TILE).astype(jnp.int32).reshape(1)
    dest_tiles = dest.reshape(2, n // ROW_TILE, ROW_TILE).transpose(1, 0, 2)
    return src.reshape(n_tiles, 1, FFN_TILE), tile_expert, n_used, dest_tiles


def _moe_experts_kernel(te_ref, nu_ref, src_ref, h_hbm, wg_ref, wu_ref, wd_ref, y_ref,
                        buf, wg_b, wu_b, wd_b, sem):
    t = pl.program_id(0)
    used = t < nu_ref[0]

    @pl.when(jnp.logical_or(t == 0, te_ref[t] != te_ref[jnp.maximum(t - 1, 0)]))
    def _():
        wg_b[...] = wg_ref[...].astype(BF16)
        wu_b[...] = wu_ref[...].astype(BF16)
        wd_b[...] = wd_ref[...].astype(BF16)

    def row_copy(r):
        tok = src_ref[0, r]
        return pltpu.make_async_copy(h_hbm.at[pl.ds(pl.multiple_of(tok * TOK_ROWS, TOK_ROWS), TOK_ROWS), :],
                                     buf.at[pl.ds(pl.multiple_of(r * TOK_ROWS, TOK_ROWS), TOK_ROWS), :], sem)

    @pl.when(used)
    def _():
        def issue(r, carry):
            row_copy(r).start()
            return carry

        lax.fori_loop(0, FFN_TILE, issue, 0, unroll=8)
        pltpu.make_async_copy(h_hbm.at[pl.ds(0, FFN_TILE * TOK_ROWS), :], buf, sem).wait()
        h = _load_token_tiles(buf, FFN_TILE).astype(BF16)
        act = (_silu(_dot(h, wg_b[...])) * _dot(h, wu_b[...])).astype(BF16)
        _store_token_tiles(y_ref, _dot(act, wd_b[...]))

    @pl.when(jnp.logical_not(used))
    def _():
        y_ref[...] = jnp.zeros_like(y_ref)


def _moe_experts(h_tt, plan, wg, wu, wd):
    src, tile_expert, n_used, _ = plan
    n_tiles = src.shape[0]
    ne, d, de = wg.shape
    grid_spec = pltpu.PrefetchScalarGridSpec(
        num_scalar_prefetch=2,
        grid=(n_tiles,),
        in_specs=[pl.BlockSpec((None, 1, FFN_TILE), lambda t, te, nu: (t, 0, 0), memory_space=pltpu.SMEM),
                  pl.BlockSpec(memory_space=pl.ANY),
                  pl.BlockSpec((None, d, de), lambda t, te, nu: (te[t], 0, 0)),
                  pl.BlockSpec((None, d, de), lambda t, te, nu: (te[t], 0, 0)),
                  pl.BlockSpec((None, de, d), lambda t, te, nu: (te[t], 0, 0))],
        out_specs=pl.BlockSpec((FFN_TILE * TOK_ROWS, 128), lambda t, te, nu: (t, 0)),
        scratch_shapes=[pltpu.VMEM((FFN_TILE * TOK_ROWS, 128), F32),
                        pltpu.VMEM((d, de), BF16), pltpu.VMEM((d, de), BF16), pltpu.VMEM((de, d), BF16),
                        pltpu.SemaphoreType.DMA(())])
    return pl.pallas_call(
        _moe_experts_kernel,
        grid_spec=grid_spec,
        out_shape=jax.ShapeDtypeStruct((n_tiles * FFN_TILE * TOK_ROWS, 128), F32),
        compiler_params=_cparams("arbitrary"),
        name="moe_experts",
    )(tile_expert, n_used, src, h_tt, wg, wu, wd)


def _moe_combine_kernel(dest_ref, x_ref, mod_ref, wcol_ref, y_hbm, o_ref, buf0, buf1, sem, *, d):
    bufs = (buf0, buf1)

    def row_copy(k, r):
        row = dest_ref[k, r]
        return pltpu.make_async_copy(y_hbm.at[pl.ds(pl.multiple_of(row * TOK_ROWS, TOK_ROWS), TOK_ROWS), :],
                                     bufs[k].at[pl.ds(pl.multiple_of(r * TOK_ROWS, TOK_ROWS), TOK_ROWS), :], sem)

    def issue(r, carry):
        row_copy(0, r).start()
        row_copy(1, r).start()
        return carry

    lax.fori_loop(0, ROW_TILE, issue, 0, unroll=8)
    for k in range(2):
        pltpu.make_async_copy(y_hbm.at[pl.ds(0, ROW_TILE * TOK_ROWS), :], bufs[k], sem).wait()
    w = wcol_ref[...]
    mix = w[:, 0:1] * _load_token_tiles(buf0, ROW_TILE) + w[:, 1:2] * _load_token_tiles(buf1, ROW_TILE)
    g2 = mod_ref[...][:, 5 * d:6 * d]
    o_ref[...] = x_ref[...] + g2 * mix


def _moe_combine(xa, mods3, wcol, y_tt, plan, n_ctx_tiles):
    bsz, t, d = xa.shape
    nt = t // ROW_TILE
    dest_tiles = plan[3]
    flat = lambda b, i: b * nt + i
    return pl.pallas_call(
        functools.partial(_moe_combine_kernel, d=d),
        grid=(bsz, nt),
        in_specs=[pl.BlockSpec((None, 2, ROW_TILE), lambda b, i: (flat(b, i), 0, 0), memory_space=pltpu.SMEM),
                  pl.BlockSpec((None, ROW_TILE, d), lambda b, i: (b, i, 0)),
                  pl.BlockSpec((None, 1, 6 * d), _mod_row_map(n_ctx_tiles, bsz)),
                  pl.BlockSpec((ROW_TILE, GATE_LANES), lambda b, i: (flat(b, i), 0)),
                  pl.BlockSpec(memory_space=pl.ANY)],
        out_specs=pl.BlockSpec((None, ROW_TILE, d), lambda b, i: (b, i, 0)),
        out_shape=jax.ShapeDtypeStruct((bsz, t, d), F32),
        scratch_shapes=[pltpu.VMEM((ROW_TILE * TOK_ROWS, 128), F32), pltpu.VMEM((ROW_TILE * TOK_ROWS, 128), F32),
                        pltpu.SemaphoreType.DMA(())],
        compiler_params=_cparams("parallel", "parallel"),
        name="moe_combine",
    )(dest_tiles, xa, mods3, wcol, y_tt)


def _moe_sparse(xa, mods3, g, rw_t, rb, wg, wu, wd, n_ctx_tiles):
    bsz, t, _ = xa.shape
    h_tt, ids, wcol = _moe_router(xa, mods3, g, rw_t, rb, n_ctx_tiles)
    plan = _moe_plan(ids, bsz * t)
    y_tt = _moe_experts(h_tt, plan, wg, wu, wd)
    return _moe_combine(xa, mods3, wcol, y_tt, plan, n_ctx_tiles)


def _final_norm_kernel(x_ref, g_ref, o_ref):
    x = x_ref[...]
    o_ref[...] = x * lax.rsqrt(jnp.mean(x * x, axis=-1, keepdims=True) + EPS) * g_ref[...]


def _final_norm(x, g):
    bsz, t, d = x.shape
    return pl.pallas_call(
        _final_norm_kernel,
        grid=(bsz, t // ROW_TILE),
        in_specs=[pl.BlockSpec((None, ROW_TILE, d), lambda b, i: (b, i, 0)), pl.BlockSpec((1, d), lambda b, i: (0, 0))],
        out_specs=pl.BlockSpec((None, ROW_TILE, d), lambda b, i: (b, i, 0)),
        out_shape=jax.ShapeDtypeStruct((bsz, t, d), F32),
        compiler_params=_cparams("parallel", "parallel"),
        name="final_norm",
    )(x, g.reshape(1, d))


def kernel(x, c, ctx, c_ctx, w_mod, b_mod, norm_mix, norm_ffn, w_in, gla_a2, gla_ab, mlstm_gate_b, gdn_conv, gdn_a_log, gdn_dt_bias, diff_lambda, head_norm, w_branch, w_out, router_w, router_b, w_gate, w_up, w_down, norm_final):
    depth = w_in.shape[0]
    st = _prepare(x, c, ctx, c_ctx, w_mod, b_mod)
    n_ctx_tiles = st["n_ctx_tiles"]
    tables = _rope_tables(x.shape[1], st["ct"])
    rw_t = router_w.T
    rb = jnp.broadcast_to(router_b.astype(F32)[:, None], (N_EXPERTS, ROW_TILE))
    for l in range(depth):
        with_ctx = l < depth - 1
        lam_init = 0.8 - 0.6 * math.exp(-0.3 * l)
        mods3 = st["mods"][l][:, None, :]
        proj = _layer_inproj(st, l, norm_mix, w_in)
        scans = (*_gla_branch(proj, gla_a2[l], gla_ab[l]),
                 *_mlstm_branch(proj, mlstm_gate_b[l]),
                 *_gdn_branch(proj, gdn_conv[l], gdn_a_log[l], gdn_dt_bias[l], n_ctx_tiles))
        attn = _diff_branch(proj, tables, diff_lambda[l], lam_init, n_ctx_tiles, with_ctx)
        hn = jnp.zeros((8, BRANCH_W), F32).at[:N_BRANCH].set(head_norm[l])
        xa = _merge(st["xa"], mods3, proj, scans, attn, hn, w_branch[l].astype(BF16), w_out[l].astype(BF16),
                    n_ctx_tiles, with_ctx, lam_init)
        if not with_ctx:
            n_ctx_tiles = 0
        xa = _moe_sparse(xa, mods3, norm_ffn[l], rw_t, rb, w_gate[l], w_up[l], w_down[l], n_ctx_tiles)
        st = dict(st, xa=xa, n_ctx_tiles=n_ctx_tiles)
    return _final_norm(st["xa"], norm_final)
```

```python
import functools
import math

import numpy as np
import jax
import jax.numpy as jnp
from jax import lax
from jax.experimental import pallas as pl
from jax.experimental.pallas import tpu as pltpu

N_HEADS = 4
HEAD_DIM = 64
BRANCH_W = N_HEADS * HEAD_DIM
CHUNK = 64
GLA_RANK = 16
GLA_NORMALIZER = 16.0
CONV_W = 5
DIFF_QK = HEAD_DIM // 2
ROPE_AXIS = DIFF_QK // 2
ROPE_BASE = 10000.0
GRID_W = 64
N_EXPERTS = 16
N_GROUPS = 4
EXPERTS_PER_GROUP = 4
EPS = 1e-6
N_BRANCH = 4

ROW_TILE = 256
PROJ_COLS = 8192
BLK_GDN_QKV, BLK_DIFF_QKV, BLK_GDN_G = 0, 3, 6
BLK_GLA_Q, BLK_GLA_K, BLK_GLA_V, BLK_GLA_G = 7, 8, 9, 10
BLK_ML_Q, BLK_ML_K, BLK_ML_V, BLK_ML_G = 11, 28, 29, 30
BLK_GATES = 12
SMALL_BLOCK = 31
SM_GLA_RF, SM_GLA_RB, SM_ML_F, SM_ML_B, SM_GD_F, SM_GD_B = 0, 16, 32, 40, 48, 56
NEG_BIG = -1e30
VMEM_LIMIT = 56 * 1024 * 1024

F32 = jnp.float32
BF16 = jnp.bfloat16


def _cparams(*sem):
    return pltpu.CompilerParams(dimension_semantics=sem, vmem_limit_bytes=VMEM_LIMIT)


def _dot(a, b):
    return jnp.dot(a, b, preferred_element_type=F32)


def _dot_nt(a, b):
    return lax.dot_general(a, b, (((1,), (1,)), ((), ())), preferred_element_type=F32)


def _dot_tn(a, b):
    return lax.dot_general(a, b, (((0,), (0,)), ((), ())), preferred_element_type=F32)


def _split_hl(x):
    hi = x.astype(BF16)
    lo = (x - hi.astype(F32)).astype(BF16)
    return hi, lo


def _dot_hl(a, x):
    hi, lo = _split_hl(x)
    return _dot(a, hi) + _dot(a, lo)


def _dot_hl_r(x, a):
    hi, lo = _split_hl(x)
    return _dot(hi, a) + _dot(lo, a)


def _iota(shape, dim):
    return lax.broadcasted_iota(jnp.int32, shape, dim)


def _softplus(x):
    return jnp.maximum(x, 0.0) + jnp.log(1.0 + jnp.exp(-jnp.abs(x)))


def _log_sigmoid(x):
    return -_softplus(-x)


def _silu(x):
    return x * jax.nn.sigmoid(x)


def _lane_head(shape):
    return _iota(shape, len(shape) - 1) // HEAD_DIM


def _stack4(x):
    lh = _lane_head(x.shape)
    zero = jnp.zeros_like(x)
    return jnp.concatenate([jnp.where(lh == h, x, zero) for h in range(N_HEADS)], axis=0)


def _blk_mask():
    r = _iota((BRANCH_W, BRANCH_W), 0) // HEAD_DIM
    c = _iota((BRANCH_W, BRANCH_W), 1) // HEAD_DIM
    return r == c


def _ones_blk():
    return jnp.where(_blk_mask(), 1.0, 0.0).astype(BF16)


def _tri_hs(reverse, strict):
    t = _iota((CHUNK, BRANCH_W), 0)
    s = _iota((CHUNK, BRANCH_W), 1) % CHUNK
    if reverse:
        return (s > t) if strict else (s >= t)
    return (s < t) if strict else (s <= t)


def _eye_hs():
    t = _iota((CHUNK, BRANCH_W), 0)
    s = _iota((CHUNK, BRANCH_W), 1) % CHUNK
    return t == s


def _cum_mat(reverse):
    t = _iota((CHUNK, CHUNK), 0)
    s = _iota((CHUNK, CHUNK), 1)
    m = (s >= t) if reverse else (s <= t)
    return jnp.where(m, 1.0, 0.0).astype(BF16)


def _row_of(col_rep):
    return jnp.sum(jnp.where(_eye_hs(), col_rep, 0.0), axis=0, keepdims=True)


def _seg_max(x):
    lane = _iota(x.shape, 1)
    n = x.shape[1]
    for sh in (1, 2, 4, 8, 16, 32):
        up = pltpu.roll(x, n - sh, 1)
        dn = pltpu.roll(x, sh, 1)
        x = jnp.maximum(x, jnp.where((lane & sh) == 0, up, dn))
    return x


def _seg_sum(x, ones_blk):
    return _dot_hl_r(x, ones_blk)


def _mod_kernel(c_ref, w_ref, b_ref, o_ref):
    a = _silu(c_ref[...]).astype(BF16)
    o_ref[...] = _dot(a, w_ref[...].astype(BF16)) + b_ref[...]


def _modulation(cond, w_mod, b_mod):
    depth, d, n = w_mod.shape
    r = cond.shape[0]
    tn = 1536
    return pl.pallas_call(
        _mod_kernel,
        grid=(depth, n // tn),
        in_specs=[pl.BlockSpec((r, d), lambda l, j: (0, 0)),
                  pl.BlockSpec((None, d, tn), lambda l, j: (l, 0, j)),
                  pl.BlockSpec((None, 1, tn), lambda l, j: (l, 0, j))],
        out_specs=pl.BlockSpec((None, r, tn), lambda l, j: (l, 0, j)),
        out_shape=jax.ShapeDtypeStruct((depth, r, n), F32),
        compiler_params=_cparams("parallel", "parallel"),
        name="modulation",
    )(cond, w_mod, b_mod.reshape(depth, 1, n))


def _rms_mod(x, g, shift, scale):
    y = x * lax.rsqrt(jnp.mean(x * x, axis=-1, keepdims=True) + EPS)
    return (y * g) * (1.0 + scale) + shift


def _inproj_kernel(x_ref, mod_ref, g_ref, w_ref, o_ref, *, d, n_chunk):
    mod = mod_ref[...]
    h = _rms_mod(x_ref[...], g_ref[...], mod[:, 0:d], mod[:, d:2 * d]).astype(BF16)
    for n0 in range(0, PROJ_COLS, n_chunk):
        o_ref[:, n0:n0 + n_chunk] = _dot(h, w_ref[:, n0:n0 + n_chunk]).astype(BF16)


def _mod_row_map(n_ctx_tiles, n_batch):
    return lambda b, i: (jnp.where(i < n_ctx_tiles, n_batch, b), 0, 0)


def _inproj(xa, mods3, g, w_perm, n_ctx_tiles):
    bsz, t, d = xa.shape
    return pl.pallas_call(
        functools.partial(_inproj_kernel, d=d, n_chunk=1024),
        grid=(bsz, t // ROW_TILE),
        in_specs=[pl.BlockSpec((None, ROW_TILE, d), lambda b, i: (b, i, 0)),
                  pl.BlockSpec((None, 1, 6 * d), _mod_row_map(n_ctx_tiles, bsz)),
                  pl.BlockSpec((1, d), lambda b, i: (0, 0)),
                  pl.BlockSpec((d, PROJ_COLS), lambda b, i: (0, 0), pipeline_mode=pl.Buffered(1))],
        out_specs=pl.BlockSpec((None, ROW_TILE, PROJ_COLS), lambda b, i: (b, i, 0)),
        out_shape=jax.ShapeDtypeStruct((bsz, t, PROJ_COLS), BF16),
        compiler_params=_cparams("parallel", "parallel"),
        name="inproj",
    )(xa, mods3, g.reshape(1, d), w_perm)


def _proj_perm():
    sizes = (256, 256, 256, 256, 16, 16, 256, 256, 256, 256, 8, 8, 256, 256, 256, 256, 8, 8, 256, 256, 256, 4096)
    off = np.concatenate([[0], np.cumsum(sizes)])
    seg = lambda i: np.arange(off[i], off[i + 1])
    wide = {BLK_GDN_QKV: 12, BLK_GDN_QKV + 1: 13, BLK_GDN_QKV + 2: 14, BLK_GDN_G: 15,
            BLK_DIFF_QKV: 18, BLK_DIFF_QKV + 1: 19, BLK_DIFF_QKV + 2: 20,
            BLK_GLA_Q: 0, BLK_GLA_K: 1, BLK_GLA_V: 2, BLK_GLA_G: 3,
            BLK_ML_Q: 6, BLK_ML_K: 7, BLK_ML_V: 8, BLK_ML_G: 9}
    perm = np.full((PROJ_COLS,), -1, np.int64)
    for j, i in wide.items():
        perm[j * BRANCH_W:(j + 1) * BRANCH_W] = seg(i)
    perm[BLK_GATES * BRANCH_W:BLK_GATES * BRANCH_W + 4096] = seg(21)
    base = SMALL_BLOCK * BRANCH_W
    for lane0, i in ((SM_GLA_RF, 4), (SM_GLA_RB, 5), (SM_ML_F, 10), (SM_ML_B, 11), (SM_GD_F, 16), (SM_GD_B, 17)):
        s = seg(i)
        perm[base + lane0:base + lane0 + len(s)] = s
    return perm


def _permute_w_in(w_in):
    perm = _proj_perm()
    cuts = [0] + [i for i in range(1, PROJ_COLS)
                  if (perm[i] < 0) != (perm[i - 1] < 0) or (perm[i] >= 0 and perm[i] != perm[i - 1] + 1)] + [PROJ_COLS]
    w = w_in.astype(BF16)
    runs = [jnp.zeros((w.shape[0], b - a), BF16) if perm[a] < 0 else w[:, perm[a]:perm[a] + b - a]
            for a, b in zip(cuts[:-1], cuts[1:])]
    return jnp.concatenate(runs, axis=-1)


SCAN_CHUNKS = ROW_TILE // CHUNK


def _scan_block_map(col, nb, reverse):
    if reverse:
        return lambda j: (0, jnp.where(j == 0, 0, nb - j), col)
    return lambda j: (0, j, col)


def _scan_steps(reverse, bsz):
    order = range(SCAN_CHUNKS - 1, -1, -1) if reverse else range(SCAN_CHUNKS)
    return [(b, slice(c * CHUNK, (c + 1) * CHUNK)) for c in order for b in range(bsz)]


def _scan_call(kernel_fn, name, arrays, cols, consts, scratch, reverse):
    bsz, t, _ = arrays[0].shape
    nb = t // ROW_TILE
    blk = lambda col: pl.BlockSpec((bsz, ROW_TILE, BRANCH_W), _scan_block_map(col, nb, reverse))
    const = lambda a: pl.BlockSpec(a.shape, lambda j: (0,) * a.ndim)
    return pl.pallas_call(
        functools.partial(kernel_fn, reverse=reverse, bsz=bsz),
        grid=(nb,),
        in_specs=[blk(c) for c in cols] + [const(a) for a in consts],
        out_specs=blk(0),
        out_shape=jax.ShapeDtypeStruct((bsz, t, BRANCH_W), BF16),
        scratch_shapes=[pltpu.VMEM((bsz,) + s, F32) for s in scratch],
        compiler_params=_cparams("arbitrary"),
        name=name + ("_bwd" if reverse else "_fwd"),
    )(*arrays, *consts)


def _expand_small(small, lane0, count):
    src = _iota((BRANCH_W, BRANCH_W), 0)
    dst_head = _iota((BRANCH_W, BRANCH_W), 1) // HEAD_DIM
    e = jnp.where(src == lane0 + dst_head, 1.0, 0.0).astype(BF16)
    return _dot(small, e)


def _gla_kernel(q_ref, k_ref, v_ref, sm_ref, a2_ref, ab_ref, o_ref, st_ref, *, reverse, bsz):
    @pl.when(pl.program_id(0) == 0)
    def _():
        st_ref[...] = jnp.zeros_like(st_ref)

    cum_mat = _cum_mat(reverse)
    tri = _tri_hs(reverse, strict=False)
    blk = _blk_mask()
    a2 = a2_ref[...]
    ab = ab_ref[...]
    steps = _scan_steps(reverse, bsz)
    la = [_log_sigmoid(_dot(sm_ref[b, rows, :], a2) + ab) / GLA_NORMALIZER for b, rows in steps]
    cum = [_dot_hl(cum_mat, x) for x in la]
    tot = [jnp.sum(x, axis=0, keepdims=True) for x in la]
    q_in = [(q_ref[b, rows, :].astype(F32) * (HEAD_DIM ** -0.5) * jnp.exp(c)).astype(BF16)
            for (b, rows), c in zip(steps, cum)]
    k_out = [(k_ref[b, rows, :].astype(F32) * jnp.exp(-c)).astype(BF16) for (b, rows), c in zip(steps, cum)]
    k_end = [(k_ref[b, rows, :].astype(F32) * jnp.exp(t - c)).astype(BF16)
             for (b, rows), c, t in zip(steps, cum, tot)]
    att = [jnp.where(tri, _dot_nt(qi, _stack4(ko)), 0.0).astype(BF16) for qi, ko in zip(q_in, k_out)]
    o_intra = [_dot(a, _stack4(v_ref[b, rows, :])) for (b, rows), a in zip(steps, att)]
    kv_t = [jnp.where(blk, _dot_tn(v_ref[b, rows, :], ke), 0.0) for (b, rows), ke in zip(steps, k_end)]
    for i, (b, rows) in enumerate(steps):
        st = st_ref[b]
        o_ref[b, rows, :] = (o_intra[i] + _dot_nt(q_in[i], st.astype(BF16))).astype(o_ref.dtype)
        st_ref[b] = st * jnp.exp(tot[i]) + kv_t[i]


def _gla_scan(proj, a2pad, ab, reverse):
    return _scan_call(_gla_kernel, "gla", [proj] * 4, [BLK_GLA_Q, BLK_GLA_K, BLK_GLA_V, SMALL_BLOCK],
                      [a2pad, ab], [(BRANCH_W, BRANCH_W)], reverse)


def _gla_params(a2, ab, d):
    lane0 = SM_GLA_RB if d else SM_GLA_RF
    pad = jnp.zeros((BRANCH_W, BRANCH_W), F32).at[lane0:lane0 + GLA_RANK].set(a2[d])
    return pad.astype(BF16), ab[d].reshape(1, BRANCH_W)


def _gla_branch(proj, a2, ab):
    return tuple(_gla_scan(proj, *_gla_params(a2, ab, d), reverse=bool(d)) for d in (0, 1))


def _mlstm_kernel(q_ref, k_ref, v_ref, sm_ref, bias_ref, o_ref, ct_ref, nm_ref, *, reverse, bsz):
    @pl.when(pl.program_id(0) == 0)
    def _():
        ct_ref[...] = jnp.zeros_like(ct_ref)
        nm_ref[...] = jnp.zeros_like(nm_ref)

    lane0 = SM_ML_B if reverse else SM_ML_F
    cum_mat = _cum_mat(reverse)
    tri = _tri_hs(reverse, strict=False)
    blk = _blk_mask()
    ones_blk = _ones_blk()
    b_i = bias_ref[0:1, :]
    b_f = bias_ref[1:2, :]
    steps = _scan_steps(reverse, bsz)
    each = lambda f, *ls: [f(*a) for a in zip(*ls)]
    sm = [sm_ref[b, rows, :] for b, rows in steps]
    q = [q_ref[b, rows, :] for b, rows in steps]
    i_pre = [_expand_small(x, lane0, N_HEADS) + b_i for x in sm]
    log_f = [_log_sigmoid(_expand_small(x, lane0 + N_HEADS, N_HEADS) + b_f) for x in sm]
    f_cum = [_dot_hl(cum_mat, x) for x in log_f]
    f_tot = [jnp.sum(x, axis=0, keepdims=True) for x in log_f]
    u = each(lambda i, f: i - f, i_pre, f_cum)
    lw = each(lambda t, x: t + x, f_tot, u)
    a_end = [jnp.max(x, axis=0, keepdims=True) for x in lw]
    k = [k_ref[b, rows, :].astype(F32) * (HEAD_DIM ** -0.5) for b, rows in steps]
    kw = each(lambda kk, x, a: kk * jnp.exp(x - a), k, lw, a_end)
    k_sum = [jnp.sum(x, axis=0, keepdims=True) for x in kw]
    kv_t = [jnp.where(blk, _dot_tn(v_ref[b, rows, :], x.astype(BF16)), 0.0) for (b, rows), x in zip(steps, kw)]
    log_d = each(lambda f, x: jnp.where(tri, f + _row_of(x), NEG_BIG), f_cum, u)
    mx = [_seg_max(x) for x in log_d]
    s = each(lambda qq, kk, ld, m: (_dot_nt(qq, _stack4(kk.astype(BF16))) * jnp.exp(ld - m)).astype(BF16),
             q, k, log_d, mx)
    num1 = [_dot(x, _stack4(v_ref[b, rows, :])) for (b, rows), x in zip(steps, s)]
    den1 = [_dot(x, ones_blk) for x in s]
    for i, (b, rows) in enumerate(steps):
        n_in = nm_ref[b, 0:1, :]
        m_in = nm_ref[b, 1:2, :]
        ct = ct_ref[b]
        g = f_cum[i] + m_in
        m_t = jnp.maximum(g, mx[i])
        e = jnp.exp(g - m_t)
        r = jnp.exp(mx[i] - m_t)
        num = r * num1[i] + e * _dot_nt(q[i], ct.astype(BF16))
        den = r * den1[i] + e * _seg_sum(q[i].astype(F32) * n_in, ones_blk)
        o_ref[b, rows, :] = (num / jnp.maximum(jnp.abs(den), jnp.exp(-m_t))).astype(o_ref.dtype)

        m_new = jnp.maximum(f_tot[i] + m_in, a_end[i])
        old = jnp.exp(f_tot[i] + m_in - m_new)
        new = jnp.exp(a_end[i] - m_new)
        ct_ref[b] = ct * old + kv_t[i] * new
        nm_ref[b, 0:1, :] = n_in * old + k_sum[i] * new
        nm_ref[b, 1:2, :] = m_new


def _mlstm_scan(proj, bias, reverse):
    return _scan_call(_mlstm_kernel, "mlstm", [proj] * 4, [BLK_ML_Q, BLK_ML_K, BLK_ML_V, SMALL_BLOCK],
                      [bias], [(BRANCH_W, BRANCH_W), (8, BRANCH_W)], reverse)


def _head_rows(vals):
    rows = [jnp.repeat(v.astype(F32), HEAD_DIM) for v in vals]
    rows += [jnp.zeros((BRANCH_W,), F32)] * (8 - len(rows))
    return jnp.stack(rows)


def _mlstm_branch(proj, gate_b):
    return tuple(_mlstm_scan(proj, _head_rows([gate_b[d, 0], gate_b[d, 1]]), reverse=bool(d)) for d in (0, 1))


HALO = 8
QKV_W = 3 * BRANCH_W


def _gdn_prep_kernel(prev_ref, cur_ref, next_ref, w_ref, o_ref, *, n_ctx_tiles, n_tiles):
    i = pl.program_id(1)
    has_prev = jnp.logical_and(i != 0, i != n_ctx_tiles).astype(F32)
    has_next = jnp.logical_and(i != n_ctx_tiles - 1, i != n_tiles - 1).astype(F32)
    padded = jnp.concatenate([prev_ref[...].astype(F32) * has_prev, cur_ref[...].astype(F32),
                              next_ref[...].astype(F32) * has_next], axis=0)
    w = w_ref[...]
    acc = jnp.zeros((ROW_TILE, QKV_W), F32)
    for j in range(CONV_W):
        off = HALO + j - CONV_W // 2
        acc = acc + padded[off:off + ROW_TILE, :] * w[j:j + 1, :]
    y = _silu(acc)
    ones_blk = _ones_blk()
    q = y[:, 0:BRANCH_W]
    k = y[:, BRANCH_W:2 * BRANCH_W]
    q = q * lax.rsqrt(_seg_sum(q * q, ones_blk) + EPS) * (HEAD_DIM ** -0.5)
    k = k * lax.rsqrt(_seg_sum(k * k, ones_blk) + EPS)
    o_ref[:, 0:BRANCH_W] = q.astype(o_ref.dtype)
    o_ref[:, BRANCH_W:2 * BRANCH_W] = k.astype(o_ref.dtype)
    o_ref[:, 2 * BRANCH_W:] = y[:, 2 * BRANCH_W:].astype(o_ref.dtype)


def _gdn_prep(proj, conv_w, n_ctx_tiles):
    bsz, t, _ = proj.shape
    nt = t // ROW_TILE
    per = ROW_TILE // HALO
    last = t // HALO - 1
    qkv_blk = BLK_GDN_QKV * BRANCH_W // QKV_W
    w8 = jnp.zeros((8, QKV_W), F32).at[:CONV_W].set(conv_w)
    return pl.pallas_call(
        functools.partial(_gdn_prep_kernel, n_ctx_tiles=n_ctx_tiles, n_tiles=nt),
        grid=(bsz, nt),
        in_specs=[pl.BlockSpec((None, HALO, QKV_W), lambda b, i: (b, jnp.maximum(i * per - 1, 0), qkv_blk)),
                  pl.BlockSpec((None, ROW_TILE, QKV_W), lambda b, i: (b, i, qkv_blk)),
                  pl.BlockSpec((None, HALO, QKV_W), lambda b, i: (b, jnp.minimum((i + 1) * per, last), qkv_blk)),
                  pl.BlockSpec((8, QKV_W), lambda b, i: (0, 0))],
        out_specs=pl.BlockSpec((None, ROW_TILE, QKV_W), lambda b, i: (b, i, 0)),
        out_shape=jax.ShapeDtypeStruct((bsz, t, QKV_W), BF16),
        compiler_params=_cparams("parallel", "parallel"),
        name="gdn_prep",
    )(proj, proj, proj, w8)


def _gdn_kernel(q_ref, k_ref, v_ref, sm_ref, par_ref, o_ref, s_ref, *, reverse, bsz):
    @pl.when(pl.program_id(0) == 0)
    def _():
        s_ref[...] = jnp.zeros_like(s_ref)

    lane0 = SM_GD_B if reverse else SM_GD_F
    cum_mat = _cum_mat(reverse)
    tri = _tri_hs(reverse, strict=False)
    tri_strict = _tri_hs(reverse, strict=True)
    blk = _blk_mask()
    eye = jnp.where(_eye_hs(), 1.0, 0.0)
    a_scale = jnp.exp(par_ref[0:1, :])
    dt_bias = par_ref[1:2, :]
    steps = _scan_steps(reverse, bsz)
    each = lambda f, *ls: [f(*a) for a in zip(*ls)]
    sm = [sm_ref[b, rows, :] for b, rows in steps]
    q = [q_ref[b, rows, :] for b, rows in steps]
    kb16 = [k_ref[b, rows, :] for b, rows in steps]
    beta = [jax.nn.sigmoid(_expand_small(x, lane0, N_HEADS)) for x in sm]
    g = [-a_scale * _softplus(_expand_small(x, lane0 + N_HEADS, N_HEADS) + dt_bias) for x in sm]
    cum = [_dot_hl(cum_mat, x) for x in g]
    tot = [jnp.sum(x, axis=0, keepdims=True) for x in g]
    gam = [jnp.where(tri, jnp.exp(jnp.where(tri, c - _row_of(c), 0.0)), 0.0) for c in cum]
    k_beta = each(lambda kk, bb: kk.astype(F32) * bb, kb16, beta)
    k4 = [_stack4(x) for x in kb16]
    a_hs = each(lambda kb, kk, gm: jnp.where(tri_strict, _dot_nt(kb.astype(BF16), kk) * gm, 0.0), k_beta, k4, gam)
    attn = each(lambda qq, kk, gm: (_dot_nt(qq, kk) * gm).astype(BF16), q, k4, gam)

    p = [-x for x in a_hs]
    t_inv = [eye + x for x in p]
    for _ in range(5):
        p = [_dot(x.astype(BF16), _stack4(x.astype(BF16))) for x in p]
        t_inv = each(lambda t, x: t + _dot(t.astype(BF16), _stack4(x.astype(BF16))), t_inv, p)
    w = each(lambda t, kb, c: _dot(t.astype(BF16), _stack4((kb * jnp.exp(c)).astype(BF16))).astype(BF16),
             t_inv, k_beta, cum)
    u = [_dot(t.astype(BF16), _stack4((v_ref[b, rows, :].astype(F32) * bb).astype(BF16)))
         for (b, rows), t, bb in zip(steps, t_inv, beta)]
    q_dec = each(lambda qq, c: (qq.astype(F32) * jnp.exp(c)).astype(BF16), q, cum)
    k_end = each(lambda kk, t, c: (kk.astype(F32) * jnp.exp(t - c)).astype(BF16), kb16, tot, cum)
    for i, (b, rows) in enumerate(steps):
        s = s_ref[b]
        sb = s.astype(BF16)
        v_new = (u[i] - _dot(w[i], sb)).astype(BF16)
        o_ref[b, rows, :] = (_dot(q_dec[i], sb) + _dot(attn[i], _stack4(v_new))).astype(o_ref.dtype)
        s_ref[b] = s * jnp.exp(tot[i]) + jnp.where(blk, _dot_tn(k_end[i], v_new), 0.0)


def _gdn_scan(gqkv, proj, par, reverse):
    return _scan_call(_gdn_kernel, "gdn", [gqkv, gqkv, gqkv, proj], [0, 1, 2, SMALL_BLOCK],
                      [par], [(BRANCH_W, BRANCH_W)], reverse)


def _gdn_branch(proj, conv_w, a_log, dt_bias, n_ctx_tiles):
    gqkv = _gdn_prep(proj, conv_w, n_ctx_tiles)
    return tuple(_gdn_scan(gqkv, proj, _head_rows([a_log[d], dt_bias[d]]), reverse=bool(d))
                 for d in (0, 1))


VX_W = 128
LOG2E = 1.4426950408889634


def _kv_tile(t):
    return next(k for k in (3 * ROW_TILE, 2 * ROW_TILE, ROW_TILE) if t % k == 0)


def _rope_tables(n_lat, n_ctx):
    pos = jnp.arange(n_lat)
    row, col = pos // GRID_W, pos % GRID_W
    inv = ROPE_BASE ** (-jnp.arange(0, ROPE_AXIS, 2, dtype=F32) / ROPE_AXIS)
    ang = jnp.concatenate([row.astype(F32)[:, None] * inv, col.astype(F32)[:, None] * inv], axis=-1)
    cos = jnp.concatenate([jnp.ones((n_ctx, ROPE_AXIS), F32), jnp.cos(ang)], axis=0)
    sin = jnp.concatenate([jnp.zeros((n_ctx, ROPE_AXIS), F32), jnp.sin(ang)], axis=0)
    reps = BRANCH_W // DIFF_QK
    cos_l = jnp.tile(jnp.concatenate([cos, cos], axis=-1), (1, reps))
    sin_l = jnp.tile(jnp.concatenate([-sin, sin], axis=-1), (1, reps))
    return cos_l, sin_l


def _rope_prep_kernel(x_ref, cos_ref, sin_ref, q_ref, kt_ref, vx_ref):
    lane = _iota((ROW_TILE, BRANCH_W), 1)
    first_half = (lane % DIFF_QK) < ROPE_AXIS
    cos = cos_ref[...]
    sin = sin_ref[...]

    def rope(x):
        partner = jnp.where(first_half, pltpu.roll(x, BRANCH_W - ROPE_AXIS, 1), pltpu.roll(x, ROPE_AXIS, 1))
        return x * cos + partner * sin

    q = rope(x_ref[:, 0:BRANCH_W].astype(F32)) * (DIFF_QK ** -0.5 * LOG2E)
    k = rope(x_ref[:, BRANCH_W:2 * BRANCH_W].astype(F32))
    q_ref[...] = q.astype(q_ref.dtype)
    kt_ref[...] = jnp.transpose(k).astype(kt_ref.dtype)
    v = x_ref[:, 2 * BRANCH_W:]
    src = _iota((BRANCH_W, VX_W), 0)
    dst = _iota((BRANCH_W, VX_W), 1)
    ones_col = jnp.where(_iota((ROW_TILE, VX_W), 1) == HEAD_DIM, 1.0, 0.0)
    for h in range(N_HEADS):
        sel = jnp.where(jnp.logical_and(src == h * HEAD_DIM + dst, dst < HEAD_DIM), 1.0, 0.0).astype(BF16)
        vx_ref[h] = (_dot(v, sel) + ones_col).astype(vx_ref.dtype)


def _rope_prep(proj, cos_l, sin_l):
    bsz, t, _ = proj.shape
    nt = t // ROW_TILE
    kvt = _kv_tile(t)
    r = kvt // ROW_TILE
    qkv_blk = BLK_DIFF_QKV * BRANCH_W // QKV_W
    tab = pl.BlockSpec((ROW_TILE, BRANCH_W), lambda b, i: (i, 0))
    return pl.pallas_call(
        _rope_prep_kernel,
        grid=(bsz, nt),
        in_specs=[pl.BlockSpec((None, ROW_TILE, QKV_W), lambda b, i: (b, i, qkv_blk)), tab, tab],
        out_specs=[pl.BlockSpec((None, ROW_TILE, BRANCH_W), lambda b, i: (b, i, 0)),
                   pl.BlockSpec((None, None, BRANCH_W, ROW_TILE), lambda b, i: (b, i // r, 0, i % r)),
                   pl.BlockSpec((None, None, N_HEADS, ROW_TILE, VX_W), lambda b, i: (b, i // r, 0, i % r, 0))],
        out_shape=[jax.ShapeDtypeStruct((bsz, t, BRANCH_W), BF16),
                   jax.ShapeDtypeStruct((bsz, t // kvt, BRANCH_W, kvt), BF16),
                   jax.ShapeDtypeStruct((bsz, t // kvt, N_HEADS, kvt, VX_W), BF16)],
        compiler_params=_cparams("parallel", "parallel"),
        name="rope_prep",
    )(proj, cos_l, sin_l)


N_CHAINS = 2 * N_HEADS


def _attn_kernel(q_ref, kt_ref, vx_ref, lp_ref, o_ref, qm_scr, s0_scr, s1_scr, m0_scr, m1_scr, al0_scr, al1_scr,
                 acc_scr, *, nk, lam_init):
    s_bufs, m_bufs, al_bufs = (s0_scr, s1_scr), (m0_scr, m1_scr), (al0_scr, al1_scr)
    q = q_ref[...]
    lane = _iota(q.shape, 1)
    zero = jnp.zeros_like(q)
    for c in range(N_CHAINS):
        lo = c * DIFF_QK
        qm_scr[c] = jnp.where(jnp.logical_and(lane >= lo, lane < lo + DIFF_QK), q, zero)
    acc_scr[...] = jnp.zeros_like(acc_scr)

    def stage_a(j, slot, first=False):
        kt = kt_ref[j]
        for c in range(N_CHAINS):
            s = _dot(qm_scr[c], kt)
            s_bufs[slot][c] = s
            row_max = jnp.max(s, axis=1, keepdims=True)
            if first:
                m_bufs[slot][c] = row_max
                al_bufs[slot][c] = jnp.zeros_like(row_max)
            else:
                m_old = m_bufs[1 - slot][c]
                m_new = jnp.maximum(m_old, row_max)
                m_bufs[slot][c] = m_new
                al_bufs[slot][c] = jnp.exp2(m_old - m_new)

    def stage_b(j, slot):
        for c in range(N_CHAINS):
            p = jnp.exp2(s_bufs[slot][c] - m_bufs[slot][c]).astype(BF16)
            acc_scr[c] = acc_scr[c] * al_bufs[slot][c] + _dot(p, vx_ref[j, c // 2])

    stage_a(0, 0, first=True)

    def body(i, carry):
        j = 2 * i + 1
        stage_a(j, 1)
        stage_b(j - 1, 0)
        stage_a(j + 1, 0)
        stage_b(j, 1)
        return carry

    n_pairs = (nk - 1) // 2
    lax.fori_loop(0, n_pairs, body, 0)
    if nk % 2 == 0:
        stage_a(nk - 1, 1)
        stage_b(nk - 2, 0)
    stage_b(nk - 1, (nk - 1) % 2)

    lp = lp_ref[...]
    grp_src = _iota((VX_W, VX_W), 0) // DIFF_QK
    prod1 = lp[0:1, :] * lp[1:2, :]
    prod2 = lp[2:3, :] * lp[3:4, :]
    src = _iota((VX_W, BRANCH_W), 0)
    dst = _iota((VX_W, BRANCH_W), 1)
    out = jnp.zeros((ROW_TILE, BRANCH_W), F32)
    for h in range(N_HEADS):
        pick = jnp.where(grp_src == h, 1.0, 0.0).astype(BF16)
        lam = jnp.exp(_dot_hl_r(prod1, pick)) - jnp.exp(_dot_hl_r(prod2, pick)) + lam_init
        a0 = acc_scr[2 * h]
        a1 = acc_scr[2 * h + 1]
        o_h = a0 / a0[:, HEAD_DIM:HEAD_DIM + 1] - lam * (a1 / a1[:, HEAD_DIM:HEAD_DIM + 1])
        place = jnp.where(jnp.logical_and(dst == src + h * HEAD_DIM, src < HEAD_DIM), 1.0, 0.0).astype(BF16)
        out = out + _dot(o_h.astype(BF16), place)
    o_ref[...] = out.astype(o_ref.dtype)


def _diff_attention(q_rot, kt, vx, lam_p, lam_init, q_tile0, n_q_tiles, kv_len):
    bsz = q_rot.shape[0]
    n_kv_arr, _, kvt = kt.shape[1:]
    tk = min(kv_len, kvt)
    assert kv_len % tk == 0 and kvt % tk == 0
    nk = kv_len // tk
    lp = jnp.zeros((8, VX_W), F32).at[:4].set(lam_p.reshape(4, N_HEADS * DIFF_QK))
    return pl.pallas_call(
        functools.partial(_attn_kernel, nk=nk, lam_init=lam_init),
        grid=(bsz, n_q_tiles),
        in_specs=[pl.BlockSpec((None, ROW_TILE, BRANCH_W), lambda b, i: (b, i + q_tile0, 0)),
                  pl.BlockSpec((None, nk, BRANCH_W, tk), lambda b, i: (b, 0, 0, 0), pipeline_mode=pl.Buffered(1)),
                  pl.BlockSpec((None, nk, N_HEADS, tk, VX_W), lambda b, i: (b, 0, 0, 0, 0),
                               pipeline_mode=pl.Buffered(1)),
                  pl.BlockSpec((8, VX_W), lambda b, i: (0, 0))],
        out_specs=pl.BlockSpec((None, ROW_TILE, BRANCH_W), lambda b, i: (b, i, 0)),
        out_shape=jax.ShapeDtypeStruct((bsz, n_q_tiles * ROW_TILE, BRANCH_W), BF16),
        scratch_shapes=[pltpu.VMEM((N_CHAINS, ROW_TILE, BRANCH_W), BF16)]
                       + [pltpu.VMEM((N_CHAINS, ROW_TILE, tk), F32)] * 2
                       + [pltpu.VMEM((N_CHAINS, ROW_TILE, 1), F32)] * 4
                       + [pltpu.VMEM((N_CHAINS, ROW_TILE, VX_W), F32)],
        compiler_params=_cparams("parallel", "arbitrary"),
        name="diff_attention",
    )(q_rot, kt, vx, lp)


def _diff_branch(proj, tables, lam_p, lam_init, n_ctx_tiles, with_ctx):
    t = proj.shape[1]
    q_rot, kt, vx = _rope_prep(proj, *tables)
    lat = _diff_attention(q_rot, kt, vx, lam_p, lam_init, n_ctx_tiles, t // ROW_TILE - n_ctx_tiles, t)
    if not with_ctx:
        return lat
    ctx = _diff_attention(q_rot, kt, vx, lam_p, lam_init, 0, n_ctx_tiles, n_ctx_tiles * ROW_TILE)
    return jnp.concatenate([ctx, lat], axis=1)


def _prepare(x, c, ctx, c_ctx, w_mod, b_mod):
    bsz, _, d = x.shape
    ct = ctx.shape[1]
    assert ct % ROW_TILE == 0 and x.shape[1] % ROW_TILE == 0
    xa = jnp.concatenate([ctx, x], axis=1)
    rows = 8 * ((bsz + 1 + 7) // 8)
    cond = jnp.zeros((rows, d), F32).at[:bsz].set(c).at[bsz].set(c_ctx)
    mods = _modulation(cond, w_mod, b_mod)
    return dict(xa=xa, mods=mods, n_ctx_tiles=ct // ROW_TILE, ct=ct)


def _layer_inproj(st, l, norm_mix, w_in):
    mods3 = st["mods"][l][:, None, :]
    return _inproj(st["xa"], mods3, norm_mix[l], _permute_w_in(w_in[l]), st["n_ctx_tiles"])


def _merge_kernel(x_ref, mod_ref, glf_ref, glb_ref, mlf_ref, mlb_ref, gdf_ref, gdb_ref, at_ref,
                  g_gla_ref, g_ml_ref, g_gd_ref, mg0_ref, mg1_ref, mg2_ref, mg3_ref,
                  hn_ref, wb_ref, wo_ref, o_ref, *, d, lam_init):
    ones_blk = _ones_blk()

    def head_norm(o, i):
        ms = _seg_sum(o * o, ones_blk) * (1.0 / HEAD_DIM)
        return o * lax.rsqrt(ms + EPS) * hn_ref[i:i + 1, :]

    f32 = lambda r: r[...].astype(F32)
    ys = [head_norm(f32(glf_ref) + f32(glb_ref), 0) * _silu(f32(g_gla_ref)),
          head_norm(f32(mlf_ref) + f32(mlb_ref), 1) * jax.nn.sigmoid(f32(g_ml_ref)),
          head_norm(f32(gdf_ref) + f32(gdb_ref), 2) * _silu(f32(g_gd_ref)),
          head_norm(f32(at_ref), 3) * (1.0 - lam_init)]
    acc = None
    for i, (y, mg_ref) in enumerate(zip(ys, (mg0_ref, mg1_ref, mg2_ref, mg3_ref))):
        term = jax.nn.sigmoid(f32(mg_ref)) * _dot(y.astype(BF16), wb_ref[i])
        acc = term if acc is None else acc + term
    out = _dot(acc.astype(BF16), wo_ref[...])
    g1 = mod_ref[...][:, 2 * d:3 * d]
    o_ref[...] = x_ref[...] + g1 * out


def _merge(xa, mods3, proj, scans, attn, hn, wb, wo, n_ctx_tiles, with_ctx, lam_init):
    bsz, t, d = xa.shape
    off = 0 if with_ctx else n_ctx_tiles
    nt = t // ROW_TILE - off
    rows = lambda col: pl.BlockSpec((None, ROW_TILE, BRANCH_W), lambda b, i: (b, i + off, col))
    gate = lambda j: pl.BlockSpec((None, ROW_TILE, d), lambda b, i: (b, i + off, BLK_GATES * BRANCH_W // d + j))
    const = lambda shape: pl.BlockSpec(shape, lambda b, i: (0,) * len(shape))
    return pl.pallas_call(
        functools.partial(_merge_kernel, d=d, lam_init=lam_init),
        grid=(bsz, nt),
        in_specs=[pl.BlockSpec((None, ROW_TILE, d), lambda b, i: (b, i + off, 0)),
                  pl.BlockSpec((None, 1, 6 * d), lambda b, i: (jnp.where(i + off < n_ctx_tiles, bsz, b), 0, 0))]
                 + [rows(0)] * 6
                 + [pl.BlockSpec((None, ROW_TILE, BRANCH_W), lambda b, i: (b, i, 0))]
                 + [rows(BLK_GLA_G), rows(BLK_ML_G), rows(BLK_GDN_G)]
                 + [gate(j) for j in range(N_BRANCH)]
                 + [const((8, BRANCH_W)), const((N_BRANCH, BRANCH_W, d)), const((d, d))],
        out_specs=pl.BlockSpec((None, ROW_TILE, d), lambda b, i: (b, i, 0)),
        out_shape=jax.ShapeDtypeStruct((bsz, nt * ROW_TILE, d), F32),
        compiler_params=_cparams("parallel", "parallel"),
        name="merge",
    )(xa, mods3, *scans, attn, proj, proj, proj, proj, proj, proj, proj, hn, wb, wo)


GATE_LANES = 128


def _router_gates_t(hf, rw_ref, rb_ref, as_gates=True):
    h_hi, h_lo = _split_hl(hf)
    w_hi, w_lo = _split_hl(rw_ref[...])
    logits = _dot_nt(w_hi, h_hi) + _dot_nt(w_hi, h_lo) + _dot_nt(w_lo, h_hi)
    scores = jax.nn.sigmoid(logits)
    sel = scores + rb_ref[...]
    s = [sel[e:e + 1, :] for e in range(N_EXPERTS)]
    sc = [scores[e:e + 1, :] for e in range(N_EXPERTS)]
    grp = []
    for g in range(N_GROUPS):
        a, b, c, dd = s[4 * g:4 * g + 4]
        grp.append(functools.reduce(jnp.maximum, [a + b, a + c, a + dd, b + c, b + dd, c + dd]))
    gmax = functools.reduce(jnp.maximum, grp)
    chosen, taken = [], None
    for g in range(N_GROUPS):
        hit = grp[g] == gmax
        if taken is not None:
            hit = jnp.logical_and(hit, jnp.logical_not(taken))
        taken = hit if taken is None else jnp.logical_or(taken, hit)
        chosen.append(hit)
    ms = [jnp.where(chosen[e // EXPERTS_PER_GROUP], s[e], NEG_BIG) for e in range(N_EXPERTS)]

    def first_argmax(vals):
        top = functools.reduce(jnp.maximum, vals)
        hits, seen = [], None
        for v in vals:
            hit = v == top
            if seen is not None:
                hit = jnp.logical_and(hit, jnp.logical_not(seen))
            seen = hit if seen is None else jnp.logical_or(seen, hit)
            hits.append(hit)
        return hits

    oh1 = first_argmax(ms)
    oh2 = first_argmax([jnp.where(o, NEG_BIG, v) for o, v in zip(oh1, ms)])
    zero = jnp.zeros_like(sc[0])
    w1 = functools.reduce(jnp.add, [jnp.where(o, v, zero) for o, v in zip(oh1, sc)])
    w2 = functools.reduce(jnp.add, [jnp.where(o, v, zero) for o, v in zip(oh2, sc)])
    tot = w1 + w2
    if not as_gates:
        ids = [functools.reduce(jnp.add, [jnp.where(o, float(e), 0.0) for e, o in enumerate(oh)]) for oh in (oh1, oh2)]
        return ids[0], ids[1], w1 / tot, w2 / tot
    rows = [jnp.where(o1, w1 / tot, zero) + jnp.where(o2, w2 / tot, zero) for o1, o2 in zip(oh1, oh2)]
    return jnp.concatenate(rows, axis=0)


def _moe_dense_kernel(x_ref, *rest, d, bsz):
    mod_refs = rest[:bsz]
    g_ref, rw_ref, rb_ref, wg_ref, wu_ref, wd_ref, o_ref, h_scr, gate_scr, acc_scr = rest[bsz:]
    e = pl.program_id(1)

    @pl.when(e == 0)
    def _():
        eye = jnp.where(_iota((N_EXPERTS, GATE_LANES), 0) == _iota((N_EXPERTS, GATE_LANES), 1), 1.0, 0.0).astype(BF16)
        for b in range(bsz):
            rows = slice(b * ROW_TILE, (b + 1) * ROW_TILE)
            mod = mod_refs[b][...]
            hf = _rms_mod(x_ref[b], g_ref[...], mod[:, 3 * d:4 * d], mod[:, 4 * d:5 * d])
            h_scr[rows, :] = hf.astype(BF16)
            g_hi, g_lo = _split_hl(_router_gates_t(hf, rw_ref, rb_ref))
            gate_scr[rows, :] = _dot_tn(g_hi, eye) + _dot_tn(g_lo, eye)
        acc_scr[...] = jnp.zeros_like(acc_scr)

    h = h_scr[...]
    pick = jnp.where(_iota((GATE_LANES, GATE_LANES), 0) == e, 1.0, 0.0).astype(BF16)
    ge = _dot_hl_r(gate_scr[...], pick)[:, 0:1]
    act = (_silu(_dot(h, wg_ref[...])) * _dot(h, wu_ref[...])).astype(BF16)
    acc_scr[...] += ge * _dot(act, wd_ref[...])

    @pl.when(e == pl.num_programs(1) - 1)
    def _():
        for b in range(bsz):
            g2 = mod_refs[b][...][:, 5 * d:6 * d]
            o_ref[b] = x_ref[b] + g2 * acc_scr[b * ROW_TILE:(b + 1) * ROW_TILE, :]


def _moe_dense(xa, mods3, g, rw_t, rb, wg, wu, wd, n_ctx_tiles):
    bsz, t, d = xa.shape
    ne, _, de = wg.shape
    mod_spec = lambda b: pl.BlockSpec((None, 1, 6 * d), lambda i, e: (jnp.where(i < n_ctx_tiles, bsz, b), 0, 0))
    return pl.pallas_call(
        functools.partial(_moe_dense_kernel, d=d, bsz=bsz),
        grid=(t // ROW_TILE, ne),
        in_specs=[pl.BlockSpec((bsz, ROW_TILE, d), lambda i, e: (0, i, 0))]
                 + [mod_spec(b) for b in range(bsz)]
                 + [pl.BlockSpec((1, d), lambda i, e: (0, 0)),
                    pl.BlockSpec((ne, d), lambda i, e: (0, 0)),
                    pl.BlockSpec((ne, ROW_TILE), lambda i, e: (0, 0)),
                    pl.BlockSpec((None, d, de), lambda i, e: (e, 0, 0)),
                    pl.BlockSpec((None, d, de), lambda i, e: (e, 0, 0)),
                    pl.BlockSpec((None, de, d), lambda i, e: (e, 0, 0))],
        out_specs=pl.BlockSpec((bsz, ROW_TILE, d), lambda i, e: (0, i, 0)),
        out_shape=jax.ShapeDtypeStruct((bsz, t, d), F32),
        scratch_shapes=[pltpu.VMEM((bsz * ROW_TILE, d), BF16), pltpu.VMEM((bsz * ROW_TILE, GATE_LANES), F32),
                        pltpu.VMEM((bsz * ROW_TILE, d), F32)],
        compiler_params=_cparams("parallel", "arbitrary"),
        name="moe_dense",
    )(xa, *([mods3] * bsz), g.reshape(1, d), rw_t, rb, wg, wu, wd)


TOK_ROWS = 8
FFN_TILE = ROW_TILE


def _store_token_tiles(ref, val):
    n = val.shape[0]
    for s in range(TOK_ROWS):
        ref[pl.ds(s, n, stride=TOK_ROWS), :] = val[:, s * 128:(s + 1) * 128]


def _load_token_tiles(ref, n):
    return jnp.concatenate([ref[pl.ds(s, n, stride=TOK_ROWS), :] for s in range(TOK_ROWS)], axis=1)


def _moe_router_kernel(x_ref, mod_ref, g_ref, rw_ref, rb_ref, h_ref, ids_ref, wcol_ref, *, d):
    mod = mod_ref[...]
    hf = _rms_mod(x_ref[...], g_ref[...], mod[:, 3 * d:4 * d], mod[:, 4 * d:5 * d])
    _store_token_tiles(h_ref, hf)
    e1, e2, w1, w2 = _router_gates_t(hf, rw_ref, rb_ref, as_gates=False)
    pad = jnp.zeros((6, ROW_TILE), F32)
    ids_ref[...] = jnp.concatenate([e1, e2, pad], axis=0).astype(jnp.int32)
    eye = jnp.where(_iota((8, GATE_LANES), 0) == _iota((8, GATE_LANES), 1), 1.0, 0.0).astype(BF16)
    w_hi, w_lo = _split_hl(jnp.concatenate([w1, w2, pad], axis=0))
    wcol_ref[...] = _dot_tn(w_hi, eye) + _dot_tn(w_lo, eye)


def _moe_router(xa, mods3, g, rw_t, rb, n_ctx_tiles):
    bsz, t, d = xa.shape
    nt = t // ROW_TILE
    n = bsz * t
    flat = lambda b, i: b * nt + i
    return pl.pallas_call(
        functools.partial(_moe_router_kernel, d=d),
        grid=(bsz, nt),
        in_specs=[pl.BlockSpec((None, ROW_TILE, d), lambda b, i: (b, i, 0)),
                  pl.BlockSpec((None, 1, 6 * d), _mod_row_map(n_ctx_tiles, bsz)),
                  pl.BlockSpec((1, d), lambda b, i: (0, 0)),
                  pl.BlockSpec((N_EXPERTS, d), lambda b, i: (0, 0)),
                  pl.BlockSpec((N_EXPERTS, ROW_TILE), lambda b, i: (0, 0))],
        out_specs=[pl.BlockSpec((ROW_TILE * TOK_ROWS, 128), lambda b, i: (flat(b, i), 0)),
                   pl.BlockSpec((8, ROW_TILE), lambda b, i: (0, flat(b, i))),
                   pl.BlockSpec((ROW_TILE, GATE_LANES), lambda b, i: (flat(b, i), 0))],
        out_shape=[jax.ShapeDtypeStruct((n * TOK_ROWS, 128), F32),
                   jax.ShapeDtypeStruct((8, n), jnp.int32),
                   jax.ShapeDtypeStruct((n, GATE_LANES), F32)],
        compiler_params=_cparams("parallel", "parallel"),
        name="moe_router",
    )(xa, mods3, g.reshape(1, d), rw_t, rb)


def _moe_plan(ids, n):
    n_tiles = -(-2 * n // FFN_TILE) + N_EXPERTS
    eid = ids[:2].reshape(2 * n)
    order = jnp.argsort(eid, stable=True).astype(jnp.int32)
    inv = jnp.argsort(order).astype(jnp.int32)
    counts = jnp.sum((eid[:, None] == jnp.arange(N_EXPERTS, dtype=jnp.int32)[None, :]).astype(jnp.int32), axis=0)
    first = jnp.cumsum(counts) - counts
    padded = (counts + FFN_TILE - 1) // FFN_TILE * FFN_TILE
    ends = jnp.cumsum(padded)
    offs = ends - padded
    dest = offs[eid] + inv - first[eid]
    tile_start = jnp.arange(n_tiles, dtype=jnp.int32) * FFN_TILE
    tile_expert = jnp.minimum(jnp.searchsorted(ends, tile_start, side="right"), N_EXPERTS - 1).astype(jnp.int32)
    te = jnp.repeat(tile_expert, FFN_TILE)
    local = jnp.arange(n_tiles * FFN_TILE, dtype=jnp.int32) - offs[te]
    tok_sorted = order % n
    src = jnp.where(local < counts[te], tok_sorted[jnp.clip(first[te] + local, 0, 2 * n - 1)], 0)
    dest_tiles = dest.reshape(2, n // ROW_TILE, ROW_TILE).transpose(1, 0, 2)
    return src.reshape(n_tiles, 1, FFN_TILE), tile_expert, dest_tiles


def _gather_rows(idx_ref, k, src_hbm, dst, sem, n_rows, unrolled):
    def start(r):
        row = idx_ref[k, r]
        pltpu.make_async_copy(src_hbm.at[pl.ds(pl.multiple_of(row * TOK_ROWS, TOK_ROWS), TOK_ROWS), :],
                              dst.at[pl.ds(pl.multiple_of(r * TOK_ROWS, TOK_ROWS), TOK_ROWS), :], sem).start()

    if unrolled:
        for r in range(n_rows):
            start(r)
    else:
        def body(r, carry):
            start(r)
            return carry

        lax.fori_loop(0, n_rows, body, 0, unroll=8)


def _wait_rows(src_hbm, dst, sem, n_rows):
    pltpu.make_async_copy(src_hbm.at[pl.ds(0, n_rows * TOK_ROWS), :], dst, sem).wait()


def _moe_experts_kernel(te_ref, src_ref, nxt_ref, h_hbm, wg_ref, wu_ref, wd_ref, y_ref,
                        buf, wg_b, wu_b, wd_b, sem):
    t = pl.program_id(0)
    last = pl.num_programs(0) - 1
    slot = lax.rem(t, 2)

    @pl.when(jnp.logical_or(t == 0, te_ref[t] != te_ref[jnp.maximum(t - 1, 0)]))
    def _():
        wg_b[...] = wg_ref[...].astype(BF16)
        wu_b[...] = wu_ref[...].astype(BF16)
        wd_b[...] = wd_ref[...].astype(BF16)

    @pl.when(t == 0)
    def _():
        _gather_rows(src_ref, 0, h_hbm, buf.at[0], sem.at[0], FFN_TILE, unrolled=False)

    _gather_rows(nxt_ref, 0, h_hbm, buf.at[1 - slot], sem.at[1 - slot], FFN_TILE, unrolled=True)
    _wait_rows(h_hbm, buf.at[slot], sem.at[slot], FFN_TILE)
    h = _load_token_tiles(buf.at[slot], FFN_TILE).astype(BF16)
    act = (_silu(_dot(h, wg_b[...])) * _dot(h, wu_b[...])).astype(BF16)
    _store_token_tiles(y_ref, _dot(act, wd_b[...]))

    @pl.when(t == last)
    def _():
        _wait_rows(h_hbm, buf.at[1 - slot], sem.at[1 - slot], FFN_TILE)


def _moe_experts(h_tt, plan, wg, wu, wd):
    src, tile_expert, _ = plan
    n_tiles = src.shape[0]
    ne, d, de = wg.shape
    idx_spec = lambda step: pl.BlockSpec((None, 1, FFN_TILE), lambda t, te: (jnp.minimum(t + step, n_tiles - 1), 0, 0),
                                         memory_space=pltpu.SMEM)
    grid_spec = pltpu.PrefetchScalarGridSpec(
        num_scalar_prefetch=1,
        grid=(n_tiles,),
        in_specs=[idx_spec(0), idx_spec(1),
                  pl.BlockSpec(memory_space=pl.ANY),
                  pl.BlockSpec((None, d, de), lambda t, te: (te[t], 0, 0)),
                  pl.BlockSpec((None, d, de), lambda t, te: (te[t], 0, 0)),
                  pl.BlockSpec((None, de, d), lambda t, te: (te[t], 0, 0))],
        out_specs=pl.BlockSpec((FFN_TILE * TOK_ROWS, 128), lambda t, te: (t, 0)),
        scratch_shapes=[pltpu.VMEM((2, FFN_TILE * TOK_ROWS, 128), F32),
                        pltpu.VMEM((d, de), BF16), pltpu.VMEM((d, de), BF16), pltpu.VMEM((de, d), BF16),
                        pltpu.SemaphoreType.DMA((2,))])
    return pl.pallas_call(
        _moe_experts_kernel,
        grid_spec=grid_spec,
        out_shape=jax.ShapeDtypeStruct((n_tiles * FFN_TILE * TOK_ROWS, 128), F32),
        compiler_params=_cparams("arbitrary"),
        name="moe_experts",
    )(tile_expert, src, src, h_tt, wg, wu, wd)


def _moe_combine_kernel(dest_ref, nxt_ref, x_ref, mod_ref, wcol_ref, y_hbm, o_ref, buf, sem, *, d):
    j = pl.program_id(0)
    last = pl.num_programs(0) - 1
    slot = lax.rem(j, 2)

    def gather(idx_ref, s, unrolled):
        for k in range(2):
            _gather_rows(idx_ref, k, y_hbm, buf.at[s, k], sem.at[s], ROW_TILE, unrolled)

    def wait(s):
        for k in range(2):
            _wait_rows(y_hbm, buf.at[s, k], sem.at[s], ROW_TILE)

    @pl.when(j == 0)
    def _():
        gather(dest_ref, 0, unrolled=False)

    gather(nxt_ref, 1 - slot, unrolled=True)
    wait(slot)
    w = wcol_ref[...]
    mix = (w[:, 0:1] * _load_token_tiles(buf.at[slot, 0], ROW_TILE)
           + w[:, 1:2] * _load_token_tiles(buf.at[slot, 1], ROW_TILE))
    g2 = mod_ref[...][:, 5 * d:6 * d]
    o_ref[...] = x_ref[...] + g2 * mix

    @pl.when(j == last)
    def _():
        wait(1 - slot)


def _moe_combine(xa, mods3, wcol, y_tt, plan, n_ctx_tiles):
    bsz, t, d = xa.shape
    nt = t // ROW_TILE
    n_tiles = bsz * nt
    dest_tiles = plan[2]
    idx_spec = lambda step: pl.BlockSpec((None, 2, ROW_TILE), lambda j: (jnp.minimum(j + step, n_tiles - 1), 0, 0),
                                         memory_space=pltpu.SMEM)
    return pl.pallas_call(
        functools.partial(_moe_combine_kernel, d=d),
        grid=(n_tiles,),
        in_specs=[idx_spec(0), idx_spec(1),
                  pl.BlockSpec((None, ROW_TILE, d), lambda j: (j // nt, j % nt, 0)),
                  pl.BlockSpec((None, 1, 6 * d), lambda j: (jnp.where(j % nt < n_ctx_tiles, bsz, j // nt), 0, 0)),
                  pl.BlockSpec((ROW_TILE, GATE_LANES), lambda j: (j, 0)),
                  pl.BlockSpec(memory_space=pl.ANY)],
        out_specs=pl.BlockSpec((None, ROW_TILE, d), lambda j: (j // nt, j % nt, 0)),
        out_shape=jax.ShapeDtypeStruct((bsz, t, d), F32),
        scratch_shapes=[pltpu.VMEM((2, 2, ROW_TILE * TOK_ROWS, 128), F32), pltpu.SemaphoreType.DMA((2,))],
        compiler_params=_cparams("arbitrary"),
        name="moe_combine",
    )(dest_tiles, dest_tiles, xa, mods3, wcol, y_tt)


def _moe_sparse(xa, mods3, g, rw_t, rb, wg, wu, wd, n_ctx_tiles):
    bsz, t, _ = xa.shape
    h_tt, ids, wcol = _moe_router(xa, mods3, g, rw_t, rb, n_ctx_tiles)
    plan = _moe_plan(ids, bsz * t)
    y_tt = _moe_experts(h_tt, plan, wg, wu, wd)
    return _moe_combine(xa, mods3, wcol, y_tt, plan, n_ctx_tiles)


def _final_norm_kernel(x_ref, g_ref, o_ref):
    x = x_ref[...]
    o_ref[...] = x * lax.rsqrt(jnp.mean(x * x, axis=-1, keepdims=True) + EPS) * g_ref[...]


def _final_norm(x, g):
    bsz, t, d = x.shape
    return pl.pallas_call(
        _final_norm_kernel,
        grid=(bsz, t // ROW_TILE),
        in_specs=[pl.BlockSpec((None, ROW_TILE, d), lambda b, i: (b, i, 0)), pl.BlockSpec((1, d), lambda b, i: (0, 0))],
        out_specs=pl.BlockSpec((None, ROW_TILE, d), lambda b, i: (b, i, 0)),
        out_shape=jax.ShapeDtypeStruct((bsz, t, d), F32),
        compiler_params=_cparams("parallel", "parallel"),
        name="final_norm",
    )(x, g.reshape(1, d))


def kernel(x, c, ctx, c_ctx, w_mod, b_mod, norm_mix, norm_ffn, w_in, gla_a2, gla_ab, mlstm_gate_b, gdn_conv, gdn_a_log, gdn_dt_bias, diff_lambda, head_norm, w_branch, w_out, router_w, router_b, w_gate, w_up, w_down, norm_final):
    depth = w_in.shape[0]
    st = _prepare(x, c, ctx, c_ctx, w_mod, b_mod)
    n_ctx_tiles = st["n_ctx_tiles"]
    tables = _rope_tables(x.shape[1], st["ct"])
    rw_t = router_w.T
    rb = jnp.broadcast_to(router_b.astype(F32)[:, None], (N_EXPERTS, ROW_TILE))
    for l in range(depth):
        with_ctx = l < depth - 1
        lam_init = 0.8 - 0.6 * math.exp(-0.3 * l)
        mods3 = st["mods"][l][:, None, :]
        proj = _layer_inproj(st, l, norm_mix, w_in)
        scans = (*_gla_branch(proj, gla_a2[l], gla_ab[l]),
                 *_mlstm_branch(proj, mlstm_gate_b[l]),
                 *_gdn_branch(proj, gdn_conv[l], gdn_a_log[l], gdn_dt_bias[l], n_ctx_tiles))
        attn = _diff_branch(proj, tables, diff_lambda[l], lam_init, n_ctx_tiles, with_ctx)
        hn = jnp.zeros((8, BRANCH_W), F32).at[:N_BRANCH].set(head_norm[l])
        xa = _merge(st["xa"], mods3, proj, scans, attn, hn, w_branch[l].astype(BF16), w_out[l].astype(BF16),
                    n_ctx_tiles, with_ctx, lam_init)
        if not with_ctx:
            n_ctx_tiles = 0
        xa = _moe_sparse(xa, mods3, norm_ffn[l], rw_t, rb, w_gate[l], w_up[l], w_down[l], n_ctx_tiles)
        st = dict(st, xa=xa, n_ctx_tiles=n_ctx_tiles)
    return _final_norm(st["xa"], norm_final)
```

```python
import functools
import math

import numpy as np
import jax
import jax.numpy as jnp
from jax import lax
from jax.experimental import pallas as pl
from jax.experimental.pallas import tpu as pltpu

N_HEADS = 4
HEAD_DIM = 64
BRANCH_W = N_HEADS * HEAD_DIM
CHUNK = 64
GLA_RANK = 16
GLA_NORMALIZER = 16.0
CONV_W = 5
DIFF_QK = HEAD_DIM // 2
ROPE_AXIS = DIFF_QK // 2
ROPE_BASE = 10000.0
GRID_W = 64
N_EXPERTS = 16
N_GROUPS = 4
EXPERTS_PER_GROUP = 4
EPS = 1e-6
N_BRANCH = 4

ROW_TILE = 256
PROJ_COLS = 8192
BLK_GDN_QKV, BLK_DIFF_QKV, BLK_GDN_G = 0, 3, 6
BLK_GLA_Q, BLK_GLA_K, BLK_GLA_V, BLK_GLA_G = 7, 8, 9, 10
BLK_ML_Q, BLK_ML_K, BLK_ML_V, BLK_ML_G = 11, 28, 29, 30
BLK_GATES = 12
SMALL_BLOCK = 31
SM_GLA_RF, SM_GLA_RB, SM_ML_F, SM_ML_B, SM_GD_F, SM_GD_B = 0, 16, 32, 40, 48, 56
NEG_BIG = -1e30
VMEM_LIMIT = 56 * 1024 * 1024

F32 = jnp.float32
BF16 = jnp.bfloat16


def _cparams(*sem):
    return pltpu.CompilerParams(dimension_semantics=sem, vmem_limit_bytes=VMEM_LIMIT)


def _dot(a, b):
    return jnp.dot(a, b, preferred_element_type=F32)


def _dot_nt(a, b):
    return lax.dot_general(a, b, (((1,), (1,)), ((), ())), preferred_element_type=F32)


def _dot_tn(a, b):
    return lax.dot_general(a, b, (((0,), (0,)), ((), ())), preferred_element_type=F32)


def _split_hl(x):
    hi = x.astype(BF16)
    lo = (x - hi.astype(F32)).astype(BF16)
    return hi, lo


def _dot_hl(a, x):
    hi, lo = _split_hl(x)
    return _dot(a, hi) + _dot(a, lo)


def _dot_hl_r(x, a):
    hi, lo = _split_hl(x)
    return _dot(hi, a) + _dot(lo, a)


def _iota(shape, dim):
    return lax.broadcasted_iota(jnp.int32, shape, dim)


def _softplus(x):
    return jnp.maximum(x, 0.0) + jnp.log(1.0 + jnp.exp(-jnp.abs(x)))


def _log_sigmoid(x):
    return -_softplus(-x)


def _silu(x):
    return x * jax.nn.sigmoid(x)


def _lane_head(shape):
    return _iota(shape, len(shape) - 1) // HEAD_DIM


def _stack4(x):
    lh = _lane_head(x.shape)
    zero = jnp.zeros_like(x)
    return jnp.concatenate([jnp.where(lh == h, x, zero) for h in range(N_HEADS)], axis=0)


def _blk_mask():
    r = _iota((BRANCH_W, BRANCH_W), 0) // HEAD_DIM
    c = _iota((BRANCH_W, BRANCH_W), 1) // HEAD_DIM
    return r == c


def _ones_blk():
    return jnp.where(_blk_mask(), 1.0, 0.0).astype(BF16)


def _tri_hs(reverse, strict):
    t = _iota((CHUNK, BRANCH_W), 0)
    s = _iota((CHUNK, BRANCH_W), 1) % CHUNK
    if reverse:
        return (s > t) if strict else (s >= t)
    return (s < t) if strict else (s <= t)


def _eye_hs():
    t = _iota((CHUNK, BRANCH_W), 0)
    s = _iota((CHUNK, BRANCH_W), 1) % CHUNK
    return t == s


def _cum_mat(reverse):
    t = _iota((CHUNK, CHUNK), 0)
    s = _iota((CHUNK, CHUNK), 1)
    m = (s >= t) if reverse else (s <= t)
    return jnp.where(m, 1.0, 0.0).astype(BF16)


def _row_of(col_rep):
    return jnp.sum(jnp.where(_eye_hs(), col_rep, 0.0), axis=0, keepdims=True)


def _seg_max(x):
    lane = _iota(x.shape, 1)
    n = x.shape[1]
    for sh in (1, 2, 4, 8, 16, 32):
        up = pltpu.roll(x, n - sh, 1)
        dn = pltpu.roll(x, sh, 1)
        x = jnp.maximum(x, jnp.where((lane & sh) == 0, up, dn))
    return x


def _seg_sum(x, ones_blk):
    return _dot_hl_r(x, ones_blk)


def _mod_kernel(c_ref, w_ref, b_ref, o_ref):
    a = _silu(c_ref[...]).astype(BF16)
    o_ref[...] = _dot(a, w_ref[...].astype(BF16)) + b_ref[...]


def _modulation(cond, w_mod, b_mod):
    depth, d, n = w_mod.shape
    r = cond.shape[0]
    tn = 1536
    return pl.pallas_call(
        _mod_kernel,
        grid=(depth, n // tn),
        in_specs=[pl.BlockSpec((r, d), lambda l, j: (0, 0)),
                  pl.BlockSpec((None, d, tn), lambda l, j: (l, 0, j)),
                  pl.BlockSpec((None, 1, tn), lambda l, j: (l, 0, j))],
        out_specs=pl.BlockSpec((None, r, tn), lambda l, j: (l, 0, j)),
        out_shape=jax.ShapeDtypeStruct((depth, r, n), F32),
        compiler_params=_cparams("parallel", "parallel"),
        name="modulation",
    )(cond, w_mod, b_mod.reshape(depth, 1, n))


def _rms_mod(x, g, shift, scale):
    y = x * lax.rsqrt(jnp.mean(x * x, axis=-1, keepdims=True) + EPS)
    return (y * g) * (1.0 + scale) + shift


def _inproj_kernel(x_ref, mod_ref, g_ref, w_ref, o_ref, *, d, n_chunk):
    mod = mod_ref[...]
    h = _rms_mod(x_ref[...], g_ref[...], mod[:, 0:d], mod[:, d:2 * d]).astype(BF16)
    for n0 in range(0, PROJ_COLS, n_chunk):
        o_ref[:, n0:n0 + n_chunk] = _dot(h, w_ref[:, n0:n0 + n_chunk]).astype(BF16)


def _mod_row_map(n_ctx_tiles, n_batch):
    return lambda b, i: (jnp.where(i < n_ctx_tiles, n_batch, b), 0, 0)


def _inproj(xa, mods3, g, w_perm, n_ctx_tiles):
    bsz, t, d = xa.shape
    return pl.pallas_call(
        functools.partial(_inproj_kernel, d=d, n_chunk=1024),
        grid=(bsz, t // ROW_TILE),
        in_specs=[pl.BlockSpec((None, ROW_TILE, d), lambda b, i: (b, i, 0)),
                  pl.BlockSpec((None, 1, 6 * d), _mod_row_map(n_ctx_tiles, bsz)),
                  pl.BlockSpec((1, d), lambda b, i: (0, 0)),
                  pl.BlockSpec((d, PROJ_COLS), lambda b, i: (0, 0), pipeline_mode=pl.Buffered(1))],
        out_specs=pl.BlockSpec((None, ROW_TILE, PROJ_COLS), lambda b, i: (b, i, 0)),
        out_shape=jax.ShapeDtypeStruct((bsz, t, PROJ_COLS), BF16),
        compiler_params=_cparams("parallel", "parallel"),
        name="inproj",
    )(xa, mods3, g.reshape(1, d), w_perm)


def _proj_perm():
    sizes = (256, 256, 256, 256, 16, 16, 256, 256, 256, 256, 8, 8, 256, 256, 256, 256, 8, 8, 256, 256, 256, 4096)
    off = np.concatenate([[0], np.cumsum(sizes)])
    seg = lambda i: np.arange(off[i], off[i + 1])
    wide = {BLK_GDN_QKV: 12, BLK_GDN_QKV + 1: 13, BLK_GDN_QKV + 2: 14, BLK_GDN_G: 15,
            BLK_DIFF_QKV: 18, BLK_DIFF_QKV + 1: 19, BLK_DIFF_QKV + 2: 20,
            BLK_GLA_Q: 0, BLK_GLA_K: 1, BLK_GLA_V: 2, BLK_GLA_G: 3,
            BLK_ML_Q: 6, BLK_ML_K: 7, BLK_ML_V: 8, BLK_ML_G: 9}
    perm = np.full((PROJ_COLS,), -1, np.int64)
    for j, i in wide.items():
        perm[j * BRANCH_W:(j + 1) * BRANCH_W] = seg(i)
    perm[BLK_GATES * BRANCH_W:BLK_GATES * BRANCH_W + 4096] = seg(21)
    base = SMALL_BLOCK * BRANCH_W
    for lane0, i in ((SM_GLA_RF, 4), (SM_GLA_RB, 5), (SM_ML_F, 10), (SM_ML_B, 11), (SM_GD_F, 16), (SM_GD_B, 17)):
        s = seg(i)
        perm[base + lane0:base + lane0 + len(s)] = s
    return perm


def _permute_w_in(w_in):
    perm = _proj_perm()
    cuts = [0] + [i for i in range(1, PROJ_COLS)
                  if (perm[i] < 0) != (perm[i - 1] < 0) or (perm[i] >= 0 and perm[i] != perm[i - 1] + 1)] + [PROJ_COLS]
    w = w_in.astype(BF16)
    runs = [jnp.zeros((w.shape[0], b - a), BF16) if perm[a] < 0 else w[:, perm[a]:perm[a] + b - a]
            for a, b in zip(cuts[:-1], cuts[1:])]
    return jnp.concatenate(runs, axis=-1)


SCAN_CHUNKS = ROW_TILE // CHUNK


def _scan_block_map(col, nb, reverse):
    if reverse:
        return lambda j: (0, jnp.where(j == 0, 0, nb - j), col)
    return lambda j: (0, j, col)


def _scan_steps(reverse, bsz):
    order = range(SCAN_CHUNKS - 1, -1, -1) if reverse else range(SCAN_CHUNKS)
    return [(b, slice(c * CHUNK, (c + 1) * CHUNK)) for c in order for b in range(bsz)]


def _scan_call(kernel_fn, name, arrays, cols, consts, scratch, reverse):
    bsz, t, _ = arrays[0].shape
    nb = t // ROW_TILE
    blk = lambda col: pl.BlockSpec((bsz, ROW_TILE, BRANCH_W), _scan_block_map(col, nb, reverse))
    const = lambda a: pl.BlockSpec(a.shape, lambda j: (0,) * a.ndim)
    return pl.pallas_call(
        functools.partial(kernel_fn, reverse=reverse, bsz=bsz),
        grid=(nb,),
        in_specs=[blk(c) for c in cols] + [const(a) for a in consts],
        out_specs=blk(0),
        out_shape=jax.ShapeDtypeStruct((bsz, t, BRANCH_W), BF16),
        scratch_shapes=[pltpu.VMEM((bsz,) + s, F32) for s in scratch],
        compiler_params=_cparams("arbitrary"),
        name=name + ("_bwd" if reverse else "_fwd"),
    )(*arrays, *consts)


def _expand_small(small, lane0, count):
    src = _iota((BRANCH_W, BRANCH_W), 0)
    dst_head = _iota((BRANCH_W, BRANCH_W), 1) // HEAD_DIM
    e = jnp.where(src == lane0 + dst_head, 1.0, 0.0).astype(BF16)
    return _dot(small, e)


def _gla_kernel(q_ref, k_ref, v_ref, sm_ref, a2_ref, ab_ref, o_ref, st_ref, *, reverse, bsz):
    @pl.when(pl.program_id(0) == 0)
    def _():
        st_ref[...] = jnp.zeros_like(st_ref)

    cum_mat = _cum_mat(reverse)
    tri = _tri_hs(reverse, strict=False)
    blk = _blk_mask()
    a2 = a2_ref[...]
    ab = ab_ref[...]
    steps = _scan_steps(reverse, bsz)
    la = [_log_sigmoid(_dot(sm_ref[b, rows, :], a2) + ab) / GLA_NORMALIZER for b, rows in steps]
    cum = [_dot_hl(cum_mat, x) for x in la]
    tot = [jnp.sum(x, axis=0, keepdims=True) for x in la]
    q_in = [(q_ref[b, rows, :].astype(F32) * (HEAD_DIM ** -0.5) * jnp.exp(c)).astype(BF16)
            for (b, rows), c in zip(steps, cum)]
    k_out = [(k_ref[b, rows, :].astype(F32) * jnp.exp(-c)).astype(BF16) for (b, rows), c in zip(steps, cum)]
    k_end = [(k_ref[b, rows, :].astype(F32) * jnp.exp(t - c)).astype(BF16)
             for (b, rows), c, t in zip(steps, cum, tot)]
    att = [jnp.where(tri, _dot_nt(qi, _stack4(ko)), 0.0).astype(BF16) for qi, ko in zip(q_in, k_out)]
    o_intra = [_dot(a, _stack4(v_ref[b, rows, :])) for (b, rows), a in zip(steps, att)]
    kv_t = [jnp.where(blk, _dot_tn(v_ref[b, rows, :], ke), 0.0) for (b, rows), ke in zip(steps, k_end)]
    for i, (b, rows) in enumerate(steps):
        st = st_ref[b]
        o_ref[b, rows, :] = (o_intra[i] + _dot_nt(q_in[i], st.astype(BF16))).astype(o_ref.dtype)
        st_ref[b] = st * jnp.exp(tot[i]) + kv_t[i]


def _gla_scan(proj, a2pad, ab, reverse):
    return _scan_call(_gla_kernel, "gla", [proj] * 4, [BLK_GLA_Q, BLK_GLA_K, BLK_GLA_V, SMALL_BLOCK],
                      [a2pad, ab], [(BRANCH_W, BRANCH_W)], reverse)


def _gla_params(a2, ab, d):
    lane0 = SM_GLA_RB if d else SM_GLA_RF
    pad = jnp.zeros((BRANCH_W, BRANCH_W), F32).at[lane0:lane0 + GLA_RANK].set(a2[d])
    return pad.astype(BF16), ab[d].reshape(1, BRANCH_W)


def _gla_branch(proj, a2, ab):
    return tuple(_gla_scan(proj, *_gla_params(a2, ab, d), reverse=bool(d)) for d in (0, 1))


def _mlstm_kernel(q_ref, k_ref, v_ref, sm_ref, bias_ref, o_ref, ct_ref, nm_ref, *, reverse, bsz):
    @pl.when(pl.program_id(0) == 0)
    def _():
        ct_ref[...] = jnp.zeros_like(ct_ref)
        nm_ref[...] = jnp.zeros_like(nm_ref)

    lane0 = SM_ML_B if reverse else SM_ML_F
    cum_mat = _cum_mat(reverse)
    tri = _tri_hs(reverse, strict=False)
    blk = _blk_mask()
    ones_blk = _ones_blk()
    b_i = bias_ref[0:1, :]
    b_f = bias_ref[1:2, :]
    steps = _scan_steps(reverse, bsz)
    each = lambda f, *ls: [f(*a) for a in zip(*ls)]
    sm = [sm_ref[b, rows, :] for b, rows in steps]
    q = [q_ref[b, rows, :] for b, rows in steps]
    i_pre = [_expand_small(x, lane0, N_HEADS) + b_i for x in sm]
    log_f = [_log_sigmoid(_expand_small(x, lane0 + N_HEADS, N_HEADS) + b_f) for x in sm]
    f_cum = [_dot_hl(cum_mat, x) for x in log_f]
    f_tot = [jnp.sum(x, axis=0, keepdims=True) for x in log_f]
    u = each(lambda i, f: i - f, i_pre, f_cum)
    lw = each(lambda t, x: t + x, f_tot, u)
    a_end = [jnp.max(x, axis=0, keepdims=True) for x in lw]
    k = [k_ref[b, rows, :].astype(F32) * (HEAD_DIM ** -0.5) for b, rows in steps]
    kw = each(lambda kk, x, a: kk * jnp.exp(x - a), k, lw, a_end)
    k_sum = [jnp.sum(x, axis=0, keepdims=True) for x in kw]
    kv_t = [jnp.where(blk, _dot_tn(v_ref[b, rows, :], x.astype(BF16)), 0.0) for (b, rows), x in zip(steps, kw)]
    log_d = each(lambda f, x: jnp.where(tri, f + _row_of(x), NEG_BIG), f_cum, u)
    mx = [_seg_max(x) for x in log_d]
    s = each(lambda qq, kk, ld, m: (_dot_nt(qq, _stack4(kk.astype(BF16))) * jnp.exp(ld - m)).astype(BF16),
             q, k, log_d, mx)
    num1 = [_dot(x, _stack4(v_ref[b, rows, :])) for (b, rows), x in zip(steps, s)]
    den1 = [_dot(x, ones_blk) for x in s]
    for i, (b, rows) in enumerate(steps):
        n_in = nm_ref[b, 0:1, :]
        m_in = nm_ref[b, 1:2, :]
        ct = ct_ref[b]
        g = f_cum[i] + m_in
        m_t = jnp.maximum(g, mx[i])
        e = jnp.exp(g - m_t)
        r = jnp.exp(mx[i] - m_t)
        num = r * num1[i] + e * _dot_nt(q[i], ct.astype(BF16))
        den = r * den1[i] + e * _seg_sum(q[i].astype(F32) * n_in, ones_blk)
        o_ref[b, rows, :] = (num / jnp.maximum(jnp.abs(den), jnp.exp(-m_t))).astype(o_ref.dtype)

        m_new = jnp.maximum(f_tot[i] + m_in, a_end[i])
        old = jnp.exp(f_tot[i] + m_in - m_new)
        new = jnp.exp(a_end[i] - m_new)
        ct_ref[b] = ct * old + kv_t[i] * new
        nm_ref[b, 0:1, :] = n_in * old + k_sum[i] * new
        nm_ref[b, 1:2, :] = m_new


def _mlstm_scan(proj, bias, reverse):
    return _scan_call(_mlstm_kernel, "mlstm", [proj] * 4, [BLK_ML_Q, BLK_ML_K, BLK_ML_V, SMALL_BLOCK],
                      [bias], [(BRANCH_W, BRANCH_W), (8, BRANCH_W)], reverse)


def _head_rows(vals):
    rows = [jnp.repeat(v.astype(F32), HEAD_DIM) for v in vals]
    rows += [jnp.zeros((BRANCH_W,), F32)] * (8 - len(rows))
    return jnp.stack(rows)


def _mlstm_branch(proj, gate_b):
    return tuple(_mlstm_scan(proj, _head_rows([gate_b[d, 0], gate_b[d, 1]]), reverse=bool(d)) for d in (0, 1))


HALO = 8
QKV_W = 3 * BRANCH_W


def _gdn_prep_kernel(prev_ref, cur_ref, next_ref, w_ref, o_ref, *, n_ctx_tiles, n_tiles):
    i = pl.program_id(1)
    has_prev = jnp.logical_and(i != 0, i != n_ctx_tiles).astype(F32)
    has_next = jnp.logical_and(i != n_ctx_tiles - 1, i != n_tiles - 1).astype(F32)
    padded = jnp.concatenate([prev_ref[...].astype(F32) * has_prev, cur_ref[...].astype(F32),
                              next_ref[...].astype(F32) * has_next], axis=0)
    w = w_ref[...]
    acc = jnp.zeros((ROW_TILE, QKV_W), F32)
    for j in range(CONV_W):
        off = HALO + j - CONV_W // 2
        acc = acc + padded[off:off + ROW_TILE, :] * w[j:j + 1, :]
    y = _silu(acc)
    ones_blk = _ones_blk()
    q = y[:, 0:BRANCH_W]
    k = y[:, BRANCH_W:2 * BRANCH_W]
    q = q * lax.rsqrt(_seg_sum(q * q, ones_blk) + EPS) * (HEAD_DIM ** -0.5)
    k = k * lax.rsqrt(_seg_sum(k * k, ones_blk) + EPS)
    o_ref[:, 0:BRANCH_W] = q.astype(o_ref.dtype)
    o_ref[:, BRANCH_W:2 * BRANCH_W] = k.astype(o_ref.dtype)
    o_ref[:, 2 * BRANCH_W:] = y[:, 2 * BRANCH_W:].astype(o_ref.dtype)


def _gdn_prep(proj, conv_w, n_ctx_tiles):
    bsz, t, _ = proj.shape
    nt = t // ROW_TILE
    per = ROW_TILE // HALO
    last = t // HALO - 1
    qkv_blk = BLK_GDN_QKV * BRANCH_W // QKV_W
    w8 = jnp.zeros((8, QKV_W), F32).at[:CONV_W].set(conv_w)
    return pl.pallas_call(
        functools.partial(_gdn_prep_kernel, n_ctx_tiles=n_ctx_tiles, n_tiles=nt),
        grid=(bsz, nt),
        in_specs=[pl.BlockSpec((None, HALO, QKV_W), lambda b, i: (b, jnp.maximum(i * per - 1, 0), qkv_blk)),
                  pl.BlockSpec((None, ROW_TILE, QKV_W), lambda b, i: (b, i, qkv_blk)),
                  pl.BlockSpec((None, HALO, QKV_W), lambda b, i: (b, jnp.minimum((i + 1) * per, last), qkv_blk)),
                  pl.BlockSpec((8, QKV_W), lambda b, i: (0, 0))],
        out_specs=pl.BlockSpec((None, ROW_TILE, QKV_W), lambda b, i: (b, i, 0)),
        out_shape=jax.ShapeDtypeStruct((bsz, t, QKV_W), BF16),
        compiler_params=_cparams("parallel", "parallel"),
        name="gdn_prep",
    )(proj, proj, proj, w8)


def _gdn_kernel(q_ref, k_ref, v_ref, sm_ref, par_ref, o_ref, s_ref, *, reverse, bsz):
    @pl.when(pl.program_id(0) == 0)
    def _():
        s_ref[...] = jnp.zeros_like(s_ref)

    lane0 = SM_GD_B if reverse else SM_GD_F
    cum_mat = _cum_mat(reverse)
    tri = _tri_hs(reverse, strict=False)
    tri_strict = _tri_hs(reverse, strict=True)
    blk = _blk_mask()
    eye = jnp.where(_eye_hs(), 1.0, 0.0)
    a_scale = jnp.exp(par_ref[0:1, :])
    dt_bias = par_ref[1:2, :]
    steps = _scan_steps(reverse, bsz)
    each = lambda f, *ls: [f(*a) for a in zip(*ls)]
    sm = [sm_ref[b, rows, :] for b, rows in steps]
    q = [q_ref[b, rows, :] for b, rows in steps]
    kb16 = [k_ref[b, rows, :] for b, rows in steps]
    beta = [jax.nn.sigmoid(_expand_small(x, lane0, N_HEADS)) for x in sm]
    g = [-a_scale * _softplus(_expand_small(x, lane0 + N_HEADS, N_HEADS) + dt_bias) for x in sm]
    cum = [_dot_hl(cum_mat, x) for x in g]
    tot = [jnp.sum(x, axis=0, keepdims=True) for x in g]
    gam = [jnp.where(tri, jnp.exp(jnp.where(tri, c - _row_of(c), 0.0)), 0.0) for c in cum]
    k_beta = each(lambda kk, bb: kk.astype(F32) * bb, kb16, beta)
    k4 = [_stack4(x) for x in kb16]
    a_hs = each(lambda kb, kk, gm: jnp.where(tri_strict, _dot_nt(kb.astype(BF16), kk) * gm, 0.0), k_beta, k4, gam)
    attn = each(lambda qq, kk, gm: (_dot_nt(qq, kk) * gm).astype(BF16), q, k4, gam)

    p = [-x for x in a_hs]
    t_inv = [eye + x for x in p]
    for _ in range(5):
        p = [_dot(x.astype(BF16), _stack4(x.astype(BF16))) for x in p]
        t_inv = each(lambda t, x: t + _dot(t.astype(BF16), _stack4(x.astype(BF16))), t_inv, p)
    w = each(lambda t, kb, c: _dot(t.astype(BF16), _stack4((kb * jnp.exp(c)).astype(BF16))).astype(BF16),
             t_inv, k_beta, cum)
    u = [_dot(t.astype(BF16), _stack4((v_ref[b, rows, :].astype(F32) * bb).astype(BF16)))
         for (b, rows), t, bb in zip(steps, t_inv, beta)]
    q_dec = each(lambda qq, c: (qq.astype(F32) * jnp.exp(c)).astype(BF16), q, cum)
    k_end = each(lambda kk, t, c: (kk.astype(F32) * jnp.exp(t - c)).astype(BF16), kb16, tot, cum)
    for i, (b, rows) in enumerate(steps):
        s = s_ref[b]
        sb = s.astype(BF16)
        v_new = (u[i] - _dot(w[i], sb)).astype(BF16)
        o_ref[b, rows, :] = (_dot(q_dec[i], sb) + _dot(attn[i], _stack4(v_new))).astype(o_ref.dtype)
        s_ref[b] = s * jnp.exp(tot[i]) + jnp.where(blk, _dot_tn(k_end[i], v_new), 0.0)


def _gdn_scan(gqkv, proj, par, reverse):
    return _scan_call(_gdn_kernel, "gdn", [gqkv, gqkv, gqkv, proj], [0, 1, 2, SMALL_BLOCK],
                      [par], [(BRANCH_W, BRANCH_W)], reverse)


def _gdn_branch(proj, conv_w, a_log, dt_bias, n_ctx_tiles):
    gqkv = _gdn_prep(proj, conv_w, n_ctx_tiles)
    return tuple(_gdn_scan(gqkv, proj, _head_rows([a_log[d], dt_bias[d]]), reverse=bool(d))
                 for d in (0, 1))


VX_W = 128
LOG2E = 1.4426950408889634


def _kv_tile(t):
    return next(k for k in (3 * ROW_TILE, 2 * ROW_TILE, ROW_TILE) if t % k == 0)


def _rope_tables(n_lat, n_ctx):
    pos = jnp.arange(n_lat)
    row, col = pos // GRID_W, pos % GRID_W
    inv = ROPE_BASE ** (-jnp.arange(0, ROPE_AXIS, 2, dtype=F32) / ROPE_AXIS)
    ang = jnp.concatenate([row.astype(F32)[:, None] * inv, col.astype(F32)[:, None] * inv], axis=-1)
    cos = jnp.concatenate([jnp.ones((n_ctx, ROPE_AXIS), F32), jnp.cos(ang)], axis=0)
    sin = jnp.concatenate([jnp.zeros((n_ctx, ROPE_AXIS), F32), jnp.sin(ang)], axis=0)
    reps = BRANCH_W // DIFF_QK
    cos_l = jnp.tile(jnp.concatenate([cos, cos], axis=-1), (1, reps))
    sin_l = jnp.tile(jnp.concatenate([-sin, sin], axis=-1), (1, reps))
    return cos_l, sin_l


def _rope_prep_kernel(x_ref, cos_ref, sin_ref, q_ref, kt_ref, vx_ref):
    lane = _iota((ROW_TILE, BRANCH_W), 1)
    first_half = (lane % DIFF_QK) < ROPE_AXIS
    cos = cos_ref[...]
    sin = sin_ref[...]

    def rope(x):
        partner = jnp.where(first_half, pltpu.roll(x, BRANCH_W - ROPE_AXIS, 1), pltpu.roll(x, ROPE_AXIS, 1))
        return x * cos + partner * sin

    q = rope(x_ref[:, 0:BRANCH_W].astype(F32)) * (DIFF_QK ** -0.5 * LOG2E)
    k = rope(x_ref[:, BRANCH_W:2 * BRANCH_W].astype(F32))
    q_ref[...] = q.astype(q_ref.dtype)
    kt_ref[...] = jnp.transpose(k).astype(kt_ref.dtype)
    v = x_ref[:, 2 * BRANCH_W:]
    src = _iota((BRANCH_W, VX_W), 0)
    dst = _iota((BRANCH_W, VX_W), 1)
    ones_col = jnp.where(_iota((ROW_TILE, VX_W), 1) == HEAD_DIM, 1.0, 0.0)
    for h in range(N_HEADS):
        sel = jnp.where(jnp.logical_and(src == h * HEAD_DIM + dst, dst < HEAD_DIM), 1.0, 0.0).astype(BF16)
        vx_ref[h] = (_dot(v, sel) + ones_col).astype(vx_ref.dtype)


def _rope_prep(proj, cos_l, sin_l):
    bsz, t, _ = proj.shape
    nt = t // ROW_TILE
    kvt = _kv_tile(t)
    r = kvt // ROW_TILE
    qkv_blk = BLK_DIFF_QKV * BRANCH_W // QKV_W
    tab = pl.BlockSpec((ROW_TILE, BRANCH_W), lambda b, i: (i, 0))
    return pl.pallas_call(
        _rope_prep_kernel,
        grid=(bsz, nt),
        in_specs=[pl.BlockSpec((None, ROW_TILE, QKV_W), lambda b, i: (b, i, qkv_blk)), tab, tab],
        out_specs=[pl.BlockSpec((None, ROW_TILE, BRANCH_W), lambda b, i: (b, i, 0)),
                   pl.BlockSpec((None, None, BRANCH_W, ROW_TILE), lambda b, i: (b, i // r, 0, i % r)),
                   pl.BlockSpec((None, None, N_HEADS, ROW_TILE, VX_W), lambda b, i: (b, i // r, 0, i % r, 0))],
        out_shape=[jax.ShapeDtypeStruct((bsz, t, BRANCH_W), BF16),
                   jax.ShapeDtypeStruct((bsz, t // kvt, BRANCH_W, kvt), BF16),
                   jax.ShapeDtypeStruct((bsz, t // kvt, N_HEADS, kvt, VX_W), BF16)],
        compiler_params=_cparams("parallel", "parallel"),
        name="rope_prep",
    )(proj, cos_l, sin_l)


N_CHAINS = 2 * N_HEADS


def _attn_kernel(q_ref, kt_ref, vx_ref, lp_ref, o_ref, qm_scr, s0_scr, s1_scr, m0_scr, m1_scr, al0_scr, al1_scr,
                 acc_scr, *, nk, lam_init):
    s_bufs, m_bufs, al_bufs = (s0_scr, s1_scr), (m0_scr, m1_scr), (al0_scr, al1_scr)
    q = q_ref[...]
    lane = _iota(q.shape, 1)
    zero = jnp.zeros_like(q)
    for c in range(N_CHAINS):
        lo = c * DIFF_QK
        qm_scr[c] = jnp.where(jnp.logical_and(lane >= lo, lane < lo + DIFF_QK), q, zero)
    acc_scr[...] = jnp.zeros_like(acc_scr)

    def stage_a(c, kt, slot, first=False):
        s = _dot(qm_scr[c], kt)
        s_bufs[slot][c] = s
        row_max = jnp.max(s, axis=1, keepdims=True)
        if first:
            m_bufs[slot][c] = row_max
            al_bufs[slot][c] = jnp.zeros_like(row_max)
        else:
            m_old = m_bufs[1 - slot][c]
            m_new = jnp.maximum(m_old, row_max)
            m_bufs[slot][c] = m_new
            al_bufs[slot][c] = jnp.exp2(m_old - m_new)

    def stage_b(c, j, slot):
        p = jnp.exp2(s_bufs[slot][c] - m_bufs[slot][c]).astype(BF16)
        acc_scr[c] = acc_scr[c] * al_bufs[slot][c] + _dot(p, vx_ref[j, c // 2])

    def step(ja, jb, slot_a, first=False):
        kt = None if ja is None else kt_ref[ja]
        for c in range(N_CHAINS):
            if ja is not None:
                stage_a(c, kt, slot_a, first)
            if jb is not None:
                stage_b(c, jb, 1 - slot_a)

    step(0, None, 0, first=True)

    def body(i, carry):
        j = 2 * i + 1
        step(j, j - 1, 1)
        step(j + 1, j, 0)
        return carry

    n_pairs = (nk - 1) // 2
    lax.fori_loop(0, n_pairs, body, 0)
    if nk % 2 == 0:
        step(nk - 1, nk - 2, 1)
    step(None, nk - 1, 1 - (nk - 1) % 2)

    lp = lp_ref[...]
    grp_src = _iota((VX_W, VX_W), 0) // DIFF_QK
    prod1 = lp[0:1, :] * lp[1:2, :]
    prod2 = lp[2:3, :] * lp[3:4, :]
    src = _iota((VX_W, BRANCH_W), 0)
    dst = _iota((VX_W, BRANCH_W), 1)
    out = jnp.zeros((ROW_TILE, BRANCH_W), F32)
    for h in range(N_HEADS):
        pick = jnp.where(grp_src == h, 1.0, 0.0).astype(BF16)
        lam = jnp.exp(_dot_hl_r(prod1, pick)) - jnp.exp(_dot_hl_r(prod2, pick)) + lam_init
        a0 = acc_scr[2 * h]
        a1 = acc_scr[2 * h + 1]
        o_h = a0 / a0[:, HEAD_DIM:HEAD_DIM + 1] - lam * (a1 / a1[:, HEAD_DIM:HEAD_DIM + 1])
        place = jnp.where(jnp.logical_and(dst == src + h * HEAD_DIM, src < HEAD_DIM), 1.0, 0.0).astype(BF16)
        out = out + _dot(o_h.astype(BF16), place)
    o_ref[...] = out.astype(o_ref.dtype)


def _diff_attention(q_rot, kt, vx, lam_p, lam_init, q_tile0, n_q_tiles, kv_len):
    bsz = q_rot.shape[0]
    n_kv_arr, _, kvt = kt.shape[1:]
    tk = min(kv_len, kvt)
    assert kv_len % tk == 0 and kvt % tk == 0
    nk = kv_len // tk
    lp = jnp.zeros((8, VX_W), F32).at[:4].set(lam_p.reshape(4, N_HEADS * DIFF_QK))
    return pl.pallas_call(
        functools.partial(_attn_kernel, nk=nk, lam_init=lam_init),
        grid=(bsz, n_q_tiles),
        in_specs=[pl.BlockSpec((None, ROW_TILE, BRANCH_W), lambda b, i: (b, i + q_tile0, 0)),
                  pl.BlockSpec((None, nk, BRANCH_W, tk), lambda b, i: (b, 0, 0, 0), pipeline_mode=pl.Buffered(1)),
                  pl.BlockSpec((None, nk, N_HEADS, tk, VX_W), lambda b, i: (b, 0, 0, 0, 0),
                               pipeline_mode=pl.Buffered(1)),
                  pl.BlockSpec((8, VX_W), lambda b, i: (0, 0))],
        out_specs=pl.BlockSpec((None, ROW_TILE, BRANCH_W), lambda b, i: (b, i, 0)),
        out_shape=jax.ShapeDtypeStruct((bsz, n_q_tiles * ROW_TILE, BRANCH_W), BF16),
        scratch_shapes=[pltpu.VMEM((N_CHAINS, ROW_TILE, BRANCH_W), BF16)]
                       + [pltpu.VMEM((N_CHAINS, ROW_TILE, tk), F32)] * 2
                       + [pltpu.VMEM((N_CHAINS, ROW_TILE, 1), F32)] * 4
                       + [pltpu.VMEM((N_CHAINS, ROW_TILE, VX_W), F32)],
        compiler_params=_cparams("parallel", "arbitrary"),
        name="diff_attention",
    )(q_rot, kt, vx, lp)


def _diff_branch(proj, tables, lam_p, lam_init, n_ctx_tiles, with_ctx):
    t = proj.shape[1]
    q_rot, kt, vx = _rope_prep(proj, *tables)
    lat = _diff_attention(q_rot, kt, vx, lam_p, lam_init, n_ctx_tiles, t // ROW_TILE - n_ctx_tiles, t)
    if not with_ctx:
        return lat
    ctx = _diff_attention(q_rot, kt, vx, lam_p, lam_init, 0, n_ctx_tiles, n_ctx_tiles * ROW_TILE)
    return jnp.concatenate([ctx, lat], axis=1)


def _prepare(x, c, ctx, c_ctx, w_mod, b_mod):
    bsz, _, d = x.shape
    ct = ctx.shape[1]
    assert ct % ROW_TILE == 0 and x.shape[1] % ROW_TILE == 0
    xa = jnp.concatenate([ctx, x], axis=1)
    rows = 8 * ((bsz + 1 + 7) // 8)
    cond = jnp.zeros((rows, d), F32).at[:bsz].set(c).at[bsz].set(c_ctx)
    mods = _modulation(cond, w_mod, b_mod)
    return dict(xa=xa, mods=mods, n_ctx_tiles=ct // ROW_TILE, ct=ct)


def _layer_inproj(st, l, norm_mix, w_in):
    mods3 = st["mods"][l][:, None, :]
    return _inproj(st["xa"], mods3, norm_mix[l], _permute_w_in(w_in[l]), st["n_ctx_tiles"])


def _merge_kernel(x_ref, mod_ref, glf_ref, glb_ref, mlf_ref, mlb_ref, gdf_ref, gdb_ref, at_ref,
                  g_gla_ref, g_ml_ref, g_gd_ref, mg0_ref, mg1_ref, mg2_ref, mg3_ref,
                  hn_ref, wb_ref, wo_ref, o_ref, *, d, lam_init):
    ones_blk = _ones_blk()

    def head_norm(o, i):
        ms = _seg_sum(o * o, ones_blk) * (1.0 / HEAD_DIM)
        return o * lax.rsqrt(ms + EPS) * hn_ref[i:i + 1, :]

    f32 = lambda r: r[...].astype(F32)
    ys = [head_norm(f32(glf_ref) + f32(glb_ref), 0) * _silu(f32(g_gla_ref)),
          head_norm(f32(mlf_ref) + f32(mlb_ref), 1) * jax.nn.sigmoid(f32(g_ml_ref)),
          head_norm(f32(gdf_ref) + f32(gdb_ref), 2) * _silu(f32(g_gd_ref)),
          head_norm(f32(at_ref), 3) * (1.0 - lam_init)]
    acc = None
    for i, (y, mg_ref) in enumerate(zip(ys, (mg0_ref, mg1_ref, mg2_ref, mg3_ref))):
        term = jax.nn.sigmoid(f32(mg_ref)) * _dot(y.astype(BF16), wb_ref[i])
        acc = term if acc is None else acc + term
    out = _dot(acc.astype(BF16), wo_ref[...])
    g1 = mod_ref[...][:, 2 * d:3 * d]
    o_ref[...] = x_ref[...] + g1 * out


def _merge(xa, mods3, proj, scans, attn, hn, wb, wo, n_ctx_tiles, with_ctx, lam_init):
    bsz, t, d = xa.shape
    off = 0 if with_ctx else n_ctx_tiles
    nt = t // ROW_TILE - off
    rows = lambda col: pl.BlockSpec((None, ROW_TILE, BRANCH_W), lambda b, i: (b, i + off, col))
    gate = lambda j: pl.BlockSpec((None, ROW_TILE, d), lambda b, i: (b, i + off, BLK_GATES * BRANCH_W // d + j))
    const = lambda shape: pl.BlockSpec(shape, lambda b, i: (0,) * len(shape))
    return pl.pallas_call(
        functools.partial(_merge_kernel, d=d, lam_init=lam_init),
        grid=(bsz, nt),
        in_specs=[pl.BlockSpec((None, ROW_TILE, d), lambda b, i: (b, i + off, 0)),
                  pl.BlockSpec((None, 1, 6 * d), lambda b, i: (jnp.where(i + off < n_ctx_tiles, bsz, b), 0, 0))]
                 + [rows(0)] * 6
                 + [pl.BlockSpec((None, ROW_TILE, BRANCH_W), lambda b, i: (b, i, 0))]
                 + [rows(BLK_GLA_G), rows(BLK_ML_G), rows(BLK_GDN_G)]
                 + [gate(j) for j in range(N_BRANCH)]
                 + [const((8, BRANCH_W)), const((N_BRANCH, BRANCH_W, d)), const((d, d))],
        out_specs=pl.BlockSpec((None, ROW_TILE, d), lambda b, i: (b, i, 0)),
        out_shape=jax.ShapeDtypeStruct((bsz, nt * ROW_TILE, d), F32),
        compiler_params=_cparams("parallel", "parallel"),
        name="merge",
    )(xa, mods3, *scans, attn, proj, proj, proj, proj, proj, proj, proj, hn, wb, wo)


GATE_LANES = 128


def _router_gates_t(hf, rw_ref, rb_ref, as_gates=True):
    h_hi, h_lo = _split_hl(hf)
    w_hi, w_lo = _split_hl(rw_ref[...])
    logits = _dot_nt(w_hi, h_hi) + _dot_nt(w_hi, h_lo) + _dot_nt(w_lo, h_hi)
    scores = jax.nn.sigmoid(logits)
    sel = scores + rb_ref[...]
    s = [sel[e:e + 1, :] for e in range(N_EXPERTS)]
    sc = [scores[e:e + 1, :] for e in range(N_EXPERTS)]
    grp = []
    for g in range(N_GROUPS):
        a, b, c, dd = s[4 * g:4 * g + 4]
        grp.append(functools.reduce(jnp.maximum, [a + b, a + c, a + dd, b + c, b + dd, c + dd]))
    gmax = functools.reduce(jnp.maximum, grp)
    chosen, taken = [], None
    for g in range(N_GROUPS):
        hit = grp[g] == gmax
        if taken is not None:
            hit = jnp.logical_and(hit, jnp.logical_not(taken))
        taken = hit if taken is None else jnp.logical_or(taken, hit)
        chosen.append(hit)
    ms = [jnp.where(chosen[e // EXPERTS_PER_GROUP], s[e], NEG_BIG) for e in range(N_EXPERTS)]

    def first_argmax(vals):
        top = functools.reduce(jnp.maximum, vals)
        hits, seen = [], None
        for v in vals:
            hit = v == top
            if seen is not None:
                hit = jnp.logical_and(hit, jnp.logical_not(seen))
            seen = hit if seen is None else jnp.logical_or(seen, hit)
            hits.append(hit)
        return hits

    oh1 = first_argmax(ms)
    oh2 = first_argmax([jnp.where(o, NEG_BIG, v) for o, v in zip(oh1, ms)])
    zero = jnp.zeros_like(sc[0])
    w1 = functools.reduce(jnp.add, [jnp.where(o, v, zero) for o, v in zip(oh1, sc)])
    w2 = functools.reduce(jnp.add, [jnp.where(o, v, zero) for o, v in zip(oh2, sc)])
    tot = w1 + w2
    if not as_gates:
        ids = [functools.reduce(jnp.add, [jnp.where(o, float(e), 0.0) for e, o in enumerate(oh)]) for oh in (oh1, oh2)]
        return ids[0], ids[1], w1 / tot, w2 / tot
    rows = [jnp.where(o1, w1 / tot, zero) + jnp.where(o2, w2 / tot, zero) for o1, o2 in zip(oh1, oh2)]
    return jnp.concatenate(rows, axis=0)


def _moe_dense_kernel(x_ref, *rest, d, bsz):
    mod_refs = rest[:bsz]
    g_ref, rw_ref, rb_ref, wg_ref, wu_ref, wd_ref, o_ref, h_scr, gate_scr, acc_scr = rest[bsz:]
    e = pl.program_id(1)

    @pl.when(e == 0)
    def _():
        eye = jnp.where(_iota((N_EXPERTS, GATE_LANES), 0) == _iota((N_EXPERTS, GATE_LANES), 1), 1.0, 0.0).astype(BF16)
        for b in range(bsz):
            rows = slice(b * ROW_TILE, (b + 1) * ROW_TILE)
            mod = mod_refs[b][...]
            hf = _rms_mod(x_ref[b], g_ref[...], mod[:, 3 * d:4 * d], mod[:, 4 * d:5 * d])
            h_scr[rows, :] = hf.astype(BF16)
            g_hi, g_lo = _split_hl(_router_gates_t(hf, rw_ref, rb_ref))
            gate_scr[rows, :] = _dot_tn(g_hi, eye) + _dot_tn(g_lo, eye)
        acc_scr[...] = jnp.zeros_like(acc_scr)

    h = h_scr[...]
    pick = jnp.where(_iota((GATE_LANES, GATE_LANES), 0) == e, 1.0, 0.0).astype(BF16)
    ge = _dot_hl_r(gate_scr[...], pick)[:, 0:1]
    act = (_silu(_dot(h, wg_ref[...])) * _dot(h, wu_ref[...])).astype(BF16)
    acc_scr[...] += ge * _dot(act, wd_ref[...])

    @pl.when(e == pl.num_programs(1) - 1)
    def _():
        for b in range(bsz):
            g2 = mod_refs[b][...][:, 5 * d:6 * d]
            o_ref[b] = x_ref[b] + g2 * acc_scr[b * ROW_TILE:(b + 1) * ROW_TILE, :]


def _moe_dense(xa, mods3, g, rw_t, rb, wg, wu, wd, n_ctx_tiles):
    bsz, t, d = xa.shape
    ne, _, de = wg.shape
    mod_spec = lambda b: pl.BlockSpec((None, 1, 6 * d), lambda i, e: (jnp.where(i < n_ctx_tiles, bsz, b), 0, 0))
    return pl.pallas_call(
        functools.partial(_moe_dense_kernel, d=d, bsz=bsz),
        grid=(t // ROW_TILE, ne),
        in_specs=[pl.BlockSpec((bsz, ROW_TILE, d), lambda i, e: (0, i, 0))]
                 + [mod_spec(b) for b in range(bsz)]
                 + [pl.BlockSpec((1, d), lambda i, e: (0, 0)),
                    pl.BlockSpec((ne, d), lambda i, e: (0, 0)),
                    pl.BlockSpec((ne, ROW_TILE), lambda i, e: (0, 0)),
                    pl.BlockSpec((None, d, de), lambda i, e: (e, 0, 0)),
                    pl.BlockSpec((None, d, de), lambda i, e: (e, 0, 0)),
                    pl.BlockSpec((None, de, d), lambda i, e: (e, 0, 0))],
        out_specs=pl.BlockSpec((bsz, ROW_TILE, d), lambda i, e: (0, i, 0)),
        out_shape=jax.ShapeDtypeStruct((bsz, t, d), F32),
        scratch_shapes=[pltpu.VMEM((bsz * ROW_TILE, d), BF16), pltpu.VMEM((bsz * ROW_TILE, GATE_LANES), F32),
                        pltpu.VMEM((bsz * ROW_TILE, d), F32)],
        compiler_params=_cparams("parallel", "arbitrary"),
        name="moe_dense",
    )(xa, *([mods3] * bsz), g.reshape(1, d), rw_t, rb, wg, wu, wd)


TOK_ROWS = 8
FFN_TILE = ROW_TILE


def _store_token_tiles(ref, val):
    n = val.shape[0]
    for s in range(TOK_ROWS):
        ref[pl.ds(s, n, stride=TOK_ROWS), :] = val[:, s * 128:(s + 1) * 128]


def _load_token_tiles(ref, n):
    return jnp.concatenate([ref[pl.ds(s, n, stride=TOK_ROWS), :] for s in range(TOK_ROWS)], axis=1)


def _moe_router_kernel(x_ref, mod_ref, g_ref, rw_ref, rb_ref, h_ref, ids_ref, wcol_ref, *, d):
    mod = mod_ref[...]
    hf = _rms_mod(x_ref[...], g_ref[...], mod[:, 3 * d:4 * d], mod[:, 4 * d:5 * d])
    _store_token_tiles(h_ref, hf)
    e1, e2, w1, w2 = _router_gates_t(hf, rw_ref, rb_ref, as_gates=False)
    pad = jnp.zeros((6, ROW_TILE), F32)
    ids_ref[...] = jnp.concatenate([e1, e2, pad], axis=0).astype(jnp.int32)
    eye = jnp.where(_iota((8, GATE_LANES), 0) == _iota((8, GATE_LANES), 1), 1.0, 0.0).astype(BF16)
    w_hi, w_lo = _split_hl(jnp.concatenate([w1, w2, pad], axis=0))
    wcol_ref[...] = _dot_tn(w_hi, eye) + _dot_tn(w_lo, eye)


def _moe_router(xa, mods3, g, rw_t, rb, n_ctx_tiles):
    bsz, t, d = xa.shape
    nt = t // ROW_TILE
    n = bsz * t
    flat = lambda b, i: b * nt + i
    return pl.pallas_call(
        functools.partial(_moe_router_kernel, d=d),
        grid=(bsz, nt),
        in_specs=[pl.BlockSpec((None, ROW_TILE, d), lambda b, i: (b, i, 0)),
                  pl.BlockSpec((None, 1, 6 * d), _mod_row_map(n_ctx_tiles, bsz)),
                  pl.BlockSpec((1, d), lambda b, i: (0, 0)),
                  pl.BlockSpec((N_EXPERTS, d), lambda b, i: (0, 0)),
                  pl.BlockSpec((N_EXPERTS, ROW_TILE), lambda b, i: (0, 0))],
        out_specs=[pl.BlockSpec((ROW_TILE * TOK_ROWS, 128), lambda b, i: (flat(b, i), 0)),
                   pl.BlockSpec((8, ROW_TILE), lambda b, i: (0, flat(b, i))),
                   pl.BlockSpec((ROW_TILE, GATE_LANES), lambda b, i: (flat(b, i), 0))],
        out_shape=[jax.ShapeDtypeStruct((n * TOK_ROWS, 128), F32),
                   jax.ShapeDtypeStruct((8, n), jnp.int32),
                   jax.ShapeDtypeStruct((n, GATE_LANES), F32)],
        compiler_params=_cparams("parallel", "parallel"),
        name="moe_router",
    )(xa, mods3, g.reshape(1, d), rw_t, rb)


def _moe_plan(ids, n):
    n_tiles = -(-2 * n // FFN_TILE) + N_EXPERTS
    eid = ids[:2].reshape(2 * n)
    order = jnp.argsort(eid, stable=True).astype(jnp.int32)
    inv = jnp.argsort(order).astype(jnp.int32)
    experts = jnp.arange(N_EXPERTS, dtype=jnp.int32)
    onehot = (eid[:, None] == experts[None, :]).astype(jnp.int32)
    counts = jnp.sum(onehot, axis=0)
    first = jnp.cumsum(counts) - counts
    padded = (counts + FFN_TILE - 1) // FFN_TILE * FFN_TILE
    ends = jnp.cumsum(padded)
    offs = ends - padded
    dest = inv + jnp.sum(onehot * (offs - first)[None, :], axis=1)
    tile_start = jnp.arange(n_tiles, dtype=jnp.int32) * FFN_TILE
    tile_expert = jnp.minimum(jnp.sum((ends[None, :] <= tile_start[:, None]).astype(jnp.int32), axis=1), N_EXPERTS - 1)
    tile_hot = (tile_expert[:, None] == experts[None, :]).astype(jnp.int32)
    lookup = lambda table: jnp.repeat(jnp.sum(tile_hot * table[None, :], axis=1), FFN_TILE)
    local = jnp.arange(n_tiles * FFN_TILE, dtype=jnp.int32) - lookup(offs)
    tok_sorted = order % n
    src = jnp.where(local < lookup(counts), tok_sorted[jnp.clip(lookup(first) + local, 0, 2 * n - 1)], 0)
    dest_tiles = dest.reshape(2, n // ROW_TILE, ROW_TILE).transpose(1, 0, 2)
    return src.reshape(n_tiles, 1, FFN_TILE), tile_expert, dest_tiles


def _gather_rows(idx_ref, k, src_hbm, dst, sem, n_rows, unrolled):
    def start(r, priority):
        row = idx_ref[k, r]
        pltpu.make_async_copy(src_hbm.at[pl.ds(pl.multiple_of(row * TOK_ROWS, TOK_ROWS), TOK_ROWS), :],
                              dst.at[pl.ds(pl.multiple_of(r * TOK_ROWS, TOK_ROWS), TOK_ROWS), :],
                              sem).start(priority=priority)

    if unrolled:
        for r in range(n_rows):
            start(r, r % 2)
    else:
        def body(r, carry):
            start(r, 0)
            return carry

        lax.fori_loop(0, n_rows, body, 0, unroll=8)


def _wait_rows(src_hbm, dst, sem, n_rows):
    pltpu.make_async_copy(src_hbm.at[pl.ds(0, n_rows * TOK_ROWS), :], dst, sem).wait()


def _moe_experts_kernel(te_ref, src_ref, nxt_ref, h_hbm, wg_ref, wu_ref, wd_ref, y_ref,
                        buf, wg_b, wu_b, wd_b, sem):
    t = pl.program_id(0)
    last = pl.num_programs(0) - 1
    slot = lax.rem(t, 2)

    @pl.when(jnp.logical_or(t == 0, te_ref[t] != te_ref[jnp.maximum(t - 1, 0)]))
    def _():
        wg_b[...] = wg_ref[...].astype(BF16)
        wu_b[...] = wu_ref[...].astype(BF16)
        wd_b[...] = wd_ref[...].astype(BF16)

    @pl.when(t == 0)
    def _():
        _gather_rows(src_ref, 0, h_hbm, buf.at[0], sem.at[0], FFN_TILE, unrolled=False)

    _gather_rows(nxt_ref, 0, h_hbm, buf.at[1 - slot], sem.at[1 - slot], FFN_TILE, unrolled=True)
    _wait_rows(h_hbm, buf.at[slot], sem.at[slot], FFN_TILE)
    h = _load_token_tiles(buf.at[slot], FFN_TILE).astype(BF16)
    act = (_silu(_dot(h, wg_b[...])) * _dot(h, wu_b[...])).astype(BF16)
    _store_token_tiles(y_ref, _dot(act, wd_b[...]))

    @pl.when(t == last)
    def _():
        _wait_rows(h_hbm, buf.at[1 - slot], sem.at[1 - slot], FFN_TILE)


def _moe_experts(h_tt, plan, wg, wu, wd):
    src, tile_expert, _ = plan
    n_tiles = src.shape[0]
    ne, d, de = wg.shape
    idx_spec = lambda step: pl.BlockSpec((None, 1, FFN_TILE), lambda t, te: (jnp.minimum(t + step, n_tiles - 1), 0, 0),
                                         memory_space=pltpu.SMEM)
    grid_spec = pltpu.PrefetchScalarGridSpec(
        num_scalar_prefetch=1,
        grid=(n_tiles,),
        in_specs=[idx_spec(0), idx_spec(1),
                  pl.BlockSpec(memory_space=pl.ANY),
                  pl.BlockSpec((None, d, de), lambda t, te: (te[t], 0, 0)),
                  pl.BlockSpec((None, d, de), lambda t, te: (te[t], 0, 0)),
                  pl.BlockSpec((None, de, d), lambda t, te: (te[t], 0, 0))],
        out_specs=pl.BlockSpec((FFN_TILE * TOK_ROWS, 128), lambda t, te: (t, 0)),
        scratch_shapes=[pltpu.VMEM((2, FFN_TILE * TOK_ROWS, 128), F32),
                        pltpu.VMEM((d, de), BF16), pltpu.VMEM((d, de), BF16), pltpu.VMEM((de, d), BF16),
                        pltpu.SemaphoreType.DMA((2,))])
    return pl.pallas_call(
        _moe_experts_kernel,
        grid_spec=grid_spec,
        out_shape=jax.ShapeDtypeStruct((n_tiles * FFN_TILE * TOK_ROWS, 128), F32),
        compiler_params=_cparams("arbitrary"),
        name="moe_experts",
    )(tile_expert, src, src, h_tt, wg, wu, wd)


def _moe_combine_kernel(dest_ref, nxt_ref, x_ref, mod_ref, wcol_ref, y_hbm, o_ref, buf, sem, *, d):
    j = pl.program_id(0)
    last = pl.num_programs(0) - 1
    slot = lax.rem(j, 2)

    def gather(idx_ref, s, unrolled):
        for k in range(2):
            _gather_rows(idx_ref, k, y_hbm, buf.at[s, k], sem.at[s], ROW_TILE, unrolled)

    def wait(s):
        for k in range(2):
            _wait_rows(y_hbm, buf.at[s, k], sem.at[s], ROW_TILE)

    @pl.when(j == 0)
    def _():
        gather(dest_ref, 0, unrolled=False)

    gather(nxt_ref, 1 - slot, unrolled=True)
    wait(slot)
    w = wcol_ref[...]
    mix = (w[:, 0:1] * _load_token_tiles(buf.at[slot, 0], ROW_TILE)
           + w[:, 1:2] * _load_token_tiles(buf.at[slot, 1], ROW_TILE))
    g2 = mod_ref[...][:, 5 * d:6 * d]
    o_ref[...] = x_ref[...] + g2 * mix

    @pl.when(j == last)
    def _():
        wait(1 - slot)


def _moe_combine(xa, mods3, wcol, y_tt, plan, n_ctx_tiles):
    bsz, t, d = xa.shape
    nt = t // ROW_TILE
    n_tiles = bsz * nt
    dest_tiles = plan[2]
    idx_spec = lambda step: pl.BlockSpec((None, 2, ROW_TILE), lambda j: (jnp.minimum(j + step, n_tiles - 1), 0, 0),
                                         memory_space=pltpu.SMEM)
    return pl.pallas_call(
        functools.partial(_moe_combine_kernel, d=d),
        grid=(n_tiles,),
        in_specs=[idx_spec(0), idx_spec(1),
                  pl.BlockSpec((None, ROW_TILE, d), lambda j: (j // nt, j % nt, 0)),
                  pl.BlockSpec((None, 1, 6 * d), lambda j: (jnp.where(j % nt < n_ctx_tiles, bsz, j // nt), 0, 0)),
                  pl.BlockSpec((ROW_TILE, GATE_LANES), lambda j: (j, 0)),
                  pl.BlockSpec(memory_space=pl.ANY)],
        out_specs=pl.BlockSpec((None, ROW_TILE, d), lambda j: (j // nt, j % nt, 0)),
        out_shape=jax.ShapeDtypeStruct((bsz, t, d), F32),
        scratch_shapes=[pltpu.VMEM((2, 2, ROW_TILE * TOK_ROWS, 128), F32), pltpu.SemaphoreType.DMA((2,))],
        compiler_params=_cparams("arbitrary"),
        name="moe_combine",
    )(dest_tiles, dest_tiles, xa, mods3, wcol, y_tt)


def _moe_sparse(xa, mods3, g, rw_t, rb, wg, wu, wd, n_ctx_tiles):
    bsz, t, _ = xa.shape
    h_tt, ids, wcol = _moe_router(xa, mods3, g, rw_t, rb, n_ctx_tiles)
    plan = _moe_plan(ids, bsz * t)
    y_tt = _moe_experts(h_tt, plan, wg, wu, wd)
    return _moe_combine(xa, mods3, wcol, y_tt, plan, n_ctx_tiles)


def _final_norm_kernel(x_ref, g_ref, o_ref):
    x = x_ref[...]
    o_ref[...] = x * lax.rsqrt(jnp.mean(x * x, axis=-1, keepdims=True) + EPS) * g_ref[...]


def _final_norm(x, g):
    bsz, t, d = x.shape
    return pl.pallas_call(
        _final_norm_kernel,
        grid=(bsz, t // ROW_TILE),
        in_specs=[pl.BlockSpec((None, ROW_TILE, d), lambda b, i: (b, i, 0)), pl.BlockSpec((1, d), lambda b, i: (0, 0))],
        out_specs=pl.BlockSpec((None, ROW_TILE, d), lambda b, i: (b, i, 0)),
        out_shape=jax.ShapeDtypeStruct((bsz, t, d), F32),
        compiler_params=_cparams("parallel", "parallel"),
        name="final_norm",
    )(x, g.reshape(1, d))


def kernel(x, c, ctx, c_ctx, w_mod, b_mod, norm_mix, norm_ffn, w_in, gla_a2, gla_ab, mlstm_gate_b, gdn_conv, gdn_a_log, gdn_dt_bias, diff_lambda, head_norm, w_branch, w_out, router_w, router_b, w_gate, w_up, w_down, norm_final):
    depth = w_in.shape[0]
    st = _prepare(x, c, ctx, c_ctx, w_mod, b_mod)
    n_ctx_tiles = st["n_ctx_tiles"]
    tables = _rope_tables(x.shape[1], st["ct"])
    rw_t = router_w.T
    rb = jnp.broadcast_to(router_b.astype(F32)[:, None], (N_EXPERTS, ROW_TILE))
    for l in range(depth):
        with_ctx = l < depth - 1
        lam_init = 0.8 - 0.6 * math.exp(-0.3 * l)
        mods3 = st["mods"][l][:, None, :]
        proj = _layer_inproj(st, l, norm_mix, w_in)
        scans = (*_gla_branch(proj, gla_a2[l], gla_ab[l]),
                 *_mlstm_branch(proj, mlstm_gate_b[l]),
                 *_gdn_branch(proj, gdn_conv[l], gdn_a_log[l], gdn_dt_bias[l], n_ctx_tiles))
        attn = _diff_branch(proj, tables, diff_lambda[l], lam_init, n_ctx_tiles, with_ctx)
        hn = jnp.zeros((8, BRANCH_W), F32).at[:N_BRANCH].set(head_norm[l])
        xa = _merge(st["xa"], mods3, proj, scans, attn, hn, w_branch[l].astype(BF16), w_out[l].astype(BF16),
                    n_ctx_tiles, with_ctx, lam_init)
        if not with_ctx:
            n_ctx_tiles = 0
        xa = _moe_sparse(xa, mods3, norm_ffn[l], rw_t, rb, w_gate[l], w_up[l], w_down[l], n_ctx_tiles)
        st = dict(st, xa=xa, n_ctx_tiles=n_ctx_tiles)
    return _final_norm(st["xa"], norm_final)
```

```python
import functools
import math

import numpy as np
import jax
import jax.numpy as jnp
from jax import lax
from jax.experimental import pallas as pl
from jax.experimental.pallas import tpu as pltpu

N_HEADS = 4
HEAD_DIM = 64
BRANCH_W = N_HEADS * HEAD_DIM
CHUNK = 64
GLA_RANK = 16
GLA_NORMALIZER = 16.0
CONV_W = 5
DIFF_QK = HEAD_DIM // 2
ROPE_AXIS = DIFF_QK // 2
ROPE_BASE = 10000.0
GRID_W = 64
N_EXPERTS = 16
N_GROUPS = 4
EXPERTS_PER_GROUP = 4
EPS = 1e-6
N_BRANCH = 4

ROW_TILE = 256
PROJ_COLS = 8192
BLK_GDN_QKV, BLK_DIFF_QKV, BLK_GDN_G = 0, 3, 6
BLK_GLA_Q, BLK_GLA_K, BLK_GLA_V, BLK_GLA_G = 7, 8, 9, 10
BLK_ML_Q, BLK_ML_K, BLK_ML_V, BLK_ML_G = 11, 28, 29, 30
BLK_GATES = 12
SMALL_BLOCK = 31
SM_GLA_RF, SM_GLA_RB, SM_ML_F, SM_ML_B, SM_GD_F, SM_GD_B = 0, 16, 32, 40, 48, 56
NEG_BIG = -1e30
VMEM_LIMIT = 56 * 1024 * 1024

F32 = jnp.float32
BF16 = jnp.bfloat16


def _cparams(*sem):
    return pltpu.CompilerParams(dimension_semantics=sem, vmem_limit_bytes=VMEM_LIMIT)


def _dot(a, b):
    return jnp.dot(a, b, preferred_element_type=F32)


def _dot_nt(a, b):
    return lax.dot_general(a, b, (((1,), (1,)), ((), ())), preferred_element_type=F32)


def _dot_tn(a, b):
    return lax.dot_general(a, b, (((0,), (0,)), ((), ())), preferred_element_type=F32)


def _split_hl(x):
    hi = x.astype(BF16)
    lo = (x - hi.astype(F32)).astype(BF16)
    return hi, lo


def _dot_hl(a, x):
    hi, lo = _split_hl(x)
    return _dot(a, hi) + _dot(a, lo)


def _dot_hl_r(x, a):
    hi, lo = _split_hl(x)
    return _dot(hi, a) + _dot(lo, a)


def _iota(shape, dim):
    return lax.broadcasted_iota(jnp.int32, shape, dim)


def _softplus(x):
    return jnp.maximum(x, 0.0) + jnp.log(1.0 + jnp.exp(-jnp.abs(x)))


def _log_sigmoid(x):
    return -_softplus(-x)


def _silu(x):
    return x * jax.nn.sigmoid(x)


def _lane_head(shape):
    return _iota(shape, len(shape) - 1) // HEAD_DIM


def _stack4(x):
    lh = _lane_head(x.shape)
    zero = jnp.zeros_like(x)
    return jnp.concatenate([jnp.where(lh == h, x, zero) for h in range(N_HEADS)], axis=0)


def _blk_mask():
    r = _iota((BRANCH_W, BRANCH_W), 0) // HEAD_DIM
    c = _iota((BRANCH_W, BRANCH_W), 1) // HEAD_DIM
    return r == c


def _ones_blk():
    return jnp.where(_blk_mask(), 1.0, 0.0).astype(BF16)


def _tri_hs(reverse, strict):
    t = _iota((CHUNK, BRANCH_W), 0)
    s = _iota((CHUNK, BRANCH_W), 1) % CHUNK
    if reverse:
        return (s > t) if strict else (s >= t)
    return (s < t) if strict else (s <= t)


def _eye_hs():
    t = _iota((CHUNK, BRANCH_W), 0)
    s = _iota((CHUNK, BRANCH_W), 1) % CHUNK
    return t == s


def _cum_mat(reverse):
    t = _iota((CHUNK, CHUNK), 0)
    s = _iota((CHUNK, CHUNK), 1)
    m = (s >= t) if reverse else (s <= t)
    return jnp.where(m, 1.0, 0.0).astype(BF16)


def _row_of(col_rep):
    return jnp.sum(jnp.where(_eye_hs(), col_rep, 0.0), axis=0, keepdims=True)


def _seg_max(x):
    lane = _iota(x.shape, 1)
    n = x.shape[1]
    for sh in (1, 2, 4, 8, 16, 32):
        up = pltpu.roll(x, n - sh, 1)
        dn = pltpu.roll(x, sh, 1)
        x = jnp.maximum(x, jnp.where((lane & sh) == 0, up, dn))
    return x


def _seg_sum(x, ones_blk):
    return _dot_hl_r(x, ones_blk)


def _mod_kernel(c_ref, w_ref, b_ref, o_ref):
    a = _silu(c_ref[...]).astype(BF16)
    o_ref[...] = _dot(a, w_ref[...].astype(BF16)) + b_ref[...]


def _modulation(cond, w_mod, b_mod):
    depth, d, n = w_mod.shape
    r = cond.shape[0]
    tn = 1536
    return pl.pallas_call(
        _mod_kernel,
        grid=(depth, n // tn),
        in_specs=[pl.BlockSpec((r, d), lambda l, j: (0, 0)),
                  pl.BlockSpec((None, d, tn), lambda l, j: (l, 0, j)),
                  pl.BlockSpec((None, 1, tn), lambda l, j: (l, 0, j))],
        out_specs=pl.BlockSpec((None, r, tn), lambda l, j: (l, 0, j)),
        out_shape=jax.ShapeDtypeStruct((depth, r, n), F32),
        compiler_params=_cparams("parallel", "parallel"),
        name="modulation",
    )(cond, w_mod, b_mod.reshape(depth, 1, n))


def _rms_mod(x, g, shift, scale):
    y = x * lax.rsqrt(jnp.mean(x * x, axis=-1, keepdims=True) + EPS)
    return (y * g) * (1.0 + scale) + shift


def _inproj_kernel(x_ref, mod_ref, g_ref, w_ref, o_ref, *, d, n_chunk):
    mod = mod_ref[...]
    h = _rms_mod(x_ref[...], g_ref[...], mod[:, 0:d], mod[:, d:2 * d]).astype(BF16)
    for n0 in range(0, PROJ_COLS, n_chunk):
        o_ref[:, n0:n0 + n_chunk] = _dot(h, w_ref[:, n0:n0 + n_chunk]).astype(BF16)


def _mod_row_map(n_ctx_tiles, n_batch):
    return lambda b, i: (jnp.where(i < n_ctx_tiles, n_batch, b), 0, 0)


def _inproj(xa, mods3, g, w_perm, n_ctx_tiles):
    bsz, t, d = xa.shape
    return pl.pallas_call(
        functools.partial(_inproj_kernel, d=d, n_chunk=1024),
        grid=(bsz, t // ROW_TILE),
        in_specs=[pl.BlockSpec((None, ROW_TILE, d), lambda b, i: (b, i, 0)),
                  pl.BlockSpec((None, 1, 6 * d), _mod_row_map(n_ctx_tiles, bsz)),
                  pl.BlockSpec((1, d), lambda b, i: (0, 0)),
                  pl.BlockSpec((d, PROJ_COLS), lambda b, i: (0, 0), pipeline_mode=pl.Buffered(1))],
        out_specs=pl.BlockSpec((None, ROW_TILE, PROJ_COLS), lambda b, i: (b, i, 0)),
        out_shape=jax.ShapeDtypeStruct((bsz, t, PROJ_COLS), BF16),
        compiler_params=_cparams("parallel", "parallel"),
        name="inproj",
    )(xa, mods3, g.reshape(1, d), w_perm)


def _proj_perm():
    sizes = (256, 256, 256, 256, 16, 16, 256, 256, 256, 256, 8, 8, 256, 256, 256, 256, 8, 8, 256, 256, 256, 4096)
    off = np.concatenate([[0], np.cumsum(sizes)])
    seg = lambda i: np.arange(off[i], off[i + 1])
    wide = {BLK_GDN_QKV: 12, BLK_GDN_QKV + 1: 13, BLK_GDN_QKV + 2: 14, BLK_GDN_G: 15,
            BLK_DIFF_QKV: 18, BLK_DIFF_QKV + 1: 19, BLK_DIFF_QKV + 2: 20,
            BLK_GLA_Q: 0, BLK_GLA_K: 1, BLK_GLA_V: 2, BLK_GLA_G: 3,
            BLK_ML_Q: 6, BLK_ML_K: 7, BLK_ML_V: 8, BLK_ML_G: 9}
    perm = np.full((PROJ_COLS,), -1, np.int64)
    for j, i in wide.items():
        perm[j * BRANCH_W:(j + 1) * BRANCH_W] = seg(i)
    perm[BLK_GATES * BRANCH_W:BLK_GATES * BRANCH_W + 4096] = seg(21)
    base = SMALL_BLOCK * BRANCH_W
    for lane0, i in ((SM_GLA_RF, 4), (SM_GLA_RB, 5), (SM_ML_F, 10), (SM_ML_B, 11), (SM_GD_F, 16), (SM_GD_B, 17)):
        s = seg(i)
        perm[base + lane0:base + lane0 + len(s)] = s
    return perm


def _permute_w_in(w_in):
    perm = _proj_perm()
    cuts = [0] + [i for i in range(1, PROJ_COLS)
                  if (perm[i] < 0) != (perm[i - 1] < 0) or (perm[i] >= 0 and perm[i] != perm[i - 1] + 1)] + [PROJ_COLS]
    w = w_in.astype(BF16)
    runs = [jnp.zeros((w.shape[0], b - a), BF16) if perm[a] < 0 else w[:, perm[a]:perm[a] + b - a]
            for a, b in zip(cuts[:-1], cuts[1:])]
    return jnp.concatenate(runs, axis=-1)


SCAN_CHUNKS = ROW_TILE // CHUNK


def _scan_block_map(col, nb, reverse):
    if reverse:
        return lambda j: (0, jnp.where(j == 0, 0, nb - j), col)
    return lambda j: (0, j, col)


def _scan_steps(reverse, bsz):
    order = range(SCAN_CHUNKS - 1, -1, -1) if reverse else range(SCAN_CHUNKS)
    return [(b, slice(c * CHUNK, (c + 1) * CHUNK)) for c in order for b in range(bsz)]


def _scan_call(kernel_fn, name, arrays, cols, consts, scratch, reverse):
    bsz, t, _ = arrays[0].shape
    nb = t // ROW_TILE
    blk = lambda col: pl.BlockSpec((bsz, ROW_TILE, BRANCH_W), _scan_block_map(col, nb, reverse))
    const = lambda a: pl.BlockSpec(a.shape, lambda j: (0,) * a.ndim)
    return pl.pallas_call(
        functools.partial(kernel_fn, reverse=reverse, bsz=bsz),
        grid=(nb,),
        in_specs=[blk(c) for c in cols] + [const(a) for a in consts],
        out_specs=blk(0),
        out_shape=jax.ShapeDtypeStruct((bsz, t, BRANCH_W), BF16),
        scratch_shapes=[pltpu.VMEM((bsz,) + s, F32) for s in scratch],
        compiler_params=_cparams("arbitrary"),
        name=name + ("_bwd" if reverse else "_fwd"),
    )(*arrays, *consts)


def _expand_small(small, lane0, count):
    src = _iota((BRANCH_W, BRANCH_W), 0)
    dst_head = _iota((BRANCH_W, BRANCH_W), 1) // HEAD_DIM
    e = jnp.where(src == lane0 + dst_head, 1.0, 0.0).astype(BF16)
    return _dot(small, e)


def _gla_kernel(q_ref, k_ref, v_ref, sm_ref, a2_ref, ab_ref, o_ref, st_ref, *, reverse, bsz):
    @pl.when(pl.program_id(0) == 0)
    def _():
        st_ref[...] = jnp.zeros_like(st_ref)

    cum_mat = _cum_mat(reverse)
    tri = _tri_hs(reverse, strict=False)
    blk = _blk_mask()
    a2 = a2_ref[...]
    ab = ab_ref[...]
    steps = _scan_steps(reverse, bsz)
    la = [_log_sigmoid(_dot(sm_ref[b, rows, :], a2) + ab) / GLA_NORMALIZER for b, rows in steps]
    cum = [_dot_hl(cum_mat, x) for x in la]
    tot = [jnp.sum(x, axis=0, keepdims=True) for x in la]
    q_in = [(q_ref[b, rows, :].astype(F32) * (HEAD_DIM ** -0.5) * jnp.exp(c)).astype(BF16)
            for (b, rows), c in zip(steps, cum)]
    k_out = [(k_ref[b, rows, :].astype(F32) * jnp.exp(-c)).astype(BF16) for (b, rows), c in zip(steps, cum)]
    k_end = [(k_ref[b, rows, :].astype(F32) * jnp.exp(t - c)).astype(BF16)
             for (b, rows), c, t in zip(steps, cum, tot)]
    att = [jnp.where(tri, _dot_nt(qi, _stack4(ko)), 0.0).astype(BF16) for qi, ko in zip(q_in, k_out)]
    o_intra = [_dot(a, _stack4(v_ref[b, rows, :])) for (b, rows), a in zip(steps, att)]
    kv_t = [jnp.where(blk, _dot_tn(v_ref[b, rows, :], ke), 0.0) for (b, rows), ke in zip(steps, k_end)]
    for i, (b, rows) in enumerate(steps):
        st = st_ref[b]
        o_ref[b, rows, :] = (o_intra[i] + _dot_nt(q_in[i], st.astype(BF16))).astype(o_ref.dtype)
        st_ref[b] = st * jnp.exp(tot[i]) + kv_t[i]


def _gla_scan(proj, a2pad, ab, reverse):
    return _scan_call(_gla_kernel, "gla", [proj] * 4, [BLK_GLA_Q, BLK_GLA_K, BLK_GLA_V, SMALL_BLOCK],
                      [a2pad, ab], [(BRANCH_W, BRANCH_W)], reverse)


def _gla_params(a2, ab, d):
    lane0 = SM_GLA_RB if d else SM_GLA_RF
    pad = jnp.zeros((BRANCH_W, BRANCH_W), F32).at[lane0:lane0 + GLA_RANK].set(a2[d])
    return pad.astype(BF16), ab[d].reshape(1, BRANCH_W)


def _gla_branch(proj, a2, ab):
    return tuple(_gla_scan(proj, *_gla_params(a2, ab, d), reverse=bool(d)) for d in (0, 1))


def _mlstm_kernel(q_ref, k_ref, v_ref, sm_ref, bias_ref, o_ref, ct_ref, nm_ref, *, reverse, bsz):
    @pl.when(pl.program_id(0) == 0)
    def _():
        ct_ref[...] = jnp.zeros_like(ct_ref)
        nm_ref[...] = jnp.zeros_like(nm_ref)

    lane0 = SM_ML_B if reverse else SM_ML_F
    cum_mat = _cum_mat(reverse)
    tri = _tri_hs(reverse, strict=False)
    blk = _blk_mask()
    ones_blk = _ones_blk()
    b_i = bias_ref[0:1, :]
    b_f = bias_ref[1:2, :]
    steps = _scan_steps(reverse, bsz)
    each = lambda f, *ls: [f(*a) for a in zip(*ls)]
    sm = [sm_ref[b, rows, :] for b, rows in steps]
    q = [q_ref[b, rows, :] for b, rows in steps]
    i_pre = [_expand_small(x, lane0, N_HEADS) + b_i for x in sm]
    log_f = [_log_sigmoid(_expand_small(x, lane0 + N_HEADS, N_HEADS) + b_f) for x in sm]
    f_cum = [_dot_hl(cum_mat, x) for x in log_f]
    f_tot = [jnp.sum(x, axis=0, keepdims=True) for x in log_f]
    u = each(lambda i, f: i - f, i_pre, f_cum)
    lw = each(lambda t, x: t + x, f_tot, u)
    a_end = [jnp.max(x, axis=0, keepdims=True) for x in lw]
    k = [k_ref[b, rows, :].astype(F32) * (HEAD_DIM ** -0.5) for b, rows in steps]
    kw = each(lambda kk, x, a: kk * jnp.exp(x - a), k, lw, a_end)
    k_sum = [jnp.sum(x, axis=0, keepdims=True) for x in kw]
    kv_t = [jnp.where(blk, _dot_tn(v_ref[b, rows, :], x.astype(BF16)), 0.0) for (b, rows), x in zip(steps, kw)]
    log_d = each(lambda f, x: jnp.where(tri, f + _row_of(x), NEG_BIG), f_cum, u)
    mx = [_seg_max(x) for x in log_d]
    s = each(lambda qq, kk, ld, m: (_dot_nt(qq, _stack4(kk.astype(BF16))) * jnp.exp(ld - m)).astype(BF16),
             q, k, log_d, mx)
    num1 = [_dot(x, _stack4(v_ref[b, rows, :])) for (b, rows), x in zip(steps, s)]
    den1 = [_dot(x, ones_blk) for x in s]
    for i, (b, rows) in enumerate(steps):
        n_in = nm_ref[b, 0:1, :]
        m_in = nm_ref[b, 1:2, :]
        ct = ct_ref[b]
        g = f_cum[i] + m_in
        m_t = jnp.maximum(g, mx[i])
        e = jnp.exp(g - m_t)
        r = jnp.exp(mx[i] - m_t)
        num = r * num1[i] + e * _dot_nt(q[i], ct.astype(BF16))
        den = r * den1[i] + e * _seg_sum(q[i].astype(F32) * n_in, ones_blk)
        o_ref[b, rows, :] = (num / jnp.maximum(jnp.abs(den), jnp.exp(-m_t))).astype(o_ref.dtype)

        m_new = jnp.maximum(f_tot[i] + m_in, a_end[i])
        old = jnp.exp(f_tot[i] + m_in - m_new)
        new = jnp.exp(a_end[i] - m_new)
        ct_ref[b] = ct * old + kv_t[i] * new
        nm_ref[b, 0:1, :] = n_in * old + k_sum[i] * new
        nm_ref[b, 1:2, :] = m_new


def _mlstm_scan(proj, bias, reverse):
    return _scan_call(_mlstm_kernel, "mlstm", [proj] * 4, [BLK_ML_Q, BLK_ML_K, BLK_ML_V, SMALL_BLOCK],
                      [bias], [(BRANCH_W, BRANCH_W), (8, BRANCH_W)], reverse)


def _head_rows(vals):
    rows = [jnp.repeat(v.astype(F32), HEAD_DIM) for v in vals]
    rows += [jnp.zeros((BRANCH_W,), F32)] * (8 - len(rows))
    return jnp.stack(rows)


def _mlstm_branch(proj, gate_b):
    return tuple(_mlstm_scan(proj, _head_rows([gate_b[d, 0], gate_b[d, 1]]), reverse=bool(d)) for d in (0, 1))


HALO = 8
QKV_W = 3 * BRANCH_W


def _gdn_prep_kernel(prev_ref, cur_ref, next_ref, w_ref, o_ref, *, n_ctx_tiles, n_tiles):
    i = pl.program_id(1)
    has_prev = jnp.logical_and(i != 0, i != n_ctx_tiles).astype(F32)
    has_next = jnp.logical_and(i != n_ctx_tiles - 1, i != n_tiles - 1).astype(F32)
    padded = jnp.concatenate([prev_ref[...].astype(F32) * has_prev, cur_ref[...].astype(F32),
                              next_ref[...].astype(F32) * has_next], axis=0)
    w = w_ref[...]
    acc = jnp.zeros((ROW_TILE, QKV_W), F32)
    for j in range(CONV_W):
        off = HALO + j - CONV_W // 2
        acc = acc + padded[off:off + ROW_TILE, :] * w[j:j + 1, :]
    y = _silu(acc)
    ones_blk = _ones_blk()
    q = y[:, 0:BRANCH_W]
    k = y[:, BRANCH_W:2 * BRANCH_W]
    q = q * lax.rsqrt(_seg_sum(q * q, ones_blk) + EPS) * (HEAD_DIM ** -0.5)
    k = k * lax.rsqrt(_seg_sum(k * k, ones_blk) + EPS)
    o_ref[:, 0:BRANCH_W] = q.astype(o_ref.dtype)
    o_ref[:, BRANCH_W:2 * BRANCH_W] = k.astype(o_ref.dtype)
    o_ref[:, 2 * BRANCH_W:] = y[:, 2 * BRANCH_W:].astype(o_ref.dtype)


def _gdn_prep(proj, conv_w, n_ctx_tiles):
    bsz, t, _ = proj.shape
    nt = t // ROW_TILE
    per = ROW_TILE // HALO
    last = t // HALO - 1
    qkv_blk = BLK_GDN_QKV * BRANCH_W // QKV_W
    w8 = jnp.zeros((8, QKV_W), F32).at[:CONV_W].set(conv_w)
    return pl.pallas_call(
        functools.partial(_gdn_prep_kernel, n_ctx_tiles=n_ctx_tiles, n_tiles=nt),
        grid=(bsz, nt),
        in_specs=[pl.BlockSpec((None, HALO, QKV_W), lambda b, i: (b, jnp.maximum(i * per - 1, 0), qkv_blk)),
                  pl.BlockSpec((None, ROW_TILE, QKV_W), lambda b, i: (b, i, qkv_blk)),
                  pl.BlockSpec((None, HALO, QKV_W), lambda b, i: (b, jnp.minimum((i + 1) * per, last), qkv_blk)),
                  pl.BlockSpec((8, QKV_W), lambda b, i: (0, 0))],
        out_specs=pl.BlockSpec((None, ROW_TILE, QKV_W), lambda b, i: (b, i, 0)),
        out_shape=jax.ShapeDtypeStruct((bsz, t, QKV_W), BF16),
        compiler_params=_cparams("parallel", "parallel"),
        name="gdn_prep",
    )(proj, proj, proj, w8)


def _gdn_kernel(q_ref, k_ref, v_ref, sm_ref, par_ref, o_ref, s_ref, *, reverse, bsz):
    @pl.when(pl.program_id(0) == 0)
    def _():
        s_ref[...] = jnp.zeros_like(s_ref)

    lane0 = SM_GD_B if reverse else SM_GD_F
    cum_mat = _cum_mat(reverse)
    tri = _tri_hs(reverse, strict=False)
    tri_strict = _tri_hs(reverse, strict=True)
    blk = _blk_mask()
    eye = jnp.where(_eye_hs(), 1.0, 0.0)
    a_scale = jnp.exp(par_ref[0:1, :])
    dt_bias = par_ref[1:2, :]
    steps = _scan_steps(reverse, bsz)
    each = lambda f, *ls: [f(*a) for a in zip(*ls)]
    sm = [sm_ref[b, rows, :] for b, rows in steps]
    q = [q_ref[b, rows, :] for b, rows in steps]
    kb16 = [k_ref[b, rows, :] for b, rows in steps]
    beta = [jax.nn.sigmoid(_expand_small(x, lane0, N_HEADS)) for x in sm]
    g = [-a_scale * _softplus(_expand_small(x, lane0 + N_HEADS, N_HEADS) + dt_bias) for x in sm]
    cum = [_dot_hl(cum_mat, x) for x in g]
    tot = [jnp.sum(x, axis=0, keepdims=True) for x in g]
    gam = [jnp.where(tri, jnp.exp(jnp.where(tri, c - _row_of(c), 0.0)), 0.0) for c in cum]
    k_beta = each(lambda kk, bb: kk.astype(F32) * bb, kb16, beta)
    k4 = [_stack4(x) for x in kb16]
    a_hs = each(lambda kb, kk, gm: jnp.where(tri_strict, _dot_nt(kb.astype(BF16), kk) * gm, 0.0), k_beta, k4, gam)
    attn = each(lambda qq, kk, gm: (_dot_nt(qq, kk) * gm).astype(BF16), q, k4, gam)

    p = [-x for x in a_hs]
    t_inv = [eye + x for x in p]
    for _ in range(5):
        p = [_dot(x.astype(BF16), _stack4(x.astype(BF16))) for x in p]
        t_inv = each(lambda t, x: t + _dot(t.astype(BF16), _stack4(x.astype(BF16))), t_inv, p)
    w = each(lambda t, kb, c: _dot(t.astype(BF16), _stack4((kb * jnp.exp(c)).astype(BF16))).astype(BF16),
             t_inv, k_beta, cum)
    u = [_dot(t.astype(BF16), _stack4((v_ref[b, rows, :].astype(F32) * bb).astype(BF16)))
         for (b, rows), t, bb in zip(steps, t_inv, beta)]
    q_dec = each(lambda qq, c: (qq.astype(F32) * jnp.exp(c)).astype(BF16), q, cum)
    k_end = each(lambda kk, t, c: (kk.astype(F32) * jnp.exp(t - c)).astype(BF16), kb16, tot, cum)
    for i, (b, rows) in enumerate(steps):
        s = s_ref[b]
        sb = s.astype(BF16)
        v_new = (u[i] - _dot(w[i], sb)).astype(BF16)
        o_ref[b, rows, :] = (_dot(q_dec[i], sb) + _dot(attn[i], _stack4(v_new))).astype(o_ref.dtype)
        s_ref[b] = s * jnp.exp(tot[i]) + jnp.where(blk, _dot_tn(k_end[i], v_new), 0.0)


def _gdn_scan(gqkv, proj, par, reverse):
    return _scan_call(_gdn_kernel, "gdn", [gqkv, gqkv, gqkv, proj], [0, 1, 2, SMALL_BLOCK],
                      [par], [(BRANCH_W, BRANCH_W)], reverse)


def _gdn_branch(proj, conv_w, a_log, dt_bias, n_ctx_tiles):
    gqkv = _gdn_prep(proj, conv_w, n_ctx_tiles)
    return tuple(_gdn_scan(gqkv, proj, _head_rows([a_log[d], dt_bias[d]]), reverse=bool(d))
                 for d in (0, 1))


VX_W = 128
LOG2E = 1.4426950408889634


def _kv_tile(t):
    return next(k for k in (3 * ROW_TILE, 2 * ROW_TILE, ROW_TILE) if t % k == 0)


def _rope_tables(n_lat, n_ctx):
    pos = jnp.arange(n_lat)
    row, col = pos // GRID_W, pos % GRID_W
    inv = ROPE_BASE ** (-jnp.arange(0, ROPE_AXIS, 2, dtype=F32) / ROPE_AXIS)
    ang = jnp.concatenate([row.astype(F32)[:, None] * inv, col.astype(F32)[:, None] * inv], axis=-1)
    cos = jnp.concatenate([jnp.ones((n_ctx, ROPE_AXIS), F32), jnp.cos(ang)], axis=0)
    sin = jnp.concatenate([jnp.zeros((n_ctx, ROPE_AXIS), F32), jnp.sin(ang)], axis=0)
    reps = BRANCH_W // DIFF_QK
    cos_l = jnp.tile(jnp.concatenate([cos, cos], axis=-1), (1, reps))
    sin_l = jnp.tile(jnp.concatenate([-sin, sin], axis=-1), (1, reps))
    return cos_l, sin_l


def _rope_prep_kernel(x_ref, cos_ref, sin_ref, q_ref, kt_ref, vx_ref):
    lane = _iota((ROW_TILE, BRANCH_W), 1)
    first_half = (lane % DIFF_QK) < ROPE_AXIS
    cos = cos_ref[...]
    sin = sin_ref[...]

    def rope(x):
        partner = jnp.where(first_half, pltpu.roll(x, BRANCH_W - ROPE_AXIS, 1), pltpu.roll(x, ROPE_AXIS, 1))
        return x * cos + partner * sin

    q = rope(x_ref[:, 0:BRANCH_W].astype(F32)) * (DIFF_QK ** -0.5 * LOG2E)
    k = rope(x_ref[:, BRANCH_W:2 * BRANCH_W].astype(F32))
    q_ref[...] = q.astype(q_ref.dtype)
    kt_ref[...] = jnp.transpose(k).astype(kt_ref.dtype)
    v = x_ref[:, 2 * BRANCH_W:]
    src = _iota((BRANCH_W, VX_W), 0)
    dst = _iota((BRANCH_W, VX_W), 1)
    ones_col = jnp.where(_iota((ROW_TILE, VX_W), 1) == HEAD_DIM, 1.0, 0.0)
    for h in range(N_HEADS):
        sel = jnp.where(jnp.logical_and(src == h * HEAD_DIM + dst, dst < HEAD_DIM), 1.0, 0.0).astype(BF16)
        vx_ref[h] = (_dot(v, sel) + ones_col).astype(vx_ref.dtype)


def _rope_prep(proj, cos_l, sin_l):
    bsz, t, _ = proj.shape
    nt = t // ROW_TILE
    kvt = _kv_tile(t)
    r = kvt // ROW_TILE
    qkv_blk = BLK_DIFF_QKV * BRANCH_W // QKV_W
    tab = pl.BlockSpec((ROW_TILE, BRANCH_W), lambda b, i: (i, 0))
    return pl.pallas_call(
        _rope_prep_kernel,
        grid=(bsz, nt),
        in_specs=[pl.BlockSpec((None, ROW_TILE, QKV_W), lambda b, i: (b, i, qkv_blk)), tab, tab],
        out_specs=[pl.BlockSpec((None, ROW_TILE, BRANCH_W), lambda b, i: (b, i, 0)),
                   pl.BlockSpec((None, None, BRANCH_W, ROW_TILE), lambda b, i: (b, i // r, 0, i % r)),
                   pl.BlockSpec((None, None, N_HEADS, ROW_TILE, VX_W), lambda b, i: (b, i // r, 0, i % r, 0))],
        out_shape=[jax.ShapeDtypeStruct((bsz, t, BRANCH_W), BF16),
                   jax.ShapeDtypeStruct((bsz, t // kvt, BRANCH_W, kvt), BF16),
                   jax.ShapeDtypeStruct((bsz, t // kvt, N_HEADS, kvt, VX_W), BF16)],
        compiler_params=_cparams("parallel", "parallel"),
        name="rope_prep",
    )(proj, cos_l, sin_l)


N_CHAINS = 2 * N_HEADS


def _attn_kernel(q_ref, kt_ref, vx_ref, lp_ref, o_ref, qm_scr, s0_scr, s1_scr, m0_scr, m1_scr, al0_scr, al1_scr,
                 acc_scr, *, nk, lam_init):
    s_bufs, m_bufs, al_bufs = (s0_scr, s1_scr), (m0_scr, m1_scr), (al0_scr, al1_scr)
    q = q_ref[...]
    lane = _iota(q.shape, 1)
    zero = jnp.zeros_like(q)
    for c in range(N_CHAINS):
        lo = c * DIFF_QK
        qm_scr[c] = jnp.where(jnp.logical_and(lane >= lo, lane < lo + DIFF_QK), q, zero)
    acc_scr[...] = jnp.zeros_like(acc_scr)

    def stage_a(c, kt, slot, first=False):
        s = _dot(qm_scr[c], kt)
        s_bufs[slot][c] = s
        row_max = jnp.max(s, axis=1, keepdims=True)
        if first:
            m_bufs[slot][c] = row_max
            al_bufs[slot][c] = jnp.zeros_like(row_max)
        else:
            m_old = m_bufs[1 - slot][c]
            m_new = jnp.maximum(m_old, row_max)
            m_bufs[slot][c] = m_new
            al_bufs[slot][c] = jnp.exp2(m_old - m_new)

    def stage_b(c, j, slot):
        p = jnp.exp2(s_bufs[slot][c] - m_bufs[slot][c]).astype(BF16)
        acc_scr[c] = acc_scr[c] * al_bufs[slot][c] + _dot(p, vx_ref[j, c // 2])

    def step(ja, jb, slot_a, first=False):
        kt = None if ja is None else kt_ref[ja]
        for c in range(N_CHAINS):
            if ja is not None:
                stage_a(c, kt, slot_a, first)
            if jb is not None:
                stage_b(c, jb, 1 - slot_a)

    step(0, None, 0, first=True)

    def body(i, carry):
        j = 2 * i + 1
        step(j, j - 1, 1)
        step(j + 1, j, 0)
        return carry

    n_pairs = (nk - 1) // 2
    lax.fori_loop(0, n_pairs, body, 0)
    if nk % 2 == 0:
        step(nk - 1, nk - 2, 1)
    step(None, nk - 1, 1 - (nk - 1) % 2)

    lp = lp_ref[...]
    grp_src = _iota((VX_W, VX_W), 0) // DIFF_QK
    prod1 = lp[0:1, :] * lp[1:2, :]
    prod2 = lp[2:3, :] * lp[3:4, :]
    src = _iota((VX_W, BRANCH_W), 0)
    dst = _iota((VX_W, BRANCH_W), 1)
    out = jnp.zeros((ROW_TILE, BRANCH_W), F32)
    for h in range(N_HEADS):
        pick = jnp.where(grp_src == h, 1.0, 0.0).astype(BF16)
        lam = jnp.exp(_dot_hl_r(prod1, pick)) - jnp.exp(_dot_hl_r(prod2, pick)) + lam_init
        a0 = acc_scr[2 * h]
        a1 = acc_scr[2 * h + 1]
        o_h = a0 / a0[:, HEAD_DIM:HEAD_DIM + 1] - lam * (a1 / a1[:, HEAD_DIM:HEAD_DIM + 1])
        place = jnp.where(jnp.logical_and(dst == src + h * HEAD_DIM, src < HEAD_DIM), 1.0, 0.0).astype(BF16)
        out = out + _dot(o_h.astype(BF16), place)
    o_ref[...] = out.astype(o_ref.dtype)


def _diff_attention(q_rot, kt, vx, lam_p, lam_init, q_tile0, n_q_tiles, kv_len):
    bsz = q_rot.shape[0]
    n_kv_arr, _, kvt = kt.shape[1:]
    tk = min(kv_len, kvt)
    assert kv_len % tk == 0 and kvt % tk == 0
    nk = kv_len // tk
    lp = jnp.zeros((8, VX_W), F32).at[:4].set(lam_p.reshape(4, N_HEADS * DIFF_QK))
    return pl.pallas_call(
        functools.partial(_attn_kernel, nk=nk, lam_init=lam_init),
        grid=(bsz, n_q_tiles),
        in_specs=[pl.BlockSpec((None, ROW_TILE, BRANCH_W), lambda b, i: (b, i + q_tile0, 0)),
                  pl.BlockSpec((None, nk, BRANCH_W, tk), lambda b, i: (b, 0, 0, 0), pipeline_mode=pl.Buffered(1)),
                  pl.BlockSpec((None, nk, N_HEADS, tk, VX_W), lambda b, i: (b, 0, 0, 0, 0),
                               pipeline_mode=pl.Buffered(1)),
                  pl.BlockSpec((8, VX_W), lambda b, i: (0, 0))],
        out_specs=pl.BlockSpec((None, ROW_TILE, BRANCH_W), lambda b, i: (b, i, 0)),
        out_shape=jax.ShapeDtypeStruct((bsz, n_q_tiles * ROW_TILE, BRANCH_W), BF16),
        scratch_shapes=[pltpu.VMEM((N_CHAINS, ROW_TILE, BRANCH_W), BF16)]
                       + [pltpu.VMEM((N_CHAINS, ROW_TILE, tk), F32)] * 2
                       + [pltpu.VMEM((N_CHAINS, ROW_TILE, 1), F32)] * 4
                       + [pltpu.VMEM((N_CHAINS, ROW_TILE, VX_W), F32)],
        compiler_params=_cparams("parallel", "arbitrary"),
        name="diff_attention",
    )(q_rot, kt, vx, lp)


def _diff_branch(proj, tables, lam_p, lam_init, n_ctx_tiles, with_ctx):
    t = proj.shape[1]
    q_rot, kt, vx = _rope_prep(proj, *tables)
    lat = _diff_attention(q_rot, kt, vx, lam_p, lam_init, n_ctx_tiles, t // ROW_TILE - n_ctx_tiles, t)
    if not with_ctx:
        return lat
    ctx = _diff_attention(q_rot, kt, vx, lam_p, lam_init, 0, n_ctx_tiles, n_ctx_tiles * ROW_TILE)
    return jnp.concatenate([ctx, lat], axis=1)


def _prepare(x, c, ctx, c_ctx, w_mod, b_mod):
    bsz, _, d = x.shape
    ct = ctx.shape[1]
    assert ct % ROW_TILE == 0 and x.shape[1] % ROW_TILE == 0
    xa = jnp.concatenate([ctx, x], axis=1)
    rows = 8 * ((bsz + 1 + 7) // 8)
    cond = jnp.zeros((rows, d), F32).at[:bsz].set(c).at[bsz].set(c_ctx)
    mods = _modulation(cond, w_mod, b_mod)
    return dict(xa=xa, mods=mods, n_ctx_tiles=ct // ROW_TILE, ct=ct)


def _layer_inproj(st, l, norm_mix, w_in):
    mods3 = st["mods"][l][:, None, :]
    return _inproj(st["xa"], mods3, norm_mix[l], _permute_w_in(w_in[l]), st["n_ctx_tiles"])


def _merge_kernel(x_ref, mod_ref, glf_ref, glb_ref, mlf_ref, mlb_ref, gdf_ref, gdb_ref, at_ref,
                  g_gla_ref, g_ml_ref, g_gd_ref, mg0_ref, mg1_ref, mg2_ref, mg3_ref,
                  hn_ref, wb_ref, wo_ref, o_ref, *, d, lam_init):
    ones_blk = _ones_blk()

    def head_norm(o, i):
        ms = _seg_sum(o * o, ones_blk) * (1.0 / HEAD_DIM)
        return o * lax.rsqrt(ms + EPS) * hn_ref[i:i + 1, :]

    f32 = lambda r: r[...].astype(F32)
    ys = [head_norm(f32(glf_ref) + f32(glb_ref), 0) * _silu(f32(g_gla_ref)),
          head_norm(f32(mlf_ref) + f32(mlb_ref), 1) * jax.nn.sigmoid(f32(g_ml_ref)),
          head_norm(f32(gdf_ref) + f32(gdb_ref), 2) * _silu(f32(g_gd_ref)),
          head_norm(f32(at_ref), 3) * (1.0 - lam_init)]
    acc = None
    for i, (y, mg_ref) in enumerate(zip(ys, (mg0_ref, mg1_ref, mg2_ref, mg3_ref))):
        term = jax.nn.sigmoid(f32(mg_ref)) * _dot(y.astype(BF16), wb_ref[i])
        acc = term if acc is None else acc + term
    out = _dot(acc.astype(BF16), wo_ref[...])
    g1 = mod_ref[...][:, 2 * d:3 * d]
    o_ref[...] = x_ref[...] + g1 * out


def _merge(xa, mods3, proj, scans, attn, hn, wb, wo, n_ctx_tiles, with_ctx, lam_init):
    bsz, t, d = xa.shape
    off = 0 if with_ctx else n_ctx_tiles
    nt = t // ROW_TILE - off
    rows = lambda col: pl.BlockSpec((None, ROW_TILE, BRANCH_W), lambda b, i: (b, i + off, col))
    gate = lambda j: pl.BlockSpec((None, ROW_TILE, d), lambda b, i: (b, i + off, BLK_GATES * BRANCH_W // d + j))
    const = lambda shape: pl.BlockSpec(shape, lambda b, i: (0,) * len(shape))
    return pl.pallas_call(
        functools.partial(_merge_kernel, d=d, lam_init=lam_init),
        grid=(bsz, nt),
        in_specs=[pl.BlockSpec((None, ROW_TILE, d), lambda b, i: (b, i + off, 0)),
                  pl.BlockSpec((None, 1, 6 * d), lambda b, i: (jnp.where(i + off < n_ctx_tiles, bsz, b), 0, 0))]
                 + [rows(0)] * 6
                 + [pl.BlockSpec((None, ROW_TILE, BRANCH_W), lambda b, i: (b, i, 0))]
                 + [rows(BLK_GLA_G), rows(BLK_ML_G), rows(BLK_GDN_G)]
                 + [gate(j) for j in range(N_BRANCH)]
                 + [const((8, BRANCH_W)), const((N_BRANCH, BRANCH_W, d)), const((d, d))],
        out_specs=pl.BlockSpec((None, ROW_TILE, d), lambda b, i: (b, i, 0)),
        out_shape=jax.ShapeDtypeStruct((bsz, nt * ROW_TILE, d), F32),
        compiler_params=_cparams("parallel", "parallel"),
        name="merge",
    )(xa, mods3, *scans, attn, proj, proj, proj, proj, proj, proj, proj, hn, wb, wo)


GATE_LANES = 128


def _router_gates_t(hf, rw_ref, rb_ref, as_gates=True):
    h_hi, h_lo = _split_hl(hf)
    w_hi, w_lo = _split_hl(rw_ref[...])
    logits = _dot_nt(w_hi, h_hi) + _dot_nt(w_hi, h_lo) + _dot_nt(w_lo, h_hi)
    scores = jax.nn.sigmoid(logits)
    sel = scores + rb_ref[...]
    s = [sel[e:e + 1, :] for e in range(N_EXPERTS)]
    sc = [scores[e:e + 1, :] for e in range(N_EXPERTS)]
    grp = []
    for g in range(N_GROUPS):
        a, b, c, dd = s[4 * g:4 * g + 4]
        grp.append(functools.reduce(jnp.maximum, [a + b, a + c, a + dd, b + c, b + dd, c + dd]))
    gmax = functools.reduce(jnp.maximum, grp)
    chosen, taken = [], None
    for g in range(N_GROUPS):
        hit = grp[g] == gmax
        if taken is not None:
            hit = jnp.logical_and(hit, jnp.logical_not(taken))
        taken = hit if taken is None else jnp.logical_or(taken, hit)
        chosen.append(hit)
    ms = [jnp.where(chosen[e // EXPERTS_PER_GROUP], s[e], NEG_BIG) for e in range(N_EXPERTS)]

    def first_argmax(vals):
        top = functools.reduce(jnp.maximum, vals)
        hits, seen = [], None
        for v in vals:
            hit = v == top
            if seen is not None:
                hit = jnp.logical_and(hit, jnp.logical_not(seen))
            seen = hit if seen is None else jnp.logical_or(seen, hit)
            hits.append(hit)
        return hits

    oh1 = first_argmax(ms)
    oh2 = first_argmax([jnp.where(o, NEG_BIG, v) for o, v in zip(oh1, ms)])
    zero = jnp.zeros_like(sc[0])
    w1 = functools.reduce(jnp.add, [jnp.where(o, v, zero) for o, v in zip(oh1, sc)])
    w2 = functools.reduce(jnp.add, [jnp.where(o, v, zero) for o, v in zip(oh2, sc)])
    tot = w1 + w2
    if not as_gates:
        ids = [functools.reduce(jnp.add, [jnp.where(o, float(e), 0.0) for e, o in enumerate(oh)]) for oh in (oh1, oh2)]
        return ids[0], ids[1], w1 / tot, w2 / tot
    rows = [jnp.where(o1, w1 / tot, zero) + jnp.where(o2, w2 / tot, zero) for o1, o2 in zip(oh1, oh2)]
    return jnp.concatenate(rows, axis=0)


def _moe_dense_kernel(x_ref, *rest, d, bsz):
    mod_refs = rest[:bsz]
    g_ref, rw_ref, rb_ref, wg_ref, wu_ref, wd_ref, o_ref, h_scr, gate_scr, acc_scr = rest[bsz:]
    e = pl.program_id(1)

    @pl.when(e == 0)
    def _():
        eye = jnp.where(_iota((N_EXPERTS, GATE_LANES), 0) == _iota((N_EXPERTS, GATE_LANES), 1), 1.0, 0.0).astype(BF16)
        for b in range(bsz):
            rows = slice(b * ROW_TILE, (b + 1) * ROW_TILE)
            mod = mod_refs[b][...]
            hf = _rms_mod(x_ref[b], g_ref[...], mod[:, 3 * d:4 * d], mod[:, 4 * d:5 * d])
            h_scr[rows, :] = hf.astype(BF16)
            g_hi, g_lo = _split_hl(_router_gates_t(hf, rw_ref, rb_ref))
            gate_scr[rows, :] = _dot_tn(g_hi, eye) + _dot_tn(g_lo, eye)
        acc_scr[...] = jnp.zeros_like(acc_scr)

    h = h_scr[...]
    pick = jnp.where(_iota((GATE_LANES, GATE_LANES), 0) == e, 1.0, 0.0).astype(BF16)
    ge = _dot_hl_r(gate_scr[...], pick)[:, 0:1]
    act = (_silu(_dot(h, wg_ref[...])) * _dot(h, wu_ref[...])).astype(BF16)
    acc_scr[...] += ge * _dot(act, wd_ref[...])

    @pl.when(e == pl.num_programs(1) - 1)
    def _():
        for b in range(bsz):
            g2 = mod_refs[b][...][:, 5 * d:6 * d]
            o_ref[b] = x_ref[b] + g2 * acc_scr[b * ROW_TILE:(b + 1) * ROW_TILE, :]


def _moe_dense(xa, mods3, g, rw_t, rb, wg, wu, wd, n_ctx_tiles):
    bsz, t, d = xa.shape
    ne, _, de = wg.shape
    mod_spec = lambda b: pl.BlockSpec((None, 1, 6 * d), lambda i, e: (jnp.where(i < n_ctx_tiles, bsz, b), 0, 0))
    return pl.pallas_call(
        functools.partial(_moe_dense_kernel, d=d, bsz=bsz),
        grid=(t // ROW_TILE, ne),
        in_specs=[pl.BlockSpec((bsz, ROW_TILE, d), lambda i, e: (0, i, 0))]
                 + [mod_spec(b) for b in range(bsz)]
                 + [pl.BlockSpec((1, d), lambda i, e: (0, 0)),
                    pl.BlockSpec((ne, d), lambda i, e: (0, 0)),
                    pl.BlockSpec((ne, ROW_TILE), lambda i, e: (0, 0)),
                    pl.BlockSpec((None, d, de), lambda i, e: (e, 0, 0)),
                    pl.BlockSpec((None, d, de), lambda i, e: (e, 0, 0)),
                    pl.BlockSpec((None, de, d), lambda i, e: (e, 0, 0))],
        out_specs=pl.BlockSpec((bsz, ROW_TILE, d), lambda i, e: (0, i, 0)),
        out_shape=jax.ShapeDtypeStruct((bsz, t, d), F32),
        scratch_shapes=[pltpu.VMEM((bsz * ROW_TILE, d), BF16), pltpu.VMEM((bsz * ROW_TILE, GATE_LANES), F32),
                        pltpu.VMEM((bsz * ROW_TILE, d), F32)],
        compiler_params=_cparams("parallel", "arbitrary"),
        name="moe_dense",
    )(xa, *([mods3] * bsz), g.reshape(1, d), rw_t, rb, wg, wu, wd)


TOK_ROWS = 8
FFN_TILE = ROW_TILE


def _store_token_tiles(ref, val):
    n = val.shape[0]
    for s in range(TOK_ROWS):
        ref[pl.ds(s, n, stride=TOK_ROWS), :] = val[:, s * 128:(s + 1) * 128]


def _load_token_tiles(ref, n):
    return jnp.concatenate([ref[pl.ds(s, n, stride=TOK_ROWS), :] for s in range(TOK_ROWS)], axis=1)


def _moe_router_kernel(x_ref, mod_ref, g_ref, rw_ref, rb_ref, h_ref, ids_ref, wcol_ref, *, d):
    mod = mod_ref[...]
    hf = _rms_mod(x_ref[...], g_ref[...], mod[:, 3 * d:4 * d], mod[:, 4 * d:5 * d])
    _store_token_tiles(h_ref, hf)
    e1, e2, w1, w2 = _router_gates_t(hf, rw_ref, rb_ref, as_gates=False)
    pad = jnp.zeros((6, ROW_TILE), F32)
    ids_ref[...] = jnp.concatenate([e1, e2, pad], axis=0).astype(jnp.int32)
    eye = jnp.where(_iota((8, GATE_LANES), 0) == _iota((8, GATE_LANES), 1), 1.0, 0.0).astype(BF16)
    w_hi, w_lo = _split_hl(jnp.concatenate([w1, w2, pad], axis=0))
    wcol_ref[...] = _dot_tn(w_hi, eye) + _dot_tn(w_lo, eye)


def _moe_router(xa, mods3, g, rw_t, rb, n_ctx_tiles):
    bsz, t, d = xa.shape
    nt = t // ROW_TILE
    n = bsz * t
    flat = lambda b, i: b * nt + i
    return pl.pallas_call(
        functools.partial(_moe_router_kernel, d=d),
        grid=(bsz, nt),
        in_specs=[pl.BlockSpec((None, ROW_TILE, d), lambda b, i: (b, i, 0)),
                  pl.BlockSpec((None, 1, 6 * d), _mod_row_map(n_ctx_tiles, bsz)),
                  pl.BlockSpec((1, d), lambda b, i: (0, 0)),
                  pl.BlockSpec((N_EXPERTS, d), lambda b, i: (0, 0)),
                  pl.BlockSpec((N_EXPERTS, ROW_TILE), lambda b, i: (0, 0))],
        out_specs=[pl.BlockSpec((ROW_TILE * TOK_ROWS, 128), lambda b, i: (flat(b, i), 0)),
                   pl.BlockSpec((8, ROW_TILE), lambda b, i: (0, flat(b, i))),
                   pl.BlockSpec((ROW_TILE, GATE_LANES), lambda b, i: (flat(b, i), 0))],
        out_shape=[jax.ShapeDtypeStruct((n * TOK_ROWS, 128), F32),
                   jax.ShapeDtypeStruct((8, n), jnp.int32),
                   jax.ShapeDtypeStruct((n, GATE_LANES), F32)],
        compiler_params=_cparams("parallel", "parallel"),
        name="moe_router",
    )(xa, mods3, g.reshape(1, d), rw_t, rb)


def _moe_plan(ids, n):
    n_tiles = -(-2 * n // FFN_TILE) + N_EXPERTS
    eid = ids[:2].reshape(2 * n)
    order = jnp.argsort(eid, stable=True).astype(jnp.int32)
    inv = jnp.argsort(order).astype(jnp.int32)
    experts = jnp.arange(N_EXPERTS, dtype=jnp.int32)
    onehot = (eid[:, None] == experts[None, :]).astype(jnp.int32)
    counts = jnp.sum(onehot, axis=0)
    first = jnp.cumsum(counts) - counts
    padded = (counts + FFN_TILE - 1) // FFN_TILE * FFN_TILE
    ends = jnp.cumsum(padded)
    offs = ends - padded
    dest = inv + jnp.sum(onehot * (offs - first)[None, :], axis=1)
    tile_start = jnp.arange(n_tiles, dtype=jnp.int32) * FFN_TILE
    tile_expert = jnp.minimum(jnp.sum((ends[None, :] <= tile_start[:, None]).astype(jnp.int32), axis=1), N_EXPERTS - 1)
    tile_hot = (tile_expert[:, None] == experts[None, :]).astype(jnp.int32)
    lookup = lambda table: jnp.repeat(jnp.sum(tile_hot * table[None, :], axis=1), FFN_TILE)
    local = jnp.arange(n_tiles * FFN_TILE, dtype=jnp.int32) - lookup(offs)
    tok_sorted = order % n
    src = jnp.where(local < lookup(counts), tok_sorted[jnp.clip(lookup(first) + local, 0, 2 * n - 1)], 0)
    dest_tiles = dest.reshape(2, n // ROW_TILE, ROW_TILE).transpose(1, 0, 2)
    return src.reshape(n_tiles, 1, FFN_TILE), tile_expert, dest_tiles


def _gather_rows(idx_ref, k, src_hbm, dst, sem, n_rows, unrolled):
    def start(r, priority):
        row = idx_ref[k, r]
        pltpu.make_async_copy(src_hbm.at[pl.ds(pl.multiple_of(row * TOK_ROWS, TOK_ROWS), TOK_ROWS), :],
                              dst.at[pl.ds(pl.multiple_of(r * TOK_ROWS, TOK_ROWS), TOK_ROWS), :],
                              sem).start(priority=priority)

    if unrolled:
        for r in range(n_rows):
            start(r, r % 2)
    else:
        def body(r, carry):
            start(r, 0)
            return carry

        lax.fori_loop(0, n_rows, body, 0, unroll=8)


def _wait_rows(src_hbm, dst, sem, n_rows):
    pltpu.make_async_copy(src_hbm.at[pl.ds(0, n_rows * TOK_ROWS), :], dst, sem).wait()


def _moe_dispatch_kernel(src_ref, h_hbm, xs_hbm, sem):
    t = pl.program_id(0)
    last = pl.num_programs(0) - 1
    slot = lax.rem(t, 2)
    base = pl.multiple_of(t * (FFN_TILE * TOK_ROWS), FFN_TILE * TOK_ROWS)
    _gather_rows(src_ref, 0, h_hbm, xs_hbm.at[pl.ds(base, FFN_TILE * TOK_ROWS), :], sem.at[slot], FFN_TILE,
                 unrolled=True)
    tile_bytes = lambda s: pltpu.make_async_copy(h_hbm.at[pl.ds(0, FFN_TILE * TOK_ROWS), :],
                                                 xs_hbm.at[pl.ds(0, FFN_TILE * TOK_ROWS), :], sem.at[s])

    @pl.when(t > 0)
    def _():
        tile_bytes(1 - slot).wait()

    @pl.when(t == last)
    def _():
        tile_bytes(slot).wait()


def _moe_dispatch(h_tt, src):
    n_tiles = src.shape[0]
    return pl.pallas_call(
        _moe_dispatch_kernel,
        grid=(n_tiles,),
        in_specs=[pl.BlockSpec((None, 1, FFN_TILE), lambda t: (t, 0, 0), memory_space=pltpu.SMEM),
                  pl.BlockSpec(memory_space=pl.ANY)],
        out_specs=pl.BlockSpec(memory_space=pl.ANY),
        out_shape=jax.ShapeDtypeStruct((n_tiles * FFN_TILE * TOK_ROWS, 128), F32),
        scratch_shapes=[pltpu.SemaphoreType.DMA((2,))],
        compiler_params=_cparams("arbitrary"),
        name="moe_dispatch",
    )(src, h_tt)


def _moe_experts_kernel(te_ref, xs_ref, wg_ref, wu_ref, wd_ref, y_ref, wg_b, wu_b, wd_b):
    t = pl.program_id(0)

    @pl.when(jnp.logical_or(t == 0, te_ref[t] != te_ref[jnp.maximum(t - 1, 0)]))
    def _():
        wg_b[...] = wg_ref[...].astype(BF16)
        wu_b[...] = wu_ref[...].astype(BF16)
        wd_b[...] = wd_ref[...].astype(BF16)

    h = _load_token_tiles(xs_ref, FFN_TILE).astype(BF16)
    act = (_silu(_dot(h, wg_b[...])) * _dot(h, wu_b[...])).astype(BF16)
    _store_token_tiles(y_ref, _dot(act, wd_b[...]))


def _moe_experts(xs, tile_expert, wg, wu, wd, layer):
    n_tiles = tile_expert.shape[0]
    _, ne, d, de = wg.shape
    tile = pl.BlockSpec((FFN_TILE * TOK_ROWS, 128), lambda t, te: (t, 0))
    grid_spec = pltpu.PrefetchScalarGridSpec(
        num_scalar_prefetch=1,
        grid=(n_tiles,),
        in_specs=[tile,
                  pl.BlockSpec((None, None, d, de), lambda t, te: (layer, te[t], 0, 0)),
                  pl.BlockSpec((None, None, d, de), lambda t, te: (layer, te[t], 0, 0)),
                  pl.BlockSpec((None, None, de, d), lambda t, te: (layer, te[t], 0, 0))],
        out_specs=tile,
        scratch_shapes=[pltpu.VMEM((d, de), BF16), pltpu.VMEM((d, de), BF16), pltpu.VMEM((de, d), BF16)])
    return pl.pallas_call(
        _moe_experts_kernel,
        grid_spec=grid_spec,
        out_shape=jax.ShapeDtypeStruct(xs.shape, F32),
        compiler_params=_cparams("arbitrary"),
        name="moe_experts",
    )(tile_expert, xs, wg, wu, wd)


def _moe_combine_kernel(dest_ref, nxt_ref, x_ref, mod_ref, wcol_ref, gf_ref, y_hbm, o_ref, buf, sem, *, d, final):
    j = pl.program_id(0)
    last = pl.num_programs(0) - 1
    slot = lax.rem(j, 2)

    def gather(idx_ref, s, unrolled):
        for k in range(2):
            _gather_rows(idx_ref, k, y_hbm, buf.at[s, k], sem.at[s], ROW_TILE, unrolled)

    def wait(s):
        for k in range(2):
            _wait_rows(y_hbm, buf.at[s, k], sem.at[s], ROW_TILE)

    @pl.when(j == 0)
    def _():
        gather(dest_ref, 0, unrolled=False)

    gather(nxt_ref, 1 - slot, unrolled=True)
    wait(slot)
    w = wcol_ref[...]
    mix = (w[:, 0:1] * _load_token_tiles(buf.at[slot, 0], ROW_TILE)
           + w[:, 1:2] * _load_token_tiles(buf.at[slot, 1], ROW_TILE))
    g2 = mod_ref[...][:, 5 * d:6 * d]
    x_new = x_ref[...] + g2 * mix
    if final:
        x_new = x_new * lax.rsqrt(jnp.mean(x_new * x_new, axis=-1, keepdims=True) + EPS) * gf_ref[...]
    o_ref[...] = x_new

    @pl.when(j == last)
    def _():
        wait(1 - slot)


def _moe_combine(xa, mods3, wcol, y_tt, plan, n_ctx_tiles, final_g=None):
    bsz, t, d = xa.shape
    nt = t // ROW_TILE
    n_tiles = bsz * nt
    dest_tiles = plan[2]
    idx_spec = lambda step: pl.BlockSpec((None, 2, ROW_TILE), lambda j: (jnp.minimum(j + step, n_tiles - 1), 0, 0),
                                         memory_space=pltpu.SMEM)
    final = final_g is not None
    gf = (final_g if final else jnp.ones((d,), F32)).reshape(1, d)
    return pl.pallas_call(
        functools.partial(_moe_combine_kernel, d=d, final=final),
        grid=(n_tiles,),
        in_specs=[idx_spec(0), idx_spec(1),
                  pl.BlockSpec((None, ROW_TILE, d), lambda j: (j // nt, j % nt, 0)),
                  pl.BlockSpec((None, 1, 6 * d), lambda j: (jnp.where(j % nt < n_ctx_tiles, bsz, j // nt), 0, 0)),
                  pl.BlockSpec((ROW_TILE, GATE_LANES), lambda j: (j, 0)),
                  pl.BlockSpec((1, d), lambda j: (0, 0)),
                  pl.BlockSpec(memory_space=pl.ANY)],
        out_specs=pl.BlockSpec((None, ROW_TILE, d), lambda j: (j // nt, j % nt, 0)),
        out_shape=jax.ShapeDtypeStruct((bsz, t, d), F32),
        scratch_shapes=[pltpu.VMEM((2, 2, ROW_TILE * TOK_ROWS, 128), F32), pltpu.SemaphoreType.DMA((2,))],
        compiler_params=_cparams("arbitrary"),
        name="moe_combine",
    )(dest_tiles, dest_tiles, xa, mods3, wcol, gf, y_tt)


def _moe_sparse(xa, mods3, g, rw_t, rb, wg, wu, wd, layer, n_ctx_tiles, final_g=None):
    bsz, t, _ = xa.shape
    h_tt, ids, wcol = _moe_router(xa, mods3, g, rw_t, rb, n_ctx_tiles)
    plan = _moe_plan(ids, bsz * t)
    y_tt = _moe_experts(_moe_dispatch(h_tt, plan[0]), plan[1], wg, wu, wd, layer)
    return _moe_combine(xa, mods3, wcol, y_tt, plan, n_ctx_tiles, final_g)


def _final_norm_kernel(x_ref, g_ref, o_ref):
    x = x_ref[...]
    o_ref[...] = x * lax.rsqrt(jnp.mean(x * x, axis=-1, keepdims=True) + EPS) * g_ref[...]


def _final_norm(x, g):
    bsz, t, d = x.shape
    return pl.pallas_call(
        _final_norm_kernel,
        grid=(bsz, t // ROW_TILE),
        in_specs=[pl.BlockSpec((None, ROW_TILE, d), lambda b, i: (b, i, 0)), pl.BlockSpec((1, d), lambda b, i: (0, 0))],
        out_specs=pl.BlockSpec((None, ROW_TILE, d), lambda b, i: (b, i, 0)),
        out_shape=jax.ShapeDtypeStruct((bsz, t, d), F32),
        compiler_params=_cparams("parallel", "parallel"),
        name="final_norm",
    )(x, g.reshape(1, d))


def kernel(x, c, ctx, c_ctx, w_mod, b_mod, norm_mix, norm_ffn, w_in, gla_a2, gla_ab, mlstm_gate_b, gdn_conv, gdn_a_log, gdn_dt_bias, diff_lambda, head_norm, w_branch, w_out, router_w, router_b, w_gate, w_up, w_down, norm_final):
    depth = w_in.shape[0]
    st = _prepare(x, c, ctx, c_ctx, w_mod, b_mod)
    n_ctx_tiles = st["n_ctx_tiles"]
    tables = _rope_tables(x.shape[1], st["ct"])
    rw_t = router_w.T
    rb = jnp.broadcast_to(router_b.astype(F32)[:, None], (N_EXPERTS, ROW_TILE))
    for l in range(depth):
        with_ctx = l < depth - 1
        lam_init = 0.8 - 0.6 * math.exp(-0.3 * l)
        mods3 = st["mods"][l][:, None, :]
        proj = _layer_inproj(st, l, norm_mix, w_in)
        scans = (*_gla_branch(proj, gla_a2[l], gla_ab[l]),
                 *_mlstm_branch(proj, mlstm_gate_b[l]),
                 *_gdn_branch(proj, gdn_conv[l], gdn_a_log[l], gdn_dt_bias[l], n_ctx_tiles))
        attn = _diff_branch(proj, tables, diff_lambda[l], lam_init, n_ctx_tiles, with_ctx)
        hn = jnp.zeros((8, BRANCH_W), F32).at[:N_BRANCH].set(head_norm[l])
        xa = _merge(st["xa"], mods3, proj, scans, attn, hn, w_branch[l].astype(BF16), w_out[l].astype(BF16),
                    n_ctx_tiles, with_ctx, lam_init)
        if not with_ctx:
            n_ctx_tiles = 0
        xa = _moe_sparse(xa, mods3, norm_ffn[l], rw_t, rb, w_gate, w_up, w_down, l, n_ctx_tiles,
                         final_g=None if with_ctx else norm_final)
        st = dict(st, xa=xa, n_ctx_tiles=n_ctx_tiles)
    return st["xa"]
```

```python
import functools
import math

import numpy as np
import jax
import jax.numpy as jnp
from jax import lax
from jax.experimental import pallas as pl
from jax.experimental.pallas import tpu as pltpu

N_HEADS = 4
HEAD_DIM = 64
BRANCH_W = N_HEADS * HEAD_DIM
CHUNK = 64
GLA_RANK = 16
GLA_NORMALIZER = 16.0
CONV_W = 5
DIFF_QK = HEAD_DIM // 2
ROPE_AXIS = DIFF_QK // 2
ROPE_BASE = 10000.0
GRID_W = 64
N_EXPERTS = 16
N_GROUPS = 4
EXPERTS_PER_GROUP = 4
EPS = 1e-6
N_BRANCH = 4

ROW_TILE = 256
PROJ_COLS = 8192
BLK_GDN_QKV, BLK_DIFF_QKV, BLK_GDN_G = 0, 3, 6
BLK_GLA_Q, BLK_GLA_K, BLK_GLA_V, BLK_GLA_G = 7, 8, 9, 10
BLK_ML_Q, BLK_ML_K, BLK_ML_V, BLK_ML_G = 11, 28, 29, 30
BLK_GATES = 12
SMALL_BLOCK = 31
SM_GLA_RF, SM_GLA_RB, SM_ML_F, SM_ML_B, SM_GD_F, SM_GD_B = 0, 16, 32, 40, 48, 56
NEG_BIG = -1e30
VMEM_LIMIT = 56 * 1024 * 1024

F32 = jnp.float32
BF16 = jnp.bfloat16


def _cparams(*sem):
    return pltpu.CompilerParams(dimension_semantics=sem, vmem_limit_bytes=VMEM_LIMIT)


def _dot(a, b):
    return jnp.dot(a, b, preferred_element_type=F32)


def _dot_nt(a, b):
    return lax.dot_general(a, b, (((1,), (1,)), ((), ())), preferred_element_type=F32)


def _dot_tn(a, b):
    return lax.dot_general(a, b, (((0,), (0,)), ((), ())), preferred_element_type=F32)


def _split_hl(x):
    hi = x.astype(BF16)
    lo = (x - hi.astype(F32)).astype(BF16)
    return hi, lo


def _dot_hl(a, x):
    hi, lo = _split_hl(x)
    return _dot(a, hi) + _dot(a, lo)


def _dot_hl_r(x, a):
    hi, lo = _split_hl(x)
    return _dot(hi, a) + _dot(lo, a)


def _iota(shape, dim):
    return lax.broadcasted_iota(jnp.int32, shape, dim)


def _softplus(x):
    return jnp.maximum(x, 0.0) + jnp.log(1.0 + jnp.exp(-jnp.abs(x)))


def _log_sigmoid(x):
    return -_softplus(-x)


def _silu(x):
    return x * jax.nn.sigmoid(x)


def _lane_head(shape):
    return _iota(shape, len(shape) - 1) // HEAD_DIM


def _stack4(x):
    lh = _lane_head(x.shape)
    zero = jnp.zeros_like(x)
    return jnp.concatenate([jnp.where(lh == h, x, zero) for h in range(N_HEADS)], axis=0)


def _blk_mask():
    r = _iota((BRANCH_W, BRANCH_W), 0) // HEAD_DIM
    c = _iota((BRANCH_W, BRANCH_W), 1) // HEAD_DIM
    return r == c


def _ones_blk():
    return jnp.where(_blk_mask(), 1.0, 0.0).astype(BF16)


def _tri_hs(reverse, strict):
    t = _iota((CHUNK, BRANCH_W), 0)
    s = _iota((CHUNK, BRANCH_W), 1) % CHUNK
    if reverse:
        return (s > t) if strict else (s >= t)
    return (s < t) if strict else (s <= t)


def _eye_hs():
    t = _iota((CHUNK, BRANCH_W), 0)
    s = _iota((CHUNK, BRANCH_W), 1) % CHUNK
    return t == s


def _cum_mat(reverse):
    t = _iota((CHUNK, CHUNK), 0)
    s = _iota((CHUNK, CHUNK), 1)
    m = (s >= t) if reverse else (s <= t)
    return jnp.where(m, 1.0, 0.0).astype(BF16)


def _row_of(col_rep):
    return jnp.sum(jnp.where(_eye_hs(), col_rep, 0.0), axis=0, keepdims=True)


def _seg_max(x):
    lane = _iota(x.shape, 1)
    n = x.shape[1]
    for sh in (1, 2, 4, 8, 16, 32):
        up = pltpu.roll(x, n - sh, 1)
        dn = pltpu.roll(x, sh, 1)
        x = jnp.maximum(x, jnp.where((lane & sh) == 0, up, dn))
    return x


def _seg_sum(x, ones_blk):
    return _dot_hl_r(x, ones_blk)


def _mod_kernel(c_ref, w_ref, b_ref, o_ref):
    a = _silu(c_ref[...]).astype(BF16)
    o_ref[...] = _dot(a, w_ref[...].astype(BF16)) + b_ref[...]


def _modulation(cond, w_mod, b_mod):
    depth, d, n = w_mod.shape
    r = cond.shape[0]
    tn = 1536
    return pl.pallas_call(
        _mod_kernel,
        grid=(depth, n // tn),
        in_specs=[pl.BlockSpec((r, d), lambda l, j: (0, 0)),
                  pl.BlockSpec((None, d, tn), lambda l, j: (l, 0, j)),
                  pl.BlockSpec((None, 1, tn), lambda l, j: (l, 0, j))],
        out_specs=pl.BlockSpec((None, r, tn), lambda l, j: (l, 0, j)),
        out_shape=jax.ShapeDtypeStruct((depth, r, n), F32),
        compiler_params=_cparams("parallel", "parallel"),
        name="modulation",
    )(cond, w_mod, b_mod.reshape(depth, 1, n))


def _rms_mod(x, g, shift, scale):
    y = x * lax.rsqrt(jnp.mean(x * x, axis=-1, keepdims=True) + EPS)
    return (y * g) * (1.0 + scale) + shift


def _inproj_kernel(x_ref, mod_ref, g_ref, w_ref, o_ref, *, d, n_chunk):
    mod = mod_ref[...]
    h = _rms_mod(x_ref[...], g_ref[...], mod[:, 0:d], mod[:, d:2 * d]).astype(BF16)
    for n0 in range(0, PROJ_COLS, n_chunk):
        o_ref[:, n0:n0 + n_chunk] = _dot(h, w_ref[:, n0:n0 + n_chunk]).astype(BF16)


def _mod_row_map(n_ctx_tiles, n_batch):
    return lambda b, i: (jnp.where(i < n_ctx_tiles, n_batch, b), 0, 0)


def _inproj(xa, mods3, g, w_perm, n_ctx_tiles):
    bsz, t, d = xa.shape
    return pl.pallas_call(
        functools.partial(_inproj_kernel, d=d, n_chunk=1024),
        grid=(bsz, t // ROW_TILE),
        in_specs=[pl.BlockSpec((None, ROW_TILE, d), lambda b, i: (b, i, 0)),
                  pl.BlockSpec((None, 1, 6 * d), _mod_row_map(n_ctx_tiles, bsz)),
                  pl.BlockSpec((1, d), lambda b, i: (0, 0)),
                  pl.BlockSpec((d, PROJ_COLS), lambda b, i: (0, 0), pipeline_mode=pl.Buffered(1))],
        out_specs=pl.BlockSpec((None, ROW_TILE, PROJ_COLS), lambda b, i: (b, i, 0)),
        out_shape=jax.ShapeDtypeStruct((bsz, t, PROJ_COLS), BF16),
        compiler_params=_cparams("parallel", "parallel"),
        name="inproj",
    )(xa, mods3, g.reshape(1, d), w_perm)


def _proj_perm():
    sizes = (256, 256, 256, 256, 16, 16, 256, 256, 256, 256, 8, 8, 256, 256, 256, 256, 8, 8, 256, 256, 256, 4096)
    off = np.concatenate([[0], np.cumsum(sizes)])
    seg = lambda i: np.arange(off[i], off[i + 1])
    wide = {BLK_GDN_QKV: 12, BLK_GDN_QKV + 1: 13, BLK_GDN_QKV + 2: 14, BLK_GDN_G: 15,
            BLK_DIFF_QKV: 18, BLK_DIFF_QKV + 1: 19, BLK_DIFF_QKV + 2: 20,
            BLK_GLA_Q: 0, BLK_GLA_K: 1, BLK_GLA_V: 2, BLK_GLA_G: 3,
            BLK_ML_Q: 6, BLK_ML_K: 7, BLK_ML_V: 8, BLK_ML_G: 9}
    perm = np.full((PROJ_COLS,), -1, np.int64)
    for j, i in wide.items():
        perm[j * BRANCH_W:(j + 1) * BRANCH_W] = seg(i)
    perm[BLK_GATES * BRANCH_W:BLK_GATES * BRANCH_W + 4096] = seg(21)
    base = SMALL_BLOCK * BRANCH_W
    for lane0, i in ((SM_GLA_RF, 4), (SM_GLA_RB, 5), (SM_ML_F, 10), (SM_ML_B, 11), (SM_GD_F, 16), (SM_GD_B, 17)):
        s = seg(i)
        perm[base + lane0:base + lane0 + len(s)] = s
    return perm


def _permute_w_in(w_in):
    perm = _proj_perm()
    cuts = [0] + [i for i in range(1, PROJ_COLS)
                  if (perm[i] < 0) != (perm[i - 1] < 0) or (perm[i] >= 0 and perm[i] != perm[i - 1] + 1)] + [PROJ_COLS]
    w = w_in.astype(BF16)
    runs = [jnp.zeros((w.shape[0], b - a), BF16) if perm[a] < 0 else w[:, perm[a]:perm[a] + b - a]
            for a, b in zip(cuts[:-1], cuts[1:])]
    return jnp.concatenate(runs, axis=-1)


SCAN_CHUNKS = ROW_TILE // CHUNK


def _scan_block_map(col, nb, reverse):
    if reverse:
        return lambda j: (0, jnp.where(j == 0, 0, nb - j), col)
    return lambda j: (0, j, col)


def _scan_steps(reverse, bsz):
    order = range(SCAN_CHUNKS - 1, -1, -1) if reverse else range(SCAN_CHUNKS)
    return [(b, slice(c * CHUNK, (c + 1) * CHUNK)) for c in order for b in range(bsz)]


def _scan_call(kernel_fn, name, arrays, cols, consts, scratch, reverse):
    bsz, t, _ = arrays[0].shape
    nb = t // ROW_TILE
    blk = lambda col: pl.BlockSpec((bsz, ROW_TILE, BRANCH_W), _scan_block_map(col, nb, reverse))
    const = lambda a: pl.BlockSpec(a.shape, lambda j: (0,) * a.ndim)
    return pl.pallas_call(
        functools.partial(kernel_fn, reverse=reverse, bsz=bsz),
        grid=(nb,),
        in_specs=[blk(c) for c in cols] + [const(a) for a in consts],
        out_specs=blk(0),
        out_shape=jax.ShapeDtypeStruct((bsz, t, BRANCH_W), BF16),
        scratch_shapes=[pltpu.VMEM((bsz,) + s, F32) for s in scratch],
        compiler_params=_cparams("arbitrary"),
        name=name + ("_bwd" if reverse else "_fwd"),
    )(*arrays, *consts)


def _expand_small(small, lane0, count):
    src = _iota((BRANCH_W, BRANCH_W), 0)
    dst_head = _iota((BRANCH_W, BRANCH_W), 1) // HEAD_DIM
    e = jnp.where(src == lane0 + dst_head, 1.0, 0.0).astype(BF16)
    return _dot(small, e)


def _gla_kernel(q_ref, k_ref, v_ref, sm_ref, a2_ref, ab_ref, o_ref, st_ref, *, reverse, bsz):
    @pl.when(pl.program_id(0) == 0)
    def _():
        st_ref[...] = jnp.zeros_like(st_ref)

    cum_mat = _cum_mat(reverse)
    tri = _tri_hs(reverse, strict=False)
    blk = _blk_mask()
    a2 = a2_ref[...]
    ab = ab_ref[...]
    steps = _scan_steps(reverse, bsz)
    la = [_log_sigmoid(_dot(sm_ref[b, rows, :], a2) + ab) / GLA_NORMALIZER for b, rows in steps]
    cum = [_dot_hl(cum_mat, x) for x in la]
    tot = [jnp.sum(x, axis=0, keepdims=True) for x in la]
    q_in = [(q_ref[b, rows, :].astype(F32) * (HEAD_DIM ** -0.5) * jnp.exp(c)).astype(BF16)
            for (b, rows), c in zip(steps, cum)]
    k_out = [(k_ref[b, rows, :].astype(F32) * jnp.exp(-c)).astype(BF16) for (b, rows), c in zip(steps, cum)]
    k_end = [(k_ref[b, rows, :].astype(F32) * jnp.exp(t - c)).astype(BF16)
             for (b, rows), c, t in zip(steps, cum, tot)]
    att = [jnp.where(tri, _dot_nt(qi, _stack4(ko)), 0.0).astype(BF16) for qi, ko in zip(q_in, k_out)]
    o_intra = [_dot(a, _stack4(v_ref[b, rows, :])) for (b, rows), a in zip(steps, att)]
    kv_t = [jnp.where(blk, _dot_tn(v_ref[b, rows, :], ke), 0.0) for (b, rows), ke in zip(steps, k_end)]
    for i, (b, rows) in enumerate(steps):
        st = st_ref[b]
        o_ref[b, rows, :] = (o_intra[i] + _dot_nt(q_in[i], st.astype(BF16))).astype(o_ref.dtype)
        st_ref[b] = st * jnp.exp(tot[i]) + kv_t[i]


def _gla_scan(proj, a2pad, ab, reverse):
    return _scan_call(_gla_kernel, "gla", [proj] * 4, [BLK_GLA_Q, BLK_GLA_K, BLK_GLA_V, SMALL_BLOCK],
                      [a2pad, ab], [(BRANCH_W, BRANCH_W)], reverse)


def _gla_params(a2, ab, d):
    lane0 = SM_GLA_RB if d else SM_GLA_RF
    pad = jnp.zeros((BRANCH_W, BRANCH_W), F32).at[lane0:lane0 + GLA_RANK].set(a2[d])
    return pad.astype(BF16), ab[d].reshape(1, BRANCH_W)


def _gla_branch(proj, a2, ab):
    return tuple(_gla_scan(proj, *_gla_params(a2, ab, d), reverse=bool(d)) for d in (0, 1))


def _mlstm_kernel(q_ref, k_ref, v_ref, sm_ref, bias_ref, o_ref, ct_ref, nm_ref, *, reverse, bsz):
    @pl.when(pl.program_id(0) == 0)
    def _():
        ct_ref[...] = jnp.zeros_like(ct_ref)
        nm_ref[...] = jnp.zeros_like(nm_ref)

    lane0 = SM_ML_B if reverse else SM_ML_F
    cum_mat = _cum_mat(reverse)
    tri = _tri_hs(reverse, strict=False)
    blk = _blk_mask()
    ones_blk = _ones_blk()
    b_i = bias_ref[0:1, :]
    b_f = bias_ref[1:2, :]
    steps = _scan_steps(reverse, bsz)
    each = lambda f, *ls: [f(*a) for a in zip(*ls)]
    sm = [sm_ref[b, rows, :] for b, rows in steps]
    q = [q_ref[b, rows, :] for b, rows in steps]
    i_pre = [_expand_small(x, lane0, N_HEADS) + b_i for x in sm]
    log_f = [_log_sigmoid(_expand_small(x, lane0 + N_HEADS, N_HEADS) + b_f) for x in sm]
    f_cum = [_dot_hl(cum_mat, x) for x in log_f]
    f_tot = [jnp.sum(x, axis=0, keepdims=True) for x in log_f]
    u = each(lambda i, f: i - f, i_pre, f_cum)
    lw = each(lambda t, x: t + x, f_tot, u)
    a_end = [jnp.max(x, axis=0, keepdims=True) for x in lw]
    k = [k_ref[b, rows, :].astype(F32) * (HEAD_DIM ** -0.5) for b, rows in steps]
    kw = each(lambda kk, x, a: kk * jnp.exp(x - a), k, lw, a_end)
    k_sum = [jnp.sum(x, axis=0, keepdims=True) for x in kw]
    kv_t = [jnp.where(blk, _dot_tn(v_ref[b, rows, :], x.astype(BF16)), 0.0) for (b, rows), x in zip(steps, kw)]
    log_d = each(lambda f, x: jnp.where(tri, f + _row_of(x), NEG_BIG), f_cum, u)
    mx = [_seg_max(x) for x in log_d]
    s = each(lambda qq, kk, ld, m: (_dot_nt(qq, _stack4(kk.astype(BF16))) * jnp.exp(ld - m)).astype(BF16),
             q, k, log_d, mx)
    num1 = [_dot(x, _stack4(v_ref[b, rows, :])) for (b, rows), x in zip(steps, s)]
    den1 = [_dot(x, ones_blk) for x in s]
    for i, (b, rows) in enumerate(steps):
        n_in = nm_ref[b, 0:1, :]
        m_in = nm_ref[b, 1:2, :]
        ct = ct_ref[b]
        g = f_cum[i] + m_in
        m_t = jnp.maximum(g, mx[i])
        e = jnp.exp(g - m_t)
        r = jnp.exp(mx[i] - m_t)
        num = r * num1[i] + e * _dot_nt(q[i], ct.astype(BF16))
        den = r * den1[i] + e * _seg_sum(q[i].astype(F32) * n_in, ones_blk)
        o_ref[b, rows, :] = (num / jnp.maximum(jnp.abs(den), jnp.exp(-m_t))).astype(o_ref.dtype)

        m_new = jnp.maximum(f_tot[i] + m_in, a_end[i])
        old = jnp.exp(f_tot[i] + m_in - m_new)
        new = jnp.exp(a_end[i] - m_new)
        ct_ref[b] = ct * old + kv_t[i] * new
        nm_ref[b, 0:1, :] = n_in * old + k_sum[i] * new
        nm_ref[b, 1:2, :] = m_new


def _mlstm_scan(proj, bias, reverse):
    return _scan_call(_mlstm_kernel, "mlstm", [proj] * 4, [BLK_ML_Q, BLK_ML_K, BLK_ML_V, SMALL_BLOCK],
                      [bias], [(BRANCH_W, BRANCH_W), (8, BRANCH_W)], reverse)


def _head_rows(vals):
    rows = [jnp.repeat(v.astype(F32), HEAD_DIM) for v in vals]
    rows += [jnp.zeros((BRANCH_W,), F32)] * (8 - len(rows))
    return jnp.stack(rows)


def _mlstm_branch(proj, gate_b):
    return tuple(_mlstm_scan(proj, _head_rows([gate_b[d, 0], gate_b[d, 1]]), reverse=bool(d)) for d in (0, 1))


HALO = 8
QKV_W = 3 * BRANCH_W


def _gdn_prep_kernel(prev_ref, cur_ref, next_ref, w_ref, o_ref, *, n_ctx_tiles, n_tiles):
    i = pl.program_id(1)
    has_prev = jnp.logical_and(i != 0, i != n_ctx_tiles).astype(F32)
    has_next = jnp.logical_and(i != n_ctx_tiles - 1, i != n_tiles - 1).astype(F32)
    padded = jnp.concatenate([prev_ref[...].astype(F32) * has_prev, cur_ref[...].astype(F32),
                              next_ref[...].astype(F32) * has_next], axis=0)
    w = w_ref[...]
    acc = jnp.zeros((ROW_TILE, QKV_W), F32)
    for j in range(CONV_W):
        off = HALO + j - CONV_W // 2
        acc = acc + padded[off:off + ROW_TILE, :] * w[j:j + 1, :]
    y = _silu(acc)
    ones_blk = _ones_blk()
    q = y[:, 0:BRANCH_W]
    k = y[:, BRANCH_W:2 * BRANCH_W]
    q = q * lax.rsqrt(_seg_sum(q * q, ones_blk) + EPS) * (HEAD_DIM ** -0.5)
    k = k * lax.rsqrt(_seg_sum(k * k, ones_blk) + EPS)
    o_ref[:, 0:BRANCH_W] = q.astype(o_ref.dtype)
    o_ref[:, BRANCH_W:2 * BRANCH_W] = k.astype(o_ref.dtype)
    o_ref[:, 2 * BRANCH_W:] = y[:, 2 * BRANCH_W:].astype(o_ref.dtype)


def _gdn_prep(proj, conv_w, n_ctx_tiles):
    bsz, t, _ = proj.shape
    nt = t // ROW_TILE
    per = ROW_TILE // HALO
    last = t // HALO - 1
    qkv_blk = BLK_GDN_QKV * BRANCH_W // QKV_W
    w8 = jnp.zeros((8, QKV_W), F32).at[:CONV_W].set(conv_w)
    return pl.pallas_call(
        functools.partial(_gdn_prep_kernel, n_ctx_tiles=n_ctx_tiles, n_tiles=nt),
        grid=(bsz, nt),
        in_specs=[pl.BlockSpec((None, HALO, QKV_W), lambda b, i: (b, jnp.maximum(i * per - 1, 0), qkv_blk)),
                  pl.BlockSpec((None, ROW_TILE, QKV_W), lambda b, i: (b, i, qkv_blk)),
                  pl.BlockSpec((None, HALO, QKV_W), lambda b, i: (b, jnp.minimum((i + 1) * per, last), qkv_blk)),
                  pl.BlockSpec((8, QKV_W), lambda b, i: (0, 0))],
        out_specs=pl.BlockSpec((None, ROW_TILE, QKV_W), lambda b, i: (b, i, 0)),
        out_shape=jax.ShapeDtypeStruct((bsz, t, QKV_W), BF16),
        compiler_params=_cparams("parallel", "parallel"),
        name="gdn_prep",
    )(proj, proj, proj, w8)


def _gdn_kernel(q_ref, k_ref, v_ref, sm_ref, par_ref, o_ref, s_ref, *, reverse, bsz):
    @pl.when(pl.program_id(0) == 0)
    def _():
        s_ref[...] = jnp.zeros_like(s_ref)

    lane0 = SM_GD_B if reverse else SM_GD_F
    cum_mat = _cum_mat(reverse)
    tri = _tri_hs(reverse, strict=False)
    tri_strict = _tri_hs(reverse, strict=True)
    blk = _blk_mask()
    eye = jnp.where(_eye_hs(), 1.0, 0.0)
    a_scale = jnp.exp(par_ref[0:1, :])
    dt_bias = par_ref[1:2, :]
    steps = _scan_steps(reverse, bsz)
    each = lambda f, *ls: [f(*a) for a in zip(*ls)]
    sm = [sm_ref[b, rows, :] for b, rows in steps]
    q = [q_ref[b, rows, :] for b, rows in steps]
    kb16 = [k_ref[b, rows, :] for b, rows in steps]
    beta = [jax.nn.sigmoid(_expand_small(x, lane0, N_HEADS)) for x in sm]
    g = [-a_scale * _softplus(_expand_small(x, lane0 + N_HEADS, N_HEADS) + dt_bias) for x in sm]
    cum = [_dot_hl(cum_mat, x) for x in g]
    tot = [jnp.sum(x, axis=0, keepdims=True) for x in g]
    gam = [jnp.where(tri, jnp.exp(jnp.where(tri, c - _row_of(c), 0.0)), 0.0) for c in cum]
    k_beta = each(lambda kk, bb: kk.astype(F32) * bb, kb16, beta)
    k4 = [_stack4(x) for x in kb16]
    a_hs = each(lambda kb, kk, gm: jnp.where(tri_strict, _dot_nt(kb.astype(BF16), kk) * gm, 0.0), k_beta, k4, gam)
    attn = each(lambda qq, kk, gm: (_dot_nt(qq, kk) * gm).astype(BF16), q, k4, gam)

    p = [-x for x in a_hs]
    t_inv = [eye + x for x in p]
    for _ in range(5):
        p = [_dot(x.astype(BF16), _stack4(x.astype(BF16))) for x in p]
        t_inv = each(lambda t, x: t + _dot(t.astype(BF16), _stack4(x.astype(BF16))), t_inv, p)
    w = each(lambda t, kb, c: _dot(t.astype(BF16), _stack4((kb * jnp.exp(c)).astype(BF16))).astype(BF16),
             t_inv, k_beta, cum)
    u = [_dot(t.astype(BF16), _stack4((v_ref[b, rows, :].astype(F32) * bb).astype(BF16)))
         for (b, rows), t, bb in zip(steps, t_inv, beta)]
    q_dec = each(lambda qq, c: (qq.astype(F32) * jnp.exp(c)).astype(BF16), q, cum)
    k_end = each(lambda kk, t, c: (kk.astype(F32) * jnp.exp(t - c)).astype(BF16), kb16, tot, cum)
    for i, (b, rows) in enumerate(steps):
        s = s_ref[b]
        sb = s.astype(BF16)
        v_new = (u[i] - _dot(w[i], sb)).astype(BF16)
        o_ref[b, rows, :] = (_dot(q_dec[i], sb) + _dot(attn[i], _stack4(v_new))).astype(o_ref.dtype)
        s_ref[b] = s * jnp.exp(tot[i]) + jnp.where(blk, _dot_tn(k_end[i], v_new), 0.0)


def _gdn_scan(gqkv, proj, par, reverse):
    return _scan_call(_gdn_kernel, "gdn", [gqkv, gqkv, gqkv, proj], [0, 1, 2, SMALL_BLOCK],
                      [par], [(BRANCH_W, BRANCH_W)], reverse)


def _gdn_branch(proj, conv_w, a_log, dt_bias, n_ctx_tiles):
    gqkv = _gdn_prep(proj, conv_w, n_ctx_tiles)
    return tuple(_gdn_scan(gqkv, proj, _head_rows([a_log[d], dt_bias[d]]), reverse=bool(d))
                 for d in (0, 1))


VX_W = 128
LOG2E = 1.4426950408889634


def _kv_tile(t):
    return next(k for k in (3 * ROW_TILE, 2 * ROW_TILE, ROW_TILE) if t % k == 0)


def _rope_tables(n_lat, n_ctx):
    pos = jnp.arange(n_lat)
    row, col = pos // GRID_W, pos % GRID_W
    inv = ROPE_BASE ** (-jnp.arange(0, ROPE_AXIS, 2, dtype=F32) / ROPE_AXIS)
    ang = jnp.concatenate([row.astype(F32)[:, None] * inv, col.astype(F32)[:, None] * inv], axis=-1)
    cos = jnp.concatenate([jnp.ones((n_ctx, ROPE_AXIS), F32), jnp.cos(ang)], axis=0)
    sin = jnp.concatenate([jnp.zeros((n_ctx, ROPE_AXIS), F32), jnp.sin(ang)], axis=0)
    reps = BRANCH_W // DIFF_QK
    cos_l = jnp.tile(jnp.concatenate([cos, cos], axis=-1), (1, reps))
    sin_l = jnp.tile(jnp.concatenate([-sin, sin], axis=-1), (1, reps))
    return cos_l, sin_l


def _rope_prep_kernel(x_ref, cos_ref, sin_ref, q_ref, kt_ref, vx_ref):
    lane = _iota((ROW_TILE, BRANCH_W), 1)
    first_half = (lane % DIFF_QK) < ROPE_AXIS
    cos = cos_ref[...]
    sin = sin_ref[...]

    def rope(x):
        partner = jnp.where(first_half, pltpu.roll(x, BRANCH_W - ROPE_AXIS, 1), pltpu.roll(x, ROPE_AXIS, 1))
        return x * cos + partner * sin

    q = rope(x_ref[:, 0:BRANCH_W].astype(F32)) * (DIFF_QK ** -0.5 * LOG2E)
    k = rope(x_ref[:, BRANCH_W:2 * BRANCH_W].astype(F32))
    q_ref[...] = q.astype(q_ref.dtype)
    kt_ref[...] = jnp.transpose(k).astype(kt_ref.dtype)
    v = x_ref[:, 2 * BRANCH_W:]
    src = _iota((BRANCH_W, VX_W), 0)
    dst = _iota((BRANCH_W, VX_W), 1)
    ones_col = jnp.where(_iota((ROW_TILE, VX_W), 1) == HEAD_DIM, 1.0, 0.0)
    for h in range(N_HEADS):
        sel = jnp.where(jnp.logical_and(src == h * HEAD_DIM + dst, dst < HEAD_DIM), 1.0, 0.0).astype(BF16)
        vx_ref[h] = (_dot(v, sel) + ones_col).astype(vx_ref.dtype)


def _rope_prep(proj, cos_l, sin_l):
    bsz, t, _ = proj.shape
    nt = t // ROW_TILE
    kvt = _kv_tile(t)
    r = kvt // ROW_TILE
    qkv_blk = BLK_DIFF_QKV * BRANCH_W // QKV_W
    tab = pl.BlockSpec((ROW_TILE, BRANCH_W), lambda b, i: (i, 0))
    return pl.pallas_call(
        _rope_prep_kernel,
        grid=(bsz, nt),
        in_specs=[pl.BlockSpec((None, ROW_TILE, QKV_W), lambda b, i: (b, i, qkv_blk)), tab, tab],
        out_specs=[pl.BlockSpec((None, ROW_TILE, BRANCH_W), lambda b, i: (b, i, 0)),
                   pl.BlockSpec((None, None, BRANCH_W, ROW_TILE), lambda b, i: (b, i // r, 0, i % r)),
                   pl.BlockSpec((None, None, N_HEADS, ROW_TILE, VX_W), lambda b, i: (b, i // r, 0, i % r, 0))],
        out_shape=[jax.ShapeDtypeStruct((bsz, t, BRANCH_W), BF16),
                   jax.ShapeDtypeStruct((bsz, t // kvt, BRANCH_W, kvt), BF16),
                   jax.ShapeDtypeStruct((bsz, t // kvt, N_HEADS, kvt, VX_W), BF16)],
        compiler_params=_cparams("parallel", "parallel"),
        name="rope_prep",
    )(proj, cos_l, sin_l)


N_CHAINS = 2 * N_HEADS


def _attn_kernel(q_ref, kt_ref, vx_ref, lp_ref, o_ref, qm_scr, s0_scr, s1_scr, m0_scr, m1_scr, al0_scr, al1_scr,
                 acc_scr, *, nk, lam_init):
    s_bufs, m_bufs, al_bufs = (s0_scr, s1_scr), (m0_scr, m1_scr), (al0_scr, al1_scr)
    q = q_ref[...]
    lane = _iota(q.shape, 1)
    zero = jnp.zeros_like(q)
    for c in range(N_CHAINS):
        lo = c * DIFF_QK
        qm_scr[c] = jnp.where(jnp.logical_and(lane >= lo, lane < lo + DIFF_QK), q, zero)
    acc_scr[...] = jnp.zeros_like(acc_scr)

    def stage_a(c, kt, slot, first=False):
        s = _dot(qm_scr[c], kt)
        s_bufs[slot][c] = s
        row_max = jnp.max(s, axis=1, keepdims=True)
        if first:
            m_bufs[slot][c] = row_max
            al_bufs[slot][c] = jnp.zeros_like(row_max)
        else:
            m_old = m_bufs[1 - slot][c]
            m_new = jnp.maximum(m_old, row_max)
            m_bufs[slot][c] = m_new
            al_bufs[slot][c] = jnp.exp2(m_old - m_new)

    def stage_b(c, j, slot):
        p = jnp.exp2(s_bufs[slot][c] - m_bufs[slot][c]).astype(BF16)
        acc_scr[c] = acc_scr[c] * al_bufs[slot][c] + _dot(p, vx_ref[j, c // 2])

    def step(ja, jb, slot_a, first=False):
        kt = None if ja is None else kt_ref[ja]
        for c in range(N_CHAINS):
            if ja is not None:
                stage_a(c, kt, slot_a, first)
            if jb is not None:
                stage_b(c, jb, 1 - slot_a)

    step(0, None, 0, first=True)

    def body(i, carry):
        j = 2 * i + 1
        step(j, j - 1, 1)
        step(j + 1, j, 0)
        return carry

    n_pairs = (nk - 1) // 2
    lax.fori_loop(0, n_pairs, body, 0)
    if nk % 2 == 0:
        step(nk - 1, nk - 2, 1)
    step(None, nk - 1, 1 - (nk - 1) % 2)

    lp = lp_ref[...]
    grp_src = _iota((VX_W, VX_W), 0) // DIFF_QK
    prod1 = lp[0:1, :] * lp[1:2, :]
    prod2 = lp[2:3, :] * lp[3:4, :]
    src = _iota((VX_W, BRANCH_W), 0)
    dst = _iota((VX_W, BRANCH_W), 1)
    out = jnp.zeros((ROW_TILE, BRANCH_W), F32)
    for h in range(N_HEADS):
        pick = jnp.where(grp_src == h, 1.0, 0.0).astype(BF16)
        lam = jnp.exp(_dot_hl_r(prod1, pick)) - jnp.exp(_dot_hl_r(prod2, pick)) + lam_init
        a0 = acc_scr[2 * h]
        a1 = acc_scr[2 * h + 1]
        o_h = a0 / a0[:, HEAD_DIM:HEAD_DIM + 1] - lam * (a1 / a1[:, HEAD_DIM:HEAD_DIM + 1])
        place = jnp.where(jnp.logical_and(dst == src + h * HEAD_DIM, src < HEAD_DIM), 1.0, 0.0).astype(BF16)
        out = out + _dot(o_h.astype(BF16), place)
    o_ref[...] = out.astype(o_ref.dtype)


def _diff_attention(q_rot, kt, vx, lam_p, lam_init, q_tile0, n_q_tiles, kv_len):
    bsz = q_rot.shape[0]
    n_kv_arr, _, kvt = kt.shape[1:]
    tk = min(kv_len, kvt)
    assert kv_len % tk == 0 and kvt % tk == 0
    nk = kv_len // tk
    lp = jnp.zeros((8, VX_W), F32).at[:4].set(lam_p.reshape(4, N_HEADS * DIFF_QK))
    return pl.pallas_call(
        functools.partial(_attn_kernel, nk=nk, lam_init=lam_init),
        grid=(bsz, n_q_tiles),
        in_specs=[pl.BlockSpec((None, ROW_TILE, BRANCH_W), lambda b, i: (b, i + q_tile0, 0)),
                  pl.BlockSpec((None, nk, BRANCH_W, tk), lambda b, i: (b, 0, 0, 0), pipeline_mode=pl.Buffered(1)),
                  pl.BlockSpec((None, nk, N_HEADS, tk, VX_W), lambda b, i: (b, 0, 0, 0, 0),
                               pipeline_mode=pl.Buffered(1)),
                  pl.BlockSpec((8, VX_W), lambda b, i: (0, 0))],
        out_specs=pl.BlockSpec((None, ROW_TILE, BRANCH_W), lambda b, i: (b, i, 0)),
        out_shape=jax.ShapeDtypeStruct((bsz, n_q_tiles * ROW_TILE, BRANCH_W), BF16),
        scratch_shapes=[pltpu.VMEM((N_CHAINS, ROW_TILE, BRANCH_W), BF16)]
                       + [pltpu.VMEM((N_CHAINS, ROW_TILE, tk), F32)] * 2
                       + [pltpu.VMEM((N_CHAINS, ROW_TILE, 1), F32)] * 4
                       + [pltpu.VMEM((N_CHAINS, ROW_TILE, VX_W), F32)],
        compiler_params=_cparams("parallel", "arbitrary"),
        name="diff_attention",
    )(q_rot, kt, vx, lp)


def _diff_branch(proj, tables, lam_p, lam_init, n_ctx_tiles, with_ctx):
    t = proj.shape[1]
    q_rot, kt, vx = _rope_prep(proj, *tables)
    lat = _diff_attention(q_rot, kt, vx, lam_p, lam_init, n_ctx_tiles, t // ROW_TILE - n_ctx_tiles, t)
    if not with_ctx:
        return lat
    ctx = _diff_attention(q_rot, kt, vx, lam_p, lam_init, 0, n_ctx_tiles, n_ctx_tiles * ROW_TILE)
    return jnp.concatenate([ctx, lat], axis=1)


def _prepare(x, c, ctx, c_ctx, w_mod, b_mod):
    bsz, _, d = x.shape
    ct = ctx.shape[1]
    assert ct % ROW_TILE == 0 and x.shape[1] % ROW_TILE == 0
    xa = jnp.concatenate([ctx, x], axis=1)
    rows = 8 * ((bsz + 1 + 7) // 8)
    cond = jnp.zeros((rows, d), F32).at[:bsz].set(c).at[bsz].set(c_ctx)
    mods = _modulation(cond, w_mod, b_mod)
    return dict(xa=xa, mods=mods, n_ctx_tiles=ct // ROW_TILE, ct=ct)


def _layer_inproj(st, l, norm_mix, w_in):
    mods3 = st["mods"][l][:, None, :]
    return _inproj(st["xa"], mods3, norm_mix[l], _permute_w_in(w_in[l]), st["n_ctx_tiles"])


def _merge_kernel(x_ref, mod_ref, glf_ref, glb_ref, mlf_ref, mlb_ref, gdf_ref, gdb_ref, at_ref,
                  g_gla_ref, g_ml_ref, g_gd_ref, mg0_ref, mg1_ref, mg2_ref, mg3_ref,
                  hn_ref, wb_ref, wo_ref, o_ref, *, d, lam_init):
    ones_blk = _ones_blk()

    def head_norm(o, i):
        ms = _seg_sum(o * o, ones_blk) * (1.0 / HEAD_DIM)
        return o * lax.rsqrt(ms + EPS) * hn_ref[i:i + 1, :]

    f32 = lambda r: r[...].astype(F32)
    ys = [head_norm(f32(glf_ref) + f32(glb_ref), 0) * _silu(f32(g_gla_ref)),
          head_norm(f32(mlf_ref) + f32(mlb_ref), 1) * jax.nn.sigmoid(f32(g_ml_ref)),
          head_norm(f32(gdf_ref) + f32(gdb_ref), 2) * _silu(f32(g_gd_ref)),
          head_norm(f32(at_ref), 3) * (1.0 - lam_init)]
    acc = None
    for i, (y, mg_ref) in enumerate(zip(ys, (mg0_ref, mg1_ref, mg2_ref, mg3_ref))):
        term = jax.nn.sigmoid(f32(mg_ref)) * _dot(y.astype(BF16), wb_ref[i])
        acc = term if acc is None else acc + term
    out = _dot(acc.astype(BF16), wo_ref[...])
    g1 = mod_ref[...][:, 2 * d:3 * d]
    o_ref[...] = x_ref[...] + g1 * out


def _merge(xa, mods3, proj, scans, attn, hn, wb, wo, n_ctx_tiles, with_ctx, lam_init):
    bsz, t, d = xa.shape
    off = 0 if with_ctx else n_ctx_tiles
    nt = t // ROW_TILE - off
    rows = lambda col: pl.BlockSpec((None, ROW_TILE, BRANCH_W), lambda b, i: (b, i + off, col))
    gate = lambda j: pl.BlockSpec((None, ROW_TILE, d), lambda b, i: (b, i + off, BLK_GATES * BRANCH_W // d + j))
    const = lambda shape: pl.BlockSpec(shape, lambda b, i: (0,) * len(shape))
    return pl.pallas_call(
        functools.partial(_merge_kernel, d=d, lam_init=lam_init),
        grid=(bsz, nt),
        in_specs=[pl.BlockSpec((None, ROW_TILE, d), lambda b, i: (b, i + off, 0)),
                  pl.BlockSpec((None, 1, 6 * d), lambda b, i: (jnp.where(i + off < n_ctx_tiles, bsz, b), 0, 0))]
                 + [rows(0)] * 6
                 + [pl.BlockSpec((None, ROW_TILE, BRANCH_W), lambda b, i: (b, i, 0))]
                 + [rows(BLK_GLA_G), rows(BLK_ML_G), rows(BLK_GDN_G)]
                 + [gate(j) for j in range(N_BRANCH)]
                 + [const((8, BRANCH_W)), const((N_BRANCH, BRANCH_W, d)), const((d, d))],
        out_specs=pl.BlockSpec((None, ROW_TILE, d), lambda b, i: (b, i, 0)),
        out_shape=jax.ShapeDtypeStruct((bsz, nt * ROW_TILE, d), F32),
        compiler_params=_cparams("parallel", "parallel"),
        name="merge",
    )(xa, mods3, *scans, attn, proj, proj, proj, proj, proj, proj, proj, hn, wb, wo)


GATE_LANES = 128


def _router_gates_t(hf, rw_ref, rb_ref, as_gates=True):
    h_hi, h_lo = _split_hl(hf)
    w_hi, w_lo = _split_hl(rw_ref[...])
    logits = _dot_nt(w_hi, h_hi) + _dot_nt(w_hi, h_lo) + _dot_nt(w_lo, h_hi)
    scores = jax.nn.sigmoid(logits)
    sel = scores + rb_ref[...]
    s = [sel[e:e + 1, :] for e in range(N_EXPERTS)]
    sc = [scores[e:e + 1, :] for e in range(N_EXPERTS)]
    grp = []
    for g in range(N_GROUPS):
        a, b, c, dd = s[4 * g:4 * g + 4]
        grp.append(functools.reduce(jnp.maximum, [a + b, a + c, a + dd, b + c, b + dd, c + dd]))
    gmax = functools.reduce(jnp.maximum, grp)
    chosen, taken = [], None
    for g in range(N_GROUPS):
        hit = grp[g] == gmax
        if taken is not None:
            hit = jnp.logical_and(hit, jnp.logical_not(taken))
        taken = hit if taken is None else jnp.logical_or(taken, hit)
        chosen.append(hit)
    ms = [jnp.where(chosen[e // EXPERTS_PER_GROUP], s[e], NEG_BIG) for e in range(N_EXPERTS)]

    def first_argmax(vals):
        top = functools.reduce(jnp.maximum, vals)
        hits, seen = [], None
        for v in vals:
            hit = v == top
            if seen is not None:
                hit = jnp.logical_and(hit, jnp.logical_not(seen))
            seen = hit if seen is None else jnp.logical_or(seen, hit)
            hits.append(hit)
        return hits

    oh1 = first_argmax(ms)
    oh2 = first_argmax([jnp.where(o, NEG_BIG, v) for o, v in zip(oh1, ms)])
    zero = jnp.zeros_like(sc[0])
    w1 = functools.reduce(jnp.add, [jnp.where(o, v, zero) for o, v in zip(oh1, sc)])
    w2 = functools.reduce(jnp.add, [jnp.where(o, v, zero) for o, v in zip(oh2, sc)])
    tot = w1 + w2
    if not as_gates:
        ids = [functools.reduce(jnp.add, [jnp.where(o, float(e), 0.0) for e, o in enumerate(oh)]) for oh in (oh1, oh2)]
        return ids[0], ids[1], w1 / tot, w2 / tot
    rows = [jnp.where(o1, w1 / tot, zero) + jnp.where(o2, w2 / tot, zero) for o1, o2 in zip(oh1, oh2)]
    return jnp.concatenate(rows, axis=0)


def _moe_dense_kernel(x_ref, *rest, d, bsz):
    mod_refs = rest[:bsz]
    g_ref, rw_ref, rb_ref, wg_ref, wu_ref, wd_ref, o_ref, h_scr, gate_scr, acc_scr = rest[bsz:]
    e = pl.program_id(1)

    @pl.when(e == 0)
    def _():
        eye = jnp.where(_iota((N_EXPERTS, GATE_LANES), 0) == _iota((N_EXPERTS, GATE_LANES), 1), 1.0, 0.0).astype(BF16)
        for b in range(bsz):
            rows = slice(b * ROW_TILE, (b + 1) * ROW_TILE)
            mod = mod_refs[b][...]
            hf = _rms_mod(x_ref[b], g_ref[...], mod[:, 3 * d:4 * d], mod[:, 4 * d:5 * d])
            h_scr[rows, :] = hf.astype(BF16)
            g_hi, g_lo = _split_hl(_router_gates_t(hf, rw_ref, rb_ref))
            gate_scr[rows, :] = _dot_tn(g_hi, eye) + _dot_tn(g_lo, eye)
        acc_scr[...] = jnp.zeros_like(acc_scr)

    h = h_scr[...]
    pick = jnp.where(_iota((GATE_LANES, GATE_LANES), 0) == e, 1.0, 0.0).astype(BF16)
    ge = _dot_hl_r(gate_scr[...], pick)[:, 0:1]
    act = (_silu(_dot(h, wg_ref[...])) * _dot(h, wu_ref[...])).astype(BF16)
    acc_scr[...] += ge * _dot(act, wd_ref[...])

    @pl.when(e == pl.num_programs(1) - 1)
    def _():
        for b in range(bsz):
            g2 = mod_refs[b][...][:, 5 * d:6 * d]
            o_ref[b] = x_ref[b] + g2 * acc_scr[b * ROW_TILE:(b + 1) * ROW_TILE, :]


def _moe_dense(xa, mods3, g, rw_t, rb, wg, wu, wd, n_ctx_tiles):
    bsz, t, d = xa.shape
    ne, _, de = wg.shape
    mod_spec = lambda b: pl.BlockSpec((None, 1, 6 * d), lambda i, e: (jnp.where(i < n_ctx_tiles, bsz, b), 0, 0))
    return pl.pallas_call(
        functools.partial(_moe_dense_kernel, d=d, bsz=bsz),
        grid=(t // ROW_TILE, ne),
        in_specs=[pl.BlockSpec((bsz, ROW_TILE, d), lambda i, e: (0, i, 0))]
                 + [mod_spec(b) for b in range(bsz)]
                 + [pl.BlockSpec((1, d), lambda i, e: (0, 0)),
                    pl.BlockSpec((ne, d), lambda i, e: (0, 0)),
                    pl.BlockSpec((ne, ROW_TILE), lambda i, e: (0, 0)),
                    pl.BlockSpec((None, d, de), lambda i, e: (e, 0, 0)),
                    pl.BlockSpec((None, d, de), lambda i, e: (e, 0, 0)),
                    pl.BlockSpec((None, de, d), lambda i, e: (e, 0, 0))],
        out_specs=pl.BlockSpec((bsz, ROW_TILE, d), lambda i, e: (0, i, 0)),
        out_shape=jax.ShapeDtypeStruct((bsz, t, d), F32),
        scratch_shapes=[pltpu.VMEM((bsz * ROW_TILE, d), BF16), pltpu.VMEM((bsz * ROW_TILE, GATE_LANES), F32),
                        pltpu.VMEM((bsz * ROW_TILE, d), F32)],
        compiler_params=_cparams("parallel", "arbitrary"),
        name="moe_dense",
    )(xa, *([mods3] * bsz), g.reshape(1, d), rw_t, rb, wg, wu, wd)


TOK_ROWS = 8
FFN_TILE = ROW_TILE


def _store_token_tiles(ref, val):
    n = val.shape[0]
    for s in range(TOK_ROWS):
        ref[pl.ds(s, n, stride=TOK_ROWS), :] = val[:, s * 128:(s + 1) * 128]


def _load_token_tiles(ref, n):
    return jnp.concatenate([ref[pl.ds(s, n, stride=TOK_ROWS), :] for s in range(TOK_ROWS)], axis=1)


def _moe_router_kernel(x_ref, mod_ref, g_ref, rw_ref, rb_ref, h_ref, ids_ref, wcol_ref, *, d):
    mod = mod_ref[...]
    hf = _rms_mod(x_ref[...], g_ref[...], mod[:, 3 * d:4 * d], mod[:, 4 * d:5 * d])
    _store_token_tiles(h_ref, hf)
    e1, e2, w1, w2 = _router_gates_t(hf, rw_ref, rb_ref, as_gates=False)
    pad = jnp.zeros((6, ROW_TILE), F32)
    ids_ref[...] = jnp.concatenate([e1, e2, pad], axis=0).astype(jnp.int32)
    eye = jnp.where(_iota((8, GATE_LANES), 0) == _iota((8, GATE_LANES), 1), 1.0, 0.0).astype(BF16)
    w_hi, w_lo = _split_hl(jnp.concatenate([w1, w2, pad], axis=0))
    wcol_ref[...] = _dot_tn(w_hi, eye) + _dot_tn(w_lo, eye)


def _moe_router(xa, mods3, g, rw_t, rb, n_ctx_tiles):
    bsz, t, d = xa.shape
    nt = t // ROW_TILE
    n = bsz * t
    flat = lambda b, i: b * nt + i
    return pl.pallas_call(
        functools.partial(_moe_router_kernel, d=d),
        grid=(bsz, nt),
        in_specs=[pl.BlockSpec((None, ROW_TILE, d), lambda b, i: (b, i, 0)),
                  pl.BlockSpec((None, 1, 6 * d), _mod_row_map(n_ctx_tiles, bsz)),
                  pl.BlockSpec((1, d), lambda b, i: (0, 0)),
                  pl.BlockSpec((N_EXPERTS, d), lambda b, i: (0, 0)),
                  pl.BlockSpec((N_EXPERTS, ROW_TILE), lambda b, i: (0, 0))],
        out_specs=[pl.BlockSpec((ROW_TILE * TOK_ROWS, 128), lambda b, i: (flat(b, i), 0)),
                   pl.BlockSpec((8, ROW_TILE), lambda b, i: (0, flat(b, i))),
                   pl.BlockSpec((ROW_TILE, GATE_LANES), lambda b, i: (flat(b, i), 0))],
        out_shape=[jax.ShapeDtypeStruct((n * TOK_ROWS, 128), F32),
                   jax.ShapeDtypeStruct((8, n), jnp.int32),
                   jax.ShapeDtypeStruct((n, GATE_LANES), F32)],
        compiler_params=_cparams("parallel", "parallel"),
        name="moe_router",
    )(xa, mods3, g.reshape(1, d), rw_t, rb)


def _moe_plan(ids, n):
    n_tiles = -(-2 * n // FFN_TILE) + N_EXPERTS
    eid = ids[:2].reshape(2 * n)
    order = jnp.argsort(eid, stable=True).astype(jnp.int32)
    inv = jnp.argsort(order).astype(jnp.int32)
    experts = jnp.arange(N_EXPERTS, dtype=jnp.int32)
    onehot = (eid[:, None] == experts[None, :]).astype(jnp.int32)
    counts = jnp.sum(onehot, axis=0)
    first = jnp.cumsum(counts) - counts
    padded = (counts + FFN_TILE - 1) // FFN_TILE * FFN_TILE
    ends = jnp.cumsum(padded)
    offs = ends - padded
    dest = inv + jnp.sum(onehot * (offs - first)[None, :], axis=1)
    tile_start = jnp.arange(n_tiles, dtype=jnp.int32) * FFN_TILE
    tile_expert = jnp.minimum(jnp.sum((ends[None, :] <= tile_start[:, None]).astype(jnp.int32), axis=1), N_EXPERTS - 1)
    tile_hot = (tile_expert[:, None] == experts[None, :]).astype(jnp.int32)
    lookup = lambda table: jnp.repeat(jnp.sum(tile_hot * table[None, :], axis=1), FFN_TILE)
    local = jnp.arange(n_tiles * FFN_TILE, dtype=jnp.int32) - lookup(offs)
    tok_sorted = order % n
    src = jnp.where(local < lookup(counts), tok_sorted[jnp.clip(lookup(first) + local, 0, 2 * n - 1)], 0)
    dest_tiles = dest.reshape(2, n // ROW_TILE, ROW_TILE).transpose(1, 0, 2)
    return src.reshape(n_tiles, 1, FFN_TILE), tile_expert, dest_tiles


def _gather_rows(idx_ref, k, src_hbm, dst, sem, n_rows, unrolled):
    def start(r, priority):
        row = idx_ref[k, r]
        pltpu.make_async_copy(src_hbm.at[pl.ds(pl.multiple_of(row * TOK_ROWS, TOK_ROWS), TOK_ROWS), :],
                              dst.at[pl.ds(pl.multiple_of(r * TOK_ROWS, TOK_ROWS), TOK_ROWS), :],
                              sem).start(priority=priority)

    if unrolled:
        for r in range(n_rows):
            start(r, r % 2)
    else:
        def body(r, carry):
            start(r, 0)
            return carry

        lax.fori_loop(0, n_rows, body, 0, unroll=8)


def _wait_rows(src_hbm, dst, sem, n_rows):
    pltpu.make_async_copy(src_hbm.at[pl.ds(0, n_rows * TOK_ROWS), :], dst, sem).wait()


def _moe_dispatch_kernel(src_ref, nxt_ref, h_hbm, xs_ref, buf, sem):
    t = pl.program_id(0)
    last = pl.num_programs(0) - 1
    slot = lax.rem(t, 2)

    @pl.when(t == 0)
    def _():
        _gather_rows(src_ref, 0, h_hbm, buf.at[0], sem.at[0], FFN_TILE, unrolled=False)

    _gather_rows(nxt_ref, 0, h_hbm, buf.at[1 - slot], sem.at[1 - slot], FFN_TILE, unrolled=True)
    _wait_rows(h_hbm, buf.at[slot], sem.at[slot], FFN_TILE)
    xs_ref[...] = buf[slot]

    @pl.when(t == last)
    def _():
        _wait_rows(h_hbm, buf.at[1 - slot], sem.at[1 - slot], FFN_TILE)


def _moe_dispatch(h_tt, src):
    n_tiles = src.shape[0]
    idx_spec = lambda step: pl.BlockSpec((None, 1, FFN_TILE), lambda t: (jnp.minimum(t + step, n_tiles - 1), 0, 0),
                                         memory_space=pltpu.SMEM)
    return pl.pallas_call(
        _moe_dispatch_kernel,
        grid=(n_tiles,),
        in_specs=[idx_spec(0), idx_spec(1), pl.BlockSpec(memory_space=pl.ANY)],
        out_specs=pl.BlockSpec((FFN_TILE * TOK_ROWS, 128), lambda t: (t, 0)),
        out_shape=jax.ShapeDtypeStruct((n_tiles * FFN_TILE * TOK_ROWS, 128), F32),
        scratch_shapes=[pltpu.VMEM((2, FFN_TILE * TOK_ROWS, 128), F32), pltpu.SemaphoreType.DMA((2,))],
        compiler_params=_cparams("arbitrary"),
        name="moe_dispatch",
    )(src, src, h_tt)


def _moe_experts_kernel(te_ref, xs_ref, wg_ref, wu_ref, wd_ref, y_ref, wg_b, wu_b, wd_b):
    t = pl.program_id(0)

    @pl.when(jnp.logical_or(t == 0, te_ref[t] != te_ref[jnp.maximum(t - 1, 0)]))
    def _():
        wg_b[...] = wg_ref[...].astype(BF16)
        wu_b[...] = wu_ref[...].astype(BF16)
        wd_b[...] = wd_ref[...].astype(BF16)

    h = _load_token_tiles(xs_ref, FFN_TILE).astype(BF16)
    act = (_silu(_dot(h, wg_b[...])) * _dot(h, wu_b[...])).astype(BF16)
    _store_token_tiles(y_ref, _dot(act, wd_b[...]))


def _moe_experts(xs, tile_expert, wg, wu, wd, layer):
    n_tiles = tile_expert.shape[0]
    _, ne, d, de = wg.shape
    tile = pl.BlockSpec((FFN_TILE * TOK_ROWS, 128), lambda t, te: (t, 0))
    grid_spec = pltpu.PrefetchScalarGridSpec(
        num_scalar_prefetch=1,
        grid=(n_tiles,),
        in_specs=[tile,
                  pl.BlockSpec((None, None, d, de), lambda t, te: (layer, te[t], 0, 0)),
                  pl.BlockSpec((None, None, d, de), lambda t, te: (layer, te[t], 0, 0)),
                  pl.BlockSpec((None, None, de, d), lambda t, te: (layer, te[t], 0, 0))],
        out_specs=tile,
        scratch_shapes=[pltpu.VMEM((d, de), BF16), pltpu.VMEM((d, de), BF16), pltpu.VMEM((de, d), BF16)])
    return pl.pallas_call(
        _moe_experts_kernel,
        grid_spec=grid_spec,
        out_shape=jax.ShapeDtypeStruct(xs.shape, F32),
        compiler_params=_cparams("arbitrary"),
        name="moe_experts",
    )(tile_expert, xs, wg, wu, wd)


def _moe_combine_kernel(dest_ref, nxt_ref, x_ref, mod_ref, wcol_ref, gf_ref, y_hbm, o_ref, buf, sem, *, d, final):
    j = pl.program_id(0)
    last = pl.num_programs(0) - 1
    slot = lax.rem(j, 2)

    def gather(idx_ref, s, unrolled):
        for k in range(2):
            _gather_rows(idx_ref, k, y_hbm, buf.at[s, k], sem.at[s], ROW_TILE, unrolled)

    def wait(s):
        for k in range(2):
            _wait_rows(y_hbm, buf.at[s, k], sem.at[s], ROW_TILE)

    @pl.when(j == 0)
    def _():
        gather(dest_ref, 0, unrolled=False)

    gather(nxt_ref, 1 - slot, unrolled=True)
    wait(slot)
    w = wcol_ref[...]
    mix = (w[:, 0:1] * _load_token_tiles(buf.at[slot, 0], ROW_TILE)
           + w[:, 1:2] * _load_token_tiles(buf.at[slot, 1], ROW_TILE))
    g2 = mod_ref[...][:, 5 * d:6 * d]
    x_new = x_ref[...] + g2 * mix
    if final:
        x_new = x_new * lax.rsqrt(jnp.mean(x_new * x_new, axis=-1, keepdims=True) + EPS) * gf_ref[...]
    o_ref[...] = x_new

    @pl.when(j == last)
    def _():
        wait(1 - slot)


def _moe_combine(xa, mods3, wcol, y_tt, plan, n_ctx_tiles, final_g=None):
    bsz, t, d = xa.shape
    nt = t // ROW_TILE
    n_tiles = bsz * nt
    dest_tiles = plan[2]
    idx_spec = lambda step: pl.BlockSpec((None, 2, ROW_TILE), lambda j: (jnp.minimum(j + step, n_tiles - 1), 0, 0),
                                         memory_space=pltpu.SMEM)
    final = final_g is not None
    gf = (final_g if final else jnp.ones((d,), F32)).reshape(1, d)
    return pl.pallas_call(
        functools.partial(_moe_combine_kernel, d=d, final=final),
        grid=(n_tiles,),
        in_specs=[idx_spec(0), idx_spec(1),
                  pl.BlockSpec((None, ROW_TILE, d), lambda j: (j // nt, j % nt, 0)),
                  pl.BlockSpec((None, 1, 6 * d), lambda j: (jnp.where(j % nt < n_ctx_tiles, bsz, j // nt), 0, 0)),
                  pl.BlockSpec((ROW_TILE, GATE_LANES), lambda j: (j, 0)),
                  pl.BlockSpec((1, d), lambda j: (0, 0)),
                  pl.BlockSpec(memory_space=pl.ANY)],
        out_specs=pl.BlockSpec((None, ROW_TILE, d), lambda j: (j // nt, j % nt, 0)),
        out_shape=jax.ShapeDtypeStruct((bsz, t, d), F32),
        scratch_shapes=[pltpu.VMEM((2, 2, ROW_TILE * TOK_ROWS, 128), F32), pltpu.SemaphoreType.DMA((2,))],
        compiler_params=_cparams("arbitrary"),
        name="moe_combine",
    )(dest_tiles, dest_tiles, xa, mods3, wcol, gf, y_tt)


def _moe_sparse(xa, mods3, g, rw_t, rb, wg, wu, wd, layer, n_ctx_tiles, final_g=None):
    bsz, t, _ = xa.shape
    h_tt, ids, wcol = _moe_router(xa, mods3, g, rw_t, rb, n_ctx_tiles)
    plan = _moe_plan(ids, bsz * t)
    y_tt = _moe_experts(_moe_dispatch(h_tt, plan[0]), plan[1], wg, wu, wd, layer)
    return _moe_combine(xa, mods3, wcol, y_tt, plan, n_ctx_tiles, final_g)


def _final_norm_kernel(x_ref, g_ref, o_ref):
    x = x_ref[...]
    o_ref[...] = x * lax.rsqrt(jnp.mean(x * x, axis=-1, keepdims=True) + EPS) * g_ref[...]


def _final_norm(x, g):
    bsz, t, d = x.shape
    return pl.pallas_call(
        _final_norm_kernel,
        grid=(bsz, t // ROW_TILE),
        in_specs=[pl.BlockSpec((None, ROW_TILE, d), lambda b, i: (b, i, 0)), pl.BlockSpec((1, d), lambda b, i: (0, 0))],
        out_specs=pl.BlockSpec((None, ROW_TILE, d), lambda b, i: (b, i, 0)),
        out_shape=jax.ShapeDtypeStruct((bsz, t, d), F32),
        compiler_params=_cparams("parallel", "parallel"),
        name="final_norm",
    )(x, g.reshape(1, d))


def kernel(x, c, ctx, c_ctx, w_mod, b_mod, norm_mix, norm_ffn, w_in, gla_a2, gla_ab, mlstm_gate_b, gdn_conv, gdn_a_log, gdn_dt_bias, diff_lambda, head_norm, w_branch, w_out, router_w, router_b, w_gate, w_up, w_down, norm_final):
    depth = w_in.shape[0]
    st = _prepare(x, c, ctx, c_ctx, w_mod, b_mod)
    n_ctx_tiles = st["n_ctx_tiles"]
    tables = _rope_tables(x.shape[1], st["ct"])
    rw_t = router_w.T
    rb = jnp.broadcast_to(router_b.astype(F32)[:, None], (N_EXPERTS, ROW_TILE))
    for l in range(depth):
        with_ctx = l < depth - 1
        lam_init = 0.8 - 0.6 * math.exp(-0.3 * l)
        mods3 = st["mods"][l][:, None, :]
        proj = _layer_inproj(st, l, norm_mix, w_in)
        scans = (*_gla_branch(proj, gla_a2[l], gla_ab[l]),
                 *_mlstm_branch(proj, mlstm_gate_b[l]),
                 *_gdn_branch(proj, gdn_conv[l], gdn_a_log[l], gdn_dt_bias[l], n_ctx_tiles))
        attn = _diff_branch(proj, tables, diff_lambda[l], lam_init, n_ctx_tiles, with_ctx)
        hn = jnp.zeros((8, BRANCH_W), F32).at[:N_BRANCH].set(head_norm[l])
        xa = _merge(st["xa"], mods3, proj, scans, attn, hn, w_branch[l].astype(BF16), w_out[l].astype(BF16),
                    n_ctx_tiles, with_ctx, lam_init)
        if not with_ctx:
            n_ctx_tiles = 0
        xa = _moe_sparse(xa, mods3, norm_ffn[l], rw_t, rb, w_gate, w_up, w_down, l, n_ctx_tiles,
                         final_g=None if with_ctx else norm_final)
        st = dict(st, xa=xa, n_ctx_tiles=n_ctx_tiles)
    return st["xa"]
```

```python
import functools
import math

import numpy as np
import jax
import jax.numpy as jnp
from jax import lax
from jax.experimental import pallas as pl
from jax.experimental.pallas import tpu as pltpu

N_HEADS = 4
HEAD_DIM = 64
BRANCH_W = N_HEADS * HEAD_DIM
CHUNK = 64
GLA_RANK = 16
GLA_NORMALIZER = 16.0
CONV_W = 5
DIFF_QK = HEAD_DIM // 2
ROPE_AXIS = DIFF_QK // 2
ROPE_BASE = 10000.0
GRID_W = 64
N_EXPERTS = 16
N_GROUPS = 4
EXPERTS_PER_GROUP = 4
EPS = 1e-6
N_BRANCH = 4

ROW_TILE = 256
PROJ_COLS = 8192
BLK_GDN_QKV, BLK_DIFF_QKV, BLK_GDN_G = 0, 3, 6
BLK_GLA_Q, BLK_GLA_K, BLK_GLA_V, BLK_GLA_G = 7, 8, 9, 10
BLK_ML_Q, BLK_ML_K, BLK_ML_V, BLK_ML_G = 11, 28, 29, 30
BLK_GATES = 12
SMALL_BLOCK = 31
SM_GLA_RF, SM_GLA_RB, SM_ML_F, SM_ML_B, SM_GD_F, SM_GD_B = 0, 16, 32, 40, 48, 56
NEG_BIG = -1e30
VMEM_LIMIT = 56 * 1024 * 1024

F32 = jnp.float32
BF16 = jnp.bfloat16


def _cparams(*sem):
    return pltpu.CompilerParams(dimension_semantics=sem, vmem_limit_bytes=VMEM_LIMIT)


def _dot(a, b):
    return jnp.dot(a, b, preferred_element_type=F32)


def _dot_nt(a, b):
    return lax.dot_general(a, b, (((1,), (1,)), ((), ())), preferred_element_type=F32)


def _dot_tn(a, b):
    return lax.dot_general(a, b, (((0,), (0,)), ((), ())), preferred_element_type=F32)


def _split_hl(x):
    hi = x.astype(BF16)
    lo = (x - hi.astype(F32)).astype(BF16)
    return hi, lo


def _dot_hl(a, x):
    hi, lo = _split_hl(x)
    return _dot(a, hi) + _dot(a, lo)


def _dot_hl_r(x, a):
    hi, lo = _split_hl(x)
    return _dot(hi, a) + _dot(lo, a)


def _iota(shape, dim):
    return lax.broadcasted_iota(jnp.int32, shape, dim)


def _softplus(x):
    return jnp.maximum(x, 0.0) + jnp.log(1.0 + jnp.exp(-jnp.abs(x)))


def _log_sigmoid(x):
    return -_softplus(-x)


def _silu(x):
    return x * jax.nn.sigmoid(x)


def _lane_head(shape):
    return _iota(shape, len(shape) - 1) // HEAD_DIM


def _stack4(x):
    lh = _lane_head(x.shape)
    zero = jnp.zeros_like(x)
    return jnp.concatenate([jnp.where(lh == h, x, zero) for h in range(N_HEADS)], axis=0)


def _blk_mask():
    r = _iota((BRANCH_W, BRANCH_W), 0) // HEAD_DIM
    c = _iota((BRANCH_W, BRANCH_W), 1) // HEAD_DIM
    return r == c


def _ones_blk():
    return jnp.where(_blk_mask(), 1.0, 0.0).astype(BF16)


def _tri_hs(reverse, strict):
    t = _iota((CHUNK, BRANCH_W), 0)
    s = _iota((CHUNK, BRANCH_W), 1) % CHUNK
    if reverse:
        return (s > t) if strict else (s >= t)
    return (s < t) if strict else (s <= t)


def _eye_hs():
    t = _iota((CHUNK, BRANCH_W), 0)
    s = _iota((CHUNK, BRANCH_W), 1) % CHUNK
    return t == s


def _cum_mat(reverse):
    t = _iota((CHUNK, CHUNK), 0)
    s = _iota((CHUNK, CHUNK), 1)
    m = (s >= t) if reverse else (s <= t)
    return jnp.where(m, 1.0, 0.0).astype(BF16)


def _row_of(col_rep):
    return jnp.sum(jnp.where(_eye_hs(), col_rep, 0.0), axis=0, keepdims=True)


def _seg_max(x):
    lane = _iota(x.shape, 1)
    n = x.shape[1]
    for sh in (1, 2, 4, 8, 16, 32):
        up = pltpu.roll(x, n - sh, 1)
        dn = pltpu.roll(x, sh, 1)
        x = jnp.maximum(x, jnp.where((lane & sh) == 0, up, dn))
    return x


def _seg_sum(x, ones_blk):
    return _dot_hl_r(x, ones_blk)


def _mod_kernel(c_ref, w_ref, b_ref, o_ref):
    a = _silu(c_ref[...]).astype(BF16)
    o_ref[...] = _dot(a, w_ref[...].astype(BF16)) + b_ref[...]


def _modulation(cond, w_mod, b_mod):
    depth, d, n = w_mod.shape
    r = cond.shape[0]
    tn = 1536
    return pl.pallas_call(
        _mod_kernel,
        grid=(depth, n // tn),
        in_specs=[pl.BlockSpec((r, d), lambda l, j: (0, 0)),
                  pl.BlockSpec((None, d, tn), lambda l, j: (l, 0, j)),
                  pl.BlockSpec((None, 1, tn), lambda l, j: (l, 0, j))],
        out_specs=pl.BlockSpec((None, r, tn), lambda l, j: (l, 0, j)),
        out_shape=jax.ShapeDtypeStruct((depth, r, n), F32),
        compiler_params=_cparams("parallel", "parallel"),
        name="modulation",
    )(cond, w_mod, b_mod.reshape(depth, 1, n))


def _rms_mod(x, g, shift, scale):
    y = x * lax.rsqrt(jnp.mean(x * x, axis=-1, keepdims=True) + EPS)
    return (y * g) * (1.0 + scale) + shift


def _inproj_kernel(x_ref, mod_ref, g_ref, w_ref, o_ref, *, d, n_chunk):
    mod = mod_ref[...]
    h = _rms_mod(x_ref[...], g_ref[...], mod[:, 0:d], mod[:, d:2 * d]).astype(BF16)
    for n0 in range(0, PROJ_COLS, n_chunk):
        o_ref[:, n0:n0 + n_chunk] = _dot(h, w_ref[:, n0:n0 + n_chunk]).astype(BF16)


def _mod_row_map(n_ctx_tiles, n_batch):
    return lambda b, i: (jnp.where(i < n_ctx_tiles, n_batch, b), 0, 0)


def _inproj(xa, mods3, g, w_perm, n_ctx_tiles):
    bsz, t, d = xa.shape
    return pl.pallas_call(
        functools.partial(_inproj_kernel, d=d, n_chunk=1024),
        grid=(bsz, t // ROW_TILE),
        in_specs=[pl.BlockSpec((None, ROW_TILE, d), lambda b, i: (b, i, 0)),
                  pl.BlockSpec((None, 1, 6 * d), _mod_row_map(n_ctx_tiles, bsz)),
                  pl.BlockSpec((1, d), lambda b, i: (0, 0)),
                  pl.BlockSpec((d, PROJ_COLS), lambda b, i: (0, 0), pipeline_mode=pl.Buffered(1))],
        out_specs=pl.BlockSpec((None, ROW_TILE, PROJ_COLS), lambda b, i: (b, i, 0)),
        out_shape=jax.ShapeDtypeStruct((bsz, t, PROJ_COLS), BF16),
        compiler_params=_cparams("parallel", "parallel"),
        name="inproj",
    )(xa, mods3, g.reshape(1, d), w_perm)


def _proj_perm():
    sizes = (256, 256, 256, 256, 16, 16, 256, 256, 256, 256, 8, 8, 256, 256, 256, 256, 8, 8, 256, 256, 256, 4096)
    off = np.concatenate([[0], np.cumsum(sizes)])
    seg = lambda i: np.arange(off[i], off[i + 1])
    wide = {BLK_GDN_QKV: 12, BLK_GDN_QKV + 1: 13, BLK_GDN_QKV + 2: 14, BLK_GDN_G: 15,
            BLK_DIFF_QKV: 18, BLK_DIFF_QKV + 1: 19, BLK_DIFF_QKV + 2: 20,
            BLK_GLA_Q: 0, BLK_GLA_K: 1, BLK_GLA_V: 2, BLK_GLA_G: 3,
            BLK_ML_Q: 6, BLK_ML_K: 7, BLK_ML_V: 8, BLK_ML_G: 9}
    perm = np.full((PROJ_COLS,), -1, np.int64)
    for j, i in wide.items():
        perm[j * BRANCH_W:(j + 1) * BRANCH_W] = seg(i)
    perm[BLK_GATES * BRANCH_W:BLK_GATES * BRANCH_W + 4096] = seg(21)
    base = SMALL_BLOCK * BRANCH_W
    for lane0, i in ((SM_GLA_RF, 4), (SM_GLA_RB, 5), (SM_ML_F, 10), (SM_ML_B, 11), (SM_GD_F, 16), (SM_GD_B, 17)):
        s = seg(i)
        perm[base + lane0:base + lane0 + len(s)] = s
    return perm


def _permute_w_in(w_in):
    perm = _proj_perm()
    cuts = [0] + [i for i in range(1, PROJ_COLS)
                  if (perm[i] < 0) != (perm[i - 1] < 0) or (perm[i] >= 0 and perm[i] != perm[i - 1] + 1)] + [PROJ_COLS]
    w = w_in.astype(BF16)
    runs = [jnp.zeros((w.shape[0], b - a), BF16) if perm[a] < 0 else w[:, perm[a]:perm[a] + b - a]
            for a, b in zip(cuts[:-1], cuts[1:])]
    return jnp.concatenate(runs, axis=-1)


SCAN_CHUNKS = ROW_TILE // CHUNK


def _scan_block_map(col, nb, reverse):
    if reverse:
        return lambda j: (0, jnp.where(j == 0, 0, nb - j), col)
    return lambda j: (0, j, col)


def _scan_steps(reverse, bsz):
    order = range(SCAN_CHUNKS - 1, -1, -1) if reverse else range(SCAN_CHUNKS)
    return [(b, slice(c * CHUNK, (c + 1) * CHUNK)) for c in order for b in range(bsz)]


def _scan_call(kernel_fn, name, arrays, cols, consts, scratch, reverse):
    bsz, t, _ = arrays[0].shape
    nb = t // ROW_TILE
    blk = lambda col: pl.BlockSpec((bsz, ROW_TILE, BRANCH_W), _scan_block_map(col, nb, reverse))
    const = lambda a: pl.BlockSpec(a.shape, lambda j: (0,) * a.ndim)
    return pl.pallas_call(
        functools.partial(kernel_fn, reverse=reverse, bsz=bsz),
        grid=(nb,),
        in_specs=[blk(c) for c in cols] + [const(a) for a in consts],
        out_specs=blk(0),
        out_shape=jax.ShapeDtypeStruct((bsz, t, BRANCH_W), BF16),
        scratch_shapes=[pltpu.VMEM((bsz,) + s, F32) for s in scratch],
        compiler_params=_cparams("arbitrary"),
        name=name + ("_bwd" if reverse else "_fwd"),
    )(*arrays, *consts)


def _expand_small(small, lane0, count):
    src = _iota((BRANCH_W, BRANCH_W), 0)
    dst_head = _iota((BRANCH_W, BRANCH_W), 1) // HEAD_DIM
    e = jnp.where(src == lane0 + dst_head, 1.0, 0.0).astype(BF16)
    return _dot(small, e)


def _gla_kernel(q_ref, k_ref, v_ref, sm_ref, a2_ref, ab_ref, o_ref, st_ref, *, reverse, bsz):
    @pl.when(pl.program_id(0) == 0)
    def _():
        st_ref[...] = jnp.zeros_like(st_ref)

    cum_mat = _cum_mat(reverse)
    tri = _tri_hs(reverse, strict=False)
    blk = _blk_mask()
    a2 = a2_ref[...]
    ab = ab_ref[...]
    steps = _scan_steps(reverse, bsz)
    la = [_log_sigmoid(_dot(sm_ref[b, rows, :], a2) + ab) / GLA_NORMALIZER for b, rows in steps]
    cum = [_dot_hl(cum_mat, x) for x in la]
    tot = [jnp.sum(x, axis=0, keepdims=True) for x in la]
    q_in = [(q_ref[b, rows, :].astype(F32) * (HEAD_DIM ** -0.5) * jnp.exp(c)).astype(BF16)
            for (b, rows), c in zip(steps, cum)]
    k_out = [(k_ref[b, rows, :].astype(F32) * jnp.exp(-c)).astype(BF16) for (b, rows), c in zip(steps, cum)]
    k_end = [(k_ref[b, rows, :].astype(F32) * jnp.exp(t - c)).astype(BF16)
             for (b, rows), c, t in zip(steps, cum, tot)]
    att = [jnp.where(tri, _dot_nt(qi, _stack4(ko)), 0.0).astype(BF16) for qi, ko in zip(q_in, k_out)]
    o_intra = [_dot(a, _stack4(v_ref[b, rows, :])) for (b, rows), a in zip(steps, att)]
    kv_t = [jnp.where(blk, _dot_tn(v_ref[b, rows, :], ke), 0.0) for (b, rows), ke in zip(steps, k_end)]
    for i, (b, rows) in enumerate(steps):
        st = st_ref[b]
        o_ref[b, rows, :] = (o_intra[i] + _dot_nt(q_in[i], st.astype(BF16))).astype(o_ref.dtype)
        st_ref[b] = st * jnp.exp(tot[i]) + kv_t[i]


def _gla_scan(proj, a2pad, ab, reverse):
    return _scan_call(_gla_kernel, "gla", [proj] * 4, [BLK_GLA_Q, BLK_GLA_K, BLK_GLA_V, SMALL_BLOCK],
                      [a2pad, ab], [(BRANCH_W, BRANCH_W)], reverse)


def _gla_params(a2, ab, d):
    lane0 = SM_GLA_RB if d else SM_GLA_RF
    pad = jnp.zeros((BRANCH_W, BRANCH_W), F32).at[lane0:lane0 + GLA_RANK].set(a2[d])
    return pad.astype(BF16), ab[d].reshape(1, BRANCH_W)


def _gla_branch(proj, a2, ab):
    return tuple(_gla_scan(proj, *_gla_params(a2, ab, d), reverse=bool(d)) for d in (0, 1))


def _mlstm_kernel(q_ref, k_ref, v_ref, sm_ref, bias_ref, o_ref, ct_ref, nm_ref, *, reverse, bsz):
    @pl.when(pl.program_id(0) == 0)
    def _():
        ct_ref[...] = jnp.zeros_like(ct_ref)
        nm_ref[...] = jnp.zeros_like(nm_ref)

    lane0 = SM_ML_B if reverse else SM_ML_F
    cum_mat = _cum_mat(reverse)
    tri = _tri_hs(reverse, strict=False)
    blk = _blk_mask()
    ones_blk = _ones_blk()
    b_i = bias_ref[0:1, :]
    b_f = bias_ref[1:2, :]
    steps = _scan_steps(reverse, bsz)
    each = lambda f, *ls: [f(*a) for a in zip(*ls)]
    sm = [sm_ref[b, rows, :] for b, rows in steps]
    q = [q_ref[b, rows, :] for b, rows in steps]
    i_pre = [_expand_small(x, lane0, N_HEADS) + b_i for x in sm]
    log_f = [_log_sigmoid(_expand_small(x, lane0 + N_HEADS, N_HEADS) + b_f) for x in sm]
    f_cum = [_dot_hl(cum_mat, x) for x in log_f]
    f_tot = [jnp.sum(x, axis=0, keepdims=True) for x in log_f]
    u = each(lambda i, f: i - f, i_pre, f_cum)
    lw = each(lambda t, x: t + x, f_tot, u)
    a_end = [jnp.max(x, axis=0, keepdims=True) for x in lw]
    k = [k_ref[b, rows, :].astype(F32) * (HEAD_DIM ** -0.5) for b, rows in steps]
    kw = each(lambda kk, x, a: kk * jnp.exp(x - a), k, lw, a_end)
    k_sum = [jnp.sum(x, axis=0, keepdims=True) for x in kw]
    kv_t = [jnp.where(blk, _dot_tn(v_ref[b, rows, :], x.astype(BF16)), 0.0) for (b, rows), x in zip(steps, kw)]
    log_d = each(lambda f, x: jnp.where(tri, f + _row_of(x), NEG_BIG), f_cum, u)
    mx = [_seg_max(x) for x in log_d]
    s = each(lambda qq, kk, ld, m: (_dot_nt(qq, _stack4(kk.astype(BF16))) * jnp.exp(ld - m)).astype(BF16),
             q, k, log_d, mx)
    num1 = [_dot(x, _stack4(v_ref[b, rows, :])) for (b, rows), x in zip(steps, s)]
    den1 = [_dot(x, ones_blk) for x in s]
    for i, (b, rows) in enumerate(steps):
        n_in = nm_ref[b, 0:1, :]
        m_in = nm_ref[b, 1:2, :]
        ct = ct_ref[b]
        g = f_cum[i] + m_in
        m_t = jnp.maximum(g, mx[i])
        e = jnp.exp(g - m_t)
        r = jnp.exp(mx[i] - m_t)
        num = r * num1[i] + e * _dot_nt(q[i], ct.astype(BF16))
        den = r * den1[i] + e * _seg_sum(q[i].astype(F32) * n_in, ones_blk)
        o_ref[b, rows, :] = (num / jnp.maximum(jnp.abs(den), jnp.exp(-m_t))).astype(o_ref.dtype)

        m_new = jnp.maximum(f_tot[i] + m_in, a_end[i])
        old = jnp.exp(f_tot[i] + m_in - m_new)
        new = jnp.exp(a_end[i] - m_new)
        ct_ref[b] = ct * old + kv_t[i] * new
        nm_ref[b, 0:1, :] = n_in * old + k_sum[i] * new
        nm_ref[b, 1:2, :] = m_new


def _mlstm_scan(proj, bias, reverse):
    return _scan_call(_mlstm_kernel, "mlstm", [proj] * 4, [BLK_ML_Q, BLK_ML_K, BLK_ML_V, SMALL_BLOCK],
                      [bias], [(BRANCH_W, BRANCH_W), (8, BRANCH_W)], reverse)


def _head_rows(vals):
    rows = [jnp.repeat(v.astype(F32), HEAD_DIM) for v in vals]
    rows += [jnp.zeros((BRANCH_W,), F32)] * (8 - len(rows))
    return jnp.stack(rows)


def _mlstm_branch(proj, gate_b):
    return tuple(_mlstm_scan(proj, _head_rows([gate_b[d, 0], gate_b[d, 1]]), reverse=bool(d)) for d in (0, 1))


HALO = 8
QKV_W = 3 * BRANCH_W


def _gdn_prep_kernel(prev_ref, cur_ref, next_ref, w_ref, o_ref, *, n_ctx_tiles, n_tiles):
    i = pl.program_id(1)
    has_prev = jnp.logical_and(i != 0, i != n_ctx_tiles).astype(F32)
    has_next = jnp.logical_and(i != n_ctx_tiles - 1, i != n_tiles - 1).astype(F32)
    padded = jnp.concatenate([prev_ref[...].astype(F32) * has_prev, cur_ref[...].astype(F32),
                              next_ref[...].astype(F32) * has_next], axis=0)
    w = w_ref[...]
    acc = jnp.zeros((ROW_TILE, QKV_W), F32)
    for j in range(CONV_W):
        off = HALO + j - CONV_W // 2
        acc = acc + padded[off:off + ROW_TILE, :] * w[j:j + 1, :]
    y = _silu(acc)
    ones_blk = _ones_blk()
    q = y[:, 0:BRANCH_W]
    k = y[:, BRANCH_W:2 * BRANCH_W]
    q = q * lax.rsqrt(_seg_sum(q * q, ones_blk) + EPS) * (HEAD_DIM ** -0.5)
    k = k * lax.rsqrt(_seg_sum(k * k, ones_blk) + EPS)
    o_ref[:, 0:BRANCH_W] = q.astype(o_ref.dtype)
    o_ref[:, BRANCH_W:2 * BRANCH_W] = k.astype(o_ref.dtype)
    o_ref[:, 2 * BRANCH_W:] = y[:, 2 * BRANCH_W:].astype(o_ref.dtype)


def _gdn_prep(proj, conv_w, n_ctx_tiles):
    bsz, t, _ = proj.shape
    nt = t // ROW_TILE
    per = ROW_TILE // HALO
    last = t // HALO - 1
    qkv_blk = BLK_GDN_QKV * BRANCH_W // QKV_W
    w8 = jnp.zeros((8, QKV_W), F32).at[:CONV_W].set(conv_w)
    return pl.pallas_call(
        functools.partial(_gdn_prep_kernel, n_ctx_tiles=n_ctx_tiles, n_tiles=nt),
        grid=(bsz, nt),
        in_specs=[pl.BlockSpec((None, HALO, QKV_W), lambda b, i: (b, jnp.maximum(i * per - 1, 0), qkv_blk)),
                  pl.BlockSpec((None, ROW_TILE, QKV_W), lambda b, i: (b, i, qkv_blk)),
                  pl.BlockSpec((None, HALO, QKV_W), lambda b, i: (b, jnp.minimum((i + 1) * per, last), qkv_blk)),
                  pl.BlockSpec((8, QKV_W), lambda b, i: (0, 0))],
        out_specs=pl.BlockSpec((None, ROW_TILE, QKV_W), lambda b, i: (b, i, 0)),
        out_shape=jax.ShapeDtypeStruct((bsz, t, QKV_W), BF16),
        compiler_params=_cparams("parallel", "parallel"),
        name="gdn_prep",
    )(proj, proj, proj, w8)


def _gdn_kernel(q_ref, k_ref, v_ref, sm_ref, par_ref, o_ref, s_ref, *, reverse, bsz):
    @pl.when(pl.program_id(0) == 0)
    def _():
        s_ref[...] = jnp.zeros_like(s_ref)

    lane0 = SM_GD_B if reverse else SM_GD_F
    cum_mat = _cum_mat(reverse)
    tri = _tri_hs(reverse, strict=False)
    tri_strict = _tri_hs(reverse, strict=True)
    blk = _blk_mask()
    eye = jnp.where(_eye_hs(), 1.0, 0.0)
    a_scale = jnp.exp(par_ref[0:1, :])
    dt_bias = par_ref[1:2, :]
    steps = _scan_steps(reverse, bsz)
    each = lambda f, *ls: [f(*a) for a in zip(*ls)]
    sm = [sm_ref[b, rows, :] for b, rows in steps]
    q = [q_ref[b, rows, :] for b, rows in steps]
    kb16 = [k_ref[b, rows, :] for b, rows in steps]
    beta = [jax.nn.sigmoid(_expand_small(x, lane0, N_HEADS)) for x in sm]
    g = [-a_scale * _softplus(_expand_small(x, lane0 + N_HEADS, N_HEADS) + dt_bias) for x in sm]
    cum = [_dot_hl(cum_mat, x) for x in g]
    tot = [jnp.sum(x, axis=0, keepdims=True) for x in g]
    gam = [jnp.where(tri, jnp.exp(jnp.where(tri, c - _row_of(c), 0.0)), 0.0) for c in cum]
    k_beta = each(lambda kk, bb: kk.astype(F32) * bb, kb16, beta)
    k4 = [_stack4(x) for x in kb16]
    a_hs = each(lambda kb, kk, gm: jnp.where(tri_strict, _dot_nt(kb.astype(BF16), kk) * gm, 0.0), k_beta, k4, gam)
    attn = each(lambda qq, kk, gm: (_dot_nt(qq, kk) * gm).astype(BF16), q, k4, gam)

    p = [-x for x in a_hs]
    t_inv = [eye + x for x in p]
    for _ in range(5):
        p = [_dot(x.astype(BF16), _stack4(x.astype(BF16))) for x in p]
        t_inv = each(lambda t, x: t + _dot(t.astype(BF16), _stack4(x.astype(BF16))), t_inv, p)
    w = each(lambda t, kb, c: _dot(t.astype(BF16), _stack4((kb * jnp.exp(c)).astype(BF16))).astype(BF16),
             t_inv, k_beta, cum)
    u = [_dot(t.astype(BF16), _stack4((v_ref[b, rows, :].astype(F32) * bb).astype(BF16)))
         for (b, rows), t, bb in zip(steps, t_inv, beta)]
    q_dec = each(lambda qq, c: (qq.astype(F32) * jnp.exp(c)).astype(BF16), q, cum)
    k_end = each(lambda kk, t, c: (kk.astype(F32) * jnp.exp(t - c)).astype(BF16), kb16, tot, cum)
    for i, (b, rows) in enumerate(steps):
        s = s_ref[b]
        sb = s.astype(BF16)
        v_new = (u[i] - _dot(w[i], sb)).astype(BF16)
        o_ref[b, rows, :] = (_dot(q_dec[i], sb) + _dot(attn[i], _stack4(v_new))).astype(o_ref.dtype)
        s_ref[b] = s * jnp.exp(tot[i]) + jnp.where(blk, _dot_tn(k_end[i], v_new), 0.0)


def _gdn_scan(gqkv, proj, par, reverse):
    return _scan_call(_gdn_kernel, "gdn", [gqkv, gqkv, gqkv, proj], [0, 1, 2, SMALL_BLOCK],
                      [par], [(BRANCH_W, BRANCH_W)], reverse)


def _gdn_branch(proj, conv_w, a_log, dt_bias, n_ctx_tiles):
    gqkv = _gdn_prep(proj, conv_w, n_ctx_tiles)
    return tuple(_gdn_scan(gqkv, proj, _head_rows([a_log[d], dt_bias[d]]), reverse=bool(d))
                 for d in (0, 1))


VX_W = 128
LOG2E = 1.4426950408889634


def _kv_tile(t):
    return next(k for k in (3 * ROW_TILE, 2 * ROW_TILE, ROW_TILE) if t % k == 0)


def _rope_tables(n_lat, n_ctx):
    pos = jnp.arange(n_lat)
    row, col = pos // GRID_W, pos % GRID_W
    inv = ROPE_BASE ** (-jnp.arange(0, ROPE_AXIS, 2, dtype=F32) / ROPE_AXIS)
    ang = jnp.concatenate([row.astype(F32)[:, None] * inv, col.astype(F32)[:, None] * inv], axis=-1)
    cos = jnp.concatenate([jnp.ones((n_ctx, ROPE_AXIS), F32), jnp.cos(ang)], axis=0)
    sin = jnp.concatenate([jnp.zeros((n_ctx, ROPE_AXIS), F32), jnp.sin(ang)], axis=0)
    reps = BRANCH_W // DIFF_QK
    cos_l = jnp.tile(jnp.concatenate([cos, cos], axis=-1), (1, reps))
    sin_l = jnp.tile(jnp.concatenate([-sin, sin], axis=-1), (1, reps))
    return cos_l, sin_l


def _rope_prep_kernel(x_ref, cos_ref, sin_ref, q_ref, kt_ref, vx_ref):
    lane = _iota((ROW_TILE, BRANCH_W), 1)
    first_half = (lane % DIFF_QK) < ROPE_AXIS
    cos = cos_ref[...]
    sin = sin_ref[...]

    def rope(x):
        partner = jnp.where(first_half, pltpu.roll(x, BRANCH_W - ROPE_AXIS, 1), pltpu.roll(x, ROPE_AXIS, 1))
        return x * cos + partner * sin

    q = rope(x_ref[:, 0:BRANCH_W].astype(F32)) * (DIFF_QK ** -0.5 * LOG2E)
    k = rope(x_ref[:, BRANCH_W:2 * BRANCH_W].astype(F32))
    q_ref[...] = q.astype(q_ref.dtype)
    kt_ref[...] = jnp.transpose(k).astype(kt_ref.dtype)
    v = x_ref[:, 2 * BRANCH_W:]
    src = _iota((BRANCH_W, VX_W), 0)
    dst = _iota((BRANCH_W, VX_W), 1)
    ones_col = jnp.where(_iota((ROW_TILE, VX_W), 1) == HEAD_DIM, 1.0, 0.0)
    for h in range(N_HEADS):
        sel = jnp.where(jnp.logical_and(src == h * HEAD_DIM + dst, dst < HEAD_DIM), 1.0, 0.0).astype(BF16)
        vx_ref[h] = (_dot(v, sel) + ones_col).astype(vx_ref.dtype)


def _rope_prep(proj, cos_l, sin_l):
    bsz, t, _ = proj.shape
    nt = t // ROW_TILE
    kvt = _kv_tile(t)
    r = kvt // ROW_TILE
    qkv_blk = BLK_DIFF_QKV * BRANCH_W // QKV_W
    tab = pl.BlockSpec((ROW_TILE, BRANCH_W), lambda b, i: (i, 0))
    return pl.pallas_call(
        _rope_prep_kernel,
        grid=(bsz, nt),
        in_specs=[pl.BlockSpec((None, ROW_TILE, QKV_W), lambda b, i: (b, i, qkv_blk)), tab, tab],
        out_specs=[pl.BlockSpec((None, ROW_TILE, BRANCH_W), lambda b, i: (b, i, 0)),
                   pl.BlockSpec((None, None, BRANCH_W, ROW_TILE), lambda b, i: (b, i // r, 0, i % r)),
                   pl.BlockSpec((None, None, N_HEADS, ROW_TILE, VX_W), lambda b, i: (b, i // r, 0, i % r, 0))],
        out_shape=[jax.ShapeDtypeStruct((bsz, t, BRANCH_W), BF16),
                   jax.ShapeDtypeStruct((bsz, t // kvt, BRANCH_W, kvt), BF16),
                   jax.ShapeDtypeStruct((bsz, t // kvt, N_HEADS, kvt, VX_W), BF16)],
        compiler_params=_cparams("parallel", "parallel"),
        name="rope_prep",
    )(proj, cos_l, sin_l)


N_CHAINS = 2 * N_HEADS


def _attn_kernel(q_ref, kt_ref, vx_ref, lp_ref, o_ref, qm_scr, s0_scr, s1_scr, m0_scr, m1_scr, al0_scr, al1_scr,
                 acc_scr, *, nk, lam_init):
    s_bufs, m_bufs, al_bufs = (s0_scr, s1_scr), (m0_scr, m1_scr), (al0_scr, al1_scr)
    q = q_ref[...]
    lane = _iota(q.shape, 1)
    zero = jnp.zeros_like(q)
    for c in range(N_CHAINS):
        lo = c * DIFF_QK
        qm_scr[c] = jnp.where(jnp.logical_and(lane >= lo, lane < lo + DIFF_QK), q, zero)
    acc_scr[...] = jnp.zeros_like(acc_scr)

    def stage_a(c, kt, slot, first=False):
        s = _dot(qm_scr[c], kt)
        s_bufs[slot][c] = s
        row_max = jnp.max(s, axis=1, keepdims=True)
        if first:
            m_bufs[slot][c] = row_max
            al_bufs[slot][c] = jnp.zeros_like(row_max)
        else:
            m_old = m_bufs[1 - slot][c]
            m_new = jnp.maximum(m_old, row_max)
            m_bufs[slot][c] = m_new
            al_bufs[slot][c] = jnp.exp2(m_old - m_new)

    def stage_b(c, j, slot):
        p = jnp.exp2(s_bufs[slot][c] - m_bufs[slot][c]).astype(BF16)
        acc_scr[c] = acc_scr[c] * al_bufs[slot][c] + _dot(p, vx_ref[j, c // 2])

    def step(ja, jb, slot_a, first=False):
        kt = None if ja is None else kt_ref[ja]
        for c in range(N_CHAINS):
            if ja is not None:
                stage_a(c, kt, slot_a, first)
            if jb is not None:
                stage_b(c, jb, 1 - slot_a)

    step(0, None, 0, first=True)

    def body(i, carry):
        j = 2 * i + 1
        step(j, j - 1, 1)
        step(j + 1, j, 0)
        return carry

    n_pairs = (nk - 1) // 2
    lax.fori_loop(0, n_pairs, body, 0)
    if nk % 2 == 0:
        step(nk - 1, nk - 2, 1)
    step(None, nk - 1, 1 - (nk - 1) % 2)

    lp = lp_ref[...]
    grp_src = _iota((VX_W, VX_W), 0) // DIFF_QK
    prod1 = lp[0:1, :] * lp[1:2, :]
    prod2 = lp[2:3, :] * lp[3:4, :]
    src = _iota((VX_W, BRANCH_W), 0)
    dst = _iota((VX_W, BRANCH_W), 1)
    out = jnp.zeros((ROW_TILE, BRANCH_W), F32)
    for h in range(N_HEADS):
        pick = jnp.where(grp_src == h, 1.0, 0.0).astype(BF16)
        lam = jnp.exp(_dot_hl_r(prod1, pick)) - jnp.exp(_dot_hl_r(prod2, pick)) + lam_init
        a0 = acc_scr[2 * h]
        a1 = acc_scr[2 * h + 1]
        o_h = a0 / a0[:, HEAD_DIM:HEAD_DIM + 1] - lam * (a1 / a1[:, HEAD_DIM:HEAD_DIM + 1])
        place = jnp.where(jnp.logical_and(dst == src + h * HEAD_DIM, src < HEAD_DIM), 1.0, 0.0).astype(BF16)
        out = out + _dot(o_h.astype(BF16), place)
    o_ref[...] = out.astype(o_ref.dtype)


def _diff_attention(q_rot, kt, vx, lam_p, lam_init, q_tile0, n_q_tiles, kv_len):
    bsz = q_rot.shape[0]
    n_kv_arr, _, kvt = kt.shape[1:]
    tk = min(kv_len, kvt)
    assert kv_len % tk == 0 and kvt % tk == 0
    nk = kv_len // tk
    lp = jnp.zeros((8, VX_W), F32).at[:4].set(lam_p.reshape(4, N_HEADS * DIFF_QK))
    return pl.pallas_call(
        functools.partial(_attn_kernel, nk=nk, lam_init=lam_init),
        grid=(bsz, n_q_tiles),
        in_specs=[pl.BlockSpec((None, ROW_TILE, BRANCH_W), lambda b, i: (b, i + q_tile0, 0)),
                  pl.BlockSpec((None, nk, BRANCH_W, tk), lambda b, i: (b, 0, 0, 0), pipeline_mode=pl.Buffered(1)),
                  pl.BlockSpec((None, nk, N_HEADS, tk, VX_W), lambda b, i: (b, 0, 0, 0, 0),
                               pipeline_mode=pl.Buffered(1)),
                  pl.BlockSpec((8, VX_W), lambda b, i: (0, 0))],
        out_specs=pl.BlockSpec((None, ROW_TILE, BRANCH_W), lambda b, i: (b, i, 0)),
        out_shape=jax.ShapeDtypeStruct((bsz, n_q_tiles * ROW_TILE, BRANCH_W), BF16),
        scratch_shapes=[pltpu.VMEM((N_CHAINS, ROW_TILE, BRANCH_W), BF16)]
                       + [pltpu.VMEM((N_CHAINS, ROW_TILE, tk), F32)] * 2
                       + [pltpu.VMEM((N_CHAINS, ROW_TILE, 1), F32)] * 4
                       + [pltpu.VMEM((N_CHAINS, ROW_TILE, VX_W), F32)],
        compiler_params=_cparams("parallel", "arbitrary"),
        name="diff_attention",
    )(q_rot, kt, vx, lp)


def _diff_branch(proj, tables, lam_p, lam_init, n_ctx_tiles, with_ctx):
    t = proj.shape[1]
    q_rot, kt, vx = _rope_prep(proj, *tables)
    lat = _diff_attention(q_rot, kt, vx, lam_p, lam_init, n_ctx_tiles, t // ROW_TILE - n_ctx_tiles, t)
    if not with_ctx:
        return lat
    ctx = _diff_attention(q_rot, kt, vx, lam_p, lam_init, 0, n_ctx_tiles, n_ctx_tiles * ROW_TILE)
    return jnp.concatenate([ctx, lat], axis=1)


def _prepare(x, c, ctx, c_ctx, w_mod, b_mod):
    bsz, _, d = x.shape
    ct = ctx.shape[1]
    assert ct % ROW_TILE == 0 and x.shape[1] % ROW_TILE == 0
    xa = jnp.concatenate([ctx, x], axis=1)
    rows = 8 * ((bsz + 1 + 7) // 8)
    cond = jnp.zeros((rows, d), F32).at[:bsz].set(c).at[bsz].set(c_ctx)
    mods = _modulation(cond, w_mod, b_mod)
    return dict(xa=xa, mods=mods, n_ctx_tiles=ct // ROW_TILE, ct=ct)


def _layer_inproj(st, l, norm_mix, w_in):
    mods3 = st["mods"][l][:, None, :]
    return _inproj(st["xa"], mods3, norm_mix[l], _permute_w_in(w_in[l]), st["n_ctx_tiles"])


def _merge_kernel(x_ref, mod_ref, glf_ref, glb_ref, mlf_ref, mlb_ref, gdf_ref, gdb_ref, at_ref,
                  g_gla_ref, g_ml_ref, g_gd_ref, mg0_ref, mg1_ref, mg2_ref, mg3_ref,
                  hn_ref, wb_ref, wo_ref, o_ref, *, d, lam_init):
    ones_blk = _ones_blk()

    def head_norm(o, i):
        ms = _seg_sum(o * o, ones_blk) * (1.0 / HEAD_DIM)
        return o * lax.rsqrt(ms + EPS) * hn_ref[i:i + 1, :]

    f32 = lambda r: r[...].astype(F32)
    ys = [head_norm(f32(glf_ref) + f32(glb_ref), 0) * _silu(f32(g_gla_ref)),
          head_norm(f32(mlf_ref) + f32(mlb_ref), 1) * jax.nn.sigmoid(f32(g_ml_ref)),
          head_norm(f32(gdf_ref) + f32(gdb_ref), 2) * _silu(f32(g_gd_ref)),
          head_norm(f32(at_ref), 3) * (1.0 - lam_init)]
    acc = None
    for i, (y, mg_ref) in enumerate(zip(ys, (mg0_ref, mg1_ref, mg2_ref, mg3_ref))):
        term = jax.nn.sigmoid(f32(mg_ref)) * _dot(y.astype(BF16), wb_ref[i])
        acc = term if acc is None else acc + term
    out = _dot(acc.astype(BF16), wo_ref[...])
    g1 = mod_ref[...][:, 2 * d:3 * d]
    o_ref[...] = x_ref[...] + g1 * out


def _merge(xa, mods3, proj, scans, attn, hn, wb, wo, n_ctx_tiles, with_ctx, lam_init):
    bsz, t, d = xa.shape
    off = 0 if with_ctx else n_ctx_tiles
    nt = t // ROW_TILE - off
    rows = lambda col: pl.BlockSpec((None, ROW_TILE, BRANCH_W), lambda b, i: (b, i + off, col))
    gate = lambda j: pl.BlockSpec((None, ROW_TILE, d), lambda b, i: (b, i + off, BLK_GATES * BRANCH_W // d + j))
    const = lambda shape: pl.BlockSpec(shape, lambda b, i: (0,) * len(shape))
    return pl.pallas_call(
        functools.partial(_merge_kernel, d=d, lam_init=lam_init),
        grid=(bsz, nt),
        in_specs=[pl.BlockSpec((None, ROW_TILE, d), lambda b, i: (b, i + off, 0)),
                  pl.BlockSpec((None, 1, 6 * d), lambda b, i: (jnp.where(i + off < n_ctx_tiles, bsz, b), 0, 0))]
                 + [rows(0)] * 6
                 + [pl.BlockSpec((None, ROW_TILE, BRANCH_W), lambda b, i: (b, i, 0))]
                 + [rows(BLK_GLA_G), rows(BLK_ML_G), rows(BLK_GDN_G)]
                 + [gate(j) for j in range(N_BRANCH)]
                 + [const((8, BRANCH_W)), const((N_BRANCH, BRANCH_W, d)), const((d, d))],
        out_specs=pl.BlockSpec((None, ROW_TILE, d), lambda b, i: (b, i, 0)),
        out_shape=jax.ShapeDtypeStruct((bsz, nt * ROW_TILE, d), F32),
        compiler_params=_cparams("parallel", "parallel"),
        name="merge",
    )(xa, mods3, *scans, attn, proj, proj, proj, proj, proj, proj, proj, hn, wb, wo)


GATE_LANES = 128


def _router_select(hf, rw_ref, rb_ref):
    h_hi, h_lo = _split_hl(hf)
    w_hi, w_lo = _split_hl(rw_ref[...])
    logits = _dot_nt(w_hi, h_hi) + _dot_nt(w_hi, h_lo) + _dot_nt(w_lo, h_hi)
    scores = jax.nn.sigmoid(logits)
    sel = scores + rb_ref[...]
    s = [sel[e:e + 1, :] for e in range(N_EXPERTS)]
    sc = [scores[e:e + 1, :] for e in range(N_EXPERTS)]
    grp = []
    for g in range(N_GROUPS):
        a, b, c, dd = s[4 * g:4 * g + 4]
        grp.append(functools.reduce(jnp.maximum, [a + b, a + c, a + dd, b + c, b + dd, c + dd]))
    gmax = functools.reduce(jnp.maximum, grp)
    chosen, taken = [], None
    for g in range(N_GROUPS):
        hit = grp[g] == gmax
        if taken is not None:
            hit = jnp.logical_and(hit, jnp.logical_not(taken))
        taken = hit if taken is None else jnp.logical_or(taken, hit)
        chosen.append(hit)
    ms = [jnp.where(chosen[e // EXPERTS_PER_GROUP], s[e], NEG_BIG) for e in range(N_EXPERTS)]

    def first_argmax(vals):
        top = functools.reduce(jnp.maximum, vals)
        hits, seen = [], None
        for v in vals:
            hit = v == top
            if seen is not None:
                hit = jnp.logical_and(hit, jnp.logical_not(seen))
            seen = hit if seen is None else jnp.logical_or(seen, hit)
            hits.append(hit)
        return hits

    oh1 = first_argmax(ms)
    oh2 = first_argmax([jnp.where(o, NEG_BIG, v) for o, v in zip(oh1, ms)])
    zero = jnp.zeros_like(sc[0])
    w1 = functools.reduce(jnp.add, [jnp.where(o, v, zero) for o, v in zip(oh1, sc)])
    w2 = functools.reduce(jnp.add, [jnp.where(o, v, zero) for o, v in zip(oh2, sc)])
    tot = w1 + w2
    ids = [functools.reduce(jnp.add, [jnp.where(o, float(e), 0.0) for e, o in enumerate(oh)]) for oh in (oh1, oh2)]
    return ids[0], ids[1], w1 / tot, w2 / tot


TOK_ROWS = 8
FFN_TILE = ROW_TILE


def _store_token_tiles(ref, val):
    n = val.shape[0]
    for s in range(TOK_ROWS):
        ref[pl.ds(s, n, stride=TOK_ROWS), :] = val[:, s * 128:(s + 1) * 128]


def _load_token_tiles(ref, n):
    return jnp.concatenate([ref[pl.ds(s, n, stride=TOK_ROWS), :] for s in range(TOK_ROWS)], axis=1)


def _moe_router_kernel(x_ref, mod_ref, g_ref, rw_ref, rb_ref, h_ref, ids_ref, wcol_ref, *, d):
    mod = mod_ref[...]
    hf = _rms_mod(x_ref[...], g_ref[...], mod[:, 3 * d:4 * d], mod[:, 4 * d:5 * d])
    _store_token_tiles(h_ref, hf)
    e1, e2, w1, w2 = _router_select(hf, rw_ref, rb_ref)
    pad = jnp.zeros((6, ROW_TILE), F32)
    ids_ref[...] = jnp.concatenate([e1, e2, pad], axis=0).astype(jnp.int32)
    eye = jnp.where(_iota((8, GATE_LANES), 0) == _iota((8, GATE_LANES), 1), 1.0, 0.0).astype(BF16)
    w_hi, w_lo = _split_hl(jnp.concatenate([w1, w2, pad], axis=0))
    wcol_ref[...] = _dot_tn(w_hi, eye) + _dot_tn(w_lo, eye)


def _moe_router(xa, mods3, g, rw_t, rb, n_ctx_tiles):
    bsz, t, d = xa.shape
    nt = t // ROW_TILE
    n = bsz * t
    flat = lambda b, i: b * nt + i
    return pl.pallas_call(
        functools.partial(_moe_router_kernel, d=d),
        grid=(bsz, nt),
        in_specs=[pl.BlockSpec((None, ROW_TILE, d), lambda b, i: (b, i, 0)),
                  pl.BlockSpec((None, 1, 6 * d), _mod_row_map(n_ctx_tiles, bsz)),
                  pl.BlockSpec((1, d), lambda b, i: (0, 0)),
                  pl.BlockSpec((N_EXPERTS, d), lambda b, i: (0, 0)),
                  pl.BlockSpec((N_EXPERTS, ROW_TILE), lambda b, i: (0, 0))],
        out_specs=[pl.BlockSpec((ROW_TILE * TOK_ROWS, 128), lambda b, i: (flat(b, i), 0)),
                   pl.BlockSpec((8, ROW_TILE), lambda b, i: (0, flat(b, i))),
                   pl.BlockSpec((ROW_TILE, GATE_LANES), lambda b, i: (flat(b, i), 0))],
        out_shape=[jax.ShapeDtypeStruct((n * TOK_ROWS, 128), F32),
                   jax.ShapeDtypeStruct((8, n), jnp.int32),
                   jax.ShapeDtypeStruct((n, GATE_LANES), F32)],
        compiler_params=_cparams("parallel", "parallel"),
        name="moe_router",
    )(xa, mods3, g.reshape(1, d), rw_t, rb)


def _moe_plan(ids, n):
    n_tiles = -(-2 * n // FFN_TILE) + N_EXPERTS
    eid = ids[:2].reshape(2 * n)
    order = jnp.argsort(eid, stable=True).astype(jnp.int32)
    inv = jnp.argsort(order).astype(jnp.int32)
    experts = jnp.arange(N_EXPERTS, dtype=jnp.int32)
    onehot = (eid[:, None] == experts[None, :]).astype(jnp.int32)
    counts = jnp.sum(onehot, axis=0)
    first = jnp.cumsum(counts) - counts
    padded = (counts + FFN_TILE - 1) // FFN_TILE * FFN_TILE
    ends = jnp.cumsum(padded)
    offs = ends - padded
    dest = inv + jnp.sum(onehot * (offs - first)[None, :], axis=1)
    tile_start = jnp.arange(n_tiles, dtype=jnp.int32) * FFN_TILE
    tile_expert = jnp.minimum(jnp.sum((ends[None, :] <= tile_start[:, None]).astype(jnp.int32), axis=1), N_EXPERTS - 1)
    tile_hot = (tile_expert[:, None] == experts[None, :]).astype(jnp.int32)
    lookup = lambda table: jnp.repeat(jnp.sum(tile_hot * table[None, :], axis=1), FFN_TILE)
    local = jnp.arange(n_tiles * FFN_TILE, dtype=jnp.int32) - lookup(offs)
    tok_sorted = order % n
    src = jnp.where(local < lookup(counts), tok_sorted[jnp.clip(lookup(first) + local, 0, 2 * n - 1)], 0)
    dest_tiles = dest.reshape(2, n // ROW_TILE, ROW_TILE).transpose(1, 0, 2)
    return src.reshape(n_tiles, 1, FFN_TILE), tile_expert, dest_tiles


def _gather_rows(idx_ref, k, src_hbm, dst, sem, n_rows, unrolled):
    def start(r, priority):
        row = idx_ref[k, r]
        pltpu.make_async_copy(src_hbm.at[pl.ds(pl.multiple_of(row * TOK_ROWS, TOK_ROWS), TOK_ROWS), :],
                              dst.at[pl.ds(pl.multiple_of(r * TOK_ROWS, TOK_ROWS), TOK_ROWS), :],
                              sem).start(priority=priority)

    if unrolled:
        for r in range(n_rows):
            start(r, r % 2)
    else:
        def body(r, carry):
            start(r, 0)
            return carry

        lax.fori_loop(0, n_rows, body, 0, unroll=8)


def _wait_rows(src_hbm, dst, sem, n_rows):
    pltpu.make_async_copy(src_hbm.at[pl.ds(0, n_rows * TOK_ROWS), :], dst, sem).wait()


def _moe_experts_kernel(te_ref, src_ref, nxt_ref, h_hbm, wg_ref, wu_ref, wd_ref, y_ref,
                        buf, wg_b, wu_b, wd_b, sem):
    t = pl.program_id(0)
    last = pl.num_programs(0) - 1
    slot = lax.rem(t, 2)

    @pl.when(jnp.logical_or(t == 0, te_ref[t] != te_ref[jnp.maximum(t - 1, 0)]))
    def _():
        wg_b[...] = wg_ref[...].astype(BF16)
        wu_b[...] = wu_ref[...].astype(BF16)
        wd_b[...] = wd_ref[...].astype(BF16)

    @pl.when(t == 0)
    def _():
        _gather_rows(src_ref, 0, h_hbm, buf.at[0], sem.at[0], FFN_TILE, unrolled=False)

    _gather_rows(nxt_ref, 0, h_hbm, buf.at[1 - slot], sem.at[1 - slot], FFN_TILE, unrolled=True)
    _wait_rows(h_hbm, buf.at[slot], sem.at[slot], FFN_TILE)
    h = _load_token_tiles(buf.at[slot], FFN_TILE).astype(BF16)
    act = (_silu(_dot(h, wg_b[...])) * _dot(h, wu_b[...])).astype(BF16)
    _store_token_tiles(y_ref, _dot(act, wd_b[...]))

    @pl.when(t == last)
    def _():
        _wait_rows(h_hbm, buf.at[1 - slot], sem.at[1 - slot], FFN_TILE)


def _moe_experts(h_tt, plan, wg, wu, wd, layer):
    src, tile_expert, _ = plan
    n_tiles = src.shape[0]
    _, ne, d, de = wg.shape
    idx_spec = lambda step: pl.BlockSpec((None, 1, FFN_TILE), lambda t, te: (jnp.minimum(t + step, n_tiles - 1), 0, 0),
                                         memory_space=pltpu.SMEM)
    grid_spec = pltpu.PrefetchScalarGridSpec(
        num_scalar_prefetch=1,
        grid=(n_tiles,),
        in_specs=[idx_spec(0), idx_spec(1),
                  pl.BlockSpec(memory_space=pl.ANY),
                  pl.BlockSpec((None, None, d, de), lambda t, te: (layer, te[t], 0, 0)),
                  pl.BlockSpec((None, None, d, de), lambda t, te: (layer, te[t], 0, 0)),
                  pl.BlockSpec((None, None, de, d), lambda t, te: (layer, te[t], 0, 0))],
        out_specs=pl.BlockSpec((FFN_TILE * TOK_ROWS, 128), lambda t, te: (t, 0)),
        scratch_shapes=[pltpu.VMEM((2, FFN_TILE * TOK_ROWS, 128), F32),
                        pltpu.VMEM((d, de), BF16), pltpu.VMEM((d, de), BF16), pltpu.VMEM((de, d), BF16),
                        pltpu.SemaphoreType.DMA((2,))])
    return pl.pallas_call(
        _moe_experts_kernel,
        grid_spec=grid_spec,
        out_shape=jax.ShapeDtypeStruct((n_tiles * FFN_TILE * TOK_ROWS, 128), F32),
        compiler_params=_cparams("arbitrary"),
        name="moe_experts",
    )(tile_expert, src, src, h_tt, wg, wu, wd)


def _moe_combine_kernel(dest_ref, nxt_ref, x_ref, mod_ref, wcol_ref, gf_ref, y_hbm, o_ref, buf, sem, *, d, final):
    j = pl.program_id(0)
    last = pl.num_programs(0) - 1
    slot = lax.rem(j, 2)

    def gather(idx_ref, s, unrolled):
        for k in range(2):
            _gather_rows(idx_ref, k, y_hbm, buf.at[s, k], sem.at[s], ROW_TILE, unrolled)

    def wait(s):
        for k in range(2):
            _wait_rows(y_hbm, buf.at[s, k], sem.at[s], ROW_TILE)

    @pl.when(j == 0)
    def _():
        gather(dest_ref, 0, unrolled=False)

    gather(nxt_ref, 1 - slot, unrolled=True)
    wait(slot)
    w = wcol_ref[...]
    mix = (w[:, 0:1] * _load_token_tiles(buf.at[slot, 0], ROW_TILE)
           + w[:, 1:2] * _load_token_tiles(buf.at[slot, 1], ROW_TILE))
    g2 = mod_ref[...][:, 5 * d:6 * d]
    x_new = x_ref[...] + g2 * mix
    if final:
        x_new = x_new * lax.rsqrt(jnp.mean(x_new * x_new, axis=-1, keepdims=True) + EPS) * gf_ref[...]
    o_ref[...] = x_new

    @pl.when(j == last)
    def _():
        wait(1 - slot)


def _moe_combine(xa, mods3, wcol, y_tt, plan, n_ctx_tiles, final_g=None):
    bsz, t, d = xa.shape
    nt = t // ROW_TILE
    n_tiles = bsz * nt
    dest_tiles = plan[2]
    idx_spec = lambda step: pl.BlockSpec((None, 2, ROW_TILE), lambda j: (jnp.minimum(j + step, n_tiles - 1), 0, 0),
                                         memory_space=pltpu.SMEM)
    final = final_g is not None
    gf = (final_g if final else jnp.ones((d,), F32)).reshape(1, d)
    return pl.pallas_call(
        functools.partial(_moe_combine_kernel, d=d, final=final),
        grid=(n_tiles,),
        in_specs=[idx_spec(0), idx_spec(1),
                  pl.BlockSpec((None, ROW_TILE, d), lambda j: (j // nt, j % nt, 0)),
                  pl.BlockSpec((None, 1, 6 * d), lambda j: (jnp.where(j % nt < n_ctx_tiles, bsz, j // nt), 0, 0)),
                  pl.BlockSpec((ROW_TILE, GATE_LANES), lambda j: (j, 0)),
                  pl.BlockSpec((1, d), lambda j: (0, 0)),
                  pl.BlockSpec(memory_space=pl.ANY)],
        out_specs=pl.BlockSpec((None, ROW_TILE, d), lambda j: (j // nt, j % nt, 0)),
        out_shape=jax.ShapeDtypeStruct((bsz, t, d), F32),
        scratch_shapes=[pltpu.VMEM((2, 2, ROW_TILE * TOK_ROWS, 128), F32), pltpu.SemaphoreType.DMA((2,))],
        compiler_params=_cparams("arbitrary"),
        name="moe_combine",
    )(dest_tiles, dest_tiles, xa, mods3, wcol, gf, y_tt)


def _moe_sparse(xa, mods3, g, rw_t, rb, wg, wu, wd, layer, n_ctx_tiles, final_g=None):
    bsz, t, _ = xa.shape
    h_tt, ids, wcol = _moe_router(xa, mods3, g, rw_t, rb, n_ctx_tiles)
    plan = _moe_plan(ids, bsz * t)
    y_tt = _moe_experts(h_tt, plan, wg, wu, wd, layer)
    return _moe_combine(xa, mods3, wcol, y_tt, plan, n_ctx_tiles, final_g)


def kernel(x, c, ctx, c_ctx, w_mod, b_mod, norm_mix, norm_ffn, w_in, gla_a2, gla_ab, mlstm_gate_b, gdn_conv, gdn_a_log, gdn_dt_bias, diff_lambda, head_norm, w_branch, w_out, router_w, router_b, w_gate, w_up, w_down, norm_final):
    depth = w_in.shape[0]
    st = _prepare(x, c, ctx, c_ctx, w_mod, b_mod)
    n_ctx_tiles = st["n_ctx_tiles"]
    tables = _rope_tables(x.shape[1], st["ct"])
    rw_t = router_w.T
    rb = jnp.broadcast_to(router_b.astype(F32)[:, None], (N_EXPERTS, ROW_TILE))
    for l in range(depth):
        with_ctx = l < depth - 1
        lam_init = 0.8 - 0.6 * math.exp(-0.3 * l)
        mods3 = st["mods"][l][:, None, :]
        proj = _layer_inproj(st, l, norm_mix, w_in)
        scans = (*_gla_branch(proj, gla_a2[l], gla_ab[l]),
                 *_mlstm_branch(proj, mlstm_gate_b[l]),
                 *_gdn_branch(proj, gdn_conv[l], gdn_a_log[l], gdn_dt_bias[l], n_ctx_tiles))
        attn = _diff_branch(proj, tables, diff_lambda[l], lam_init, n_ctx_tiles, with_ctx)
        hn = jnp.zeros((8, BRANCH_W), F32).at[:N_BRANCH].set(head_norm[l])
        xa = _merge(st["xa"], mods3, proj, scans, attn, hn, w_branch[l].astype(BF16), w_out[l].astype(BF16),
                    n_ctx_tiles, with_ctx, lam_init)
        if not with_ctx:
            n_ctx_tiles = 0
        xa = _moe_sparse(xa, mods3, norm_ffn[l], rw_t, rb, w_gate, w_up, w_down, l, n_ctx_tiles,
                         final_g=None if with_ctx else norm_final)
        st = dict(st, xa=xa, n_ctx_tiles=n_ctx_tiles)
    return st["xa"]
```

```python
import functools
import math

import numpy as np
import jax
import jax.numpy as jnp
from jax import lax
from jax.experimental import pallas as pl
from jax.experimental.pallas import tpu as pltpu

N_HEADS = 4
HEAD_DIM = 64
BRANCH_W = N_HEADS * HEAD_DIM
CHUNK = 64
GLA_RANK = 16
GLA_NORMALIZER = 16.0
CONV_W = 5
DIFF_QK = HEAD_DIM // 2
ROPE_AXIS = DIFF_QK // 2
ROPE_BASE = 10000.0
GRID_W = 64
N_EXPERTS = 16
N_GROUPS = 4
EXPERTS_PER_GROUP = 4
EPS = 1e-6
N_BRANCH = 4

ROW_TILE = 256
PROJ_COLS = 8192
BLK_GDN_QKV, BLK_DIFF_QKV, BLK_GDN_G = 0, 3, 6
BLK_GLA_Q, BLK_GLA_K, BLK_GLA_V, BLK_GLA_G = 7, 8, 9, 10
BLK_ML_Q, BLK_ML_K, BLK_ML_V, BLK_ML_G = 11, 28, 29, 30
BLK_GATES = 12
SMALL_BLOCK = 31
SM_GLA_RF, SM_GLA_RB, SM_ML_F, SM_ML_B, SM_GD_F, SM_GD_B = 0, 16, 32, 40, 48, 56
NEG_BIG = -1e30
VMEM_LIMIT = 56 * 1024 * 1024

F32 = jnp.float32
BF16 = jnp.bfloat16


def _cparams(*sem):
    return pltpu.CompilerParams(dimension_semantics=sem, vmem_limit_bytes=VMEM_LIMIT)


def _dot(a, b):
    return jnp.dot(a, b, preferred_element_type=F32)


def _dot_nt(a, b):
    return lax.dot_general(a, b, (((1,), (1,)), ((), ())), preferred_element_type=F32)


def _dot_tn(a, b):
    return lax.dot_general(a, b, (((0,), (0,)), ((), ())), preferred_element_type=F32)


def _split_hl(x):
    hi = x.astype(BF16)
    lo = (x - hi.astype(F32)).astype(BF16)
    return hi, lo


def _dot_hl(a, x):
    hi, lo = _split_hl(x)
    return _dot(a, hi) + _dot(a, lo)


def _dot_hl_r(x, a):
    hi, lo = _split_hl(x)
    return _dot(hi, a) + _dot(lo, a)


def _iota(shape, dim):
    return lax.broadcasted_iota(jnp.int32, shape, dim)


def _softplus(x):
    return jnp.maximum(x, 0.0) + jnp.log(1.0 + jnp.exp(-jnp.abs(x)))


def _log_sigmoid(x):
    return -_softplus(-x)


def _silu(x):
    return x * jax.nn.sigmoid(x)


def _lane_head(shape):
    return _iota(shape, len(shape) - 1) // HEAD_DIM


def _stack4(x):
    lh = _lane_head(x.shape)
    zero = jnp.zeros_like(x)
    return jnp.concatenate([jnp.where(lh == h, x, zero) for h in range(N_HEADS)], axis=0)


def _blk_mask():
    r = _iota((BRANCH_W, BRANCH_W), 0) // HEAD_DIM
    c = _iota((BRANCH_W, BRANCH_W), 1) // HEAD_DIM
    return r == c


def _ones_blk():
    return jnp.where(_blk_mask(), 1.0, 0.0).astype(BF16)


def _tri_hs(reverse, strict):
    t = _iota((CHUNK, BRANCH_W), 0)
    s = _iota((CHUNK, BRANCH_W), 1) % CHUNK
    if reverse:
        return (s > t) if strict else (s >= t)
    return (s < t) if strict else (s <= t)


def _eye_hs():
    t = _iota((CHUNK, BRANCH_W), 0)
    s = _iota((CHUNK, BRANCH_W), 1) % CHUNK
    return t == s


def _cum_mat(reverse):
    t = _iota((CHUNK, CHUNK), 0)
    s = _iota((CHUNK, CHUNK), 1)
    m = (s >= t) if reverse else (s <= t)
    return jnp.where(m, 1.0, 0.0).astype(BF16)


def _row_of(col_rep):
    return jnp.sum(jnp.where(_eye_hs(), col_rep, 0.0), axis=0, keepdims=True)


def _seg_max(x):
    lane = _iota(x.shape, 1)
    n = x.shape[1]
    for sh in (1, 2, 4, 8, 16, 32):
        up = pltpu.roll(x, n - sh, 1)
        dn = pltpu.roll(x, sh, 1)
        x = jnp.maximum(x, jnp.where((lane & sh) == 0, up, dn))
    return x


def _seg_sum(x, ones_blk):
    return _dot_hl_r(x, ones_blk)


def _mod_kernel(c_ref, w_ref, b_ref, o_ref):
    a = _silu(c_ref[...]).astype(BF16)
    o_ref[...] = _dot(a, w_ref[...].astype(BF16)) + b_ref[...]


def _modulation(cond, w_mod, b_mod):
    depth, d, n = w_mod.shape
    r = cond.shape[0]
    tn = 1536
    return pl.pallas_call(
        _mod_kernel,
        grid=(depth, n // tn),
        in_specs=[pl.BlockSpec((r, d), lambda l, j: (0, 0)),
                  pl.BlockSpec((None, d, tn), lambda l, j: (l, 0, j)),
                  pl.BlockSpec((None, 1, tn), lambda l, j: (l, 0, j))],
        out_specs=pl.BlockSpec((None, r, tn), lambda l, j: (l, 0, j)),
        out_shape=jax.ShapeDtypeStruct((depth, r, n), F32),
        compiler_params=_cparams("parallel", "parallel"),
        name="modulation",
    )(cond, w_mod, b_mod.reshape(depth, 1, n))


def _rms_mod(x, g, shift, scale):
    y = x * lax.rsqrt(jnp.mean(x * x, axis=-1, keepdims=True) + EPS)
    return (y * g) * (1.0 + scale) + shift


def _inproj_kernel(x_ref, mod_ref, g_ref, w_ref, o_ref, *, d, n_chunk):
    mod = mod_ref[...]
    h = _rms_mod(x_ref[...], g_ref[...], mod[:, 0:d], mod[:, d:2 * d]).astype(BF16)
    for n0 in range(0, PROJ_COLS, n_chunk):
        o_ref[:, n0:n0 + n_chunk] = _dot(h, w_ref[:, n0:n0 + n_chunk]).astype(BF16)


def _mod_row_map(n_ctx_tiles, n_batch):
    return lambda b, i: (jnp.where(i < n_ctx_tiles, n_batch, b), 0, 0)


def _inproj(xa, mods3, g, w_perm, n_ctx_tiles):
    bsz, t, d = xa.shape
    return pl.pallas_call(
        functools.partial(_inproj_kernel, d=d, n_chunk=1024),
        grid=(bsz, t // ROW_TILE),
        in_specs=[pl.BlockSpec((None, ROW_TILE, d), lambda b, i: (b, i, 0)),
                  pl.BlockSpec((None, 1, 6 * d), _mod_row_map(n_ctx_tiles, bsz)),
                  pl.BlockSpec((1, d), lambda b, i: (0, 0)),
                  pl.BlockSpec((d, PROJ_COLS), lambda b, i: (0, 0), pipeline_mode=pl.Buffered(1))],
        out_specs=pl.BlockSpec((None, ROW_TILE, PROJ_COLS), lambda b, i: (b, i, 0)),
        out_shape=jax.ShapeDtypeStruct((bsz, t, PROJ_COLS), BF16),
        compiler_params=_cparams("parallel", "parallel"),
        name="inproj",
    )(xa, mods3, g.reshape(1, d), w_perm)


def _proj_perm():
    sizes = (256, 256, 256, 256, 16, 16, 256, 256, 256, 256, 8, 8, 256, 256, 256, 256, 8, 8, 256, 256, 256, 4096)
    off = np.concatenate([[0], np.cumsum(sizes)])
    seg = lambda i: np.arange(off[i], off[i + 1])
    wide = {BLK_GDN_QKV: 12, BLK_GDN_QKV + 1: 13, BLK_GDN_QKV + 2: 14, BLK_GDN_G: 15,
            BLK_DIFF_QKV: 18, BLK_DIFF_QKV + 1: 19, BLK_DIFF_QKV + 2: 20,
            BLK_GLA_Q: 0, BLK_GLA_K: 1, BLK_GLA_V: 2, BLK_GLA_G: 3,
            BLK_ML_Q: 6, BLK_ML_K: 7, BLK_ML_V: 8, BLK_ML_G: 9}
    perm = np.full((PROJ_COLS,), -1, np.int64)
    for j, i in wide.items():
        perm[j * BRANCH_W:(j + 1) * BRANCH_W] = seg(i)
    perm[BLK_GATES * BRANCH_W:BLK_GATES * BRANCH_W + 4096] = seg(21)
    base = SMALL_BLOCK * BRANCH_W
    for lane0, i in ((SM_GLA_RF, 4), (SM_GLA_RB, 5), (SM_ML_F, 10), (SM_ML_B, 11), (SM_GD_F, 16), (SM_GD_B, 17)):
        s = seg(i)
        perm[base + lane0:base + lane0 + len(s)] = s
    return perm


def _permute_w_in(w_in):
    perm = _proj_perm()
    cuts = [0] + [i for i in range(1, PROJ_COLS)
                  if (perm[i] < 0) != (perm[i - 1] < 0) or (perm[i] >= 0 and perm[i] != perm[i - 1] + 1)] + [PROJ_COLS]
    w = w_in.astype(BF16)
    runs = [jnp.zeros((w.shape[0], b - a), BF16) if perm[a] < 0 else w[:, perm[a]:perm[a] + b - a]
            for a, b in zip(cuts[:-1], cuts[1:])]
    return jnp.concatenate(runs, axis=-1)


SCAN_CHUNKS = ROW_TILE // CHUNK


def _scan_block_map(col, nb, reverse):
    if reverse:
        return lambda j: (0, jnp.where(j == 0, 0, nb - j), col)
    return lambda j: (0, j, col)


def _scan_steps(reverse, bsz):
    order = range(SCAN_CHUNKS - 1, -1, -1) if reverse else range(SCAN_CHUNKS)
    return [(b, slice(c * CHUNK, (c + 1) * CHUNK)) for c in order for b in range(bsz)]


def _scan_call(kernel_fn, name, arrays, cols, consts, scratch, reverse):
    bsz, t, _ = arrays[0].shape
    nb = t // ROW_TILE
    blk = lambda col: pl.BlockSpec((bsz, ROW_TILE, BRANCH_W), _scan_block_map(col, nb, reverse))
    const = lambda a: pl.BlockSpec(a.shape, lambda j: (0,) * a.ndim)
    return pl.pallas_call(
        functools.partial(kernel_fn, reverse=reverse, bsz=bsz),
        grid=(nb,),
        in_specs=[blk(c) for c in cols] + [const(a) for a in consts],
        out_specs=blk(0),
        out_shape=jax.ShapeDtypeStruct((bsz, t, BRANCH_W), BF16),
        scratch_shapes=[pltpu.VMEM((bsz,) + s, F32) for s in scratch],
        compiler_params=_cparams("arbitrary"),
        name=name + ("_bwd" if reverse else "_fwd"),
    )(*arrays, *consts)


def _expand_small(small, lane0, count):
    src = _iota((BRANCH_W, BRANCH_W), 0)
    dst_head = _iota((BRANCH_W, BRANCH_W), 1) // HEAD_DIM
    e = jnp.where(src == lane0 + dst_head, 1.0, 0.0).astype(BF16)
    return _dot(small, e)


def _gla_kernel(q_ref, k_ref, v_ref, sm_ref, a2_ref, ab_ref, o_ref, st_ref, *, reverse, bsz):
    @pl.when(pl.program_id(0) == 0)
    def _():
        st_ref[...] = jnp.zeros_like(st_ref)

    cum_mat = _cum_mat(reverse)
    tri = _tri_hs(reverse, strict=False)
    blk = _blk_mask()
    a2 = a2_ref[...]
    ab = ab_ref[...]
    steps = _scan_steps(reverse, bsz)
    la = [_log_sigmoid(_dot(sm_ref[b, rows, :], a2) + ab) / GLA_NORMALIZER for b, rows in steps]
    cum = [_dot_hl(cum_mat, x) for x in la]
    tot = [jnp.sum(x, axis=0, keepdims=True) for x in la]
    q_in = [(q_ref[b, rows, :].astype(F32) * (HEAD_DIM ** -0.5) * jnp.exp(c)).astype(BF16)
            for (b, rows), c in zip(steps, cum)]
    k_out = [(k_ref[b, rows, :].astype(F32) * jnp.exp(-c)).astype(BF16) for (b, rows), c in zip(steps, cum)]
    k_end = [(k_ref[b, rows, :].astype(F32) * jnp.exp(t - c)).astype(BF16)
             for (b, rows), c, t in zip(steps, cum, tot)]
    att = [jnp.where(tri, _dot_nt(qi, _stack4(ko)), 0.0).astype(BF16) for qi, ko in zip(q_in, k_out)]
    o_intra = [_dot(a, _stack4(v_ref[b, rows, :])) for (b, rows), a in zip(steps, att)]
    kv_t = [jnp.where(blk, _dot_tn(v_ref[b, rows, :], ke), 0.0) for (b, rows), ke in zip(steps, k_end)]
    for i, (b, rows) in enumerate(steps):
        st = st_ref[b]
        o_ref[b, rows, :] = (o_intra[i] + _dot_nt(q_in[i], st.astype(BF16))).astype(o_ref.dtype)
        st_ref[b] = st * jnp.exp(tot[i]) + kv_t[i]


def _gla_scan(proj, a2pad, ab, reverse):
    return _scan_call(_gla_kernel, "gla", [proj] * 4, [BLK_GLA_Q, BLK_GLA_K, BLK_GLA_V, SMALL_BLOCK],
                      [a2pad, ab], [(BRANCH_W, BRANCH_W)], reverse)


def _gla_params(a2, ab, d):
    lane0 = SM_GLA_RB if d else SM_GLA_RF
    pad = jnp.zeros((BRANCH_W, BRANCH_W), F32).at[lane0:lane0 + GLA_RANK].set(a2[d])
    return pad.astype(BF16), ab[d].reshape(1, BRANCH_W)


def _gla_branch(proj, a2, ab):
    return tuple(_gla_scan(proj, *_gla_params(a2, ab, d), reverse=bool(d)) for d in (0, 1))


def _mlstm_kernel(q_ref, k_ref, v_ref, sm_ref, bias_ref, o_ref, ct_ref, nm_ref, *, reverse, bsz):
    @pl.when(pl.program_id(0) == 0)
    def _():
        ct_ref[...] = jnp.zeros_like(ct_ref)
        nm_ref[...] = jnp.zeros_like(nm_ref)

    lane0 = SM_ML_B if reverse else SM_ML_F
    cum_mat = _cum_mat(reverse)
    tri = _tri_hs(reverse, strict=False)
    blk = _blk_mask()
    ones_blk = _ones_blk()
    b_i = bias_ref[0:1, :]
    b_f = bias_ref[1:2, :]
    steps = _scan_steps(reverse, bsz)
    each = lambda f, *ls: [f(*a) for a in zip(*ls)]
    sm = [sm_ref[b, rows, :] for b, rows in steps]
    q = [q_ref[b, rows, :] for b, rows in steps]
    i_pre = [_expand_small(x, lane0, N_HEADS) + b_i for x in sm]
    log_f = [_log_sigmoid(_expand_small(x, lane0 + N_HEADS, N_HEADS) + b_f) for x in sm]
    f_cum = [_dot_hl(cum_mat, x) for x in log_f]
    f_tot = [jnp.sum(x, axis=0, keepdims=True) for x in log_f]
    u = each(lambda i, f: i - f, i_pre, f_cum)
    lw = each(lambda t, x: t + x, f_tot, u)
    a_end = [jnp.max(x, axis=0, keepdims=True) for x in lw]
    k = [k_ref[b, rows, :].astype(F32) * (HEAD_DIM ** -0.5) for b, rows in steps]
    kw = each(lambda kk, x, a: kk * jnp.exp(x - a), k, lw, a_end)
    k_sum = [jnp.sum(x, axis=0, keepdims=True) for x in kw]
    kv_t = [jnp.where(blk, _dot_tn(v_ref[b, rows, :], x.astype(BF16)), 0.0) for (b, rows), x in zip(steps, kw)]
    log_d = each(lambda f, x: jnp.where(tri, f + _row_of(x), NEG_BIG), f_cum, u)
    mx = [_seg_max(x) for x in log_d]
    s = each(lambda qq, kk, ld, m: (_dot_nt(qq, _stack4(kk.astype(BF16))) * jnp.exp(ld - m)).astype(BF16),
             q, k, log_d, mx)
    num1 = [_dot(x, _stack4(v_ref[b, rows, :])) for (b, rows), x in zip(steps, s)]
    den1 = [_dot(x, ones_blk) for x in s]
    for i, (b, rows) in enumerate(steps):
        n_in = nm_ref[b, 0:1, :]
        m_in = nm_ref[b, 1:2, :]
        ct = ct_ref[b]
        g = f_cum[i] + m_in
        m_t = jnp.maximum(g, mx[i])
        e = jnp.exp(g - m_t)
        r = jnp.exp(mx[i] - m_t)
        num = r * num1[i] + e * _dot_nt(q[i], ct.astype(BF16))
        den = r * den1[i] + e * _seg_sum(q[i].astype(F32) * n_in, ones_blk)
        o_ref[b, rows, :] = (num / jnp.maximum(jnp.abs(den), jnp.exp(-m_t))).astype(o_ref.dtype)

        m_new = jnp.maximum(f_tot[i] + m_in, a_end[i])
        old = jnp.exp(f_tot[i] + m_in - m_new)
        new = jnp.exp(a_end[i] - m_new)
        ct_ref[b] = ct * old + kv_t[i] * new
        nm_ref[b, 0:1, :] = n_in * old + k_sum[i] * new
        nm_ref[b, 1:2, :] = m_new


def _mlstm_scan(proj, bias, reverse):
    return _scan_call(_mlstm_kernel, "mlstm", [proj] * 4, [BLK_ML_Q, BLK_ML_K, BLK_ML_V, SMALL_BLOCK],
                      [bias], [(BRANCH_W, BRANCH_W), (8, BRANCH_W)], reverse)


def _head_rows(vals):
    rows = [jnp.repeat(v.astype(F32), HEAD_DIM) for v in vals]
    rows += [jnp.zeros((BRANCH_W,), F32)] * (8 - len(rows))
    return jnp.stack(rows)


def _mlstm_branch(proj, gate_b):
    return tuple(_mlstm_scan(proj, _head_rows([gate_b[d, 0], gate_b[d, 1]]), reverse=bool(d)) for d in (0, 1))


HALO = 8
QKV_W = 3 * BRANCH_W


def _gdn_prep_kernel(prev_ref, cur_ref, next_ref, w_ref, o_ref, *, n_ctx_tiles, n_tiles):
    i = pl.program_id(1)
    has_prev = jnp.logical_and(i != 0, i != n_ctx_tiles).astype(F32)
    has_next = jnp.logical_and(i != n_ctx_tiles - 1, i != n_tiles - 1).astype(F32)
    padded = jnp.concatenate([prev_ref[...].astype(F32) * has_prev, cur_ref[...].astype(F32),
                              next_ref[...].astype(F32) * has_next], axis=0)
    w = w_ref[...]
    acc = jnp.zeros((ROW_TILE, QKV_W), F32)
    for j in range(CONV_W):
        off = HALO + j - CONV_W // 2
        acc = acc + padded[off:off + ROW_TILE, :] * w[j:j + 1, :]
    y = _silu(acc)
    ones_blk = _ones_blk()
    q = y[:, 0:BRANCH_W]
    k = y[:, BRANCH_W:2 * BRANCH_W]
    q = q * lax.rsqrt(_seg_sum(q * q, ones_blk) + EPS) * (HEAD_DIM ** -0.5)
    k = k * lax.rsqrt(_seg_sum(k * k, ones_blk) + EPS)
    o_ref[:, 0:BRANCH_W] = q.astype(o_ref.dtype)
    o_ref[:, BRANCH_W:2 * BRANCH_W] = k.astype(o_ref.dtype)
    o_ref[:, 2 * BRANCH_W:] = y[:, 2 * BRANCH_W:].astype(o_ref.dtype)


def _gdn_prep(proj, conv_w, n_ctx_tiles):
    bsz, t, _ = proj.shape
    nt = t // ROW_TILE
    per = ROW_TILE // HALO
    last = t // HALO - 1
    qkv_blk = BLK_GDN_QKV * BRANCH_W // QKV_W
    w8 = jnp.zeros((8, QKV_W), F32).at[:CONV_W].set(conv_w)
    return pl.pallas_call(
        functools.partial(_gdn_prep_kernel, n_ctx_tiles=n_ctx_tiles, n_tiles=nt),
        grid=(bsz, nt),
        in_specs=[pl.BlockSpec((None, HALO, QKV_W), lambda b, i: (b, jnp.maximum(i * per - 1, 0), qkv_blk)),
                  pl.BlockSpec((None, ROW_TILE, QKV_W), lambda b, i: (b, i, qkv_blk)),
                  pl.BlockSpec((None, HALO, QKV_W), lambda b, i: (b, jnp.minimum((i + 1) * per, last), qkv_blk)),
                  pl.BlockSpec((8, QKV_W), lambda b, i: (0, 0))],
        out_specs=pl.BlockSpec((None, ROW_TILE, QKV_W), lambda b, i: (b, i, 0)),
        out_shape=jax.ShapeDtypeStruct((bsz, t, QKV_W), BF16),
        compiler_params=_cparams("parallel", "parallel"),
        name="gdn_prep",
    )(proj, proj, proj, w8)


def _gdn_kernel(q_ref, k_ref, v_ref, sm_ref, par_ref, o_ref, s_ref, *, reverse, bsz):
    @pl.when(pl.program_id(0) == 0)
    def _():
        s_ref[...] = jnp.zeros_like(s_ref)

    lane0 = SM_GD_B if reverse else SM_GD_F
    cum_mat = _cum_mat(reverse)
    tri = _tri_hs(reverse, strict=False)
    tri_strict = _tri_hs(reverse, strict=True)
    blk = _blk_mask()
    eye = jnp.where(_eye_hs(), 1.0, 0.0)
    a_scale = jnp.exp(par_ref[0:1, :])
    dt_bias = par_ref[1:2, :]
    steps = _scan_steps(reverse, bsz)
    each = lambda f, *ls: [f(*a) for a in zip(*ls)]
    sm = [sm_ref[b, rows, :] for b, rows in steps]
    q = [q_ref[b, rows, :] for b, rows in steps]
    kb16 = [k_ref[b, rows, :] for b, rows in steps]
    beta = [jax.nn.sigmoid(_expand_small(x, lane0, N_HEADS)) for x in sm]
    g = [-a_scale * _softplus(_expand_small(x, lane0 + N_HEADS, N_HEADS) + dt_bias) for x in sm]
    cum = [_dot_hl(cum_mat, x) for x in g]
    tot = [jnp.sum(x, axis=0, keepdims=True) for x in g]
    gam = [jnp.where(tri, jnp.exp(jnp.where(tri, c - _row_of(c), 0.0)), 0.0) for c in cum]
    k_beta = each(lambda kk, bb: kk.astype(F32) * bb, kb16, beta)
    k4 = [_stack4(x) for x in kb16]
    a_hs = each(lambda kb, kk, gm: jnp.where(tri_strict, _dot_nt(kb.astype(BF16), kk) * gm, 0.0), k_beta, k4, gam)
    attn = each(lambda qq, kk, gm: (_dot_nt(qq, kk) * gm).astype(BF16), q, k4, gam)

    p = [-x for x in a_hs]
    t_inv = [eye + x for x in p]
    for _ in range(5):
        p = [_dot(x.astype(BF16), _stack4(x.astype(BF16))) for x in p]
        t_inv = each(lambda t, x: t + _dot(t.astype(BF16), _stack4(x.astype(BF16))), t_inv, p)
    w = each(lambda t, kb, c: _dot(t.astype(BF16), _stack4((kb * jnp.exp(c)).astype(BF16))).astype(BF16),
             t_inv, k_beta, cum)
    u = [_dot(t.astype(BF16), _stack4((v_ref[b, rows, :].astype(F32) * bb).astype(BF16)))
         for (b, rows), t, bb in zip(steps, t_inv, beta)]
    q_dec = each(lambda qq, c: (qq.astype(F32) * jnp.exp(c)).astype(BF16), q, cum)
    k_end = each(lambda kk, t, c: (kk.astype(F32) * jnp.exp(t - c)).astype(BF16), kb16, tot, cum)
    for i, (b, rows) in enumerate(steps):
        s = s_ref[b]
        sb = s.astype(BF16)
        v_new = (u[i] - _dot(w[i], sb)).astype(BF16)
        o_ref[b, rows, :] = (_dot(q_dec[i], sb) + _dot(attn[i], _stack4(v_new))).astype(o_ref.dtype)
        s_ref[b] = s * jnp.exp(tot[i]) + jnp.where(blk, _dot_tn(k_end[i], v_new), 0.0)


def _gdn_scan(gqkv, proj, par, reverse):
    return _scan_call(_gdn_kernel, "gdn", [gqkv, gqkv, gqkv, proj], [0, 1, 2, SMALL_BLOCK],
                      [par], [(BRANCH_W, BRANCH_W)], reverse)


def _gdn_branch(proj, conv_w, a_log, dt_bias, n_ctx_tiles):
    gqkv = _gdn_prep(proj, conv_w, n_ctx_tiles)
    return tuple(_gdn_scan(gqkv, proj, _head_rows([a_log[d], dt_bias[d]]), reverse=bool(d))
                 for d in (0, 1))


VX_W = 128
LOG2E = 1.4426950408889634


def _kv_tile(t):
    return next(k for k in (3 * ROW_TILE, 2 * ROW_TILE, ROW_TILE) if t % k == 0)


def _rope_tables(n_lat, n_ctx):
    pos = jnp.arange(n_lat)
    row, col = pos // GRID_W, pos % GRID_W
    inv = ROPE_BASE ** (-jnp.arange(0, ROPE_AXIS, 2, dtype=F32) / ROPE_AXIS)
    ang = jnp.concatenate([row.astype(F32)[:, None] * inv, col.astype(F32)[:, None] * inv], axis=-1)
    cos = jnp.concatenate([jnp.ones((n_ctx, ROPE_AXIS), F32), jnp.cos(ang)], axis=0)
    sin = jnp.concatenate([jnp.zeros((n_ctx, ROPE_AXIS), F32), jnp.sin(ang)], axis=0)
    reps = BRANCH_W // DIFF_QK
    cos_l = jnp.tile(jnp.concatenate([cos, cos], axis=-1), (1, reps))
    sin_l = jnp.tile(jnp.concatenate([-sin, sin], axis=-1), (1, reps))
    return cos_l, sin_l


def _rope_prep_kernel(x_ref, cos_ref, sin_ref, q_ref, kt_ref, vx_ref):
    lane = _iota((ROW_TILE, BRANCH_W), 1)
    first_half = (lane % DIFF_QK) < ROPE_AXIS
    cos = cos_ref[...]
    sin = sin_ref[...]

    def rope(x):
        partner = jnp.where(first_half, pltpu.roll(x, BRANCH_W - ROPE_AXIS, 1), pltpu.roll(x, ROPE_AXIS, 1))
        return x * cos + partner * sin

    q = rope(x_ref[:, 0:BRANCH_W].astype(F32)) * (DIFF_QK ** -0.5 * LOG2E)
    k = rope(x_ref[:, BRANCH_W:2 * BRANCH_W].astype(F32))
    q_ref[...] = q.astype(q_ref.dtype)
    kt_ref[...] = jnp.transpose(k).astype(kt_ref.dtype)
    v = x_ref[:, 2 * BRANCH_W:]
    src = _iota((BRANCH_W, VX_W), 0)
    dst = _iota((BRANCH_W, VX_W), 1)
    ones_col = jnp.where(_iota((ROW_TILE, VX_W), 1) == HEAD_DIM, 1.0, 0.0)
    for h in range(N_HEADS):
        sel = jnp.where(jnp.logical_and(src == h * HEAD_DIM + dst, dst < HEAD_DIM), 1.0, 0.0).astype(BF16)
        vx_ref[h] = (_dot(v, sel) + ones_col).astype(vx_ref.dtype)


def _rope_prep(proj, cos_l, sin_l):
    bsz, t, _ = proj.shape
    nt = t // ROW_TILE
    kvt = _kv_tile(t)
    r = kvt // ROW_TILE
    qkv_blk = BLK_DIFF_QKV * BRANCH_W // QKV_W
    tab = pl.BlockSpec((ROW_TILE, BRANCH_W), lambda b, i: (i, 0))
    return pl.pallas_call(
        _rope_prep_kernel,
        grid=(bsz, nt),
        in_specs=[pl.BlockSpec((None, ROW_TILE, QKV_W), lambda b, i: (b, i, qkv_blk)), tab, tab],
        out_specs=[pl.BlockSpec((None, ROW_TILE, BRANCH_W), lambda b, i: (b, i, 0)),
                   pl.BlockSpec((None, None, BRANCH_W, ROW_TILE), lambda b, i: (b, i // r, 0, i % r)),
                   pl.BlockSpec((None, None, N_HEADS, ROW_TILE, VX_W), lambda b, i: (b, i // r, 0, i % r, 0))],
        out_shape=[jax.ShapeDtypeStruct((bsz, t, BRANCH_W), BF16),
                   jax.ShapeDtypeStruct((bsz, t // kvt, BRANCH_W, kvt), BF16),
                   jax.ShapeDtypeStruct((bsz, t // kvt, N_HEADS, kvt, VX_W), BF16)],
        compiler_params=_cparams("parallel", "parallel"),
        name="rope_prep",
    )(proj, cos_l, sin_l)


N_CHAINS = 2 * N_HEADS


def _attn_kernel(q_ref, kt_ref, vx_ref, lp_ref, o_ref, qm_scr, s0_scr, s1_scr, m0_scr, m1_scr, al0_scr, al1_scr,
                 acc_scr, *, nk, lam_init):
    s_bufs, m_bufs, al_bufs = (s0_scr, s1_scr), (m0_scr, m1_scr), (al0_scr, al1_scr)
    q = q_ref[...]
    lane = _iota(q.shape, 1)
    zero = jnp.zeros_like(q)
    for c in range(N_CHAINS):
        lo = c * DIFF_QK
        qm_scr[c] = jnp.where(jnp.logical_and(lane >= lo, lane < lo + DIFF_QK), q, zero)
    acc_scr[...] = jnp.zeros_like(acc_scr)

    def stage_a(c, kt, slot, first=False):
        s = _dot(qm_scr[c], kt)
        s_bufs[slot][c] = s
        row_max = jnp.broadcast_to(jnp.max(s, axis=1, keepdims=True), (ROW_TILE, VX_W))
        if first:
            m_bufs[slot][c] = row_max
            al_bufs[slot][c] = jnp.zeros_like(row_max)
        else:
            m_old = m_bufs[1 - slot][c]
            m_new = jnp.maximum(m_old, row_max)
            m_bufs[slot][c] = m_new
            al_bufs[slot][c] = jnp.exp2(m_old - m_new)

    def stage_b(c, j, slot):
        m = m_bufs[slot][c]
        s_ref = s_bufs[slot]
        tk = s_ref.shape[2]
        p = jnp.concatenate([jnp.exp2(s_ref[c, :, k0:k0 + VX_W] - m) for k0 in range(0, tk, VX_W)], axis=1)
        acc_scr[c] = acc_scr[c] * al_bufs[slot][c] + _dot(p.astype(BF16), vx_ref[j, c // 2])

    def step(ja, jb, slot_a, first=False):
        kt = None if ja is None else kt_ref[ja]
        for c in range(N_CHAINS):
            if ja is not None:
                stage_a(c, kt, slot_a, first)
            if jb is not None:
                stage_b(c, jb, 1 - slot_a)

    step(0, None, 0, first=True)

    def body(i, carry):
        j = 2 * i + 1
        step(j, j - 1, 1)
        step(j + 1, j, 0)
        return carry

    n_pairs = (nk - 1) // 2
    lax.fori_loop(0, n_pairs, body, 0)
    if nk % 2 == 0:
        step(nk - 1, nk - 2, 1)
    step(None, nk - 1, 1 - (nk - 1) % 2)

    lp = lp_ref[...]
    grp_src = _iota((VX_W, VX_W), 0) // DIFF_QK
    prod1 = lp[0:1, :] * lp[1:2, :]
    prod2 = lp[2:3, :] * lp[3:4, :]
    src = _iota((VX_W, BRANCH_W), 0)
    dst = _iota((VX_W, BRANCH_W), 1)
    out = jnp.zeros((ROW_TILE, BRANCH_W), F32)
    for h in range(N_HEADS):
        pick = jnp.where(grp_src == h, 1.0, 0.0).astype(BF16)
        lam = jnp.exp(_dot_hl_r(prod1, pick)) - jnp.exp(_dot_hl_r(prod2, pick)) + lam_init
        a0 = acc_scr[2 * h]
        a1 = acc_scr[2 * h + 1]
        o_h = a0 / a0[:, HEAD_DIM:HEAD_DIM + 1] - lam * (a1 / a1[:, HEAD_DIM:HEAD_DIM + 1])
        place = jnp.where(jnp.logical_and(dst == src + h * HEAD_DIM, src < HEAD_DIM), 1.0, 0.0).astype(BF16)
        out = out + _dot(o_h.astype(BF16), place)
    o_ref[...] = out.astype(o_ref.dtype)


def _diff_attention(q_rot, kt, vx, lam_p, lam_init, q_tile0, n_q_tiles, kv_len):
    bsz = q_rot.shape[0]
    n_kv_arr, _, kvt = kt.shape[1:]
    tk = min(kv_len, kvt)
    assert kv_len % tk == 0 and kvt % tk == 0
    nk = kv_len // tk
    lp = jnp.zeros((8, VX_W), F32).at[:4].set(lam_p.reshape(4, N_HEADS * DIFF_QK))
    return pl.pallas_call(
        functools.partial(_attn_kernel, nk=nk, lam_init=lam_init),
        grid=(bsz, n_q_tiles),
        in_specs=[pl.BlockSpec((None, ROW_TILE, BRANCH_W), lambda b, i: (b, i + q_tile0, 0)),
                  pl.BlockSpec((None, nk, BRANCH_W, tk), lambda b, i: (b, 0, 0, 0), pipeline_mode=pl.Buffered(1)),
                  pl.BlockSpec((None, nk, N_HEADS, tk, VX_W), lambda b, i: (b, 0, 0, 0, 0),
                               pipeline_mode=pl.Buffered(1)),
                  pl.BlockSpec((8, VX_W), lambda b, i: (0, 0))],
        out_specs=pl.BlockSpec((None, ROW_TILE, BRANCH_W), lambda b, i: (b, i, 0)),
        out_shape=jax.ShapeDtypeStruct((bsz, n_q_tiles * ROW_TILE, BRANCH_W), BF16),
        scratch_shapes=[pltpu.VMEM((N_CHAINS, ROW_TILE, BRANCH_W), BF16)]
                       + [pltpu.VMEM((N_CHAINS, ROW_TILE, tk), F32)] * 2
                       + [pltpu.VMEM((N_CHAINS, ROW_TILE, VX_W), F32)] * 4
                       + [pltpu.VMEM((N_CHAINS, ROW_TILE, VX_W), F32)],
        compiler_params=_cparams("parallel", "arbitrary"),
        name="diff_attention",
    )(q_rot, kt, vx, lp)


def _diff_branch(proj, tables, lam_p, lam_init, n_ctx_tiles, with_ctx):
    t = proj.shape[1]
    q_rot, kt, vx = _rope_prep(proj, *tables)
    lat = _diff_attention(q_rot, kt, vx, lam_p, lam_init, n_ctx_tiles, t // ROW_TILE - n_ctx_tiles, t)
    if not with_ctx:
        return lat
    ctx = _diff_attention(q_rot, kt, vx, lam_p, lam_init, 0, n_ctx_tiles, n_ctx_tiles * ROW_TILE)
    return jnp.concatenate([ctx, lat], axis=1)


def _prepare(x, c, ctx, c_ctx, w_mod, b_mod):
    bsz, _, d = x.shape
    ct = ctx.shape[1]
    assert ct % ROW_TILE == 0 and x.shape[1] % ROW_TILE == 0
    xa = jnp.concatenate([ctx, x], axis=1)
    rows = 8 * ((bsz + 1 + 7) // 8)
    cond = jnp.zeros((rows, d), F32).at[:bsz].set(c).at[bsz].set(c_ctx)
    mods = _modulation(cond, w_mod, b_mod)
    return dict(xa=xa, mods=mods, n_ctx_tiles=ct // ROW_TILE, ct=ct)


def _layer_inproj(st, l, norm_mix, w_in):
    mods3 = st["mods"][l][:, None, :]
    return _inproj(st["xa"], mods3, norm_mix[l], _permute_w_in(w_in[l]), st["n_ctx_tiles"])


def _merge_kernel(x_ref, mod_ref, glf_ref, glb_ref, mlf_ref, mlb_ref, gdf_ref, gdb_ref, at_ref,
                  g_gla_ref, g_ml_ref, g_gd_ref, mg0_ref, mg1_ref, mg2_ref, mg3_ref,
                  hn_ref, wb_ref, wo_ref, o_ref, *, d, lam_init):
    ones_blk = _ones_blk()

    def head_norm(o, i):
        ms = _seg_sum(o * o, ones_blk) * (1.0 / HEAD_DIM)
        return o * lax.rsqrt(ms + EPS) * hn_ref[i:i + 1, :]

    f32 = lambda r: r[...].astype(F32)
    ys = [head_norm(f32(glf_ref) + f32(glb_ref), 0) * _silu(f32(g_gla_ref)),
          head_norm(f32(mlf_ref) + f32(mlb_ref), 1) * jax.nn.sigmoid(f32(g_ml_ref)),
          head_norm(f32(gdf_ref) + f32(gdb_ref), 2) * _silu(f32(g_gd_ref)),
          head_norm(f32(at_ref), 3) * (1.0 - lam_init)]
    acc = None
    for i, (y, mg_ref) in enumerate(zip(ys, (mg0_ref, mg1_ref, mg2_ref, mg3_ref))):
        term = jax.nn.sigmoid(f32(mg_ref)) * _dot(y.astype(BF16), wb_ref[i])
        acc = term if acc is None else acc + term
    out = _dot(acc.astype(BF16), wo_ref[...])
    g1 = mod_ref[...][:, 2 * d:3 * d]
    o_ref[...] = x_ref[...] + g1 * out


def _merge(xa, mods3, proj, scans, attn, hn, wb, wo, n_ctx_tiles, with_ctx, lam_init):
    bsz, t, d = xa.shape
    off = 0 if with_ctx else n_ctx_tiles
    nt = t // ROW_TILE - off
    rows = lambda col: pl.BlockSpec((None, ROW_TILE, BRANCH_W), lambda b, i: (b, i + off, col))
    gate = lambda j: pl.BlockSpec((None, ROW_TILE, d), lambda b, i: (b, i + off, BLK_GATES * BRANCH_W // d + j))
    const = lambda shape: pl.BlockSpec(shape, lambda b, i: (0,) * len(shape))
    return pl.pallas_call(
        functools.partial(_merge_kernel, d=d, lam_init=lam_init),
        grid=(bsz, nt),
        in_specs=[pl.BlockSpec((None, ROW_TILE, d), lambda b, i: (b, i + off, 0)),
                  pl.BlockSpec((None, 1, 6 * d), lambda b, i: (jnp.where(i + off < n_ctx_tiles, bsz, b), 0, 0))]
                 + [rows(0)] * 6
                 + [pl.BlockSpec((None, ROW_TILE, BRANCH_W), lambda b, i: (b, i, 0))]
                 + [rows(BLK_GLA_G), rows(BLK_ML_G), rows(BLK_GDN_G)]
                 + [gate(j) for j in range(N_BRANCH)]
                 + [const((8, BRANCH_W)), const((N_BRANCH, BRANCH_W, d)), const((d, d))],
        out_specs=pl.BlockSpec((None, ROW_TILE, d), lambda b, i: (b, i, 0)),
        out_shape=jax.ShapeDtypeStruct((bsz, nt * ROW_TILE, d), F32),
        compiler_params=_cparams("parallel", "parallel"),
        name="merge",
    )(xa, mods3, *scans, attn, proj, proj, proj, proj, proj, proj, proj, hn, wb, wo)


GATE_LANES = 128


def _router_select(hf, rw_ref, rb_ref):
    h_hi, h_lo = _split_hl(hf)
    w_hi, w_lo = _split_hl(rw_ref[...])
    logits = _dot_nt(w_hi, h_hi) + _dot_nt(w_hi, h_lo) + _dot_nt(w_lo, h_hi)
    scores = jax.nn.sigmoid(logits)
    sel = scores + rb_ref[...]
    s = [sel[e:e + 1, :] for e in range(N_EXPERTS)]
    sc = [scores[e:e + 1, :] for e in range(N_EXPERTS)]
    grp = []
    for g in range(N_GROUPS):
        a, b, c, dd = s[4 * g:4 * g + 4]
        grp.append(functools.reduce(jnp.maximum, [a + b, a + c, a + dd, b + c, b + dd, c + dd]))
    gmax = functools.reduce(jnp.maximum, grp)
    chosen, taken = [], None
    for g in range(N_GROUPS):
        hit = grp[g] == gmax
        if taken is not None:
            hit = jnp.logical_and(hit, jnp.logical_not(taken))
        taken = hit if taken is None else jnp.logical_or(taken, hit)
        chosen.append(hit)
    ms = [jnp.where(chosen[e // EXPERTS_PER_GROUP], s[e], NEG_BIG) for e in range(N_EXPERTS)]

    def first_argmax(vals):
        top = functools.reduce(jnp.maximum, vals)
        hits, seen = [], None
        for v in vals:
            hit = v == top
            if seen is not None:
                hit = jnp.logical_and(hit, jnp.logical_not(seen))
            seen = hit if seen is None else jnp.logical_or(seen, hit)
            hits.append(hit)
        return hits

    oh1 = first_argmax(ms)
    oh2 = first_argmax([jnp.where(o, NEG_BIG, v) for o, v in zip(oh1, ms)])
    zero = jnp.zeros_like(sc[0])
    w1 = functools.reduce(jnp.add, [jnp.where(o, v, zero) for o, v in zip(oh1, sc)])
    w2 = functools.reduce(jnp.add, [jnp.where(o, v, zero) for o, v in zip(oh2, sc)])
    tot = w1 + w2
    ids = [functools.reduce(jnp.add, [jnp.where(o, float(e), 0.0) for e, o in enumerate(oh)]) for oh in (oh1, oh2)]
    return ids[0], ids[1], w1 / tot, w2 / tot


TOK_ROWS = 8
FFN_TILE = ROW_TILE


def _store_token_tiles(ref, val):
    n = val.shape[0]
    for s in range(TOK_ROWS):
        ref[pl.ds(s, n, stride=TOK_ROWS), :] = val[:, s * 128:(s + 1) * 128]


def _load_token_tiles(ref, n):
    return jnp.concatenate([ref[pl.ds(s, n, stride=TOK_ROWS), :] for s in range(TOK_ROWS)], axis=1)


def _moe_router_kernel(x_ref, mod_ref, g_ref, rw_ref, rb_ref, h_ref, ids_ref, wcol_ref, *, d):
    mod = mod_ref[...]
    hf = _rms_mod(x_ref[...], g_ref[...], mod[:, 3 * d:4 * d], mod[:, 4 * d:5 * d])
    _store_token_tiles(h_ref, hf)
    e1, e2, w1, w2 = _router_select(hf, rw_ref, rb_ref)
    pad = jnp.zeros((6, ROW_TILE), F32)
    ids_ref[...] = jnp.concatenate([e1, e2, pad], axis=0).astype(jnp.int32)
    eye = jnp.where(_iota((8, GATE_LANES), 0) == _iota((8, GATE_LANES), 1), 1.0, 0.0).astype(BF16)
    w_hi, w_lo = _split_hl(jnp.concatenate([w1, w2, pad], axis=0))
    wcol_ref[...] = _dot_tn(w_hi, eye) + _dot_tn(w_lo, eye)


def _moe_router(xa, mods3, g, rw_t, rb, n_ctx_tiles):
    bsz, t, d = xa.shape
    nt = t // ROW_TILE
    n = bsz * t
    flat = lambda b, i: b * nt + i
    return pl.pallas_call(
        functools.partial(_moe_router_kernel, d=d),
        grid=(bsz, nt),
        in_specs=[pl.BlockSpec((None, ROW_TILE, d), lambda b, i: (b, i, 0)),
                  pl.BlockSpec((None, 1, 6 * d), _mod_row_map(n_ctx_tiles, bsz)),
                  pl.BlockSpec((1, d), lambda b, i: (0, 0)),
                  pl.BlockSpec((N_EXPERTS, d), lambda b, i: (0, 0)),
                  pl.BlockSpec((N_EXPERTS, ROW_TILE), lambda b, i: (0, 0))],
        out_specs=[pl.BlockSpec((ROW_TILE * TOK_ROWS, 128), lambda b, i: (flat(b, i), 0)),
                   pl.BlockSpec((8, ROW_TILE), lambda b, i: (0, flat(b, i))),
                   pl.BlockSpec((ROW_TILE, GATE_LANES), lambda b, i: (flat(b, i), 0))],
        out_shape=[jax.ShapeDtypeStruct((n * TOK_ROWS, 128), F32),
                   jax.ShapeDtypeStruct((8, n), jnp.int32),
                   jax.ShapeDtypeStruct((n, GATE_LANES), F32)],
        compiler_params=_cparams("parallel", "parallel"),
        name="moe_router",
    )(xa, mods3, g.reshape(1, d), rw_t, rb)


def _moe_plan(ids, n):
    n_tiles = -(-2 * n // FFN_TILE) + N_EXPERTS
    eid = ids[:2].reshape(2 * n)
    order = jnp.argsort(eid, stable=True).astype(jnp.int32)
    inv = jnp.argsort(order).astype(jnp.int32)
    experts = jnp.arange(N_EXPERTS, dtype=jnp.int32)
    onehot = (eid[:, None] == experts[None, :]).astype(jnp.int32)
    counts = jnp.sum(onehot, axis=0)
    first = jnp.cumsum(counts) - counts
    padded = (counts + FFN_TILE - 1) // FFN_TILE * FFN_TILE
    ends = jnp.cumsum(padded)
    offs = ends - padded
    dest = inv + jnp.sum(onehot * (offs - first)[None, :], axis=1)
    tile_start = jnp.arange(n_tiles, dtype=jnp.int32) * FFN_TILE
    tile_expert = jnp.minimum(jnp.sum((ends[None, :] <= tile_start[:, None]).astype(jnp.int32), axis=1), N_EXPERTS - 1)
    tile_hot = (tile_expert[:, None] == experts[None, :]).astype(jnp.int32)
    lookup = lambda table: jnp.repeat(jnp.sum(tile_hot * table[None, :], axis=1), FFN_TILE)
    local = jnp.arange(n_tiles * FFN_TILE, dtype=jnp.int32) - lookup(offs)
    tok_sorted = order % n
    src = jnp.where(local < lookup(counts), tok_sorted[jnp.clip(lookup(first) + local, 0, 2 * n - 1)], 0)
    dest_tiles = dest.reshape(2, n // ROW_TILE, ROW_TILE).transpose(1, 0, 2)
    return src.reshape(n_tiles, 1, FFN_TILE), tile_expert, dest_tiles


def _gather_rows(idx_ref, k, src_hbm, dst, sem, n_rows, unrolled):
    def start(r, priority):
        row = idx_ref[k, r]
        pltpu.make_async_copy(src_hbm.at[pl.ds(pl.multiple_of(row * TOK_ROWS, TOK_ROWS), TOK_ROWS), :],
                              dst.at[pl.ds(pl.multiple_of(r * TOK_ROWS, TOK_ROWS), TOK_ROWS), :],
                              sem).start(priority=priority)

    if unrolled:
        for r in range(n_rows):
            start(r, r % 2)
    else:
        def body(r, carry):
            start(r, 0)
            return carry

        lax.fori_loop(0, n_rows, body, 0, unroll=8)


def _wait_rows(src_hbm, dst, sem, n_rows):
    pltpu.make_async_copy(src_hbm.at[pl.ds(0, n_rows * TOK_ROWS), :], dst, sem).wait()


def _moe_experts_kernel(te_ref, src_ref, nxt_ref, h_hbm, wg_ref, wu_ref, wd_ref, y_ref,
                        buf, wg_b, wu_b, wd_b, sem):
    t = pl.program_id(0)
    last = pl.num_programs(0) - 1
    slot = lax.rem(t, 2)

    @pl.when(jnp.logical_or(t == 0, te_ref[t] != te_ref[jnp.maximum(t - 1, 0)]))
    def _():
        wg_b[...] = wg_ref[...].astype(BF16)
        wu_b[...] = wu_ref[...].astype(BF16)
        wd_b[...] = wd_ref[...].astype(BF16)

    @pl.when(t == 0)
    def _():
        _gather_rows(src_ref, 0, h_hbm, buf.at[0], sem.at[0], FFN_TILE, unrolled=False)

    _gather_rows(nxt_ref, 0, h_hbm, buf.at[1 - slot], sem.at[1 - slot], FFN_TILE, unrolled=True)
    _wait_rows(h_hbm, buf.at[slot], sem.at[slot], FFN_TILE)
    h = _load_token_tiles(buf.at[slot], FFN_TILE).astype(BF16)
    act = (_silu(_dot(h, wg_b[...])) * _dot(h, wu_b[...])).astype(BF16)
    _store_token_tiles(y_ref, _dot(act, wd_b[...]))

    @pl.when(t == last)
    def _():
        _wait_rows(h_hbm, buf.at[1 - slot], sem.at[1 - slot], FFN_TILE)


def _moe_experts(h_tt, plan, wg, wu, wd, layer):
    src, tile_expert, _ = plan
    n_tiles = src.shape[0]
    _, ne, d, de = wg.shape
    idx_spec = lambda step: pl.BlockSpec((None, 1, FFN_TILE), lambda t, te: (jnp.minimum(t + step, n_tiles - 1), 0, 0),
                                         memory_space=pltpu.SMEM)
    grid_spec = pltpu.PrefetchScalarGridSpec(
        num_scalar_prefetch=1,
        grid=(n_tiles,),
        in_specs=[idx_spec(0), idx_spec(1),
                  pl.BlockSpec(memory_space=pl.ANY),
                  pl.BlockSpec((None, None, d, de), lambda t, te: (layer, te[t], 0, 0)),
                  pl.BlockSpec((None, None, d, de), lambda t, te: (layer, te[t], 0, 0)),
                  pl.BlockSpec((None, None, de, d), lambda t, te: (layer, te[t], 0, 0))],
        out_specs=pl.BlockSpec((FFN_TILE * TOK_ROWS, 128), lambda t, te: (t, 0)),
        scratch_shapes=[pltpu.VMEM((2, FFN_TILE * TOK_ROWS, 128), F32),
                        pltpu.VMEM((d, de), BF16), pltpu.VMEM((d, de), BF16), pltpu.VMEM((de, d), BF16),
                        pltpu.SemaphoreType.DMA((2,))])
    return pl.pallas_call(
        _moe_experts_kernel,
        grid_spec=grid_spec,
        out_shape=jax.ShapeDtypeStruct((n_tiles * FFN_TILE * TOK_ROWS, 128), F32),
        compiler_params=_cparams("arbitrary"),
        name="moe_experts",
    )(tile_expert, src, src, h_tt, wg, wu, wd)


def _moe_combine_kernel(dest_ref, nxt_ref, x_ref, mod_ref, wcol_ref, gf_ref, y_hbm, o_ref, buf, sem, *, d, final):
    j = pl.program_id(0)
    last = pl.num_programs(0) - 1
    slot = lax.rem(j, 2)

    def gather(idx_ref, s, unrolled):
        for k in range(2):
            _gather_rows(idx_ref, k, y_hbm, buf.at[s, k], sem.at[s], ROW_TILE, unrolled)

    def wait(s):
        for k in range(2):
            _wait_rows(y_hbm, buf.at[s, k], sem.at[s], ROW_TILE)

    @pl.when(j == 0)
    def _():
        gather(dest_ref, 0, unrolled=False)

    gather(nxt_ref, 1 - slot, unrolled=True)
    wait(slot)
    w = wcol_ref[...]
    mix = (w[:, 0:1] * _load_token_tiles(buf.at[slot, 0], ROW_TILE)
           + w[:, 1:2] * _load_token_tiles(buf.at[slot, 1], ROW_TILE))
    g2 = mod_ref[...][:, 5 * d:6 * d]
    x_new = x_ref[...] + g2 * mix
    if final:
        x_new = x_new * lax.rsqrt(jnp.mean(x_new * x_new, axis=-1, keepdims=True) + EPS) * gf_ref[...]
    o_ref[...] = x_new

    @pl.when(j == last)
    def _():
        wait(1 - slot)


def _moe_combine(xa, mods3, wcol, y_tt, plan, n_ctx_tiles, final_g=None):
    bsz, t, d = xa.shape
    nt = t // ROW_TILE
    n_tiles = bsz * nt
    dest_tiles = plan[2]
    idx_spec = lambda step: pl.BlockSpec((None, 2, ROW_TILE), lambda j: (jnp.minimum(j + step, n_tiles - 1), 0, 0),
                                         memory_space=pltpu.SMEM)
    final = final_g is not None
    gf = (final_g if final else jnp.ones((d,), F32)).reshape(1, d)
    return pl.pallas_call(
        functools.partial(_moe_combine_kernel, d=d, final=final),
        grid=(n_tiles,),
        in_specs=[idx_spec(0), idx_spec(1),
                  pl.BlockSpec((None, ROW_TILE, d), lambda j: (j // nt, j % nt, 0)),
                  pl.BlockSpec((None, 1, 6 * d), lambda j: (jnp.where(j % nt < n_ctx_tiles, bsz, j // nt), 0, 0)),
                  pl.BlockSpec((ROW_TILE, GATE_LANES), lambda j: (j, 0)),
                  pl.BlockSpec((1, d), lambda j: (0, 0)),
                  pl.BlockSpec(memory_space=pl.ANY)],
        out_specs=pl.BlockSpec((None, ROW_TILE, d), lambda j: (j // nt, j % nt, 0)),
        out_shape=jax.ShapeDtypeStruct((bsz, t, d), F32),
        scratch_shapes=[pltpu.VMEM((2, 2, ROW_TILE * TOK_ROWS, 128), F32), pltpu.SemaphoreType.DMA((2,))],
        compiler_params=_cparams("arbitrary"),
        name="moe_combine",
    )(dest_tiles, dest_tiles, xa, mods3, wcol, gf, y_tt)


def _moe_sparse(xa, mods3, g, rw_t, rb, wg, wu, wd, layer, n_ctx_tiles, final_g=None):
    bsz, t, _ = xa.shape
    h_tt, ids, wcol = _moe_router(xa, mods3, g, rw_t, rb, n_ctx_tiles)
    plan = _moe_plan(ids, bsz * t)
    y_tt = _moe_experts(h_tt, plan, wg, wu, wd, layer)
    return _moe_combine(xa, mods3, wcol, y_tt, plan, n_ctx_tiles, final_g)


def kernel(x, c, ctx, c_ctx, w_mod, b_mod, norm_mix, norm_ffn, w_in, gla_a2, gla_ab, mlstm_gate_b, gdn_conv, gdn_a_log, gdn_dt_bias, diff_lambda, head_norm, w_branch, w_out, router_w, router_b, w_gate, w_up, w_down, norm_final):
    depth = w_in.shape[0]
    st = _prepare(x, c, ctx, c_ctx, w_mod, b_mod)
    n_ctx_tiles = st["n_ctx_tiles"]
    tables = _rope_tables(x.shape[1], st["ct"])
    rw_t = router_w.T
    rb = jnp.broadcast_to(router_b.astype(F32)[:, None], (N_EXPERTS, ROW_TILE))
    for l in range(depth):
        with_ctx = l < depth - 1
        lam_init = 0.8 - 0.6 * math.exp(-0.3 * l)
        mods3 = st["mods"][l][:, None, :]
        proj = _layer_inproj(st, l, norm_mix, w_in)
        scans = (*_gla_branch(proj, gla_a2[l], gla_ab[l]),
                 *_mlstm_branch(proj, mlstm_gate_b[l]),
                 *_gdn_branch(proj, gdn_conv[l], gdn_a_log[l], gdn_dt_bias[l], n_ctx_tiles))
        attn = _diff_branch(proj, tables, diff_lambda[l], lam_init, n_ctx_tiles, with_ctx)
        hn = jnp.zeros((8, BRANCH_W), F32).at[:N_BRANCH].set(head_norm[l])
        xa = _merge(st["xa"], mods3, proj, scans, attn, hn, w_branch[l].astype(BF16), w_out[l].astype(BF16),
                    n_ctx_tiles, with_ctx, lam_init)
        if not with_ctx:
            n_ctx_tiles = 0
        xa = _moe_sparse(xa, mods3, norm_ffn[l], rw_t, rb, w_gate, w_up, w_down, l, n_ctx_tiles,
                         final_g=None if with_ctx else norm_final)
        st = dict(st, xa=xa, n_ctx_tiles=n_ctx_tiles)
    return st["xa"]
```

```python
import functools
import math

import numpy as np
import jax
import jax.numpy as jnp
from jax import lax
from jax.experimental import pallas as pl
from jax.experimental.pallas import tpu as pltpu

N_HEADS = 4
HEAD_DIM = 64
BRANCH_W = N_HEADS * HEAD_DIM
CHUNK = 64
GLA_RANK = 16
GLA_NORMALIZER = 16.0
CONV_W = 5
DIFF_QK = HEAD_DIM // 2
ROPE_AXIS = DIFF_QK // 2
ROPE_BASE = 10000.0
GRID_W = 64
N_EXPERTS = 16
N_GROUPS = 4
EXPERTS_PER_GROUP = 4
EPS = 1e-6
N_BRANCH = 4

ROW_TILE = 256
PROJ_COLS = 8192
BLK_GDN_QKV, BLK_DIFF_QKV, BLK_GDN_G = 0, 3, 6
BLK_GLA_Q, BLK_GLA_K, BLK_GLA_V, BLK_GLA_G = 7, 8, 9, 10
BLK_ML_Q, BLK_ML_K, BLK_ML_V, BLK_ML_G = 11, 28, 29, 30
BLK_GATES = 12
SMALL_BLOCK = 31
SM_GLA_RF, SM_GLA_RB, SM_ML_F, SM_ML_B, SM_GD_F, SM_GD_B = 0, 16, 32, 40, 48, 56
NEG_BIG = -1e30
VMEM_LIMIT = 56 * 1024 * 1024

F32 = jnp.float32
BF16 = jnp.bfloat16


def _cparams(*sem):
    return pltpu.CompilerParams(dimension_semantics=sem, vmem_limit_bytes=VMEM_LIMIT)


def _dot(a, b):
    return jnp.dot(a, b, preferred_element_type=F32)


def _dot_nt(a, b):
    return lax.dot_general(a, b, (((1,), (1,)), ((), ())), preferred_element_type=F32)


def _dot_tn(a, b):
    return lax.dot_general(a, b, (((0,), (0,)), ((), ())), preferred_element_type=F32)


def _split_hl(x):
    hi = x.astype(BF16)
    lo = (x - hi.astype(F32)).astype(BF16)
    return hi, lo


def _dot_hl(a, x):
    hi, lo = _split_hl(x)
    return _dot(a, hi) + _dot(a, lo)


def _dot_hl_r(x, a):
    hi, lo = _split_hl(x)
    return _dot(hi, a) + _dot(lo, a)


def _iota(shape, dim):
    return lax.broadcasted_iota(jnp.int32, shape, dim)


def _softplus(x):
    return jnp.maximum(x, 0.0) + jnp.log(1.0 + jnp.exp(-jnp.abs(x)))


def _log_sigmoid(x):
    return -_softplus(-x)


def _silu(x):
    return x * jax.nn.sigmoid(x)


def _lane_head(shape):
    return _iota(shape, len(shape) - 1) // HEAD_DIM


def _stack4(x):
    lh = _lane_head(x.shape)
    zero = jnp.zeros_like(x)
    return jnp.concatenate([jnp.where(lh == h, x, zero) for h in range(N_HEADS)], axis=0)


def _blk_mask():
    r = _iota((BRANCH_W, BRANCH_W), 0) // HEAD_DIM
    c = _iota((BRANCH_W, BRANCH_W), 1) // HEAD_DIM
    return r == c


def _ones_blk():
    return jnp.where(_blk_mask(), 1.0, 0.0).astype(BF16)


def _tri_hs(reverse, strict):
    t = _iota((CHUNK, BRANCH_W), 0)
    s = _iota((CHUNK, BRANCH_W), 1) % CHUNK
    if reverse:
        return (s > t) if strict else (s >= t)
    return (s < t) if strict else (s <= t)


def _eye_hs():
    t = _iota((CHUNK, BRANCH_W), 0)
    s = _iota((CHUNK, BRANCH_W), 1) % CHUNK
    return t == s


def _cum_mat(reverse):
    t = _iota((CHUNK, CHUNK), 0)
    s = _iota((CHUNK, CHUNK), 1)
    m = (s >= t) if reverse else (s <= t)
    return jnp.where(m, 1.0, 0.0).astype(BF16)


def _row_of(col_rep):
    return jnp.sum(jnp.where(_eye_hs(), col_rep, 0.0), axis=0, keepdims=True)


def _seg_max(x):
    lane = _iota(x.shape, 1)
    n = x.shape[1]
    for sh in (1, 2, 4, 8, 16, 32):
        up = pltpu.roll(x, n - sh, 1)
        dn = pltpu.roll(x, sh, 1)
        x = jnp.maximum(x, jnp.where((lane & sh) == 0, up, dn))
    return x


def _seg_sum(x, ones_blk):
    return _dot_hl_r(x, ones_blk)


def _mod_kernel(c_ref, w_ref, b_ref, o_ref):
    a = _silu(c_ref[...]).astype(BF16)
    o_ref[...] = _dot(a, w_ref[...].astype(BF16)) + b_ref[...]


def _modulation(cond, w_mod, b_mod):
    depth, d, n = w_mod.shape
    r = cond.shape[0]
    tn = 1536
    return pl.pallas_call(
        _mod_kernel,
        grid=(depth, n // tn),
        in_specs=[pl.BlockSpec((r, d), lambda l, j: (0, 0)),
                  pl.BlockSpec((None, d, tn), lambda l, j: (l, 0, j)),
                  pl.BlockSpec((None, 1, tn), lambda l, j: (l, 0, j))],
        out_specs=pl.BlockSpec((None, r, tn), lambda l, j: (l, 0, j)),
        out_shape=jax.ShapeDtypeStruct((depth, r, n), F32),
        compiler_params=_cparams("parallel", "parallel"),
        name="modulation",
    )(cond, w_mod, b_mod.reshape(depth, 1, n))


def _rms_mod(x, g, shift, scale):
    y = x * lax.rsqrt(jnp.mean(x * x, axis=-1, keepdims=True) + EPS)
    return (y * g) * (1.0 + scale) + shift


def _inproj_kernel(x_ref, mod_ref, g_ref, w_ref, o_ref, *, d, n_chunk):
    mod = mod_ref[...]
    h = _rms_mod(x_ref[...], g_ref[...], mod[:, 0:d], mod[:, d:2 * d]).astype(BF16)
    for n0 in range(0, PROJ_COLS, n_chunk):
        o_ref[:, n0:n0 + n_chunk] = _dot(h, w_ref[:, n0:n0 + n_chunk]).astype(BF16)


def _mod_row_map(n_ctx_tiles, n_batch):
    return lambda b, i: (jnp.where(i < n_ctx_tiles, n_batch, b), 0, 0)


def _inproj(xa, mods3, g, w_perm, n_ctx_tiles):
    bsz, t, d = xa.shape
    return pl.pallas_call(
        functools.partial(_inproj_kernel, d=d, n_chunk=1024),
        grid=(bsz, t // ROW_TILE),
        in_specs=[pl.BlockSpec((None, ROW_TILE, d), lambda b, i: (b, i, 0)),
                  pl.BlockSpec((None, 1, 6 * d), _mod_row_map(n_ctx_tiles, bsz)),
                  pl.BlockSpec((1, d), lambda b, i: (0, 0)),
                  pl.BlockSpec((d, PROJ_COLS), lambda b, i: (0, 0), pipeline_mode=pl.Buffered(1))],
        out_specs=pl.BlockSpec((None, ROW_TILE, PROJ_COLS), lambda b, i: (b, i, 0)),
        out_shape=jax.ShapeDtypeStruct((bsz, t, PROJ_COLS), BF16),
        compiler_params=_cparams("parallel", "parallel"),
        name="inproj",
    )(xa, mods3, g.reshape(1, d), w_perm)


def _proj_perm():
    sizes = (256, 256, 256, 256, 16, 16, 256, 256, 256, 256, 8, 8, 256, 256, 256, 256, 8, 8, 256, 256, 256, 4096)
    off = np.concatenate([[0], np.cumsum(sizes)])
    seg = lambda i: np.arange(off[i], off[i + 1])
    wide = {BLK_GDN_QKV: 12, BLK_GDN_QKV + 1: 13, BLK_GDN_QKV + 2: 14, BLK_GDN_G: 15,
            BLK_DIFF_QKV: 18, BLK_DIFF_QKV + 1: 19, BLK_DIFF_QKV + 2: 20,
            BLK_GLA_Q: 0, BLK_GLA_K: 1, BLK_GLA_V: 2, BLK_GLA_G: 3,
            BLK_ML_Q: 6, BLK_ML_K: 7, BLK_ML_V: 8, BLK_ML_G: 9}
    perm = np.full((PROJ_COLS,), -1, np.int64)
    for j, i in wide.items():
        perm[j * BRANCH_W:(j + 1) * BRANCH_W] = seg(i)
    perm[BLK_GATES * BRANCH_W:BLK_GATES * BRANCH_W + 4096] = seg(21)
    base = SMALL_BLOCK * BRANCH_W
    for lane0, i in ((SM_GLA_RF, 4), (SM_GLA_RB, 5), (SM_ML_F, 10), (SM_ML_B, 11), (SM_GD_F, 16), (SM_GD_B, 17)):
        s = seg(i)
        perm[base + lane0:base + lane0 + len(s)] = s
    return perm


def _permute_w_in(w_in):
    perm = _proj_perm()
    cuts = [0] + [i for i in range(1, PROJ_COLS)
                  if (perm[i] < 0) != (perm[i - 1] < 0) or (perm[i] >= 0 and perm[i] != perm[i - 1] + 1)] + [PROJ_COLS]
    w = w_in.astype(BF16)
    runs = [jnp.zeros((w.shape[0], b - a), BF16) if perm[a] < 0 else w[:, perm[a]:perm[a] + b - a]
            for a, b in zip(cuts[:-1], cuts[1:])]
    return jnp.concatenate(runs, axis=-1)


SCAN_CHUNKS = ROW_TILE // CHUNK


def _scan_block_map(col, nb, reverse):
    if reverse:
        return lambda j: (0, jnp.where(j == 0, 0, nb - j), col)
    return lambda j: (0, j, col)


def _scan_steps(reverse, bsz):
    order = range(SCAN_CHUNKS - 1, -1, -1) if reverse else range(SCAN_CHUNKS)
    return [(b, slice(c * CHUNK, (c + 1) * CHUNK)) for c in order for b in range(bsz)]


def _scan_call(kernel_fn, name, arrays, cols, consts, scratch, reverse):
    bsz, t, _ = arrays[0].shape
    nb = t // ROW_TILE
    blk = lambda col: pl.BlockSpec((bsz, ROW_TILE, BRANCH_W), _scan_block_map(col, nb, reverse))
    const = lambda a: pl.BlockSpec(a.shape, lambda j: (0,) * a.ndim)
    return pl.pallas_call(
        functools.partial(kernel_fn, reverse=reverse, bsz=bsz),
        grid=(nb,),
        in_specs=[blk(c) for c in cols] + [const(a) for a in consts],
        out_specs=blk(0),
        out_shape=jax.ShapeDtypeStruct((bsz, t, BRANCH_W), BF16),
        scratch_shapes=[pltpu.VMEM((bsz,) + s, F32) for s in scratch],
        compiler_params=_cparams("arbitrary"),
        name=name + ("_bwd" if reverse else "_fwd"),
    )(*arrays, *consts)


def _expand_small(small, lane0, count):
    src = _iota((BRANCH_W, BRANCH_W), 0)
    dst_head = _iota((BRANCH_W, BRANCH_W), 1) // HEAD_DIM
    e = jnp.where(src == lane0 + dst_head, 1.0, 0.0).astype(BF16)
    return _dot(small, e)


def _gla_kernel(q_ref, k_ref, v_ref, sm_ref, a2_ref, ab_ref, o_ref, st_ref, *, reverse, bsz):
    @pl.when(pl.program_id(0) == 0)
    def _():
        st_ref[...] = jnp.zeros_like(st_ref)

    cum_mat = _cum_mat(reverse)
    tri = _tri_hs(reverse, strict=False)
    blk = _blk_mask()
    a2 = a2_ref[...]
    ab = ab_ref[...]
    steps = _scan_steps(reverse, bsz)
    la = [_log_sigmoid(_dot(sm_ref[b, rows, :], a2) + ab) / GLA_NORMALIZER for b, rows in steps]
    cum = [_dot_hl(cum_mat, x) for x in la]
    tot = [jnp.sum(x, axis=0, keepdims=True) for x in la]
    q_in = [(q_ref[b, rows, :].astype(F32) * (HEAD_DIM ** -0.5) * jnp.exp(c)).astype(BF16)
            for (b, rows), c in zip(steps, cum)]
    k_out = [(k_ref[b, rows, :].astype(F32) * jnp.exp(-c)).astype(BF16) for (b, rows), c in zip(steps, cum)]
    k_end = [(k_ref[b, rows, :].astype(F32) * jnp.exp(t - c)).astype(BF16)
             for (b, rows), c, t in zip(steps, cum, tot)]
    att = [jnp.where(tri, _dot_nt(qi, _stack4(ko)), 0.0).astype(BF16) for qi, ko in zip(q_in, k_out)]
    o_intra = [_dot(a, _stack4(v_ref[b, rows, :])) for (b, rows), a in zip(steps, att)]
    kv_t = [jnp.where(blk, _dot_tn(v_ref[b, rows, :], ke), 0.0) for (b, rows), ke in zip(steps, k_end)]
    for i, (b, rows) in enumerate(steps):
        st = st_ref[b]
        o_ref[b, rows, :] = (o_intra[i] + _dot_nt(q_in[i], st.astype(BF16))).astype(o_ref.dtype)
        st_ref[b] = st * jnp.exp(tot[i]) + kv_t[i]


def _gla_scan(proj, a2pad, ab, reverse):
    return _scan_call(_gla_kernel, "gla", [proj] * 4, [BLK_GLA_Q, BLK_GLA_K, BLK_GLA_V, SMALL_BLOCK],
                      [a2pad, ab], [(BRANCH_W, BRANCH_W)], reverse)


def _gla_params(a2, ab, d):
    lane0 = SM_GLA_RB if d else SM_GLA_RF
    pad = jnp.zeros((BRANCH_W, BRANCH_W), F32).at[lane0:lane0 + GLA_RANK].set(a2[d])
    return pad.astype(BF16), ab[d].reshape(1, BRANCH_W)


def _gla_branch(proj, a2, ab):
    return tuple(_gla_scan(proj, *_gla_params(a2, ab, d), reverse=bool(d)) for d in (0, 1))


def _mlstm_kernel(q_ref, k_ref, v_ref, sm_ref, bias_ref, o_ref, ct_ref, nm_ref, *, reverse, bsz):
    @pl.when(pl.program_id(0) == 0)
    def _():
        ct_ref[...] = jnp.zeros_like(ct_ref)
        nm_ref[...] = jnp.zeros_like(nm_ref)

    lane0 = SM_ML_B if reverse else SM_ML_F
    cum_mat = _cum_mat(reverse)
    tri = _tri_hs(reverse, strict=False)
    blk = _blk_mask()
    ones_blk = _ones_blk()
    b_i = bias_ref[0:1, :]
    b_f = bias_ref[1:2, :]
    steps = _scan_steps(reverse, bsz)
    each = lambda f, *ls: [f(*a) for a in zip(*ls)]
    sm = [sm_ref[b, rows, :] for b, rows in steps]
    q = [q_ref[b, rows, :] for b, rows in steps]
    i_pre = [_expand_small(x, lane0, N_HEADS) + b_i for x in sm]
    log_f = [_log_sigmoid(_expand_small(x, lane0 + N_HEADS, N_HEADS) + b_f) for x in sm]
    f_cum = [_dot_hl(cum_mat, x) for x in log_f]
    f_tot = [jnp.sum(x, axis=0, keepdims=True) for x in log_f]
    u = each(lambda i, f: i - f, i_pre, f_cum)
    lw = each(lambda t, x: t + x, f_tot, u)
    a_end = [jnp.max(x, axis=0, keepdims=True) for x in lw]
    k = [k_ref[b, rows, :].astype(F32) * (HEAD_DIM ** -0.5) for b, rows in steps]
    kw = each(lambda kk, x, a: kk * jnp.exp(x - a), k, lw, a_end)
    k_sum = [jnp.sum(x, axis=0, keepdims=True) for x in kw]
    kv_t = [jnp.where(blk, _dot_tn(v_ref[b, rows, :], x.astype(BF16)), 0.0) for (b, rows), x in zip(steps, kw)]
    log_d = each(lambda f, x: jnp.where(tri, f + _row_of(x), NEG_BIG), f_cum, u)
    mx = [_seg_max(x) for x in log_d]
    s = each(lambda qq, kk, ld, m: (_dot_nt(qq, _stack4(kk.astype(BF16))) * jnp.exp(ld - m)).astype(BF16),
             q, k, log_d, mx)
    num1 = [_dot(x, _stack4(v_ref[b, rows, :])) for (b, rows), x in zip(steps, s)]
    den1 = [_dot(x, ones_blk) for x in s]
    for i, (b, rows) in enumerate(steps):
        n_in = nm_ref[b, 0:1, :]
        m_in = nm_ref[b, 1:2, :]
        ct = ct_ref[b]
        g = f_cum[i] + m_in
        m_t = jnp.maximum(g, mx[i])
        e = jnp.exp(g - m_t)
        r = jnp.exp(mx[i] - m_t)
        num = r * num1[i] + e * _dot_nt(q[i], ct.astype(BF16))
        den = r * den1[i] + e * _seg_sum(q[i].astype(F32) * n_in, ones_blk)
        o_ref[b, rows, :] = (num / jnp.maximum(jnp.abs(den), jnp.exp(-m_t))).astype(o_ref.dtype)

        m_new = jnp.maximum(f_tot[i] + m_in, a_end[i])
        old = jnp.exp(f_tot[i] + m_in - m_new)
        new = jnp.exp(a_end[i] - m_new)
        ct_ref[b] = ct * old + kv_t[i] * new
        nm_ref[b, 0:1, :] = n_in * old + k_sum[i] * new
        nm_ref[b, 1:2, :] = m_new


def _mlstm_scan(proj, bias, reverse):
    return _scan_call(_mlstm_kernel, "mlstm", [proj] * 4, [BLK_ML_Q, BLK_ML_K, BLK_ML_V, SMALL_BLOCK],
                      [bias], [(BRANCH_W, BRANCH_W), (8, BRANCH_W)], reverse)


def _head_rows(vals):
    rows = [jnp.repeat(v.astype(F32), HEAD_DIM) for v in vals]
    rows += [jnp.zeros((BRANCH_W,), F32)] * (8 - len(rows))
    return jnp.stack(rows)


def _mlstm_branch(proj, gate_b):
    return tuple(_mlstm_scan(proj, _head_rows([gate_b[d, 0], gate_b[d, 1]]), reverse=bool(d)) for d in (0, 1))


HALO = 8
QKV_W = 3 * BRANCH_W


def _gdn_prep_kernel(prev_ref, cur_ref, next_ref, w_ref, o_ref, *, n_ctx_tiles, n_tiles):
    i = pl.program_id(1)
    has_prev = jnp.logical_and(i != 0, i != n_ctx_tiles).astype(F32)
    has_next = jnp.logical_and(i != n_ctx_tiles - 1, i != n_tiles - 1).astype(F32)
    padded = jnp.concatenate([prev_ref[...].astype(F32) * has_prev, cur_ref[...].astype(F32),
                              next_ref[...].astype(F32) * has_next], axis=0)
    w = w_ref[...]
    acc = jnp.zeros((ROW_TILE, QKV_W), F32)
    for j in range(CONV_W):
        off = HALO + j - CONV_W // 2
        acc = acc + padded[off:off + ROW_TILE, :] * w[j:j + 1, :]
    y = _silu(acc)
    ones_blk = _ones_blk()
    q = y[:, 0:BRANCH_W]
    k = y[:, BRANCH_W:2 * BRANCH_W]
    q = q * lax.rsqrt(_seg_sum(q * q, ones_blk) + EPS) * (HEAD_DIM ** -0.5)
    k = k * lax.rsqrt(_seg_sum(k * k, ones_blk) + EPS)
    o_ref[:, 0:BRANCH_W] = q.astype(o_ref.dtype)
    o_ref[:, BRANCH_W:2 * BRANCH_W] = k.astype(o_ref.dtype)
    o_ref[:, 2 * BRANCH_W:] = y[:, 2 * BRANCH_W:].astype(o_ref.dtype)


def _gdn_prep(proj, conv_w, n_ctx_tiles):
    bsz, t, _ = proj.shape
    nt = t // ROW_TILE
    per = ROW_TILE // HALO
    last = t // HALO - 1
    qkv_blk = BLK_GDN_QKV * BRANCH_W // QKV_W
    w8 = jnp.zeros((8, QKV_W), F32).at[:CONV_W].set(conv_w)
    return pl.pallas_call(
        functools.partial(_gdn_prep_kernel, n_ctx_tiles=n_ctx_tiles, n_tiles=nt),
        grid=(bsz, nt),
        in_specs=[pl.BlockSpec((None, HALO, QKV_W), lambda b, i: (b, jnp.maximum(i * per - 1, 0), qkv_blk)),
                  pl.BlockSpec((None, ROW_TILE, QKV_W), lambda b, i: (b, i, qkv_blk)),
                  pl.BlockSpec((None, HALO, QKV_W), lambda b, i: (b, jnp.minimum((i + 1) * per, last), qkv_blk)),
                  pl.BlockSpec((8, QKV_W), lambda b, i: (0, 0))],
        out_specs=pl.BlockSpec((None, ROW_TILE, QKV_W), lambda b, i: (b, i, 0)),
        out_shape=jax.ShapeDtypeStruct((bsz, t, QKV_W), BF16),
        compiler_params=_cparams("parallel", "parallel"),
        name="gdn_prep",
    )(proj, proj, proj, w8)


def _gdn_kernel(q_ref, k_ref, v_ref, sm_ref, par_ref, o_ref, s_ref, *, reverse, bsz):
    @pl.when(pl.program_id(0) == 0)
    def _():
        s_ref[...] = jnp.zeros_like(s_ref)

    lane0 = SM_GD_B if reverse else SM_GD_F
    cum_mat = _cum_mat(reverse)
    tri = _tri_hs(reverse, strict=False)
    tri_strict = _tri_hs(reverse, strict=True)
    blk = _blk_mask()
    eye = jnp.where(_eye_hs(), 1.0, 0.0)
    a_scale = jnp.exp(par_ref[0:1, :])
    dt_bias = par_ref[1:2, :]
    steps = _scan_steps(reverse, bsz)
    each = lambda f, *ls: [f(*a) for a in zip(*ls)]
    sm = [sm_ref[b, rows, :] for b, rows in steps]
    q = [q_ref[b, rows, :] for b, rows in steps]
    kb16 = [k_ref[b, rows, :] for b, rows in steps]
    beta = [jax.nn.sigmoid(_expand_small(x, lane0, N_HEADS)) for x in sm]
    g = [-a_scale * _softplus(_expand_small(x, lane0 + N_HEADS, N_HEADS) + dt_bias) for x in sm]
    cum = [_dot_hl(cum_mat, x) for x in g]
    tot = [jnp.sum(x, axis=0, keepdims=True) for x in g]
    gam = [jnp.where(tri, jnp.exp(jnp.where(tri, c - _row_of(c), 0.0)), 0.0) for c in cum]
    k_beta = each(lambda kk, bb: kk.astype(F32) * bb, kb16, beta)
    k4 = [_stack4(x) for x in kb16]
    a_hs = each(lambda kb, kk, gm: jnp.where(tri_strict, _dot_nt(kb.astype(BF16), kk) * gm, 0.0), k_beta, k4, gam)
    attn = each(lambda qq, kk, gm: (_dot_nt(qq, kk) * gm).astype(BF16), q, k4, gam)

    p = [-x for x in a_hs]
    t_inv = [eye + x for x in p]
    for _ in range(5):
        p = [_dot(x.astype(BF16), _stack4(x.astype(BF16))) for x in p]
        t_inv = each(lambda t, x: t + _dot(t.astype(BF16), _stack4(x.astype(BF16))), t_inv, p)
    w = each(lambda t, kb, c: _dot(t.astype(BF16), _stack4((kb * jnp.exp(c)).astype(BF16))).astype(BF16),
             t_inv, k_beta, cum)
    u = [_dot(t.astype(BF16), _stack4((v_ref[b, rows, :].astype(F32) * bb).astype(BF16)))
         for (b, rows), t, bb in zip(steps, t_inv, beta)]
    q_dec = each(lambda qq, c: (qq.astype(F32) * jnp.exp(c)).astype(BF16), q, cum)
    k_end = each(lambda kk, t, c: (kk.astype(F32) * jnp.exp(t - c)).astype(BF16), kb16, tot, cum)
    for i, (b, rows) in enumerate(steps):
        s = s_ref[b]
        sb = s.astype(BF16)
        v_new = (u[i] - _dot(w[i], sb)).astype(BF16)
        o_ref[b, rows, :] = (_dot(q_dec[i], sb) + _dot(attn[i], _stack4(v_new))).astype(o_ref.dtype)
        s_ref[b] = s * jnp.exp(tot[i]) + jnp.where(blk, _dot_tn(k_end[i], v_new), 0.0)


def _gdn_scan(gqkv, proj, par, reverse):
    return _scan_call(_gdn_kernel, "gdn", [gqkv, gqkv, gqkv, proj], [0, 1, 2, SMALL_BLOCK],
                      [par], [(BRANCH_W, BRANCH_W)], reverse)


def _gdn_branch(proj, conv_w, a_log, dt_bias, n_ctx_tiles):
    gqkv = _gdn_prep(proj, conv_w, n_ctx_tiles)
    return tuple(_gdn_scan(gqkv, proj, _head_rows([a_log[d], dt_bias[d]]), reverse=bool(d))
                 for d in (0, 1))


VX_W = 128
LOG2E = 1.4426950408889634


def _kv_tile(t):
    return next(k for k in (3 * ROW_TILE, 2 * ROW_TILE, ROW_TILE) if t % k == 0)


def _rope_tables(n_lat, n_ctx):
    pos = jnp.arange(n_lat)
    row, col = pos // GRID_W, pos % GRID_W
    inv = ROPE_BASE ** (-jnp.arange(0, ROPE_AXIS, 2, dtype=F32) / ROPE_AXIS)
    ang = jnp.concatenate([row.astype(F32)[:, None] * inv, col.astype(F32)[:, None] * inv], axis=-1)
    cos = jnp.concatenate([jnp.ones((n_ctx, ROPE_AXIS), F32), jnp.cos(ang)], axis=0)
    sin = jnp.concatenate([jnp.zeros((n_ctx, ROPE_AXIS), F32), jnp.sin(ang)], axis=0)
    reps = BRANCH_W // DIFF_QK
    cos_l = jnp.tile(jnp.concatenate([cos, cos], axis=-1), (1, reps))
    sin_l = jnp.tile(jnp.concatenate([-sin, sin], axis=-1), (1, reps))
    return cos_l, sin_l


def _rope_prep_kernel(x_ref, cos_ref, sin_ref, q_ref, kt_ref, vx_ref):
    lane = _iota((ROW_TILE, BRANCH_W), 1)
    first_half = (lane % DIFF_QK) < ROPE_AXIS
    cos = cos_ref[...]
    sin = sin_ref[...]

    def rope(x):
        partner = jnp.where(first_half, pltpu.roll(x, BRANCH_W - ROPE_AXIS, 1), pltpu.roll(x, ROPE_AXIS, 1))
        return x * cos + partner * sin

    q = rope(x_ref[:, 0:BRANCH_W].astype(F32)) * (DIFF_QK ** -0.5 * LOG2E)
    k = rope(x_ref[:, BRANCH_W:2 * BRANCH_W].astype(F32))
    q_ref[...] = q.astype(q_ref.dtype)
    kt_ref[...] = jnp.transpose(k).astype(kt_ref.dtype)
    v = x_ref[:, 2 * BRANCH_W:]
    src = _iota((BRANCH_W, VX_W), 0)
    dst = _iota((BRANCH_W, VX_W), 1)
    ones_col = jnp.where(_iota((ROW_TILE, VX_W), 1) == HEAD_DIM, 1.0, 0.0)
    for h in range(N_HEADS):
        sel = jnp.where(jnp.logical_and(src == h * HEAD_DIM + dst, dst < HEAD_DIM), 1.0, 0.0).astype(BF16)
        vx_ref[h] = (_dot(v, sel) + ones_col).astype(vx_ref.dtype)


def _rope_prep(proj, cos_l, sin_l):
    bsz, t, _ = proj.shape
    nt = t // ROW_TILE
    kvt = _kv_tile(t)
    r = kvt // ROW_TILE
    qkv_blk = BLK_DIFF_QKV * BRANCH_W // QKV_W
    tab = pl.BlockSpec((ROW_TILE, BRANCH_W), lambda b, i: (i, 0))
    return pl.pallas_call(
        _rope_prep_kernel,
        grid=(bsz, nt),
        in_specs=[pl.BlockSpec((None, ROW_TILE, QKV_W), lambda b, i: (b, i, qkv_blk)), tab, tab],
        out_specs=[pl.BlockSpec((None, ROW_TILE, BRANCH_W), lambda b, i: (b, i, 0)),
                   pl.BlockSpec((None, None, BRANCH_W, ROW_TILE), lambda b, i: (b, i // r, 0, i % r)),
                   pl.BlockSpec((None, None, N_HEADS, ROW_TILE, VX_W), lambda b, i: (b, i // r, 0, i % r, 0))],
        out_shape=[jax.ShapeDtypeStruct((bsz, t, BRANCH_W), BF16),
                   jax.ShapeDtypeStruct((bsz, t // kvt, BRANCH_W, kvt), BF16),
                   jax.ShapeDtypeStruct((bsz, t // kvt, N_HEADS, kvt, VX_W), BF16)],
        compiler_params=_cparams("parallel", "parallel"),
        name="rope_prep",
    )(proj, cos_l, sin_l)


N_CHAINS = 2 * N_HEADS


def _attn_kernel(q_ref, kt_ref, vx_ref, lp_ref, o_ref, qm_scr, s0_scr, s1_scr, m0_scr, m1_scr, al0_scr, al1_scr,
                 acc_scr, *, nk, lam_init):
    s_bufs, m_bufs, al_bufs = (s0_scr, s1_scr), (m0_scr, m1_scr), (al0_scr, al1_scr)
    q = q_ref[...]
    lane = _iota(q.shape, 1)
    zero = jnp.zeros_like(q)
    for c in range(N_CHAINS):
        lo = c * DIFF_QK
        qm_scr[c] = jnp.where(jnp.logical_and(lane >= lo, lane < lo + DIFF_QK), q, zero)
    acc_scr[...] = jnp.zeros_like(acc_scr)

    def stage_a(c, kt, slot, first=False):
        s = _dot(qm_scr[c], kt)
        s_bufs[slot][c] = s
        row_max = jnp.broadcast_to(jnp.max(s, axis=1, keepdims=True), (ROW_TILE, VX_W))
        if first:
            m_bufs[slot][c] = row_max
            al_bufs[slot][c] = jnp.zeros_like(row_max)
        else:
            m_old = m_bufs[1 - slot][c]
            m_new = jnp.maximum(m_old, row_max)
            m_bufs[slot][c] = m_new
            al_bufs[slot][c] = jnp.exp2(m_old - m_new)

    def stage_b(c, j, slot):
        m = m_bufs[slot][c]
        s_ref = s_bufs[slot]
        tk = s_ref.shape[2]
        p = jnp.concatenate([jnp.exp2(s_ref[c, :, k0:k0 + VX_W] - m) for k0 in range(0, tk, VX_W)], axis=1)
        acc_scr[c] = acc_scr[c] * al_bufs[slot][c] + _dot(p.astype(BF16), vx_ref[j, c // 2])

    def step(ja, jb, slot_a, first=False):
        kt = None if ja is None else kt_ref[ja]
        for c in range(N_CHAINS):
            if ja is not None:
                stage_a(c, kt, slot_a, first)
            if jb is not None:
                stage_b(c, jb, 1 - slot_a)

    step(0, None, 0, first=True)

    def body(i, carry):
        j = 2 * i + 1
        step(j, j - 1, 1)
        step(j + 1, j, 0)
        return carry

    n_pairs = (nk - 1) // 2
    lax.fori_loop(0, n_pairs, body, 0)
    if nk % 2 == 0:
        step(nk - 1, nk - 2, 1)
    step(None, nk - 1, 1 - (nk - 1) % 2)

    lp = lp_ref[...]
    grp_src = _iota((VX_W, VX_W), 0) // DIFF_QK
    prod1 = lp[0:1, :] * lp[1:2, :]
    prod2 = lp[2:3, :] * lp[3:4, :]
    src = _iota((VX_W, BRANCH_W), 0)
    dst = _iota((VX_W, BRANCH_W), 1)
    out = jnp.zeros((ROW_TILE, BRANCH_W), F32)
    for h in range(N_HEADS):
        pick = jnp.where(grp_src == h, 1.0, 0.0).astype(BF16)
        lam = jnp.exp(_dot_hl_r(prod1, pick)) - jnp.exp(_dot_hl_r(prod2, pick)) + lam_init
        a0 = acc_scr[2 * h]
        a1 = acc_scr[2 * h + 1]
        o_h = a0 / a0[:, HEAD_DIM:HEAD_DIM + 1] - lam * (a1 / a1[:, HEAD_DIM:HEAD_DIM + 1])
        place = jnp.where(jnp.logical_and(dst == src + h * HEAD_DIM, src < HEAD_DIM), 1.0, 0.0).astype(BF16)
        out = out + _dot(o_h.astype(BF16), place)
    o_ref[...] = out.astype(o_ref.dtype)


def _diff_attention(q_rot, kt, vx, lam_p, lam_init, q_tile0, n_q_tiles, kv_len):
    bsz = q_rot.shape[0]
    n_kv_arr, _, kvt = kt.shape[1:]
    tk = min(kv_len, kvt)
    assert kv_len % tk == 0 and kvt % tk == 0
    nk = kv_len // tk
    lp = jnp.zeros((8, VX_W), F32).at[:4].set(lam_p.reshape(4, N_HEADS * DIFF_QK))
    return pl.pallas_call(
        functools.partial(_attn_kernel, nk=nk, lam_init=lam_init),
        grid=(bsz, n_q_tiles),
        in_specs=[pl.BlockSpec((None, ROW_TILE, BRANCH_W), lambda b, i: (b, i + q_tile0, 0)),
                  pl.BlockSpec((None, nk, BRANCH_W, tk), lambda b, i: (b, 0, 0, 0), pipeline_mode=pl.Buffered(1)),
                  pl.BlockSpec((None, nk, N_HEADS, tk, VX_W), lambda b, i: (b, 0, 0, 0, 0),
                               pipeline_mode=pl.Buffered(1)),
                  pl.BlockSpec((8, VX_W), lambda b, i: (0, 0))],
        out_specs=pl.BlockSpec((None, ROW_TILE, BRANCH_W), lambda b, i: (b, i, 0)),
        out_shape=jax.ShapeDtypeStruct((bsz, n_q_tiles * ROW_TILE, BRANCH_W), BF16),
        scratch_shapes=[pltpu.VMEM((N_CHAINS, ROW_TILE, BRANCH_W), BF16)]
                       + [pltpu.VMEM((N_CHAINS, ROW_TILE, tk), F32)] * 2
                       + [pltpu.VMEM((N_CHAINS, ROW_TILE, VX_W), F32)] * 4
                       + [pltpu.VMEM((N_CHAINS, ROW_TILE, VX_W), F32)],
        compiler_params=_cparams("parallel", "arbitrary"),
        name="diff_attention",
    )(q_rot, kt, vx, lp)


def _diff_branch(proj, tables, lam_p, lam_init, n_ctx_tiles, with_ctx):
    t = proj.shape[1]
    q_rot, kt, vx = _rope_prep(proj, *tables)
    lat = _diff_attention(q_rot, kt, vx, lam_p, lam_init, n_ctx_tiles, t // ROW_TILE - n_ctx_tiles, t)
    if not with_ctx:
        return lat
    ctx = _diff_attention(q_rot, kt, vx, lam_p, lam_init, 0, n_ctx_tiles, n_ctx_tiles * ROW_TILE)
    return jnp.concatenate([ctx, lat], axis=1)


def _prepare(x, c, ctx, c_ctx, w_mod, b_mod):
    bsz, _, d = x.shape
    ct = ctx.shape[1]
    assert ct % ROW_TILE == 0 and x.shape[1] % ROW_TILE == 0
    xa = jnp.concatenate([ctx, x], axis=1)
    rows = 8 * ((bsz + 1 + 7) // 8)
    cond = jnp.zeros((rows, d), F32).at[:bsz].set(c).at[bsz].set(c_ctx)
    mods = _modulation(cond, w_mod, b_mod)
    return dict(xa=xa, mods=mods, n_ctx_tiles=ct // ROW_TILE, ct=ct)


def _layer_inproj(st, l, norm_mix, w_in):
    mods3 = st["mods"][l][:, None, :]
    return _inproj(st["xa"], mods3, norm_mix[l], _permute_w_in(w_in[l]), st["n_ctx_tiles"])


def _merge_kernel(x_ref, mod_ref, glf_ref, glb_ref, mlf_ref, mlb_ref, gdf_ref, gdb_ref, at_ref,
                  g_gla_ref, g_ml_ref, g_gd_ref, mg0_ref, mg1_ref, mg2_ref, mg3_ref,
                  hn_ref, wb_ref, wo_ref, o_ref, *, d, lam_init):
    ones_blk = _ones_blk()

    def head_norm(o, i):
        ms = _seg_sum(o * o, ones_blk) * (1.0 / HEAD_DIM)
        return o * lax.rsqrt(ms + EPS) * hn_ref[i:i + 1, :]

    f32 = lambda r: r[...].astype(F32)
    ys = [head_norm(f32(glf_ref) + f32(glb_ref), 0) * _silu(f32(g_gla_ref)),
          head_norm(f32(mlf_ref) + f32(mlb_ref), 1) * jax.nn.sigmoid(f32(g_ml_ref)),
          head_norm(f32(gdf_ref) + f32(gdb_ref), 2) * _silu(f32(g_gd_ref)),
          head_norm(f32(at_ref), 3) * (1.0 - lam_init)]
    acc = None
    for i, (y, mg_ref) in enumerate(zip(ys, (mg0_ref, mg1_ref, mg2_ref, mg3_ref))):
        term = jax.nn.sigmoid(f32(mg_ref)) * _dot(y.astype(BF16), wb_ref[i])
        acc = term if acc is None else acc + term
    out = _dot(acc.astype(BF16), wo_ref[...])
    g1 = mod_ref[...][:, 2 * d:3 * d]
    o_ref[...] = x_ref[...] + g1 * out


def _merge(xa, mods3, proj, scans, attn, hn, wb, wo, n_ctx_tiles, with_ctx, lam_init):
    bsz, t, d = xa.shape
    off = 0 if with_ctx else n_ctx_tiles
    nt = t // ROW_TILE - off
    rows = lambda col: pl.BlockSpec((None, ROW_TILE, BRANCH_W), lambda b, i: (b, i + off, col))
    gate = lambda j: pl.BlockSpec((None, ROW_TILE, d), lambda b, i: (b, i + off, BLK_GATES * BRANCH_W // d + j))
    const = lambda shape: pl.BlockSpec(shape, lambda b, i: (0,) * len(shape))
    return pl.pallas_call(
        functools.partial(_merge_kernel, d=d, lam_init=lam_init),
        grid=(bsz, nt),
        in_specs=[pl.BlockSpec((None, ROW_TILE, d), lambda b, i: (b, i + off, 0)),
                  pl.BlockSpec((None, 1, 6 * d), lambda b, i: (jnp.where(i + off < n_ctx_tiles, bsz, b), 0, 0))]
                 + [rows(0)] * 6
                 + [pl.BlockSpec((None, ROW_TILE, BRANCH_W), lambda b, i: (b, i, 0))]
                 + [rows(BLK_GLA_G), rows(BLK_ML_G), rows(BLK_GDN_G)]
                 + [gate(j) for j in range(N_BRANCH)]
                 + [const((8, BRANCH_W)), const((N_BRANCH, BRANCH_W, d)), const((d, d))],
        out_specs=pl.BlockSpec((None, ROW_TILE, d), lambda b, i: (b, i, 0)),
        out_shape=jax.ShapeDtypeStruct((bsz, nt * ROW_TILE, d), F32),
        compiler_params=_cparams("parallel", "parallel"),
        name="merge",
    )(xa, mods3, *scans, attn, proj, proj, proj, proj, proj, proj, proj, hn, wb, wo)


GATE_LANES = 128


def _router_select(hf, rw_ref, rb_ref):
    h_hi, h_lo = _split_hl(hf)
    w_hi, w_lo = _split_hl(rw_ref[...])
    logits = _dot_nt(w_hi, h_hi) + _dot_nt(w_hi, h_lo) + _dot_nt(w_lo, h_hi)
    scores = jax.nn.sigmoid(logits)
    sel = scores + rb_ref[...]
    s = [sel[e:e + 1, :] for e in range(N_EXPERTS)]
    sc = [scores[e:e + 1, :] for e in range(N_EXPERTS)]
    grp = []
    for g in range(N_GROUPS):
        a, b, c, dd = s[4 * g:4 * g + 4]
        grp.append(functools.reduce(jnp.maximum, [a + b, a + c, a + dd, b + c, b + dd, c + dd]))
    gmax = functools.reduce(jnp.maximum, grp)
    chosen, taken = [], None
    for g in range(N_GROUPS):
        hit = grp[g] == gmax
        if taken is not None:
            hit = jnp.logical_and(hit, jnp.logical_not(taken))
        taken = hit if taken is None else jnp.logical_or(taken, hit)
        chosen.append(hit)
    ms = [jnp.where(chosen[e // EXPERTS_PER_GROUP], s[e], NEG_BIG) for e in range(N_EXPERTS)]

    def first_argmax(vals):
        top = functools.reduce(jnp.maximum, vals)
        hits, seen = [], None
        for v in vals:
            hit = v == top
            if seen is not None:
                hit = jnp.logical_and(hit, jnp.logical_not(seen))
            seen = hit if seen is None else jnp.logical_or(seen, hit)
            hits.append(hit)
        return hits

    oh1 = first_argmax(ms)
    oh2 = first_argmax([jnp.where(o, NEG_BIG, v) for o, v in zip(oh1, ms)])
    zero = jnp.zeros_like(sc[0])
    w1 = functools.reduce(jnp.add, [jnp.where(o, v, zero) for o, v in zip(oh1, sc)])
    w2 = functools.reduce(jnp.add, [jnp.where(o, v, zero) for o, v in zip(oh2, sc)])
    tot = w1 + w2
    ids = [functools.reduce(jnp.add, [jnp.where(o, float(e), 0.0) for e, o in enumerate(oh)]) for oh in (oh1, oh2)]
    return ids[0], ids[1], w1 / tot, w2 / tot


TOK_ROWS = 8
FFN_TILE = ROW_TILE


def _store_token_tiles(ref, val):
    n = val.shape[0]
    for s in range(TOK_ROWS):
        ref[pl.ds(s, n, stride=TOK_ROWS), :] = val[:, s * 128:(s + 1) * 128]


def _load_token_tiles(ref, n):
    return jnp.concatenate([ref[pl.ds(s, n, stride=TOK_ROWS), :] for s in range(TOK_ROWS)], axis=1)


def _moe_router_kernel(x_ref, mod_ref, g_ref, rw_ref, rb_ref, h_ref, ids_ref, wcol_ref, *, d):
    mod = mod_ref[...]
    hf = _rms_mod(x_ref[...], g_ref[...], mod[:, 3 * d:4 * d], mod[:, 4 * d:5 * d])
    _store_token_tiles(h_ref, hf)
    e1, e2, w1, w2 = _router_select(hf, rw_ref, rb_ref)
    pad = jnp.zeros((6, ROW_TILE), F32)
    ids_ref[...] = jnp.concatenate([e1, e2, pad], axis=0).astype(jnp.int32)
    eye = jnp.where(_iota((8, GATE_LANES), 0) == _iota((8, GATE_LANES), 1), 1.0, 0.0).astype(BF16)
    w_hi, w_lo = _split_hl(jnp.concatenate([w1, w2, pad], axis=0))
    wcol_ref[...] = _dot_tn(w_hi, eye) + _dot_tn(w_lo, eye)


def _moe_router(xa, mods3, g, rw_t, rb, n_ctx_tiles):
    bsz, t, d = xa.shape
    nt = t // ROW_TILE
    n = bsz * t
    flat = lambda b, i: b * nt + i
    return pl.pallas_call(
        functools.partial(_moe_router_kernel, d=d),
        grid=(bsz, nt),
        in_specs=[pl.BlockSpec((None, ROW_TILE, d), lambda b, i: (b, i, 0)),
                  pl.BlockSpec((None, 1, 6 * d), _mod_row_map(n_ctx_tiles, bsz)),
                  pl.BlockSpec((1, d), lambda b, i: (0, 0)),
                  pl.BlockSpec((N_EXPERTS, d), lambda b, i: (0, 0)),
                  pl.BlockSpec((N_EXPERTS, ROW_TILE), lambda b, i: (0, 0))],
        out_specs=[pl.BlockSpec((ROW_TILE * TOK_ROWS, 128), lambda b, i: (flat(b, i), 0)),
                   pl.BlockSpec((8, ROW_TILE), lambda b, i: (0, flat(b, i))),
                   pl.BlockSpec((ROW_TILE, GATE_LANES), lambda b, i: (flat(b, i), 0))],
        out_shape=[jax.ShapeDtypeStruct((n * TOK_ROWS, 128), F32),
                   jax.ShapeDtypeStruct((8, n), jnp.int32),
                   jax.ShapeDtypeStruct((n, GATE_LANES), F32)],
        compiler_params=_cparams("parallel", "parallel"),
        name="moe_router",
    )(xa, mods3, g.reshape(1, d), rw_t, rb)


def _moe_plan(ids, n):
    n_tiles = -(-2 * n // FFN_TILE) + N_EXPERTS
    eid = ids[:2].reshape(2 * n)
    order = jnp.argsort(eid, stable=True).astype(jnp.int32)
    inv = jnp.argsort(order).astype(jnp.int32)
    experts = jnp.arange(N_EXPERTS, dtype=jnp.int32)
    onehot = (eid[:, None] == experts[None, :]).astype(jnp.int32)
    counts = jnp.sum(onehot, axis=0)
    first = jnp.cumsum(counts) - counts
    padded = (counts + FFN_TILE - 1) // FFN_TILE * FFN_TILE
    ends = jnp.cumsum(padded)
    offs = ends - padded
    dest = inv + jnp.sum(onehot * (offs - first)[None, :], axis=1)
    tile_start = jnp.arange(n_tiles, dtype=jnp.int32) * FFN_TILE
    tile_expert = jnp.minimum(jnp.sum((ends[None, :] <= tile_start[:, None]).astype(jnp.int32), axis=1), N_EXPERTS - 1)
    tile_hot = (tile_expert[:, None] == experts[None, :]).astype(jnp.int32)
    lookup = lambda table: jnp.repeat(jnp.sum(tile_hot * table[None, :], axis=1), FFN_TILE)
    local = jnp.arange(n_tiles * FFN_TILE, dtype=jnp.int32) - lookup(offs)
    tok_sorted = order % n
    src = jnp.where(local < lookup(counts), tok_sorted[jnp.clip(lookup(first) + local, 0, 2 * n - 1)], 0)
    dest_tiles = dest.reshape(2, n // ROW_TILE, ROW_TILE).transpose(1, 0, 2)
    return src.reshape(n_tiles, 1, FFN_TILE), tile_expert, dest_tiles


def _gather_rows(idx_ref, k, src_hbm, dst, sem, n_rows, unrolled):
    def start(r, priority):
        row = idx_ref[k, r]
        pltpu.make_async_copy(src_hbm.at[pl.ds(pl.multiple_of(row * TOK_ROWS, TOK_ROWS), TOK_ROWS), :],
                              dst.at[pl.ds(pl.multiple_of(r * TOK_ROWS, TOK_ROWS), TOK_ROWS), :],
                              sem).start(priority=priority)

    if unrolled:
        for r in range(n_rows):
            start(r, r % 2)
    else:
        def body(r, carry):
            start(r, 0)
            return carry

        lax.fori_loop(0, n_rows, body, 0, unroll=8)


def _wait_rows(src_hbm, dst, sem, n_rows):
    pltpu.make_async_copy(src_hbm.at[pl.ds(0, n_rows * TOK_ROWS), :], dst, sem).wait()


def _moe_experts_kernel(te_ref, src_ref, nxt_ref, h_hbm, wg_ref, wu_ref, wd_ref, y_ref,
                        buf, wg_b, wu_b, wd_b, sem):
    t = pl.program_id(0)
    last = pl.num_programs(0) - 1
    slot = lax.rem(t, 2)

    @pl.when(jnp.logical_or(t == 0, te_ref[t] != te_ref[jnp.maximum(t - 1, 0)]))
    def _():
        wg_b[...] = wg_ref[...].astype(BF16)
        wu_b[...] = wu_ref[...].astype(BF16)
        wd_b[...] = wd_ref[...].astype(BF16)

    @pl.when(t == 0)
    def _():
        _gather_rows(src_ref, 0, h_hbm, buf.at[0], sem.at[0], FFN_TILE, unrolled=False)

    _wait_rows(h_hbm, buf.at[slot], sem.at[slot], FFN_TILE)
    _gather_rows(nxt_ref, 0, h_hbm, buf.at[1 - slot], sem.at[1 - slot], FFN_TILE, unrolled=True)
    h = _load_token_tiles(buf.at[slot], FFN_TILE).astype(BF16)
    act = (_silu(_dot(h, wg_b[...])) * _dot(h, wu_b[...])).astype(BF16)
    _store_token_tiles(y_ref, _dot(act, wd_b[...]))

    @pl.when(t == last)
    def _():
        _wait_rows(h_hbm, buf.at[1 - slot], sem.at[1 - slot], FFN_TILE)


def _moe_experts(h_tt, plan, wg, wu, wd, layer):
    src, tile_expert, _ = plan
    n_tiles = src.shape[0]
    _, ne, d, de = wg.shape
    idx_spec = lambda step: pl.BlockSpec((None, 1, FFN_TILE), lambda t, te: (jnp.minimum(t + step, n_tiles - 1), 0, 0),
                                         memory_space=pltpu.SMEM)
    grid_spec = pltpu.PrefetchScalarGridSpec(
        num_scalar_prefetch=1,
        grid=(n_tiles,),
        in_specs=[idx_spec(0), idx_spec(1),
                  pl.BlockSpec(memory_space=pl.ANY),
                  pl.BlockSpec((None, None, d, de), lambda t, te: (layer, te[t], 0, 0)),
                  pl.BlockSpec((None, None, d, de), lambda t, te: (layer, te[t], 0, 0)),
                  pl.BlockSpec((None, None, de, d), lambda t, te: (layer, te[t], 0, 0))],
        out_specs=pl.BlockSpec((FFN_TILE * TOK_ROWS, 128), lambda t, te: (t, 0)),
        scratch_shapes=[pltpu.VMEM((2, FFN_TILE * TOK_ROWS, 128), F32),
                        pltpu.VMEM((d, de), BF16), pltpu.VMEM((d, de), BF16), pltpu.VMEM((de, d), BF16),
                        pltpu.SemaphoreType.DMA((2,))])
    return pl.pallas_call(
        _moe_experts_kernel,
        grid_spec=grid_spec,
        out_shape=jax.ShapeDtypeStruct((n_tiles * FFN_TILE * TOK_ROWS, 128), F32),
        compiler_params=_cparams("arbitrary"),
        name="moe_experts",
    )(tile_expert, src, src, h_tt, wg, wu, wd)


def _moe_combine_kernel(dest_ref, nxt_ref, x_ref, mod_ref, wcol_ref, gf_ref, y_hbm, o_ref, buf, sem, *, d, final):
    j = pl.program_id(0)
    last = pl.num_programs(0) - 1
    slot = lax.rem(j, 2)

    def gather(idx_ref, s, unrolled):
        for k in range(2):
            _gather_rows(idx_ref, k, y_hbm, buf.at[s, k], sem.at[s], ROW_TILE, unrolled)

    def wait(s):
        for k in range(2):
            _wait_rows(y_hbm, buf.at[s, k], sem.at[s], ROW_TILE)

    @pl.when(j == 0)
    def _():
        gather(dest_ref, 0, unrolled=False)

    wait(slot)
    gather(nxt_ref, 1 - slot, unrolled=True)
    w = wcol_ref[...]
    mix = (w[:, 0:1] * _load_token_tiles(buf.at[slot, 0], ROW_TILE)
           + w[:, 1:2] * _load_token_tiles(buf.at[slot, 1], ROW_TILE))
    g2 = mod_ref[...][:, 5 * d:6 * d]
    x_new = x_ref[...] + g2 * mix
    if final:
        x_new = x_new * lax.rsqrt(jnp.mean(x_new * x_new, axis=-1, keepdims=True) + EPS) * gf_ref[...]
    o_ref[...] = x_new

    @pl.when(j == last)
    def _():
        wait(1 - slot)


def _moe_combine(xa, mods3, wcol, y_tt, plan, n_ctx_tiles, final_g=None):
    bsz, t, d = xa.shape
    nt = t // ROW_TILE
    n_tiles = bsz * nt
    dest_tiles = plan[2]
    idx_spec = lambda step: pl.BlockSpec((None, 2, ROW_TILE), lambda j: (jnp.minimum(j + step, n_tiles - 1), 0, 0),
                                         memory_space=pltpu.SMEM)
    final = final_g is not None
    gf = (final_g if final else jnp.ones((d,), F32)).reshape(1, d)
    return pl.pallas_call(
        functools.partial(_moe_combine_kernel, d=d, final=final),
        grid=(n_tiles,),
        in_specs=[idx_spec(0), idx_spec(1),
                  pl.BlockSpec((None, ROW_TILE, d), lambda j: (j // nt, j % nt, 0)),
                  pl.BlockSpec((None, 1, 6 * d), lambda j: (jnp.where(j % nt < n_ctx_tiles, bsz, j // nt), 0, 0)),
                  pl.BlockSpec((ROW_TILE, GATE_LANES), lambda j: (j, 0)),
                  pl.BlockSpec((1, d), lambda j: (0, 0)),
                  pl.BlockSpec(memory_space=pl.ANY)],
        out_specs=pl.BlockSpec((None, ROW_TILE, d), lambda j: (j // nt, j % nt, 0)),
        out_shape=jax.ShapeDtypeStruct((bsz, t, d), F32),
        scratch_shapes=[pltpu.VMEM((2, 2, ROW_TILE * TOK_ROWS, 128), F32), pltpu.SemaphoreType.DMA((2,))],
        compiler_params=_cparams("arbitrary"),
        name="moe_combine",
    )(dest_tiles, dest_tiles, xa, mods3, wcol, gf, y_tt)


def _moe_sparse(xa, mods3, g, rw_t, rb, wg, wu, wd, layer, n_ctx_tiles, final_g=None):
    bsz, t, _ = xa.shape
    h_tt, ids, wcol = _moe_router(xa, mods3, g, rw_t, rb, n_ctx_tiles)
    plan = _moe_plan(ids, bsz * t)
    y_tt = _moe_experts(h_tt, plan, wg, wu, wd, layer)
    return _moe_combine(xa, mods3, wcol, y_tt, plan, n_ctx_tiles, final_g)


def kernel(x, c, ctx, c_ctx, w_mod, b_mod, norm_mix, norm_ffn, w_in, gla_a2, gla_ab, mlstm_gate_b, gdn_conv, gdn_a_log, gdn_dt_bias, diff_lambda, head_norm, w_branch, w_out, router_w, router_b, w_gate, w_up, w_down, norm_final):
    depth = w_in.shape[0]
    st = _prepare(x, c, ctx, c_ctx, w_mod, b_mod)
    n_ctx_tiles = st["n_ctx_tiles"]
    tables = _rope_tables(x.shape[1], st["ct"])
    rw_t = router_w.T
    rb = jnp.broadcast_to(router_b.astype(F32)[:, None], (N_EXPERTS, ROW_TILE))
    for l in range(depth):
        with_ctx = l < depth - 1
        lam_init = 0.8 - 0.6 * math.exp(-0.3 * l)
        mods3 = st["mods"][l][:, None, :]
        proj = _layer_inproj(st, l, norm_mix, w_in)
        scans = (*_gla_branch(proj, gla_a2[l], gla_ab[l]),
                 *_mlstm_branch(proj, mlstm_gate_b[l]),
                 *_gdn_branch(proj, gdn_conv[l], gdn_a_log[l], gdn_dt_bias[l], n_ctx_tiles))
        attn = _diff_branch(proj, tables, diff_lambda[l], lam_init, n_ctx_tiles, with_ctx)
        hn = jnp.zeros((8, BRANCH_W), F32).at[:N_BRANCH].set(head_norm[l])
        xa = _merge(st["xa"], mods3, proj, scans, attn, hn, w_branch[l].astype(BF16), w_out[l].astype(BF16),
                    n_ctx_tiles, with_ctx, lam_init)
        if not with_ctx:
            n_ctx_tiles = 0
        xa = _moe_sparse(xa, mods3, norm_ffn[l], rw_t, rb, w_gate, w_up, w_down, l, n_ctx_tiles,
                         final_g=None if with_ctx else norm_final)
        st = dict(st, xa=xa, n_ctx_tiles=n_ctx_tiles)
    return st["xa"]
```
